```python
import functools
import jax, jax.numpy as jnp
from jax import lax
import numpy as np

D_MODEL = 1024
BATCH = 32
SEQ = 256
DEPTH = 2
DEC_BATCH = 8
DEC_SEQ = 4096
PAST_LEN = 512

GRID_W = 64
HEAD_DIM = 64
WIN_HEADS = 8
WIN_KV_HEADS = 2
WINDOW = 128
BLOCK = 128
NBR_HEADS = 8
NBR_ROWS = 8
NBR_COLS = 16
MLA_HEADS = 8
Q_LORA = 256
KV_LORA = 128
NOPE_DIM = 64
ROPE_DIM = 32
V_DIM = 64
QK_DIM = NOPE_DIM + ROPE_DIM
N_EXPERTS = 16
N_GROUPS = 4
EXPERTS_PER_GROUP = N_EXPERTS // N_GROUPS
TOP_K = 2
D_FF = 512
N_BRANCHES = 3
ROPE_BASE = 10000.0
EPS = 1e-6

WIN_Q = WIN_HEADS * HEAD_DIM
WIN_KV = WIN_KV_HEADS * HEAD_DIM
NBR_D = NBR_HEADS * HEAD_DIM
MLA_OUT = MLA_HEADS * V_DIM
SPLIT_SIZES = (WIN_Q, WIN_KV, WIN_KV, NBR_D, NBR_D, NBR_D, Q_LORA, KV_LORA, ROPE_DIM, N_BRANCHES * D_MODEL)
IN_COLS = sum(SPLIT_SIZES)
SPLIT_POINTS = tuple(int(s) for s in np.cumsum(SPLIT_SIZES)[:-1])

kernel_name = 'hybrid_diffusion_prefix_trunk_step'


def rms_norm(x, g):
    xf = x.astype(jnp.float32)
    y = xf * lax.rsqrt(jnp.mean(xf * xf, axis=-1, keepdims=True) + EPS)
    return (y * g.astype(jnp.float32)).astype(x.dtype)


def heads(x, n):
    return x.reshape(x.shape[:2] + (n, -1))


def _rotate(x, pos):
    half = x.shape[-1] // 2
    freqs = ROPE_BASE ** (-jnp.arange(half, dtype=jnp.float32) / half)
    ang = pos.astype(jnp.float32)[:, None] * freqs[None, :]
    cos = jnp.cos(ang)[None, :, None, :]
    sin = jnp.sin(ang)[None, :, None, :]
    xf = x.astype(jnp.float32)
    x1, x2 = xf[..., :half], xf[..., half:]
    return jnp.concatenate([x1 * cos - x2 * sin, x1 * sin + x2 * cos], axis=-1).astype(x.dtype)


def axial_rope(x, rows, cols):
    a = x.shape[-1] // 2
    return jnp.concatenate([_rotate(x[..., :a], rows), _rotate(x[..., a:], cols)], axis=-1)


def rope_tail(x, rows, cols):
    return jnp.concatenate([x[..., :NOPE_DIM], axial_rope(x[..., NOPE_DIM:], rows, cols)], axis=-1)


def sdpa(q, k, v, sink=None, mask=None):
    scale = q.shape[-1] ** -0.5
    s = jnp.einsum('bthd,blhd->bhtl', q, k).astype(jnp.float32) * scale
    if mask is not None:
        s = jnp.where(mask[None, None], s, -jnp.inf)
    if sink is not None:
        sk = jnp.broadcast_to(sink.astype(jnp.float32)[None, :, None, None], s.shape[:-1] + (1,))
        p = jax.nn.softmax(jnp.concatenate([s, sk], axis=-1), axis=-1)[..., :-1]
    else:
        p = jax.nn.softmax(s, axis=-1)
    return jnp.einsum('bhtl,blhd->bthd', p.astype(v.dtype), v)


def blocked_attention(q, k, v, sink=None):
    b, t, h, d = q.shape
    qb = jnp.moveaxis(q.reshape(b, t // BLOCK, BLOCK, h, d), 1, 0)
    o = lax.map(lambda qi: sdpa(qi, k, v, sink), qb)
    return jnp.moveaxis(o, 0, 1).reshape(b, t, h, v.shape[-1])


def window_attention(q, k, v, k_ctx, v_ctx, sink):
    b, n, h, d = q.shape
    rep = h // k.shape[2]
    pad = ((0, 0), (BLOCK, BLOCK), (0, 0), (0, 0))
    k_pad, v_pad = jnp.pad(k, pad), jnp.pad(v, pad)
    k_ctx_h, v_ctx_h = jnp.repeat(k_ctx, rep, axis=2), jnp.repeat(v_ctx, rep, axis=2)
    ctx_mask = jnp.ones((BLOCK, k_ctx.shape[1]), dtype=bool)

    def block(i):
        start = i * BLOCK
        q_i = lax.dynamic_slice_in_dim(q, start, BLOCK, axis=1)
        k_i = jnp.repeat(lax.dynamic_slice_in_dim(k_pad, start, 3 * BLOCK, axis=1), rep, axis=2)
        v_i = jnp.repeat(lax.dynamic_slice_in_dim(v_pad, start, 3 * BLOCK, axis=1), rep, axis=2)
        q_pos = start + jnp.arange(BLOCK)
        k_pos = start - BLOCK + jnp.arange(3 * BLOCK)
        band = ((jnp.abs(q_pos[:, None] - k_pos[None, :]) <= WINDOW)
                & (k_pos[None, :] >= 0) & (k_pos[None, :] < n))
        mask = jnp.concatenate([band, ctx_mask], axis=1)
        return sdpa(q_i, jnp.concatenate([k_i, k_ctx_h], axis=1),
                    jnp.concatenate([v_i, v_ctx_h], axis=1), sink, mask)

    o = lax.map(block, jnp.arange(n // BLOCK))
    return jnp.moveaxis(o, 0, 1).reshape(b, n, h, d)


def neighbourhood_attention(q, k, v, k_ctx, v_ctx, rel_bias):
    b, n, h, d = q.shape
    rows = n // GRID_W
    kr, kc = min(NBR_ROWS, rows), NBR_COLS
    scale = d ** -0.5
    qg = q.reshape(b, rows, GRID_W, h, d)
    kg = k.reshape(b, rows, GRID_W, h, d)
    vg = v.reshape(b, rows, GRID_W, h, d)
    col = np.arange(GRID_W)
    col_idx = np.clip(col - kc // 2, 0, GRID_W - kc)[:, None] + np.arange(kc)[None, :]
    dc_idx = col_idx - col[:, None] + (NBR_COLS - 1)

    def row_block(r):
        rs = jnp.clip(r - kr // 2, 0, rows - kr)
        k_nb = lax.dynamic_slice_in_dim(kg, rs, kr, axis=1)[:, :, col_idx]
        v_nb = lax.dynamic_slice_in_dim(vg, rs, kr, axis=1)[:, :, col_idx]
        k_nb = k_nb.transpose(0, 2, 1, 3, 4, 5).reshape(b, GRID_W, kr * kc, h, d)
        v_nb = v_nb.transpose(0, 2, 1, 3, 4, 5).reshape(b, GRID_W, kr * kc, h, d)
        dr_idx = rs + jnp.arange(kr) - r + (NBR_ROWS - 1)
        bias = rel_bias[:, dr_idx[None, :, None], dc_idx[:, None, :]].reshape(h, GRID_W, kr * kc)
        q_r = lax.dynamic_index_in_dim(qg, r, axis=1, keepdims=False)
        s_nb = jnp.einsum('bwhd,bwkhd->bhwk', q_r, k_nb).astype(jnp.float32) * scale + bias.astype(jnp.float32)[None]
        s_ctx = jnp.einsum('bwhd,blhd->bhwl', q_r, k_ctx).astype(jnp.float32) * scale
        p = jax.nn.softmax(jnp.concatenate([s_nb, s_ctx], axis=-1), axis=-1).astype(v.dtype)
        return (jnp.einsum('bhwk,bwkhd->bwhd', p[..., :kr * kc], v_nb)
                + jnp.einsum('bhwl,blhd->bwhd', p[..., kr * kc:], v_ctx))

    o = lax.map(row_block, jnp.arange(rows))
    return jnp.moveaxis(o, 0, 1).reshape(b, n, h, d)


def mla_queries(cq, p):
    b, t, _ = cq.shape
    q = rms_norm(cq, p['g_q_lora']) @ p['w_uq']
    return rms_norm(q.reshape(b, t, MLA_HEADS, QK_DIM), p['g_qk_mla'][0])


def mla_keys(ckv, k_rope, p):
    b, l, _ = ckv.shape
    kv = (ckv @ p['w_ukv']).reshape(b, l, MLA_HEADS, NOPE_DIM + V_DIM)
    k_rope_h = jnp.broadcast_to(k_rope[:, :, None, :], (b, l, MLA_HEADS, ROPE_DIM))
    k = rms_norm(jnp.concatenate([kv[..., :NOPE_DIM], k_rope_h], axis=-1), p['g_qk_mla'][1])
    return k, kv[..., NOPE_DIM:]


def gated_merge(oa, ob, oc, gates, p):
    b, t = gates.shape[:2]
    g = jax.nn.sigmoid(gates.astype(jnp.float32)).astype(gates.dtype).reshape(b, t, N_BRANCHES, D_MODEL)
    m = (g[:, :, 0] * (oa.reshape(b, t, WIN_Q) @ p['w_o_win'])
         + g[:, :, 1] * (ob.reshape(b, t, NBR_D) @ p['w_o_nbr'])
         + g[:, :, 2] * (oc.reshape(b, t, MLA_OUT) @ p['w_o_mla']))
    return m @ p['w_out']


def mix_context(h, p):
    qa, ka, va, qb, kb, vb, cq, ckv, krope, gates = jnp.split(h @ p['w_in'], SPLIT_POINTS, axis=-1)
    rep = WIN_HEADS // WIN_KV_HEADS
    qa = rms_norm(heads(qa, WIN_HEADS), p['g_qk_win'][0])
    ka = rms_norm(heads(ka, WIN_KV_HEADS), p['g_qk_win'][1])
    va = heads(va, WIN_KV_HEADS)
    oa = blocked_attention(qa, jnp.repeat(ka, rep, axis=2), jnp.repeat(va, rep, axis=2), p['win_sink'])
    qb = rms_norm(heads(qb, NBR_HEADS), p['g_qk_nbr'][0])
    kb = rms_norm(heads(kb, NBR_HEADS), p['g_qk_nbr'][1])
    vb = heads(vb, NBR_HEADS)
    ob = blocked_attention(qb, kb, vb)
    ckv = rms_norm(ckv, p['g_kv_lora'])
    kc, vc = mla_keys(ckv, krope, p)
    oc = blocked_attention(mla_queries(cq, p), kc, vc)
    return gated_merge(oa, ob, oc, gates, p), (ka, va, kb, vb, ckv, krope)


def mix_latent(h, p, ctx):
    ka_ctx, va_ctx, kb_ctx, vb_ctx, ckv_ctx, krope_ctx = ctx
    n = h.shape[1]
    pos = jnp.arange(n)
    rows, cols = pos // GRID_W, pos % GRID_W
    qa, ka, va, qb, kb, vb, cq, ckv, krope, gates = jnp.split(h @ p['w_in'], SPLIT_POINTS, axis=-1)
    qa = axial_rope(rms_norm(heads(qa, WIN_HEADS), p['g_qk_win'][0]), rows, cols)
    ka = axial_rope(rms_norm(heads(ka, WIN_KV_HEADS), p['g_qk_win'][1]), rows, cols)
    oa = window_attention(qa, ka, heads(va, WIN_KV_HEADS), ka_ctx, va_ctx, p['win_sink'])
    qb = rms_norm(heads(qb, NBR_HEADS), p['g_qk_nbr'][0])
    kb = rms_norm(heads(kb, NBR_HEADS), p['g_qk_nbr'][1])
    ob = neighbourhood_attention(qb, kb, heads(vb, NBR_HEADS), kb_ctx, vb_ctx, p['nbr_rel_bias'])
    q_c = rope_tail(mla_queries(cq, p), rows, cols)
    k_lat, v_lat = mla_keys(rms_norm(ckv, p['g_kv_lora']), krope, p)
    k_lat = rope_tail(k_lat, rows, cols)
    k_ctx, v_ctx = mla_keys(ckv_ctx, krope_ctx, p)
    oc = blocked_attention(q_c, jnp.concatenate([k_lat, k_ctx], axis=1),
                           jnp.concatenate([v_lat, v_ctx], axis=1))
    return gated_merge(oa, ob, oc, gates, p), None


def moe(h, p):
    scores = jax.nn.sigmoid((h @ p['w_router']).astype(jnp.float32))
    sel = scores + p['b_router'].astype(jnp.float32)
    grouped = sel.reshape(sel.shape[:-1] + (N_GROUPS, EXPERTS_PER_GROUP))
    group_score = lax.top_k(grouped, TOP_K)[0].sum(axis=-1)
    best = jnp.argmax(group_score, axis=-1)
    in_group = (jnp.arange(N_EXPERTS) // EXPERTS_PER_GROUP) == best[..., None]
    _, idx = lax.top_k(jnp.where(in_group, sel, -jnp.inf), TOP_K)
    w = jnp.take_along_axis(scores, idx, axis=-1)
    w = w / jnp.sum(w, axis=-1, keepdims=True)
    combine = jnp.sum(jax.nn.one_hot(idx, N_EXPERTS, dtype=jnp.float32) * w[..., None], axis=-2).astype(h.dtype)
    out = jnp.zeros_like(h)
    for e in range(N_EXPERTS):
        a = jax.nn.silu(h @ p['w_exp_gate'][e]) * (h @ p['w_exp_up'][e])
        out = out + combine[..., e:e + 1] * (a @ p['w_exp_down'][e])
    return out


def trunk_layer(x, c_vec, p, mixer):
    mods = jnp.split(jax.nn.silu(c_vec) @ p['w_ada'] + p['b_ada'], 6, axis=-1)
    sh1, sc1, gt1, sh2, sc2, gt2 = [m[:, None, :] for m in mods]
    h = rms_norm(x, p['g_norm_mix']) * (1 + sc1) + sh1
    y, ctx_tensors = mixer(h, p)
    x = x + gt1 * y
    h = rms_norm(x, p['g_norm_ffn']) * (1 + sc2) + sh2
    x = x + gt2 * moe(h, p)
    return x, ctx_tensors


def setup_inputs(seed: int = 0) -> dict:
    key = jax.random.key(seed)
    ks = iter(jax.random.split(key, 40))

    def nrm(shape, scale):
        return scale * jax.random.normal(next(ks), shape, jnp.float32)

    def gain(shape):
        return 1.0 + nrm(shape, 0.05)

    L = DEPTH
    return {
        'x_prompt': nrm((BATCH, SEQ, D_MODEL), 1.0),
        'x_sample': nrm((DEC_BATCH, DEC_SEQ, D_MODEL), 1.0),
        'cache_win_k': nrm((DEC_BATCH, L, PAST_LEN, WIN_KV_HEADS, HEAD_DIM), 1.0),
        'cache_win_v': nrm((DEC_BATCH, L, PAST_LEN, WIN_KV_HEADS, HEAD_DIM), 1.0),
        'cache_nbr_k': nrm((DEC_BATCH, L, PAST_LEN, NBR_HEADS, HEAD_DIM), 1.0),
        'cache_nbr_v': nrm((DEC_BATCH, L, PAST_LEN, NBR_HEADS, HEAD_DIM), 1.0),
        'cache_mla_ckv': nrm((DEC_BATCH, L, PAST_LEN, KV_LORA), 1.0),
        'cache_mla_krope': nrm((DEC_BATCH, L, PAST_LEN, ROPE_DIM), 1.0),
        'c': nrm((DEC_BATCH, D_MODEL), 1.0),
        'c_ctx': nrm((D_MODEL,), 1.0),
        'g_norm_mix': gain((L, D_MODEL)),
        'g_norm_ffn': gain((L, D_MODEL)),
        'w_ada': nrm((L, D_MODEL, 6 * D_MODEL), 0.5 * D_MODEL ** -0.5),
        'b_ada': nrm((L, 6 * D_MODEL), 0.02),
        'w_in': nrm((L, D_MODEL, IN_COLS), D_MODEL ** -0.5),
        'g_qk_win': gain((L, 2, HEAD_DIM)),
        'win_sink': nrm((L, WIN_HEADS), 1.0),
        'g_qk_nbr': gain((L, 2, HEAD_DIM)),
        'nbr_rel_bias': nrm((L, NBR_HEADS, 2 * NBR_ROWS - 1, 2 * NBR_COLS - 1), 0.1),
        'g_q_lora': gain((L, Q_LORA)),
        'g_kv_lora': gain((L, KV_LORA)),
        'w_uq': nrm((L, Q_LORA, MLA_HEADS * QK_DIM), Q_LORA ** -0.5),
        'w_ukv': nrm((L, KV_LORA, MLA_HEADS * (NOPE_DIM + V_DIM)), KV_LORA ** -0.5),
        'g_qk_mla': gain((L, 2, QK_DIM)),
        'w_o_win': nrm((L, WIN_Q, D_MODEL), WIN_Q ** -0.5),
        'w_o_nbr': nrm((L, NBR_D, D_MODEL), NBR_D ** -0.5),
        'w_o_mla': nrm((L, MLA_OUT, D_MODEL), MLA_OUT ** -0.5),
        'w_out': nrm((L, D_MODEL, D_MODEL), D_MODEL ** -0.5),
        'w_router': nrm((D_MODEL, N_EXPERTS), D_MODEL ** -0.5),
        'b_router': nrm((N_EXPERTS,), 0.01),
        'w_exp_gate': nrm((L, N_EXPERTS, D_MODEL, D_FF), D_MODEL ** -0.5),
        'w_exp_up': nrm((L, N_EXPERTS, D_MODEL, D_FF), D_MODEL ** -0.5),
        'w_exp_down': nrm((L, N_EXPERTS, D_FF, D_MODEL), D_FF ** -0.5),
    }


def reference(x_prompt, x_sample, cache_win_k, cache_win_v, cache_nbr_k, cache_nbr_v, cache_mla_ckv,
              cache_mla_krope, c, c_ctx, g_norm_mix, g_norm_ffn, w_ada, b_ada, w_in, g_qk_win, win_sink,
              g_qk_nbr, nbr_rel_bias, g_q_lora, g_kv_lora, w_uq, w_ukv, g_qk_mla, w_o_win, w_o_nbr, w_o_mla,
              w_out, w_router, b_router, w_exp_gate, w_exp_up, w_exp_down):
    def layer_params(i):
        return dict(g_norm_mix=g_norm_mix[i], g_norm_ffn=g_norm_ffn[i], w_ada=w_ada[i], b_ada=b_ada[i],
                    w_in=w_in[i], g_qk_win=g_qk_win[i], win_sink=win_sink[i], g_qk_nbr=g_qk_nbr[i],
                    nbr_rel_bias=nbr_rel_bias[i], g_q_lora=g_q_lora[i], g_kv_lora=g_kv_lora[i],
                    w_uq=w_uq[i], w_ukv=w_ukv[i], g_qk_mla=g_qk_mla[i], w_o_win=w_o_win[i],
                    w_o_nbr=w_o_nbr[i], w_o_mla=w_o_mla[i], w_out=w_out[i], w_router=w_router,
                    b_router=b_router, w_exp_gate=w_exp_gate[i], w_exp_up=w_exp_up[i],
                    w_exp_down=w_exp_down[i])

    y_prompt = x_prompt
    ctx_states = []
    for i in range(DEPTH):
        y_prompt, st = trunk_layer(y_prompt, c_ctx[None, :], layer_params(i), mix_context)
        ctx_states.append(st)

    y_sample = x_sample
    for i in range(DEPTH):
        caches = (cache_win_k[:, i], cache_win_v[:, i], cache_nbr_k[:, i], cache_nbr_v[:, i],
                  cache_mla_ckv[:, i], cache_mla_krope[:, i])
        y_sample, _ = trunk_layer(y_sample, c, layer_params(i), functools.partial(mix_latent, ctx=caches))

    state_win_k = jnp.stack([s[0] for s in ctx_states], axis=1)
    state_win_v = jnp.stack([s[1] for s in ctx_states], axis=1)
    state_nbr_k = jnp.stack([s[2] for s in ctx_states], axis=1)
    state_nbr_v = jnp.stack([s[3] for s in ctx_states], axis=1)
    state_mla_ckv = jnp.stack([s[4] for s in ctx_states], axis=1)
    state_mla_krope = jnp.stack([s[5] for s in ctx_states], axis=1)
    return (y_prompt, y_sample, state_win_k, state_win_v, state_nbr_k, state_nbr_v, state_mla_ckv, state_mla_krope)
```

```python
import functools

import numpy as np
import jax
import jax.numpy as jnp
from jax import lax
from jax.experimental import pallas as pl
from jax.experimental.pallas import tpu as pltpu

D_MODEL = 1024
GRID_W = 64
HEAD_DIM = 64
WIN_HEADS = 8
WIN_KV_HEADS = 2
WINDOW = 128
NBR_HEADS = 8
NBR_ROWS = 8
NBR_COLS = 16
MLA_HEADS = 8
Q_LORA = 256
KV_LORA = 128
NOPE_DIM = 64
ROPE_DIM = 32
V_DIM = 64
QK_DIM = NOPE_DIM + ROPE_DIM
N_EXPERTS = 16
N_GROUPS = 4
EXPERTS_PER_GROUP = 4
D_FF = 512
ROPE_BASE = 10000.0
EPS = 1e-6

LANES = 128
LOG2E = 1.4426950408889634
NEG = -1e30
VMEM_LIMIT = 56 * 1024 * 1024

F32 = jnp.float32
BF16 = jnp.bfloat16

C_QA, C_KA, C_VA, C_QB, C_KB, C_VB, C_CQ, C_CKV, C_KR, C_END = (
    0, 512, 640, 768, 1280, 1792, 2304, 2560, 2688, 2944)
WIN_Q_ORDER = (0, 4, 1, 5, 2, 6, 3, 7)

MOE_TILE = 1024
MOE_CHUNK = 256


def _cparams(sem):
    return pltpu.CompilerParams(dimension_semantics=sem, vmem_limit_bytes=VMEM_LIMIT)


def _dot(a, b):
    return jnp.dot(a, b, preferred_element_type=F32)


def _dot_nt(a, b):
    return lax.dot_general(a, b, (((1,), (1,)), ((), ())), preferred_element_type=F32)


def _split(x):
    hi = x.astype(BF16)
    lo = (x - hi.astype(F32)).astype(BF16)
    return hi, lo


def _gsum(x2, bmat):
    hi, lo = _split(x2)
    return _dot(hi, bmat) + _dot(lo, bmat)


def _tile_lanes(t, width):
    reps = width // t.shape[-1]
    return t if reps == 1 else jnp.concatenate([t] * reps, axis=-1)


def _rotate(x, cos, sin_a, sin_b, half):
    w = x.shape[-1]
    up = pltpu.roll(x, w - half, 1)
    dn = pltpu.roll(x, half, 1)
    return (x * _tile_lanes(cos, w) + up * _tile_lanes(sin_a, w) + dn * _tile_lanes(sin_b, w))


def _norm_mod(x, gain, scale, shift):
    ms = jnp.mean(x * x, axis=-1, keepdims=True)
    return (x * lax.rsqrt(ms + EPS) * gain) * (1.0 + scale) + shift


def _ada_body(c_ref, w_ref, b_ref, o_ref):
    c = c_ref[...]
    a = c * (1.0 / (1.0 + jnp.exp(-c)))
    a_hi, a_lo = _split(a)
    w_hi, w_lo = _split(w_ref[0])
    o_ref[0] = _dot(a_hi, w_hi) + _dot(a_hi, w_lo) + _dot(a_lo, w_hi) + b_ref[0]


def _ada(c_all, w_ada, b_ada):
    depth = w_ada.shape[0]
    rows = c_all.shape[0]
    tn = 1536
    return pl.pallas_call(
        _ada_body,
        grid=(depth, 6 * D_MODEL // tn),
        in_specs=[pl.BlockSpec((rows, D_MODEL), lambda l, j: (0, 0)),
                  pl.BlockSpec((1, D_MODEL, tn), lambda l, j: (l, 0, j)),
                  pl.BlockSpec((1, 1, tn), lambda l, j: (l, 0, j))],
        out_specs=pl.BlockSpec((1, rows, tn), lambda l, j: (l, 0, j)),
        out_shape=jax.ShapeDtypeStruct((depth, rows, 6 * D_MODEL), F32),
        compiler_params=_cparams(("parallel", "parallel")),
        name="ada",
    )(c_all, w_ada, b_ada.reshape(depth, 1, 6 * D_MODEL))


def _mla_key_tail(ckvn_b, kr_t, g, wukv_ref, bnn, bnr, brr, rope_tabs, kc_ref, vc_ref):
    kv = _dot(ckvn_b, wukv_ref[...])
    kn = kv[:, 0:512]
    vc_ref[...] = kv[:, 512:1024].astype(BF16)
    kn2 = kn * kn
    kr2 = kr_t * kr_t
    kr_sum32 = _gsum(kr2, brr)
    ssn = (_gsum(kn2, bnn) + jnp.concatenate([kr_sum32, kr_sum32], axis=-1)) * (1.0 / QK_DIM)
    ssr = (_gsum(kn2, bnr) + kr_sum32) * (1.0 / QK_DIM)
    kn = kn * lax.rsqrt(ssn + EPS) * g[5:6, 0:512]
    kr = kr_t * lax.rsqrt(ssr + EPS) * g[5:6, 512:768]
    if rope_tabs is not None:
        kr = _rotate(kr, *rope_tabs, 8)
    for p in range(4):
        kc_ref[:, 256 * p:256 * p + 128] = kn[:, 128 * p:128 * p + 128].astype(BF16)
        q4 = 128 * (p // 2)
        kc_ref[:, 256 * p + 128:256 * p + 256] = kr[:, q4:q4 + 128].astype(BF16)


def _inproj_body(rope, states, *refs):
    (x_ref, mod_ref, g_ref, w_ref, wuq_ref, wukv_ref, b64_ref, bnn_ref, brn_ref, bnr_ref,
     brr_ref) = refs[:11]
    refs = refs[11:]
    if rope:
        tabs_w = tuple(r[...] for r in refs[0:3])
        tabs_m = tuple(r[...] for r in refs[3:6])
        refs = refs[6:]
    else:
        tabs_w = tabs_m = None
    qa_ref, ka_ref, va_ref, qb_ref, kb_ref, vb_ref, qc_ref, kc_ref, vc_ref = refs[:9]
    st = refs[9:]

    g = g_ref[...]
    mod = mod_ref[0]
    hb = _norm_mod(x_ref[...], g[0:1], mod[1:2], mod[0:1]).astype(BF16)

    def proj(a, b):
        return _dot(hb, w_ref[:, a:b])

    b64 = b64_ref[...]

    def head_norm(z, bmat, gain):
        return z * lax.rsqrt(_gsum(z * z, bmat) + EPS) * gain

    qa = head_norm(proj(C_QA, C_KA), b64, g[1:2, 0:512])
    ka = head_norm(proj(C_KA, C_VA), b64[0:128, 0:128], g[1:2, 512:640])
    va = proj(C_VA, C_QB)
    if states:
        st[0][...] = ka
        st[1][...] = va
    if rope:
        qa = _rotate(qa, *tabs_w, 16)
        ka = _rotate(ka, *tabs_w, 16)
    qa_ref[...] = qa.astype(BF16)
    ka_ref[...] = ka.astype(BF16)
    va_ref[...] = va.astype(BF16)

    qb = head_norm(proj(C_QB, C_KB), b64, g[2:3, 0:512])
    kb = head_norm(proj(C_KB, C_VB), b64, g[2:3, 512:1024])
    vb = proj(C_VB, C_CQ)
    if states:
        st[2][...] = kb
        st[3][...] = vb
    qb_ref[...] = qb.astype(BF16)
    kb_ref[...] = kb.astype(BF16)
    vb_ref[...] = vb.astype(BF16)

    cq = proj(C_CQ, C_CKV)
    cqn = cq * lax.rsqrt(jnp.mean(cq * cq, axis=-1, keepdims=True) + EPS) * g[3:4, 0:256]
    qq = _dot(cqn.astype(BF16), wuq_ref[...])
    qn, qr = qq[:, 0:512], qq[:, 512:768]
    qn2, qr2 = qn * qn, qr * qr
    bnn, brn, bnr, brr = bnn_ref[...], brn_ref[...], bnr_ref[...], brr_ref[...]
    ssn = (_gsum(qn2, bnn) + _gsum(qr2, brn)) * (1.0 / QK_DIM)
    ssr = (_gsum(qn2, bnr) + _gsum(qr2, brr)) * (1.0 / QK_DIM)
    qn = qn * lax.rsqrt(ssn + EPS) * g[4:5, 0:512]
    qr = qr * lax.rsqrt(ssr + EPS) * g[4:5, 512:768]
    if rope:
        qr = _rotate(qr, *tabs_m, 8)
    for p in range(4):
        qc_ref[:, 256 * p:256 * p + 128] = qn[:, 128 * p:128 * p + 128].astype(BF16)
        q4 = 128 * (p // 2)
        qc_ref[:, 256 * p + 128:256 * p + 256] = qr[:, q4:q4 + 128].astype(BF16)

    ckv = proj(C_CKV, C_KR)
    ckvn = ckv * lax.rsqrt(jnp.mean(ckv * ckv, axis=-1, keepdims=True) + EPS) * g[3:4, 256:384]
    kr_t = proj(C_KR, C_END)
    if states:
        st[4][...] = ckvn
        st[5][...] = kr_t[:, 0:ROPE_DIM]
    _mla_key_tail(ckvn.astype(BF16), kr_t, g, wukv_ref, bnn, bnr, brr, tabs_m, kc_ref, vc_ref)


def _const_spec(shape):
    nd = len(shape)
    return pl.BlockSpec(shape, lambda i, _nd=nd: (0,) * _nd)


def _inproj(x, mods, rows_per_mod, gains, w_a, w_uq, w_ukv, mats, rope_tabs, seq_len, states):
    n = x.shape[0]
    tm = min(512, n)
    rope = rope_tabs is not None
    row = lambda w: pl.BlockSpec((tm, w), lambda i: (i, 0))
    in_specs = [row(D_MODEL),
                pl.BlockSpec((1, 8, D_MODEL), lambda i: ((i * tm) // rows_per_mod, 0, 0)),
                _const_spec(gains.shape), _const_spec(w_a.shape), _const_spec(w_uq.shape),
                _const_spec(w_ukv.shape)] + [_const_spec(m.shape) for m in mats]
    args = [x, mods, gains, w_a, w_uq, w_ukv, *mats]
    if rope:
        tiles_per_seq = seq_len // tm
        in_specs += [pl.BlockSpec((tm, LANES), lambda i: (i % tiles_per_seq, 0))] * 6
        args += list(rope_tabs)
    widths = [512, 128, 128, 512, 512, 512, 1024, 1024, 512]
    out_shape = [jax.ShapeDtypeStruct((n, w), BF16) for w in widths]
    out_specs = [row(w) for w in widths]
    if states:
        swidths = [128, 128, 512, 512, KV_LORA, ROPE_DIM]
        out_shape += [jax.ShapeDtypeStruct((n, w), F32) for w in swidths]
        out_specs += [row(w) for w in swidths]
    return pl.pallas_call(
        functools.partial(_inproj_body, rope, states),
        grid=(n // tm,), in_specs=in_specs, out_specs=out_specs, out_shape=out_shape,
        compiler_params=_cparams(("parallel",)),
        name="inproj_lat" if rope else "inproj_ctx",
    )(*args)


def _mla_cache_body(ckv_ref, kr_ref, g_ref, wukv_ref, bnn_ref, bnr_ref, brr_ref, kc_ref, vc_ref):
    _mla_key_tail(ckv_ref[...].astype(BF16), kr_ref[...], g_ref[...], wukv_ref, bnn_ref[...],
                  bnr_ref[...], brr_ref[...], None, kc_ref, vc_ref)


def _mla_cache_keys(ckv, kr_t, gains, w_ukv, bnn, bnr, brr):
    n = ckv.shape[0]
    tm = min(512, n)
    row = lambda w: pl.BlockSpec((tm, w), lambda i: (i, 0))
    return pl.pallas_call(
        _mla_cache_body,
        grid=(n // tm,),
        in_specs=[row(KV_LORA), row(256), _const_spec(gains.shape), _const_spec(w_ukv.shape),
                  _const_spec(bnn.shape), _const_spec(bnr.shape), _const_spec(brr.shape)],
        out_specs=[row(1024), row(512)],
        out_shape=[jax.ShapeDtypeStruct((n, 1024), BF16), jax.ShapeDtypeStruct((n, 512), BF16)],
        compiler_params=_cparams(("parallel",)),
        name="mla_cache_keys",
    )(ckv, kr_t, gains, w_ukv, bnn, bnr, brr)


def _lane_mask(width, ranges):
    lane = lax.broadcasted_iota(jnp.int32, (1, width), 1)
    m = None
    for lo, hi in ranges:
        c = (lane >= lo) & (lane < hi)
        m = c if m is None else (m | c)
    return jnp.where(m, 1.0, 0.0).astype(BF16)


def _stack_heads(q, mask0, mask1):
    return jnp.concatenate([q * mask0, q * mask1], axis=0)


def _softmax_pv(scores, values, sink=None):
    m = None
    for s in scores:
        ms = jnp.max(s, axis=-1, keepdims=True)
        m = ms if m is None else jnp.maximum(m, ms)
    if sink is not None:
        m = jnp.maximum(m, sink)
    l = None
    o = None
    for s, v in zip(scores, values):
        p = jnp.exp2(s - m)
        ls = jnp.sum(p, axis=-1, keepdims=True)
        os_ = _dot(p.astype(BF16), v)
        l = ls if l is None else l + ls
        o = os_ if o is None else o + os_
    if sink is not None:
        l = l + jnp.exp2(sink - m)
    return o / l


def _merge_heads(o, tq):
    lane = lax.broadcasted_iota(jnp.int32, (tq, LANES), 1)
    return jnp.where(lane < HEAD_DIM, o[0:tq], o[tq:2 * tq])


def _mla_masks(p_mod2):
    lane = lax.broadcasted_iota(jnp.int32, (1, 256), 1)
    r0 = 128 + 32 * (2 * p_mod2)
    m0 = (lane < 64) | ((lane >= r0) & (lane < r0 + 32))
    m1 = ((lane >= 64) & (lane < 128)) | ((lane >= r0 + 32) & (lane < r0 + 64))
    return (jnp.where(m0, 1.0, 0.0).astype(BF16), jnp.where(m1, 1.0, 0.0).astype(BF16))


def _sink_col(sink_ref, layer, h0, h1, tq):
    row = lax.broadcasted_iota(jnp.int32, (2 * tq, 1), 0)
    return jnp.where(row < tq, sink_ref[layer, h0], sink_ref[layer, h1])


def _ctx_attn_body(layer, sink_ref, qa_ref, ka_ref, va_ref, qb_ref, kb_ref, vb_ref, qc_ref, kc_ref,
                   vc_ref, oa_ref, ob_ref, oc_ref):
    tq = qa_ref.shape[0]
    lo = _lane_mask(LANES, [(0, 64)])
    hi = _lane_mask(LANES, [(64, 128)])
    ka, va = ka_ref[...], va_ref[...]
    for j in range(4):
        sl = slice(128 * j, 128 * j + 128)
        qs = _stack_heads(qa_ref[:, sl], lo, hi)
        sink = _sink_col(sink_ref, layer, j, 4 + j, tq)
        o = _softmax_pv([_dot_nt(qs, ka)], [va], sink)
        oa_ref[:, sl] = _merge_heads(o, tq).astype(BF16)

        qs = _stack_heads(qb_ref[:, sl], lo, hi)
        o = _softmax_pv([_dot_nt(qs, kb_ref[:, sl])], [vb_ref[:, sl]])
        ob_ref[:, sl] = _merge_heads(o, tq).astype(BF16)

        m0, m1 = _mla_masks(j % 2)
        s2 = slice(256 * j, 256 * j + 256)
        qs = _stack_heads(qc_ref[:, s2], m0, m1)
        o = _softmax_pv([_dot_nt(qs, kc_ref[:, s2])], [vc_ref[:, sl]])
        oc_ref[:, sl] = _merge_heads(o, tq).astype(BF16)


def _ctx_attn(layer, sink, seq, qa, ka, va, qb, kb, vb, qc, kc, vc):
    n = qa.shape[0]
    row = lambda w: pl.BlockSpec((seq, w), lambda b: (b, 0))
    ins = [qa, ka, va, qb, kb, vb, qc, kc, vc]
    return pl.pallas_call(
        functools.partial(_ctx_attn_body, layer),
        grid=(n // seq,),
        in_specs=[pl.BlockSpec(memory_space=pltpu.SMEM)] + [row(a.shape[1]) for a in ins],
        out_specs=[row(512)] * 3,
        out_shape=[jax.ShapeDtypeStruct((n, 512), BF16)] * 3,
        compiler_params=_cparams(("parallel",)),
        name="ctx_attn",
    )(sink, *ins)


def _win_body(layer, sink_ref, q_ref, k_ref, v_ref, kc_ref, vc_ref, o_ref):
    tq = q_ref.shape[1]
    seq = k_ref.shape[1]
    kw = 3 * tq
    i = pl.program_id(1)
    kstart = pl.multiple_of(jnp.clip((i - 1) * tq, 0, seq - kw), tq)
    k = k_ref[0, pl.ds(kstart, kw), :]
    v = v_ref[0, pl.ds(kstart, kw), :]
    kc, vc = kc_ref[0, 0], vc_ref[0, 0]
    q_pos = i * tq + lax.broadcasted_iota(jnp.int32, (2 * tq, kw), 0) % tq
    k_pos = kstart + lax.broadcasted_iota(jnp.int32, (2 * tq, kw), 1)
    band = jnp.abs(q_pos - k_pos) <= WINDOW
    lo = _lane_mask(LANES, [(0, 64)])
    hi = _lane_mask(LANES, [(64, 128)])
    for j in range(4):
        sl = slice(128 * j, 128 * j + 128)
        qs = _stack_heads(q_ref[0, :, sl], lo, hi)
        s_band = jnp.where(band, _dot_nt(qs, k), NEG)
        s_ctx = _dot_nt(qs, kc)
        sink = _sink_col(sink_ref, layer, j, 4 + j, tq)
        o = _softmax_pv([s_band, s_ctx], [v, vc], sink)
        o_ref[0, :, sl] = _merge_heads(o, tq).astype(BF16)


def _win_attn(layer, sink, q, k, v, kc, vc):
    b, seq, _ = q.shape
    tq = 128
    past = kc.shape[2]
    return pl.pallas_call(
        functools.partial(_win_body, layer),
        grid=(b, seq // tq),
        in_specs=[pl.BlockSpec(memory_space=pltpu.SMEM),
                  pl.BlockSpec((1, tq, 512), lambda bi, i: (bi, i, 0)),
                  pl.BlockSpec((1, seq, 128), lambda bi, i: (bi, 0, 0)),
                  pl.BlockSpec((1, seq, 128), lambda bi, i: (bi, 0, 0)),
                  pl.BlockSpec((1, 1, past, 128), lambda bi, i: (bi, layer, 0, 0)),
                  pl.BlockSpec((1, 1, past, 128), lambda bi, i: (bi, layer, 0, 0))],
        out_specs=pl.BlockSpec((1, tq, 512), lambda bi, i: (bi, i, 0)),
        out_shape=jax.ShapeDtypeStruct((b, seq, 512), BF16),
        compiler_params=_cparams(("parallel", "arbitrary")),
        name="win_attn",
    )(sink, q, k, v, kc, vc)


NBR_TILE_ROWS = 2
NBR_WIN_ROWS = 10


def _nbr_body(rows, q_ref, k_ref, v_ref, kc_ref, vc_ref, tab_ref, o_ref):
    tq = NBR_TILE_ROWS * GRID_W
    kw = NBR_WIN_ROWS * GRID_W
    i = pl.program_id(1)
    r0 = NBR_TILE_ROWS * i
    ws = jnp.clip(r0 - NBR_ROWS // 2, 0, rows - NBR_WIN_ROWS)
    kstart = pl.multiple_of(ws * GRID_W, LANES)
    q_row = r0 + lax.broadcasted_iota(jnp.int32, (2 * tq, kw), 0) % tq // GRID_W
    k_row = ws + lax.broadcasted_iota(jnp.int32, (2 * tq, kw), 1) // GRID_W
    rs = jnp.clip(q_row - NBR_ROWS // 2, 0, rows - NBR_ROWS)
    valid = (k_row >= rs) & (k_row < rs + NBR_ROWS)
    lo = _lane_mask(LANES, [(0, 64)])
    hi = _lane_mask(LANES, [(64, 128)])
    for j in range(4):
        sl = slice(128 * j, 128 * j + 128)
        qs = _stack_heads(q_ref[0, :, sl], lo, hi)
        k = k_ref[0, pl.ds(kstart, kw), sl]
        v = v_ref[0, pl.ds(kstart, kw), sl]
        pieces = []
        for h in (2 * j, 2 * j + 1):
            for ql in range(NBR_TILE_ROWS):
                d0 = ws - r0 - ql + (NBR_ROWS - 1) + 2
                pieces.append(jnp.concatenate(
                    [tab_ref[h, d0 + 2 * m] for m in range(NBR_WIN_ROWS // 2)], axis=-1))
        bias = jnp.concatenate(pieces, axis=0)
        s_nb = jnp.where(valid, _dot_nt(qs, k) + bias, NEG)
        s_ctx = _dot_nt(qs, kc_ref[0, 0, :, sl])
        o = _softmax_pv([s_nb, s_ctx], [v, vc_ref[0, 0, :, sl]])
        o_ref[0, :, sl] = _merge_heads(o, tq).astype(BF16)


def _nbr_attn(layer, q, k, v, kc, vc, table):
    b, seq, _ = q.shape
    rows = seq // GRID_W
    tq = NBR_TILE_ROWS * GRID_W
    past = kc.shape[2]
    return pl.pallas_call(
        functools.partial(_nbr_body, rows),
        grid=(b, seq // tq),
        in_specs=[pl.BlockSpec((1, tq, 512), lambda bi, i: (bi, i, 0)),
                  pl.BlockSpec((1, seq, 512), lambda bi, i: (bi, 0, 0)),
                  pl.BlockSpec((1, seq, 512), lambda bi, i: (bi, 0, 0)),
                  pl.BlockSpec((1, 1, past, 512), lambda bi, i: (bi, layer, 0, 0)),
                  pl.BlockSpec((1, 1, past, 512), lambda bi, i: (bi, layer, 0, 0)),
                  pl.BlockSpec(table.shape, lambda bi, i: (0, 0, 0, 0))],
        out_specs=pl.BlockSpec((1, tq, 512), lambda bi, i: (bi, i, 0)),
        out_shape=jax.ShapeDtypeStruct((b, seq, 512), BF16),
        compiler_params=_cparams(("parallel", "arbitrary")),
        name="nbr_attn",
    )(q, k, v, kc, vc, table)


def _nbr_bias_table(rel_bias):
    col = np.arange(GRID_W)
    cs = np.clip(col - NBR_COLS // 2, 0, GRID_W - NBR_COLS)
    kc = np.arange(GRID_W)
    ok = (kc[None, :] >= cs[:, None]) & (kc[None, :] < cs[:, None] + NBR_COLS)
    dc = np.clip(kc[None, :] - col[:, None] + (NBR_COLS - 1), 0, 2 * NBR_COLS - 2)
    t = rel_bias.astype(F32)[:, :, dc] * LOG2E
    t = jnp.where(jnp.asarray(ok)[None, None], t, NEG)
    t = jnp.pad(t, ((0, 0), (2, 2), (0, 0), (0, 0)))
    return jnp.concatenate([t[:, :-1], t[:, 1:]], axis=-1)


def _mla_body(nk_lat, q_ref, kl_ref, kc_ref, vl_ref, vc_ref, o_ref, m_sc, l_sc, acc_sc):
    tq = q_ref.shape[1]
    p = pl.program_id(1)
    ki = pl.program_id(3)

    @pl.when(ki == 0)
    def _():
        m_sc[...] = jnp.full(m_sc.shape, NEG, F32)
        l_sc[...] = jnp.zeros(l_sc.shape, F32)
        acc_sc[...] = jnp.zeros(acc_sc.shape, F32)

    m0, m1 = _mla_masks(p % 2)
    qs = _stack_heads(q_ref[0], m0, m1)
    is_lat = ki < nk_lat
    k = jnp.where(is_lat, kl_ref[0], kc_ref[0])
    v = jnp.where(is_lat, vl_ref[0], vc_ref[0])
    s = _dot_nt(qs, k)
    m_prev = m_sc[...]
    m_new = jnp.maximum(m_prev, jnp.max(s, axis=-1, keepdims=True))
    alpha = jnp.exp2(m_prev - m_new)
    pr = jnp.exp2(s - m_new)
    l_sc[...] = alpha * l_sc[...] + jnp.sum(pr, axis=-1, keepdims=True)
    acc_sc[...] = alpha * acc_sc[...] + _dot(pr.astype(BF16), v)
    m_sc[...] = m_new

    @pl.when(ki == pl.num_programs(3) - 1)
    def _():
        o_ref[0] = _merge_heads(acc_sc[...] / l_sc[...], tq).astype(BF16)


def _mla_attn(q, kl, vl, kc, vc):
    b, seq, _ = q.shape
    past = kc.shape[1]
    tq = min(512, seq)
    tk = past
    nk_lat = seq // tk
    nk = nk_lat + 1
    lat = lambda ki: jnp.minimum(ki, nk_lat - 1)
    return pl.pallas_call(
        functools.partial(_mla_body, nk_lat),
        grid=(b, 4, seq // tq, nk),
        in_specs=[pl.BlockSpec((1, tq, 256), lambda bi, p, qi, ki: (bi, qi, p)),
                  pl.BlockSpec((1, tk, 256), lambda bi, p, qi, ki: (bi, lat(ki), p)),
                  pl.BlockSpec((1, tk, 256), lambda bi, p, qi, ki: (bi, 0, p)),
                  pl.BlockSpec((1, tk, 128), lambda bi, p, qi, ki: (bi, lat(ki), p)),
                  pl.BlockSpec((1, tk, 128), lambda bi, p, qi, ki: (bi, 0, p))],
        out_specs=pl.BlockSpec((1, tq, 128), lambda bi, p, qi, ki: (bi, qi, p)),
        out_shape=jax.ShapeDtypeStruct((b, seq, 512), BF16),
        scratch_shapes=[pltpu.VMEM((2 * tq, 1), F32), pltpu.VMEM((2 * tq, 1), F32),
                        pltpu.VMEM((2 * tq, LANES), F32)],
        compiler_params=_cparams(("parallel", "parallel", "parallel", "arbitrary")),
        name="mla_attn",
    )(q, kl, kc, vl, vc)


def _merge_body(x_ref, oa_ref, ob_ref, oc_ref, mod_ref, g_ref, wg_ref, woa_ref, wob_ref, woc_ref,
                wout_ref, wr_ref, br_ref, x1_ref, route_ref):
    x = x_ref[...]
    g = g_ref[...]
    mod = mod_ref[0]
    hb = _norm_mod(x, g[0:1], mod[1:2], mod[0:1]).astype(BF16)
    m = None
    for br, (o_ref, wo_ref) in enumerate(((oa_ref, woa_ref), (ob_ref, wob_ref), (oc_ref, woc_ref))):
        z = _dot(hb, wg_ref[:, D_MODEL * br:D_MODEL * (br + 1)])
        gate = 1.0 / (1.0 + jnp.exp(-z))
        t = gate * _dot(o_ref[...], wo_ref[...])
        m = t if m is None else m + t
    y = _dot(m.astype(BF16), wout_ref[...])
    x1 = x + mod[2:3] * y
    x1_ref[...] = x1

    h2 = _norm_mod(x1, g[6:7], mod[4:5], mod[3:4])
    h_hi, h_lo = _split(h2)
    wr = wr_ref[...]
    w_hi, w_lo = _split(wr)
    logits = _dot_nt(w_hi, h_hi) + _dot_nt(w_hi, h_lo) + _dot_nt(w_lo, h_hi)
    score = 1.0 / (1.0 + jnp.exp(-logits))
    sel = score + br_ref[...]
    sel_r = [sel[e:e + 1] for e in range(N_EXPERTS)]
    sc_r = [score[e:e + 1] for e in range(N_EXPERTS)]
    picked = []
    for e in range(N_EXPERTS):
        grp, a = divmod(e, EXPERTS_PER_GROUP)
        rank = None
        for bb in range(EXPERTS_PER_GROUP):
            if bb == a:
                continue
            o = sel_r[grp * EXPERTS_PER_GROUP + bb]
            beats = (o >= sel_r[e]) if bb < a else (o > sel_r[e])
            r = jnp.where(beats, 1.0, 0.0)
            rank = r if rank is None else rank + r
        picked.append(rank < 2.0)
    gscore = []
    for grp in range(N_GROUPS):
        tot = None
        for a in range(EXPERTS_PER_GROUP):
            e = grp * EXPERTS_PER_GROUP + a
            t = jnp.where(picked[e], sel_r[e], 0.0)
            tot = t if tot is None else tot + t
        gscore.append(tot)
    best = jnp.zeros_like(gscore[0])
    best_v = gscore[0]
    for grp in range(1, N_GROUPS):
        upd = gscore[grp] > best_v
        best = jnp.where(upd, float(grp), best)
        best_v = jnp.where(upd, gscore[grp], best_v)
    cw = []
    for a in range(EXPERTS_PER_GROUP):
        tot = None
        for grp in range(N_GROUPS):
            e = grp * EXPERTS_PER_GROUP + a
            t = jnp.where((best == float(grp)) & picked[e], sc_r[e], 0.0)
            tot = t if tot is None else tot + t
        cw.append(tot)
    den = cw[0] + cw[1] + cw[2] + cw[3]
    for a in range(EXPERTS_PER_GROUP):
        route_ref[a:a + 1, :] = cw[a] / den
    route_ref[4:5, :] = best
    route_ref[5:8, :] = jnp.zeros((3, best.shape[1]), F32)


def _merge(x, oa, ob, oc, mods, rows_per_mod, gains, w_gate, wo_a, wo_b, wo_c, w_out, w_r_t, b_r):
    n = x.shape[0]
    tm = min(512, n)
    row = lambda w: pl.BlockSpec((tm, w), lambda i: (i, 0))
    consts = [gains, w_gate, wo_a, wo_b, wo_c, w_out, w_r_t, b_r]
    return pl.pallas_call(
        _merge_body,
        grid=(n // tm,),
        in_specs=[row(D_MODEL), row(512), row(512), row(512),
                  pl.BlockSpec((1, 8, D_MODEL), lambda i: ((i * tm) // rows_per_mod, 0, 0))]
                 + [_const_spec(c.shape) for c in consts],
        out_specs=[row(D_MODEL), pl.BlockSpec((8, tm), lambda i: (0, i))],
        out_shape=[jax.ShapeDtypeStruct((n, D_MODEL), F32), jax.ShapeDtypeStruct((8, n), F32)],
        compiler_params=_cparams(("parallel",)),
        name="merge",
    )(x, oa, ob, oc, mods, *consts)


def _moe_body(max_chunks, cg_ref, nch_ref, x1_ref, side_ref, idxr_ref, idxc_ref, mod_ref, g_ref,
              wg_ref, wu_ref, wd_ref, o_ref, h2_sc):
    t = pl.program_id(0)
    c = pl.program_id(1)
    tile = x1_ref.shape[0]
    chunk = idxr_ref.shape[2]

    @pl.when(c == 0)
    def _():
        mod = mod_ref[0]
        h2_sc[...] = _norm_mod(x1_ref[...], g_ref[6:7, :], mod[4:5], mod[3:4]).astype(BF16)
        o_ref[...] = jnp.zeros(o_ref.shape, F32)

    @pl.when(c < nch_ref[t])
    def _():
        idx_col = _tile_lanes(idxc_ref[0], tile)
        sel = jnp.where(idx_col == lax.broadcasted_iota(jnp.int32, (chunk, tile), 1), 1.0, 0.0)
        sel = sel.astype(BF16)
        idx_row = idxr_ref[0]
        sel_t = jnp.where(idx_row == lax.broadcasted_iota(jnp.int32, (tile, chunk), 0), 1.0, 0.0)
        sel_t = sel_t.astype(BF16)
        xg = _dot(sel, h2_sc[...]).astype(BF16)
        cwg = _dot(sel, side_ref[...])
        y = None
        for a in range(EXPERTS_PER_GROUP):
            w = cwg[:, a:a + 1] + cwg[:, 4 + a:5 + a]
            zg = _dot(xg, wg_ref[0, a])
            act = zg * (1.0 / (1.0 + jnp.exp(-zg))) * _dot(xg, wu_ref[0, a])
            ya = _dot((act * w).astype(BF16), wd_ref[0, a])
            y = ya if y is None else y + ya
        o_ref[...] += _dot(sel_t, y.astype(BF16))

    @pl.when(c == max_chunks - 1)
    def _():
        o_ref[...] = x1_ref[...] + mod_ref[0][5:6] * o_ref[...]


def _moe(x1, route, mods, rows_per_mod, gains, wg, wu, wd):
    n = x1.shape[0]
    tile = min(MOE_TILE, n)
    chunk = min(MOE_CHUNK, tile)
    nt = n // tile
    max_chunks = (tile + N_GROUPS * (chunk - 1)) // chunk

    cw = route[0:4].T
    gid = route[4].astype(jnp.int32).reshape(nt, tile)
    hi = cw.astype(BF16)
    lo = (cw - hi.astype(F32)).astype(BF16)
    side = jnp.pad(jnp.concatenate([hi, lo], axis=1), ((0, 0), (0, LANES - 8)))
    key = jnp.sort(gid * tile + jnp.arange(tile, dtype=jnp.int32)[None], axis=-1)
    stok = key % tile
    counts = jnp.sum(gid[:, :, None] == jnp.arange(N_GROUPS)[None, None], axis=1).astype(jnp.int32)
    starts = jnp.cumsum(counts, axis=-1) - counts
    nch_g = (counts + chunk - 1) // chunk
    cend = jnp.cumsum(nch_g, axis=-1)
    total = cend[:, -1]
    cidx = jnp.arange(max_chunks, dtype=jnp.int32)
    g_c = jnp.sum(cidx[None, :, None] >= cend[:, None, :], axis=-1).astype(jnp.int32)
    last = jnp.take_along_axis(g_c, jnp.maximum(total - 1, 0)[:, None], axis=1)
    valid_c = cidx[None] < total[:, None]
    g_c = jnp.where(valid_c, jnp.minimum(g_c, N_GROUPS - 1), last)
    k_in_g = cidx[None] - jnp.take_along_axis(cend - nch_g, g_c, axis=1)
    base = jnp.take_along_axis(starts, g_c, axis=1) + k_in_g * chunk
    cnt = jnp.take_along_axis(counts, g_c, axis=1)
    off = jnp.arange(chunk, dtype=jnp.int32)
    ok = valid_c[:, :, None] & ((k_in_g * chunk)[:, :, None] + off[None, None] < cnt[:, :, None])
    pos = jnp.clip(base[:, :, None] + off[None, None], 0, tile - 1).reshape(nt, max_chunks * chunk)
    idx = jnp.where(ok, jnp.take_along_axis(stok, pos, axis=1).reshape(nt, max_chunks, chunk), -1)
    idx_row = idx.reshape(nt * max_chunks, 1, chunk)
    idx_col = jnp.broadcast_to(idx.reshape(nt * max_chunks, chunk, 1), (nt * max_chunks, chunk, LANES))

    wspec = lambda shape: pl.BlockSpec(
        shape, lambda t, c, cg, nc: (cg[t * max_chunks + c], 0, 0, 0))
    grid_spec = pltpu.PrefetchScalarGridSpec(
        num_scalar_prefetch=2,
        grid=(nt, max_chunks),
        in_specs=[pl.BlockSpec((tile, D_MODEL), lambda t, c, cg, nc: (t, 0)),
                  pl.BlockSpec((tile, LANES), lambda t, c, cg, nc: (t, 0)),
                  pl.BlockSpec((1, 1, chunk), lambda t, c, cg, nc: (t * max_chunks + c, 0, 0)),
                  pl.BlockSpec((1, chunk, LANES), lambda t, c, cg, nc: (t * max_chunks + c, 0, 0)),
                  pl.BlockSpec((1, 8, D_MODEL),
                               lambda t, c, cg, nc: ((t * tile) // rows_per_mod, 0, 0)),
                  pl.BlockSpec(gains.shape, lambda t, c, cg, nc: (0, 0)),
                  wspec((1, EXPERTS_PER_GROUP, D_MODEL, D_FF)),
                  wspec((1, EXPERTS_PER_GROUP, D_MODEL, D_FF)),
                  wspec((1, EXPERTS_PER_GROUP, D_FF, D_MODEL))],
        out_specs=pl.BlockSpec((tile, D_MODEL), lambda t, c, cg, nc: (t, 0)),
        scratch_shapes=[pltpu.VMEM((tile, D_MODEL), BF16)])
    return pl.pallas_call(
        functools.partial(_moe_body, max_chunks),
        grid_spec=grid_spec,
        out_shape=jax.ShapeDtypeStruct((n, D_MODEL), F32),
        compiler_params=_cparams(("parallel", "arbitrary")),
        name="moe",
    )(g_c.reshape(-1), total.astype(jnp.int32), x1, side, idx_row, idx_col, mods, gains, wg, wu, wd)


def _block_ones(n_in, g_in, n_out, g_out, value=1.0):
    r = np.arange(n_in)[:, None] // g_in
    c = np.arange(n_out)[None, :] // g_out
    return jnp.asarray(np.where(r == c, value, 0.0), dtype=BF16)


def _rope_tables(seq, head_w):
    pos = np.arange(seq)
    rows, cols = pos // GRID_W, pos % GRID_W
    a = head_w // 2
    half = a // 2
    freqs = (ROPE_BASE ** (-np.arange(half, dtype=np.float32) / half)).astype(np.float32)
    lane = np.arange(LANES) % head_w
    within = lane % a
    first = within < half
    p = np.where((lane // a == 0)[None, :], rows[:, None], cols[:, None]).astype(np.float32)
    ang = (p * freqs[within % half][None, :]).astype(np.float32)
    cos, sin = np.cos(ang), np.sin(ang)
    return (jnp.asarray(cos, F32), jnp.asarray(np.where(first[None], -sin, 0.0), F32),
            jnp.asarray(np.where(first[None], 0.0, sin), F32))


def _tile_to(v, width):
    return jnp.tile(v, width // v.shape[0])


def _layer_params(i, p):
    w_in = p["w_in"][i]
    sp = np.cumsum((512, 128, 128, 512, 512, 512, Q_LORA, KV_LORA, ROPE_DIM))
    qa, ka, va, qb, kb, vb, cq, ckv, kr, gates = jnp.split(w_in, [int(s) for s in sp], axis=1)
    qa = qa.reshape(D_MODEL, WIN_HEADS, HEAD_DIM)[:, WIN_Q_ORDER, :].reshape(D_MODEL, 512)
    w_a = jnp.concatenate([qa, ka, va, qb, kb, vb, cq, ckv, jnp.tile(kr, (1, MLA_HEADS))],
                          axis=1).astype(BF16)
    w_uq = p["w_uq"][i].reshape(Q_LORA, MLA_HEADS, QK_DIM)
    w_uq = jnp.concatenate([w_uq[:, :, :NOPE_DIM].reshape(Q_LORA, 512),
                            w_uq[:, :, NOPE_DIM:].reshape(Q_LORA, 256)], axis=1).astype(BF16)
    w_ukv = p["w_ukv"][i].reshape(KV_LORA, MLA_HEADS, NOPE_DIM + V_DIM)
    w_ukv = jnp.concatenate([w_ukv[:, :, :NOPE_DIM].reshape(KV_LORA, 512),
                             w_ukv[:, :, NOPE_DIM:].reshape(KV_LORA, 512)], axis=1).astype(BF16)
    z = jnp.zeros((D_MODEL,), F32)
    row = lambda *parts: jnp.concatenate(list(parts) + [z])[:D_MODEL]
    q_scale = HEAD_DIM ** -0.5 * LOG2E
    c_scale = QK_DIM ** -0.5 * LOG2E
    g_mla = p["g_qk_mla"][i]
    gains = jnp.stack([
        p["g_norm_mix"][i],
        row(_tile_to(p["g_qk_win"][i, 0], 512) * q_scale, _tile_to(p["g_qk_win"][i, 1], 128)),
        row(_tile_to(p["g_qk_nbr"][i, 0], 512) * q_scale, _tile_to(p["g_qk_nbr"][i, 1], 512)),
        row(p["g_q_lora"][i], p["g_kv_lora"][i]),
        row(_tile_to(g_mla[0, :NOPE_DIM], 512) * c_scale, _tile_to(g_mla[0, NOPE_DIM:], 256) * c_scale),
        row(_tile_to(g_mla[1, :NOPE_DIM], 512), _tile_to(g_mla[1, NOPE_DIM:], 256)),
        p["g_norm_ffn"][i],
        z]).astype(F32)
    wo_a = p["w_o_win"][i].reshape(WIN_HEADS, HEAD_DIM, D_MODEL)[WIN_Q_ORDER, :, :].reshape(512, D_MODEL)
    grp = lambda w: w.astype(BF16).reshape((N_GROUPS, EXPERTS_PER_GROUP) + w.shape[1:])
    return dict(
        w_a=w_a, w_uq=w_uq, w_ukv=w_ukv, gains=gains, w_gate=gates.astype(BF16),
        wo_a=wo_a.astype(BF16), wo_b=p["w_o_nbr"][i].astype(BF16), wo_c=p["w_o_mla"][i].astype(BF16),
        w_out=p["w_out"][i].astype(BF16),
        wg=grp(p["w_exp_gate"][i]), wu=grp(p["w_exp_up"][i]), wd=grp(p["w_exp_down"][i]),
        nbr_table=_nbr_bias_table(p["nbr_rel_bias"][i]))


def kernel(x_prompt, x_sample, cache_win_k, cache_win_v, cache_nbr_k, cache_nbr_v, cache_mla_ckv, cache_mla_krope, c, c_ctx, g_norm_mix, g_norm_ffn, w_ada, b_ada, w_in, g_qk_win, win_sink, g_qk_nbr, nbr_rel_bias, g_q_lora, g_kv_lora, w_uq, w_ukv, g_qk_mla, w_o_win, w_o_nbr, w_o_mla, w_out, w_router, b_router, w_exp_gate, w_exp_up, w_exp_down):
    p = dict(g_norm_mix=g_norm_mix, g_norm_ffn=g_norm_ffn, w_in=w_in, g_qk_win=g_qk_win,
             g_qk_nbr=g_qk_nbr, nbr_rel_bias=nbr_rel_bias, g_q_lora=g_q_lora, g_kv_lora=g_kv_lora,
             w_uq=w_uq, w_ukv=w_ukv, g_qk_mla=g_qk_mla, w_o_win=w_o_win, w_o_nbr=w_o_nbr,
             w_o_mla=w_o_mla, w_out=w_out, w_exp_gate=w_exp_gate, w_exp_up=w_exp_up,
             w_exp_down=w_exp_down)
    depth = w_in.shape[0]
    batch, seq, _ = x_prompt.shape
    dec_batch, dec_seq, _ = x_sample.shape
    past = cache_win_k.shape[2]

    n_c = 1 + dec_batch
    c_rows = -(-n_c // 8) * 8
    c_all = jnp.concatenate([c_ctx[None], c, jnp.zeros((c_rows - n_c, D_MODEL), F32)], axis=0)
    mods = _ada(c_all, w_ada, b_ada).reshape(depth, c_rows, 6, D_MODEL)
    mods = jnp.pad(mods, ((0, 0), (0, 0), (0, 2), (0, 0)))

    mats = (_block_ones(512, 64, 512, 64, 1.0 / HEAD_DIM), _block_ones(512, 64, 512, 64),
            _block_ones(256, 32, 512, 64), _block_ones(512, 64, 256, 32), _block_ones(256, 32, 256, 32))
    tabs = _rope_tables(dec_seq, 64) + _rope_tables(dec_seq, 32)
    sink = win_sink.astype(F32) * LOG2E
    w_r_t = w_router.T.astype(F32)
    b_r = b_router.astype(F32).reshape(N_EXPERTS, 1)
    layers = [_layer_params(i, p) for i in range(depth)]

    def ffn(x1_route, mod, rows_per_mod, lp):
        x1, route = x1_route
        return _moe(x1, route, mod, rows_per_mod, lp["gains"], lp["wg"], lp["wu"], lp["wd"])

    def merge(x, oa, ob, oc, mod, rows_per_mod, lp):
        return _merge(x, oa, ob, oc, mod, rows_per_mod, lp["gains"], lp["w_gate"], lp["wo_a"],
                      lp["wo_b"], lp["wo_c"], lp["w_out"], w_r_t, b_r)

    n_ctx = batch * seq
    x = x_prompt.reshape(n_ctx, D_MODEL)
    states = []
    for i, lp in enumerate(layers):
        mod = mods[i, 0:1]
        outs = _inproj(x, mod, n_ctx, lp["gains"], lp["w_a"], lp["w_uq"], lp["w_ukv"], mats, None,
                       seq, True)
        oa, ob, oc = _ctx_attn(i, sink, seq, *outs[:9])
        states.append(outs[9:])
        x = ffn(merge(x, oa, ob, oc, mod, n_ctx, lp), mod, n_ctx, lp)
    y_prompt = x.reshape(batch, seq, D_MODEL)

    n_lat = dec_batch * dec_seq
    x = x_sample.reshape(n_lat, D_MODEL)
    cwk = cache_win_k.reshape(dec_batch, depth, past, 128).astype(BF16)
    cwv = cache_win_v.reshape(dec_batch, depth, past, 128).astype(BF16)
    cnk = cache_nbr_k.reshape(dec_batch, depth, past, 512).astype(BF16)
    cnv = cache_nbr_v.reshape(dec_batch, depth, past, 512).astype(BF16)
    for i, lp in enumerate(layers):
        mod = mods[i, 1:1 + dec_batch]
        qa, ka, va, qb, kb, vb, qc, kc, vc = _inproj(
            x, mod, dec_seq, lp["gains"], lp["w_a"], lp["w_uq"], lp["w_ukv"], mats, tabs, dec_seq, False)
        kr_t = jnp.tile(cache_mla_krope[:, i].reshape(dec_batch * past, ROPE_DIM), (1, MLA_HEADS))
        kc_c, vc_c = _mla_cache_keys(cache_mla_ckv[:, i].reshape(dec_batch * past, KV_LORA), kr_t,
                                     lp["gains"], lp["w_ukv"], mats[1], mats[3], mats[4])
        r3 = lambda a: a.reshape(dec_batch, dec_seq, a.shape[-1])
        oa = _win_attn(i, sink, r3(qa), r3(ka), r3(va), cwk, cwv)
        ob = _nbr_attn(i, r3(qb), r3(kb), r3(vb), cnk, cnv, lp["nbr_table"])
        oc = _mla_attn(r3(qc), r3(kc), r3(vc), kc_c.reshape(dec_batch, past, 1024),
                       vc_c.reshape(dec_batch, past, 512))
        flat = lambda a: a.reshape(n_lat, 512)
        x = ffn(merge(x, flat(oa), flat(ob), flat(oc), mod, dec_seq, lp), mod, dec_seq, lp)
    y_sample = x.reshape(dec_batch, dec_seq, D_MODEL)

    def stack(k, shape):
        return jnp.stack([s[k].reshape((batch, seq) + shape) for s in states], axis=1)

    return (y_prompt, y_sample,
            stack(0, (WIN_KV_HEADS, HEAD_DIM)), stack(1, (WIN_KV_HEADS, HEAD_DIM)),
            stack(2, (NBR_HEADS, HEAD_DIM)), stack(3, (NBR_HEADS, HEAD_DIM)),
            stack(4, (KV_LORA,)), stack(5, (ROPE_DIM,)))
```

```python
import functools

import numpy as np
import jax
import jax.numpy as jnp
from jax import lax
from jax.experimental import pallas as pl
from jax.experimental.pallas import tpu as pltpu

D_MODEL = 1024
GRID_W = 64
HEAD_DIM = 64
WIN_HEADS = 8
WIN_KV_HEADS = 2
WINDOW = 128
NBR_HEADS = 8
NBR_ROWS = 8
NBR_COLS = 16
MLA_HEADS = 8
Q_LORA = 256
KV_LORA = 128
NOPE_DIM = 64
ROPE_DIM = 32
V_DIM = 64
QK_DIM = NOPE_DIM + ROPE_DIM
N_EXPERTS = 16
N_GROUPS = 4
EXPERTS_PER_GROUP = 4
D_FF = 512
ROPE_BASE = 10000.0
EPS = 1e-6

LANES = 128
LOG2E = 1.4426950408889634
NEG = -1e30
VMEM_LIMIT = 56 * 1024 * 1024

F32 = jnp.float32
BF16 = jnp.bfloat16

C_QA, C_KA, C_VA, C_QB, C_KB, C_VB, C_CQ, C_CKV, C_KR, C_END = (
    0, 512, 640, 768, 1280, 1792, 2304, 2560, 2688, 2944)
WIN_Q_ORDER = (0, 4, 1, 5, 2, 6, 3, 7)

MOE_TILE = 1024
MOE_CHUNK = 256


def _cparams(sem):
    return pltpu.CompilerParams(dimension_semantics=sem, vmem_limit_bytes=VMEM_LIMIT)


def _dot(a, b):
    return jnp.dot(a, b, preferred_element_type=F32)


def _dot_nt(a, b):
    return lax.dot_general(a, b, (((1,), (1,)), ((), ())), preferred_element_type=F32)


def _split(x):
    hi = x.astype(BF16)
    lo = (x - hi.astype(F32)).astype(BF16)
    return hi, lo


def _gsum(x2, bmat):
    return _dot(x2.astype(BF16), bmat)


def _tile_lanes(t, width):
    reps = width // t.shape[-1]
    return t if reps == 1 else jnp.concatenate([t] * reps, axis=-1)


def _rotate(x, cos, sin_a, sin_b, half):
    w = x.shape[-1]
    up = pltpu.roll(x, w - half, 1)
    dn = pltpu.roll(x, half, 1)
    return (x * _tile_lanes(cos, w) + up * _tile_lanes(sin_a, w) + dn * _tile_lanes(sin_b, w))


def _norm_mod(x, gain, scale, shift):
    ms = jnp.mean(x * x, axis=-1, keepdims=True)
    return (x * lax.rsqrt(ms + EPS) * gain) * (1.0 + scale) + shift


def _ada_body(c_ref, w_ref, b_ref, o_ref):
    c = c_ref[...]
    a = c * (1.0 / (1.0 + jnp.exp(-c)))
    a_hi, a_lo = _split(a)
    w_hi, w_lo = _split(w_ref[0])
    o_ref[0] = _dot(a_hi, w_hi) + _dot(a_hi, w_lo) + _dot(a_lo, w_hi) + b_ref[0]


def _ada(c_all, w_ada, b_ada):
    depth = w_ada.shape[0]
    rows = c_all.shape[0]
    tn = 1536
    return pl.pallas_call(
        _ada_body,
        grid=(depth, 6 * D_MODEL // tn),
        in_specs=[pl.BlockSpec((rows, D_MODEL), lambda l, j: (0, 0)),
                  pl.BlockSpec((1, D_MODEL, tn), lambda l, j: (l, 0, j)),
                  pl.BlockSpec((1, 1, tn), lambda l, j: (l, 0, j))],
        out_specs=pl.BlockSpec((1, rows, tn), lambda l, j: (l, 0, j)),
        out_shape=jax.ShapeDtypeStruct((depth, rows, 6 * D_MODEL), F32),
        compiler_params=_cparams(("parallel", "parallel")),
        name="ada",
    )(c_all, w_ada, b_ada.reshape(depth, 1, 6 * D_MODEL))


def _mla_key_tail(ckvn_b, kr_t, g, wukv_ref, bnn, bnr, brr, rope_tabs, kc_ref, vc_ref):
    kv = _dot(ckvn_b, wukv_ref[...])
    kn = kv[:, 0:512]
    vc_ref[...] = kv[:, 512:1024].astype(BF16)
    kn2 = kn * kn
    kr2 = kr_t * kr_t
    kr_sum32 = _gsum(kr2, brr)
    ssn = (_gsum(kn2, bnn) + jnp.concatenate([kr_sum32, kr_sum32], axis=-1)) * (1.0 / QK_DIM)
    ssr = (_gsum(kn2, bnr) + kr_sum32) * (1.0 / QK_DIM)
    kn = kn * lax.rsqrt(ssn + EPS) * g[5:6, 0:512]
    kr = kr_t * lax.rsqrt(ssr + EPS) * g[5:6, 512:768]
    if rope_tabs is not None:
        kr = _rotate(kr, *rope_tabs, 8)
    for p in range(4):
        kc_ref[:, 256 * p:256 * p + 128] = kn[:, 128 * p:128 * p + 128].astype(BF16)
        q4 = 128 * (p // 2)
        kc_ref[:, 256 * p + 128:256 * p + 256] = kr[:, q4:q4 + 128].astype(BF16)


def _inproj_body(rope, states, *refs):
    (x_ref, mod_ref, g_ref, w_ref, wuq_ref, wukv_ref, b64_ref, bnn_ref, brn_ref, bnr_ref,
     brr_ref) = refs[:11]
    refs = refs[11:]
    if rope:
        tabs_w = tuple(r[...] for r in refs[0:3])
        tabs_m = tuple(r[...] for r in refs[3:6])
        refs = refs[6:]
    else:
        tabs_w = tabs_m = None
    qa_ref, ka_ref, va_ref, qb_ref, kb_ref, vb_ref, qc_ref, kc_ref, vc_ref = refs[:9]
    st = refs[9:]

    g = g_ref[...]
    mod = mod_ref[0]
    hb = _norm_mod(x_ref[...], g[0:1], mod[1:2], mod[0:1]).astype(BF16)

    def proj(a, b):
        return _dot(hb, w_ref[:, a:b])

    b64 = b64_ref[...]

    def head_norm(z, bmat, gain):
        return z * lax.rsqrt(_gsum(z * z, bmat) + EPS) * gain

    qa = head_norm(proj(C_QA, C_KA), b64, g[1:2, 0:512])
    ka = head_norm(proj(C_KA, C_VA), b64[0:128, 0:128], g[1:2, 512:640])
    va = proj(C_VA, C_QB)
    if states:
        st[0][...] = ka
        st[1][...] = va
    if rope:
        qa = _rotate(qa, *tabs_w, 16)
        ka = _rotate(ka, *tabs_w, 16)
    qa_ref[...] = qa.astype(BF16)
    ka_ref[...] = ka.astype(BF16)
    va_ref[...] = va.astype(BF16)

    qb = head_norm(proj(C_QB, C_KB), b64, g[2:3, 0:512])
    kb = head_norm(proj(C_KB, C_VB), b64, g[2:3, 512:1024])
    vb = proj(C_VB, C_CQ)
    if states:
        st[2][...] = kb
        st[3][...] = vb
    qb_ref[...] = qb.astype(BF16)
    kb_ref[...] = kb.astype(BF16)
    vb_ref[...] = vb.astype(BF16)

    cq = proj(C_CQ, C_CKV)
    cqn = cq * lax.rsqrt(jnp.mean(cq * cq, axis=-1, keepdims=True) + EPS) * g[3:4, 0:256]
    qq = _dot(cqn.astype(BF16), wuq_ref[...])
    qn, qr = qq[:, 0:512], qq[:, 512:768]
    qn2, qr2 = qn * qn, qr * qr
    bnn, brn, bnr, brr = bnn_ref[...], brn_ref[...], bnr_ref[...], brr_ref[...]
    ssn = (_gsum(qn2, bnn) + _gsum(qr2, brn)) * (1.0 / QK_DIM)
    ssr = (_gsum(qn2, bnr) + _gsum(qr2, brr)) * (1.0 / QK_DIM)
    qn = qn * lax.rsqrt(ssn + EPS) * g[4:5, 0:512]
    qr = qr * lax.rsqrt(ssr + EPS) * g[4:5, 512:768]
    if rope:
        qr = _rotate(qr, *tabs_m, 8)
    for p in range(4):
        qc_ref[:, 256 * p:256 * p + 128] = qn[:, 128 * p:128 * p + 128].astype(BF16)
        q4 = 128 * (p // 2)
        qc_ref[:, 256 * p + 128:256 * p + 256] = qr[:, q4:q4 + 128].astype(BF16)

    ckv = proj(C_CKV, C_KR)
    ckvn = ckv * lax.rsqrt(jnp.mean(ckv * ckv, axis=-1, keepdims=True) + EPS) * g[3:4, 256:384]
    kr_t = proj(C_KR, C_END)
    if states:
        st[4][...] = ckvn
        st[5][...] = kr_t[:, 0:ROPE_DIM]
    _mla_key_tail(ckvn.astype(BF16), kr_t, g, wukv_ref, bnn, bnr, brr, tabs_m, kc_ref, vc_ref)


def _const_spec(shape):
    nd = len(shape)
    return pl.BlockSpec(shape, lambda i, _nd=nd: (0,) * _nd)


def _inproj(x, mods, rows_per_mod, gains, w_a, w_uq, w_ukv, mats, rope_tabs, seq_len, states):
    n = x.shape[0]
    tm = min(512, n)
    rope = rope_tabs is not None
    row = lambda w: pl.BlockSpec((tm, w), lambda i: (i, 0))
    in_specs = [row(D_MODEL),
                pl.BlockSpec((1, 8, D_MODEL), lambda i: ((i * tm) // rows_per_mod, 0, 0)),
                _const_spec(gains.shape), _const_spec(w_a.shape), _const_spec(w_uq.shape),
                _const_spec(w_ukv.shape)] + [_const_spec(m.shape) for m in mats]
    args = [x, mods, gains, w_a, w_uq, w_ukv, *mats]
    if rope:
        tiles_per_seq = seq_len // tm
        in_specs += [pl.BlockSpec((tm, LANES), lambda i: (i % tiles_per_seq, 0))] * 6
        args += list(rope_tabs)
    widths = [512, 128, 128, 512, 512, 512, 1024, 1024, 512]
    out_shape = [jax.ShapeDtypeStruct((n, w), BF16) for w in widths]
    out_specs = [row(w) for w in widths]
    if states:
        swidths = [128, 128, 512, 512, KV_LORA, ROPE_DIM]
        out_shape += [jax.ShapeDtypeStruct((n, w), F32) for w in swidths]
        out_specs += [row(w) for w in swidths]
    return pl.pallas_call(
        functools.partial(_inproj_body, rope, states),
        grid=(n // tm,), in_specs=in_specs, out_specs=out_specs, out_shape=out_shape,
        compiler_params=_cparams(("parallel",)),
        name="inproj_lat" if rope else "inproj_ctx",
    )(*args)


def _mla_cache_body(ckv_ref, kr_ref, g_ref, wukv_ref, bnn_ref, bnr_ref, brr_ref, kc_ref, vc_ref):
    _mla_key_tail(ckv_ref[...].astype(BF16), kr_ref[...], g_ref[...], wukv_ref, bnn_ref[...],
                  bnr_ref[...], brr_ref[...], None, kc_ref, vc_ref)


def _mla_cache_keys(ckv, kr_t, gains, w_ukv, bnn, bnr, brr):
    n = ckv.shape[0]
    tm = min(512, n)
    row = lambda w: pl.BlockSpec((tm, w), lambda i: (i, 0))
    return pl.pallas_call(
        _mla_cache_body,
        grid=(n // tm,),
        in_specs=[row(KV_LORA), row(256), _const_spec(gains.shape), _const_spec(w_ukv.shape),
                  _const_spec(bnn.shape), _const_spec(bnr.shape), _const_spec(brr.shape)],
        out_specs=[row(1024), row(512)],
        out_shape=[jax.ShapeDtypeStruct((n, 1024), BF16), jax.ShapeDtypeStruct((n, 512), BF16)],
        compiler_params=_cparams(("parallel",)),
        name="mla_cache_keys",
    )(ckv, kr_t, gains, w_ukv, bnn, bnr, brr)


def _lane_mask(width, ranges):
    lane = lax.broadcasted_iota(jnp.int32, (1, width), 1)
    m = None
    for lo, hi in ranges:
        c = (lane >= lo) & (lane < hi)
        m = c if m is None else (m | c)
    return jnp.where(m, 1.0, 0.0).astype(BF16)


def _stack_heads(q, mask0, mask1):
    return jnp.concatenate([q * mask0, q * mask1], axis=0)


def _lane_tiles(x):
    return [x[:, j:j + LANES] for j in range(0, x.shape[1], LANES)]


def _softmax_pv(scores, values, sink=None):
    rows = scores[0].shape[0]
    mp = None
    for s in scores:
        for t in _lane_tiles(s):
            mp = t if mp is None else jnp.maximum(mp, t)
    base = sink if sink is not None else jnp.full((rows, LANES), NEG, F32)
    m = jnp.maximum(base, jnp.max(mp, axis=-1, keepdims=True))
    lp = None
    o = None
    for s, v in zip(scores, values):
        p = jnp.exp2(s - _tile_lanes(m, s.shape[1]))
        for t in _lane_tiles(p):
            lp = t if lp is None else lp + t
        os_ = _dot(p.astype(BF16), v)
        o = os_ if o is None else o + os_
    if sink is not None:
        lane = lax.broadcasted_iota(jnp.int32, (rows, LANES), 1)
        lp = lp + jnp.where(lane == 0, jnp.exp2(sink - m), 0.0)
    return o / jnp.sum(lp, axis=-1, keepdims=True)


def _merge_heads(o, tq):
    lane = lax.broadcasted_iota(jnp.int32, (tq, LANES), 1)
    return jnp.where(lane < HEAD_DIM, o[0:tq], o[tq:2 * tq])


def _mla_masks(p_mod2):
    lane = lax.broadcasted_iota(jnp.int32, (1, 256), 1)
    r0 = 128 + 32 * (2 * p_mod2)
    m0 = (lane < 64) | ((lane >= r0) & (lane < r0 + 32))
    m1 = ((lane >= 64) & (lane < 128)) | ((lane >= r0 + 32) & (lane < r0 + 64))
    return (jnp.where(m0, 1.0, 0.0).astype(BF16), jnp.where(m1, 1.0, 0.0).astype(BF16))


def _sink_col(sink_ref, layer, h0, h1, tq):
    row = lax.broadcasted_iota(jnp.int32, (2 * tq, LANES), 0)
    return jnp.where(row < tq, sink_ref[layer, h0], sink_ref[layer, h1])


def _ctx_attn_body(layer, sink_ref, qa_ref, ka_ref, va_ref, qb_ref, kb_ref, vb_ref, qc_ref, kc_ref,
                   vc_ref, oa_ref, ob_ref, oc_ref):
    tq = qa_ref.shape[0]
    lo = _lane_mask(LANES, [(0, 64)])
    hi = _lane_mask(LANES, [(64, 128)])
    ka, va = ka_ref[...], va_ref[...]
    for j in range(4):
        sl = slice(128 * j, 128 * j + 128)
        qs = _stack_heads(qa_ref[:, sl], lo, hi)
        sink = _sink_col(sink_ref, layer, j, 4 + j, tq)
        o = _softmax_pv([_dot_nt(qs, ka)], [va], sink)
        oa_ref[:, sl] = _merge_heads(o, tq).astype(BF16)

        qs = _stack_heads(qb_ref[:, sl], lo, hi)
        o = _softmax_pv([_dot_nt(qs, kb_ref[:, sl])], [vb_ref[:, sl]])
        ob_ref[:, sl] = _merge_heads(o, tq).astype(BF16)

        m0, m1 = _mla_masks(j % 2)
        s2 = slice(256 * j, 256 * j + 256)
        qs = _stack_heads(qc_ref[:, s2], m0, m1)
        o = _softmax_pv([_dot_nt(qs, kc_ref[:, s2])], [vc_ref[:, sl]])
        oc_ref[:, sl] = _merge_heads(o, tq).astype(BF16)


def _ctx_attn(layer, sink, seq, qa, ka, va, qb, kb, vb, qc, kc, vc):
    n = qa.shape[0]
    row = lambda w: pl.BlockSpec((seq, w), lambda b: (b, 0))
    ins = [qa, ka, va, qb, kb, vb, qc, kc, vc]
    return pl.pallas_call(
        functools.partial(_ctx_attn_body, layer),
        grid=(n // seq,),
        in_specs=[pl.BlockSpec(memory_space=pltpu.SMEM)] + [row(a.shape[1]) for a in ins],
        out_specs=[row(512)] * 3,
        out_shape=[jax.ShapeDtypeStruct((n, 512), BF16)] * 3,
        compiler_params=_cparams(("parallel",)),
        name="ctx_attn",
    )(sink, *ins)


def _win_body(layer, sink_ref, q_ref, k_ref, v_ref, kc_ref, vc_ref, o_ref):
    tq = q_ref.shape[1]
    seq = k_ref.shape[1]
    kw = 3 * tq
    i = pl.program_id(1)
    kstart = pl.multiple_of(jnp.clip((i - 1) * tq, 0, seq - kw), tq)
    k = k_ref[0, pl.ds(kstart, kw), :]
    v = v_ref[0, pl.ds(kstart, kw), :]
    kc, vc = kc_ref[0, 0], vc_ref[0, 0]
    q_pos = i * tq + lax.broadcasted_iota(jnp.int32, (2 * tq, kw), 0) % tq
    k_pos = kstart + lax.broadcasted_iota(jnp.int32, (2 * tq, kw), 1)
    band = jnp.abs(q_pos - k_pos) <= WINDOW
    lo = _lane_mask(LANES, [(0, 64)])
    hi = _lane_mask(LANES, [(64, 128)])
    for j in range(4):
        sl = slice(128 * j, 128 * j + 128)
        qs = _stack_heads(q_ref[0, :, sl], lo, hi)
        s_band = jnp.where(band, _dot_nt(qs, k), NEG)
        s_ctx = _dot_nt(qs, kc)
        sink = _sink_col(sink_ref, layer, j, 4 + j, tq)
        o = _softmax_pv([s_band, s_ctx], [v, vc], sink)
        o_ref[0, :, sl] = _merge_heads(o, tq).astype(BF16)


def _win_attn(layer, sink, q, k, v, kc, vc):
    b, seq, _ = q.shape
    tq = 128
    past = kc.shape[2]
    return pl.pallas_call(
        functools.partial(_win_body, layer),
        grid=(b, seq // tq),
        in_specs=[pl.BlockSpec(memory_space=pltpu.SMEM),
                  pl.BlockSpec((1, tq, 512), lambda bi, i: (bi, i, 0)),
                  pl.BlockSpec((1, seq, 128), lambda bi, i: (bi, 0, 0)),
                  pl.BlockSpec((1, seq, 128), lambda bi, i: (bi, 0, 0)),
                  pl.BlockSpec((1, 1, past, 128), lambda bi, i: (bi, layer, 0, 0)),
                  pl.BlockSpec((1, 1, past, 128), lambda bi, i: (bi, layer, 0, 0))],
        out_specs=pl.BlockSpec((1, tq, 512), lambda bi, i: (bi, i, 0)),
        out_shape=jax.ShapeDtypeStruct((b, seq, 512), BF16),
        compiler_params=_cparams(("parallel", "arbitrary")),
        name="win_attn",
    )(sink, q, k, v, kc, vc)


NBR_TILE_ROWS = 2
NBR_WIN_ROWS = 10


def _nbr_body(rows, q_ref, k_ref, v_ref, kc_ref, vc_ref, tab_ref, o_ref):
    tq = NBR_TILE_ROWS * GRID_W
    kw = NBR_WIN_ROWS * GRID_W
    i = pl.program_id(1)
    r0 = NBR_TILE_ROWS * i
    ws = jnp.clip(r0 - NBR_ROWS // 2, 0, rows - NBR_WIN_ROWS)
    kstart = pl.multiple_of(ws * GRID_W, LANES)
    q_row = r0 + lax.broadcasted_iota(jnp.int32, (2 * tq, kw), 0) % tq // GRID_W
    k_row = ws + lax.broadcasted_iota(jnp.int32, (2 * tq, kw), 1) // GRID_W
    rs = jnp.clip(q_row - NBR_ROWS // 2, 0, rows - NBR_ROWS)
    valid = (k_row >= rs) & (k_row < rs + NBR_ROWS)
    lo = _lane_mask(LANES, [(0, 64)])
    hi = _lane_mask(LANES, [(64, 128)])
    for j in range(4):
        sl = slice(128 * j, 128 * j + 128)
        qs = _stack_heads(q_ref[0, :, sl], lo, hi)
        k = k_ref[0, pl.ds(kstart, kw), sl]
        v = v_ref[0, pl.ds(kstart, kw), sl]
        pieces = []
        for h in (2 * j, 2 * j + 1):
            for ql in range(NBR_TILE_ROWS):
                d0 = ws - r0 - ql + (NBR_ROWS - 1) + 2
                pieces.append(jnp.concatenate(
                    [tab_ref[h, d0 + 2 * m] for m in range(NBR_WIN_ROWS // 2)], axis=-1))
        bias = jnp.concatenate(pieces, axis=0)
        s_nb = jnp.where(valid, _dot_nt(qs, k) + bias, NEG)
        s_ctx = _dot_nt(qs, kc_ref[0, 0, :, sl])
        o = _softmax_pv([s_nb, s_ctx], [v, vc_ref[0, 0, :, sl]])
        o_ref[0, :, sl] = _merge_heads(o, tq).astype(BF16)


def _nbr_attn(layer, q, k, v, kc, vc, table):
    b, seq, _ = q.shape
    rows = seq // GRID_W
    tq = NBR_TILE_ROWS * GRID_W
    past = kc.shape[2]
    return pl.pallas_call(
        functools.partial(_nbr_body, rows),
        grid=(b, seq // tq),
        in_specs=[pl.BlockSpec((1, tq, 512), lambda bi, i: (bi, i, 0)),
                  pl.BlockSpec((1, seq, 512), lambda bi, i: (bi, 0, 0)),
                  pl.BlockSpec((1, seq, 512), lambda bi, i: (bi, 0, 0)),
                  pl.BlockSpec((1, 1, past, 512), lambda bi, i: (bi, layer, 0, 0)),
                  pl.BlockSpec((1, 1, past, 512), lambda bi, i: (bi, layer, 0, 0)),
                  pl.BlockSpec(table.shape, lambda bi, i: (0, 0, 0, 0))],
        out_specs=pl.BlockSpec((1, tq, 512), lambda bi, i: (bi, i, 0)),
        out_shape=jax.ShapeDtypeStruct((b, seq, 512), BF16),
        compiler_params=_cparams(("parallel", "arbitrary")),
        name="nbr_attn",
    )(q, k, v, kc, vc, table)


def _nbr_bias_table(rel_bias):
    col = np.arange(GRID_W)
    cs = np.clip(col - NBR_COLS // 2, 0, GRID_W - NBR_COLS)
    kc = np.arange(GRID_W)
    ok = (kc[None, :] >= cs[:, None]) & (kc[None, :] < cs[:, None] + NBR_COLS)
    dc = np.clip(kc[None, :] - col[:, None] + (NBR_COLS - 1), 0, 2 * NBR_COLS - 2)
    t = rel_bias.astype(F32)[:, :, dc] * LOG2E
    t = jnp.where(jnp.asarray(ok)[None, None], t, NEG)
    t = jnp.pad(t, ((0, 0), (2, 2), (0, 0), (0, 0)))
    return jnp.concatenate([t[:, :-1], t[:, 1:]], axis=-1)


MLA_KEY_CHUNK = 512
MLA_ROW_BLOCK = 256


def _mla_body(q_ref, kl_ref, kc_ref, vl_ref, vc_ref, o_ref, qs_sc, m_sc, l_sc, acc_sc):
    tq = q_ref.shape[1]
    rows = 2 * tq
    seq = kl_ref.shape[1]
    past = kc_ref.shape[1]
    tk = min(MLA_KEY_CHUNK, past)
    rb = min(MLA_ROW_BLOCK, rows)
    m0, m1 = _mla_masks(pl.program_id(1) % 2)
    qs_sc[...] = _stack_heads(q_ref[0], m0, m1)
    m_sc[...] = jnp.full(m_sc.shape, NEG, F32)
    l_sc[...] = jnp.zeros(l_sc.shape, F32)
    acc_sc[...] = jnp.zeros(acc_sc.shape, F32)

    def step(k, v):
        s = _dot_nt(qs_sc[...], k)
        for r in range(rows // rb):
            sl = slice(r * rb, (r + 1) * rb)
            sb = s[sl]
            m_prev = m_sc[sl]
            m_new = jnp.maximum(m_prev, jnp.max(sb, axis=-1, keepdims=True))
            alpha = jnp.exp2(m_prev - m_new)
            p = jnp.exp2(sb - _tile_lanes(m_new, tk))
            psum = p[:, 0:LANES]
            for j in range(1, tk // LANES):
                psum = psum + p[:, j * LANES:(j + 1) * LANES]
            l_sc[sl] = alpha * l_sc[sl] + psum
            acc_sc[sl] = alpha * acc_sc[sl] + _dot(p.astype(BF16), v)
            m_sc[sl] = m_new

    def lat_step(c, carry):
        start = pl.multiple_of(c * tk, tk)
        step(kl_ref[0, pl.ds(start, tk), :], vl_ref[0, pl.ds(start, tk), :])
        return carry

    lax.fori_loop(0, seq // tk, lat_step, 0)
    for c in range(past // tk):
        step(kc_ref[0, c * tk:(c + 1) * tk, :], vc_ref[0, c * tk:(c + 1) * tk, :])
    l = jnp.sum(l_sc[...], axis=-1, keepdims=True)
    o_ref[0] = _merge_heads(acc_sc[...] / l, tq).astype(BF16)


def _mla_attn(q, kl, vl, kc, vc):
    b, seq, _ = q.shape
    past = kc.shape[1]
    tq = min(512, seq)
    return pl.pallas_call(
        _mla_body,
        grid=(b, 4, seq // tq),
        in_specs=[pl.BlockSpec((1, tq, 256), lambda bi, p, qi: (bi, qi, p)),
                  pl.BlockSpec((1, seq, 256), lambda bi, p, qi: (bi, 0, p)),
                  pl.BlockSpec((1, past, 256), lambda bi, p, qi: (bi, 0, p)),
                  pl.BlockSpec((1, seq, 128), lambda bi, p, qi: (bi, 0, p)),
                  pl.BlockSpec((1, past, 128), lambda bi, p, qi: (bi, 0, p))],
        out_specs=pl.BlockSpec((1, tq, 128), lambda bi, p, qi: (bi, qi, p)),
        out_shape=jax.ShapeDtypeStruct((b, seq, 512), BF16),
        scratch_shapes=[pltpu.VMEM((2 * tq, 256), BF16), pltpu.VMEM((2 * tq, LANES), F32),
                        pltpu.VMEM((2 * tq, LANES), F32), pltpu.VMEM((2 * tq, LANES), F32)],
        compiler_params=_cparams(("parallel", "parallel", "arbitrary")),
        name="mla_attn",
    )(q, kl, kc, vl, vc)


def _merge_body(x_ref, oa_ref, ob_ref, oc_ref, mod_ref, g_ref, wg_ref, woa_ref, wob_ref, woc_ref,
                wout_ref, wr_ref, br_ref, x1_ref, route_ref):
    x = x_ref[...]
    g = g_ref[...]
    mod = mod_ref[0]
    hb = _norm_mod(x, g[0:1], mod[1:2], mod[0:1]).astype(BF16)
    m = None
    for br, (o_ref, wo_ref) in enumerate(((oa_ref, woa_ref), (ob_ref, wob_ref), (oc_ref, woc_ref))):
        z = _dot(hb, wg_ref[:, D_MODEL * br:D_MODEL * (br + 1)])
        gate = 1.0 / (1.0 + jnp.exp(-z))
        t = gate * _dot(o_ref[...], wo_ref[...])
        m = t if m is None else m + t
    y = _dot(m.astype(BF16), wout_ref[...])
    x1 = x + mod[2:3] * y
    x1_ref[...] = x1

    h2 = _norm_mod(x1, g[6:7], mod[4:5], mod[3:4])
    h_hi, h_lo = _split(h2)
    wr = wr_ref[...]
    w_hi, w_lo = _split(wr)
    logits = _dot_nt(w_hi, h_hi) + _dot_nt(w_hi, h_lo) + _dot_nt(w_lo, h_hi)
    score = 1.0 / (1.0 + jnp.exp(-logits))
    sel = score + br_ref[...]
    sel_r = [sel[e:e + 1] for e in range(N_EXPERTS)]
    sc_r = [score[e:e + 1] for e in range(N_EXPERTS)]
    picked = []
    for e in range(N_EXPERTS):
        grp, a = divmod(e, EXPERTS_PER_GROUP)
        rank = None
        for bb in range(EXPERTS_PER_GROUP):
            if bb == a:
                continue
            o = sel_r[grp * EXPERTS_PER_GROUP + bb]
            beats = (o >= sel_r[e]) if bb < a else (o > sel_r[e])
            r = jnp.where(beats, 1.0, 0.0)
            rank = r if rank is None else rank + r
        picked.append(rank < 2.0)
    gscore = []
    for grp in range(N_GROUPS):
        tot = None
        for a in range(EXPERTS_PER_GROUP):
            e = grp * EXPERTS_PER_GROUP + a
            t = jnp.where(picked[e], sel_r[e], 0.0)
            tot = t if tot is None else tot + t
        gscore.append(tot)
    best = jnp.zeros_like(gscore[0])
    best_v = gscore[0]
    for grp in range(1, N_GROUPS):
        upd = gscore[grp] > best_v
        best = jnp.where(upd, float(grp), best)
        best_v = jnp.where(upd, gscore[grp], best_v)
    cw = []
    for a in range(EXPERTS_PER_GROUP):
        tot = None
        for grp in range(N_GROUPS):
            e = grp * EXPERTS_PER_GROUP + a
            t = jnp.where((best == float(grp)) & picked[e], sc_r[e], 0.0)
            tot = t if tot is None else tot + t
        cw.append(tot)
    den = cw[0] + cw[1] + cw[2] + cw[3]
    for a in range(EXPERTS_PER_GROUP):
        route_ref[a:a + 1, :] = cw[a] / den
    route_ref[4:5, :] = best
    route_ref[5:8, :] = jnp.zeros((3, best.shape[1]), F32)


def _merge(x, oa, ob, oc, mods, rows_per_mod, gains, w_gate, wo_a, wo_b, wo_c, w_out, w_r_t, b_r):
    n = x.shape[0]
    tm = min(512, n)
    row = lambda w: pl.BlockSpec((tm, w), lambda i: (i, 0))
    consts = [gains, w_gate, wo_a, wo_b, wo_c, w_out, w_r_t, b_r]
    return pl.pallas_call(
        _merge_body,
        grid=(n // tm,),
        in_specs=[row(D_MODEL), row(512), row(512), row(512),
                  pl.BlockSpec((1, 8, D_MODEL), lambda i: ((i * tm) // rows_per_mod, 0, 0))]
                 + [_const_spec(c.shape) for c in consts],
        out_specs=[row(D_MODEL), pl.BlockSpec((8, tm), lambda i: (0, i))],
        out_shape=[jax.ShapeDtypeStruct((n, D_MODEL), F32), jax.ShapeDtypeStruct((8, n), F32)],
        compiler_params=_cparams(("parallel",)),
        name="merge",
    )(x, oa, ob, oc, mods, *consts)


def _moe_body(max_chunks, cg_ref, nch_ref, x1_ref, side_ref, idxr_ref, idxc_ref, mod_ref, g_ref,
              wg_ref, wu_ref, wd_ref, o_ref, h2_sc):
    t = pl.program_id(0)
    c = pl.program_id(1)
    tile = x1_ref.shape[0]
    chunk = idxr_ref.shape[2]

    @pl.when(c == 0)
    def _():
        mod = mod_ref[0]
        h2_sc[...] = _norm_mod(x1_ref[...], g_ref[6:7, :], mod[4:5], mod[3:4]).astype(BF16)
        o_ref[...] = jnp.zeros(o_ref.shape, F32)

    @pl.when(c < nch_ref[t])
    def _():
        idx_col = _tile_lanes(idxc_ref[0], tile)
        sel = jnp.where(idx_col == lax.broadcasted_iota(jnp.int32, (chunk, tile), 1), 1.0, 0.0)
        sel = sel.astype(BF16)
        idx_row = idxr_ref[0]
        sel_t = jnp.where(idx_row == lax.broadcasted_iota(jnp.int32, (tile, chunk), 0), 1.0, 0.0)
        sel_t = sel_t.astype(BF16)
        xg = _dot(sel, h2_sc[...]).astype(BF16)
        cwg = _dot(sel, side_ref[...])
        y = None
        for a in range(EXPERTS_PER_GROUP):
            w = cwg[:, a:a + 1] + cwg[:, 4 + a:5 + a]
            zg = _dot(xg, wg_ref[0, a])
            act = zg * (1.0 / (1.0 + jnp.exp(-zg))) * _dot(xg, wu_ref[0, a])
            ya = _dot((act * w).astype(BF16), wd_ref[0, a])
            y = ya if y is None else y + ya
        o_ref[...] += _dot(sel_t, y.astype(BF16))

    @pl.when(c == max_chunks - 1)
    def _():
        o_ref[...] = x1_ref[...] + mod_ref[0][5:6] * o_ref[...]


def _moe(x1, route, mods, rows_per_mod, gains, wg, wu, wd):
    n = x1.shape[0]
    tile = min(MOE_TILE, n)
    chunk = min(MOE_CHUNK, tile)
    nt = n // tile
    max_chunks = (tile + N_GROUPS * (chunk - 1)) // chunk

    cw = route[0:4].T
    gid = route[4].astype(jnp.int32).reshape(nt, tile)
    hi = cw.astype(BF16)
    lo = (cw - hi.astype(F32)).astype(BF16)
    side = jnp.pad(jnp.concatenate([hi, lo], axis=1), ((0, 0), (0, LANES - 8)))
    key = jnp.sort(gid * tile + jnp.arange(tile, dtype=jnp.int32)[None], axis=-1)
    stok = key % tile
    counts = jnp.sum(gid[:, :, None] == jnp.arange(N_GROUPS)[None, None], axis=1).astype(jnp.int32)
    starts = jnp.cumsum(counts, axis=-1) - counts
    nch_g = (counts + chunk - 1) // chunk
    cend = jnp.cumsum(nch_g, axis=-1)
    total = cend[:, -1]
    cidx = jnp.arange(max_chunks, dtype=jnp.int32)
    g_c = jnp.sum(cidx[None, :, None] >= cend[:, None, :], axis=-1).astype(jnp.int32)
    last = jnp.take_along_axis(g_c, jnp.maximum(total - 1, 0)[:, None], axis=1)
    valid_c = cidx[None] < total[:, None]
    g_c = jnp.where(valid_c, jnp.minimum(g_c, N_GROUPS - 1), last)
    k_in_g = cidx[None] - jnp.take_along_axis(cend - nch_g, g_c, axis=1)
    base = jnp.take_along_axis(starts, g_c, axis=1) + k_in_g * chunk
    cnt = jnp.take_along_axis(counts, g_c, axis=1)
    off = jnp.arange(chunk, dtype=jnp.int32)
    ok = valid_c[:, :, None] & ((k_in_g * chunk)[:, :, None] + off[None, None] < cnt[:, :, None])
    pos = jnp.clip(base[:, :, None] + off[None, None], 0, tile - 1).reshape(nt, max_chunks * chunk)
    idx = jnp.where(ok, jnp.take_along_axis(stok, pos, axis=1).reshape(nt, max_chunks, chunk), -1)
    idx_row = idx.reshape(nt * max_chunks, 1, chunk)
    idx_col = jnp.broadcast_to(idx.reshape(nt * max_chunks, chunk, 1), (nt * max_chunks, chunk, LANES))

    wspec = lambda shape: pl.BlockSpec(
        shape, lambda t, c, cg, nc: (cg[t * max_chunks + c], 0, 0, 0))
    grid_spec = pltpu.PrefetchScalarGridSpec(
        num_scalar_prefetch=2,
        grid=(nt, max_chunks),
        in_specs=[pl.BlockSpec((tile, D_MODEL), lambda t, c, cg, nc: (t, 0)),
                  pl.BlockSpec((tile, LANES), lambda t, c, cg, nc: (t, 0)),
                  pl.BlockSpec((1, 1, chunk), lambda t, c, cg, nc: (t * max_chunks + c, 0, 0)),
                  pl.BlockSpec((1, chunk, LANES), lambda t, c, cg, nc: (t * max_chunks + c, 0, 0)),
                  pl.BlockSpec((1, 8, D_MODEL),
                               lambda t, c, cg, nc: ((t * tile) // rows_per_mod, 0, 0)),
                  pl.BlockSpec(gains.shape, lambda t, c, cg, nc: (0, 0)),
                  wspec((1, EXPERTS_PER_GROUP, D_MODEL, D_FF)),
                  wspec((1, EXPERTS_PER_GROUP, D_MODEL, D_FF)),
                  wspec((1, EXPERTS_PER_GROUP, D_FF, D_MODEL))],
        out_specs=pl.BlockSpec((tile, D_MODEL), lambda t, c, cg, nc: (t, 0)),
        scratch_shapes=[pltpu.VMEM((tile, D_MODEL), BF16)])
    return pl.pallas_call(
        functools.partial(_moe_body, max_chunks),
        grid_spec=grid_spec,
        out_shape=jax.ShapeDtypeStruct((n, D_MODEL), F32),
        compiler_params=_cparams(("parallel", "arbitrary")),
        name="moe",
    )(g_c.reshape(-1), total.astype(jnp.int32), x1, side, idx_row, idx_col, mods, gains, wg, wu, wd)


def _block_ones(n_in, g_in, n_out, g_out, value=1.0):
    r = np.arange(n_in)[:, None] // g_in
    c = np.arange(n_out)[None, :] // g_out
    return jnp.asarray(np.where(r == c, value, 0.0), dtype=BF16)


def _rope_tables(seq, head_w):
    pos = np.arange(seq)
    rows, cols = pos // GRID_W, pos % GRID_W
    a = head_w // 2
    half = a // 2
    freqs = (ROPE_BASE ** (-np.arange(half, dtype=np.float32) / half)).astype(np.float32)
    lane = np.arange(LANES) % head_w
    within = lane % a
    first = within < half
    p = np.where((lane // a == 0)[None, :], rows[:, None], cols[:, None]).astype(np.float32)
    ang = (p * freqs[within % half][None, :]).astype(np.float32)
    cos, sin = np.cos(ang), np.sin(ang)
    return (jnp.asarray(cos, F32), jnp.asarray(np.where(first[None], -sin, 0.0), F32),
            jnp.asarray(np.where(first[None], 0.0, sin), F32))


def _tile_to(v, width):
    return jnp.tile(v, width // v.shape[0])


def _layer_params(i, p):
    w_in = p["w_in"][i]
    sp = np.cumsum((512, 128, 128, 512, 512, 512, Q_LORA, KV_LORA, ROPE_DIM))
    qa, ka, va, qb, kb, vb, cq, ckv, kr, gates = jnp.split(w_in, [int(s) for s in sp], axis=1)
    qa = qa.reshape(D_MODEL, WIN_HEADS, HEAD_DIM)[:, WIN_Q_ORDER, :].reshape(D_MODEL, 512)
    w_a = jnp.concatenate([qa, ka, va, qb, kb, vb, cq, ckv, jnp.tile(kr, (1, MLA_HEADS))],
                          axis=1).astype(BF16)
    w_uq = p["w_uq"][i].reshape(Q_LORA, MLA_HEADS, QK_DIM)
    w_uq = jnp.concatenate([w_uq[:, :, :NOPE_DIM].reshape(Q_LORA, 512),
                            w_uq[:, :, NOPE_DIM:].reshape(Q_LORA, 256)], axis=1).astype(BF16)
    w_ukv = p["w_ukv"][i].reshape(KV_LORA, MLA_HEADS, NOPE_DIM + V_DIM)
    w_ukv = jnp.concatenate([w_ukv[:, :, :NOPE_DIM].reshape(KV_LORA, 512),
                             w_ukv[:, :, NOPE_DIM:].reshape(KV_LORA, 512)], axis=1).astype(BF16)
    z = jnp.zeros((D_MODEL,), F32)
    row = lambda *parts: jnp.concatenate(list(parts) + [z])[:D_MODEL]
    q_scale = HEAD_DIM ** -0.5 * LOG2E
    c_scale = QK_DIM ** -0.5 * LOG2E
    g_mla = p["g_qk_mla"][i]
    gains = jnp.stack([
        p["g_norm_mix"][i],
        row(_tile_to(p["g_qk_win"][i, 0], 512) * q_scale, _tile_to(p["g_qk_win"][i, 1], 128)),
        row(_tile_to(p["g_qk_nbr"][i, 0], 512) * q_scale, _tile_to(p["g_qk_nbr"][i, 1], 512)),
        row(p["g_q_lora"][i], p["g_kv_lora"][i]),
        row(_tile_to(g_mla[0, :NOPE_DIM], 512) * c_scale, _tile_to(g_mla[0, NOPE_DIM:], 256) * c_scale),
        row(_tile_to(g_mla[1, :NOPE_DIM], 512), _tile_to(g_mla[1, NOPE_DIM:], 256)),
        p["g_norm_ffn"][i],
        z]).astype(F32)
    wo_a = p["w_o_win"][i].reshape(WIN_HEADS, HEAD_DIM, D_MODEL)[WIN_Q_ORDER, :, :].reshape(512, D_MODEL)
    grp = lambda w: w.astype(BF16).reshape((N_GROUPS, EXPERTS_PER_GROUP) + w.shape[1:])
    return dict(
        w_a=w_a, w_uq=w_uq, w_ukv=w_ukv, gains=gains, w_gate=gates.astype(BF16),
        wo_a=wo_a.astype(BF16), wo_b=p["w_o_nbr"][i].astype(BF16), wo_c=p["w_o_mla"][i].astype(BF16),
        w_out=p["w_out"][i].astype(BF16),
        wg=grp(p["w_exp_gate"][i]), wu=grp(p["w_exp_up"][i]), wd=grp(p["w_exp_down"][i]),
        nbr_table=_nbr_bias_table(p["nbr_rel_bias"][i]))


def kernel(x_prompt, x_sample, cache_win_k, cache_win_v, cache_nbr_k, cache_nbr_v, cache_mla_ckv, cache_mla_krope, c, c_ctx, g_norm_mix, g_norm_ffn, w_ada, b_ada, w_in, g_qk_win, win_sink, g_qk_nbr, nbr_rel_bias, g_q_lora, g_kv_lora, w_uq, w_ukv, g_qk_mla, w_o_win, w_o_nbr, w_o_mla, w_out, w_router, b_router, w_exp_gate, w_exp_up, w_exp_down):
    p = dict(g_norm_mix=g_norm_mix, g_norm_ffn=g_norm_ffn, w_in=w_in, g_qk_win=g_qk_win,
             g_qk_nbr=g_qk_nbr, nbr_rel_bias=nbr_rel_bias, g_q_lora=g_q_lora, g_kv_lora=g_kv_lora,
             w_uq=w_uq, w_ukv=w_ukv, g_qk_mla=g_qk_mla, w_o_win=w_o_win, w_o_nbr=w_o_nbr,
             w_o_mla=w_o_mla, w_out=w_out, w_exp_gate=w_exp_gate, w_exp_up=w_exp_up,
             w_exp_down=w_exp_down)
    depth = w_in.shape[0]
    batch, seq, _ = x_prompt.shape
    dec_batch, dec_seq, _ = x_sample.shape
    past = cache_win_k.shape[2]

    n_c = 1 + dec_batch
    c_rows = -(-n_c // 8) * 8
    c_all = jnp.concatenate([c_ctx[None], c, jnp.zeros((c_rows - n_c, D_MODEL), F32)], axis=0)
    mods = _ada(c_all, w_ada, b_ada).reshape(depth, c_rows, 6, D_MODEL)
    mods = jnp.pad(mods, ((0, 0), (0, 0), (0, 2), (0, 0)))

    mats = (_block_ones(512, 64, 512, 64, 1.0 / HEAD_DIM), _block_ones(512, 64, 512, 64),
            _block_ones(256, 32, 512, 64), _block_ones(512, 64, 256, 32), _block_ones(256, 32, 256, 32))
    tabs = _rope_tables(dec_seq, 64) + _rope_tables(dec_seq, 32)
    sink = win_sink.astype(F32) * LOG2E
    w_r_t = w_router.T.astype(F32)
    b_r = b_router.astype(F32).reshape(N_EXPERTS, 1)
    layers = [_layer_params(i, p) for i in range(depth)]

    def ffn(x1_route, mod, rows_per_mod, lp):
        x1, route = x1_route
        return _moe(x1, route, mod, rows_per_mod, lp["gains"], lp["wg"], lp["wu"], lp["wd"])

    def merge(x, oa, ob, oc, mod, rows_per_mod, lp):
        return _merge(x, oa, ob, oc, mod, rows_per_mod, lp["gains"], lp["w_gate"], lp["wo_a"],
                      lp["wo_b"], lp["wo_c"], lp["w_out"], w_r_t, b_r)

    n_ctx = batch * seq
    x = x_prompt.reshape(n_ctx, D_MODEL)
    states = []
    for i, lp in enumerate(layers):
        mod = mods[i, 0:1]
        outs = _inproj(x, mod, n_ctx, lp["gains"], lp["w_a"], lp["w_uq"], lp["w_ukv"], mats, None,
                       seq, True)
        oa, ob, oc = _ctx_attn(i, sink, seq, *outs[:9])
        states.append(outs[9:])
        x = ffn(merge(x, oa, ob, oc, mod, n_ctx, lp), mod, n_ctx, lp)
    y_prompt = x.reshape(batch, seq, D_MODEL)

    n_lat = dec_batch * dec_seq
    x = x_sample.reshape(n_lat, D_MODEL)
    cwk = cache_win_k.reshape(dec_batch, depth, past, 128).astype(BF16)
    cwv = cache_win_v.reshape(dec_batch, depth, past, 128).astype(BF16)
    cnk = cache_nbr_k.reshape(dec_batch, depth, past, 512).astype(BF16)
    cnv = cache_nbr_v.reshape(dec_batch, depth, past, 512).astype(BF16)
    for i, lp in enumerate(layers):
        mod = mods[i, 1:1 + dec_batch]
        qa, ka, va, qb, kb, vb, qc, kc, vc = _inproj(
            x, mod, dec_seq, lp["gains"], lp["w_a"], lp["w_uq"], lp["w_ukv"], mats, tabs, dec_seq, False)
        kr_t = jnp.tile(cache_mla_krope[:, i].reshape(dec_batch * past, ROPE_DIM), (1, MLA_HEADS))
        kc_c, vc_c = _mla_cache_keys(cache_mla_ckv[:, i].reshape(dec_batch * past, KV_LORA), kr_t,
                                     lp["gains"], lp["w_ukv"], mats[1], mats[3], mats[4])
        r3 = lambda a: a.reshape(dec_batch, dec_seq, a.shape[-1])
        oa = _win_attn(i, sink, r3(qa), r3(ka), r3(va), cwk, cwv)
        ob = _nbr_attn(i, r3(qb), r3(kb), r3(vb), cnk, cnv, lp["nbr_table"])
        oc = _mla_attn(r3(qc), r3(kc), r3(vc), kc_c.reshape(dec_batch, past, 1024),
                       vc_c.reshape(dec_batch, past, 512))
        flat = lambda a: a.reshape(n_lat, 512)
        x = ffn(merge(x, flat(oa), flat(ob), flat(oc), mod, dec_seq, lp), mod, dec_seq, lp)
    y_sample = x.reshape(dec_batch, dec_seq, D_MODEL)

    def stack(k, shape):
        return jnp.stack([s[k].reshape((batch, seq) + shape) for s in states], axis=1)

    return (y_prompt, y_sample,
            stack(0, (WIN_KV_HEADS, HEAD_DIM)), stack(1, (WIN_KV_HEADS, HEAD_DIM)),
            stack(2, (NBR_HEADS, HEAD_DIM)), stack(3, (NBR_HEADS, HEAD_DIM)),
            stack(4, (KV_LORA,)), stack(5, (ROPE_DIM,)))
```

```python
import functools

import numpy as np
import jax
import jax.numpy as jnp
from jax import lax
from jax.experimental import pallas as pl
from jax.experimental.pallas import tpu as pltpu

D_MODEL = 1024
GRID_W = 64
HEAD_DIM = 64
WIN_HEADS = 8
WIN_KV_HEADS = 2
WINDOW = 128
NBR_HEADS = 8
NBR_ROWS = 8
NBR_COLS = 16
MLA_HEADS = 8
Q_LORA = 256
KV_LORA = 128
NOPE_DIM = 64
ROPE_DIM = 32
V_DIM = 64
QK_DIM = NOPE_DIM + ROPE_DIM
N_EXPERTS = 16
N_GROUPS = 4
EXPERTS_PER_GROUP = 4
D_FF = 512
ROPE_BASE = 10000.0
EPS = 1e-6

LANES = 128
LOG2E = 1.4426950408889634
NEG = -1e30
VMEM_LIMIT = 56 * 1024 * 1024

F32 = jnp.float32
BF16 = jnp.bfloat16

C_QA, C_KA, C_VA, C_QB, C_KB, C_VB, C_CQ, C_CKV, C_KR, C_END = (
    0, 512, 640, 768, 1280, 1792, 2304, 2560, 2688, 2944)
WIN_Q_ORDER = (0, 4, 1, 5, 2, 6, 3, 7)

MOE_TILE = 1024
MOE_CHUNK = 256


def _cparams(sem):
    return pltpu.CompilerParams(dimension_semantics=sem, vmem_limit_bytes=VMEM_LIMIT)


def _dot(a, b):
    return jnp.dot(a, b, preferred_element_type=F32)


def _dot_nt(a, b):
    return lax.dot_general(a, b, (((1,), (1,)), ((), ())), preferred_element_type=F32)


def _split(x):
    hi = x.astype(BF16)
    lo = (x - hi.astype(F32)).astype(BF16)
    return hi, lo


def _gsum(x2, bmat):
    return _dot(x2.astype(BF16), bmat)


def _tile_lanes(t, width):
    reps = width // t.shape[-1]
    return t if reps == 1 else jnp.concatenate([t] * reps, axis=-1)


def _rotate(x, cos, sin_a, sin_b, half):
    w = x.shape[-1]
    up = pltpu.roll(x, w - half, 1)
    dn = pltpu.roll(x, half, 1)
    return (x * _tile_lanes(cos, w) + up * _tile_lanes(sin_a, w) + dn * _tile_lanes(sin_b, w))


def _norm_mod(x, gain, scale, shift):
    ms = jnp.mean(x * x, axis=-1, keepdims=True)
    return (x * lax.rsqrt(ms + EPS) * gain) * (1.0 + scale) + shift


def _ada_body(c_ref, w_ref, b_ref, o_ref):
    c = c_ref[...]
    a = c * (1.0 / (1.0 + jnp.exp(-c)))
    a_hi, a_lo = _split(a)
    w_hi, w_lo = _split(w_ref[0])
    o_ref[0] = _dot(a_hi, w_hi) + _dot(a_hi, w_lo) + _dot(a_lo, w_hi) + b_ref[0]


def _ada(c_all, w_ada, b_ada):
    depth = w_ada.shape[0]
    rows = c_all.shape[0]
    tn = 1536
    return pl.pallas_call(
        _ada_body,
        grid=(depth, 6 * D_MODEL // tn),
        in_specs=[pl.BlockSpec((rows, D_MODEL), lambda l, j: (0, 0)),
                  pl.BlockSpec((1, D_MODEL, tn), lambda l, j: (l, 0, j)),
                  pl.BlockSpec((1, 1, tn), lambda l, j: (l, 0, j))],
        out_specs=pl.BlockSpec((1, rows, tn), lambda l, j: (l, 0, j)),
        out_shape=jax.ShapeDtypeStruct((depth, rows, 6 * D_MODEL), F32),
        compiler_params=_cparams(("parallel", "parallel")),
        name="ada",
    )(c_all, w_ada, b_ada.reshape(depth, 1, 6 * D_MODEL))


def _mla_key_tail(ckvn_b, kr_t, g, wukv_ref, bnn, bnr, brr, rope_tabs, kc_ref, vc_ref):
    kv = _dot(ckvn_b, wukv_ref[...])
    kn = kv[:, 0:512]
    vc_ref[...] = kv[:, 512:1024].astype(BF16)
    kn2 = kn * kn
    kr2 = kr_t * kr_t
    kr_sum32 = _gsum(kr2, brr)
    ssn = (_gsum(kn2, bnn) + jnp.concatenate([kr_sum32, kr_sum32], axis=-1)) * (1.0 / QK_DIM)
    ssr = (_gsum(kn2, bnr) + kr_sum32) * (1.0 / QK_DIM)
    kn = kn * lax.rsqrt(ssn + EPS) * g[5:6, 0:512]
    kr = kr_t * lax.rsqrt(ssr + EPS) * g[5:6, 512:768]
    if rope_tabs is not None:
        kr = _rotate(kr, *rope_tabs, 8)
    for p in range(4):
        kc_ref[:, 256 * p:256 * p + 128] = kn[:, 128 * p:128 * p + 128].astype(BF16)
        q4 = 128 * (p // 2)
        kc_ref[:, 256 * p + 128:256 * p + 256] = kr[:, q4:q4 + 128].astype(BF16)


def _inproj_body(rope, states, *refs):
    (x_ref, mod_ref, g_ref, w_ref, wuq_ref, wukv_ref, b64_ref, bnn_ref, brn_ref, bnr_ref,
     brr_ref) = refs[:11]
    refs = refs[11:]
    if rope:
        tabs_w = tuple(r[...] for r in refs[0:3])
        tabs_m = tuple(r[...] for r in refs[3:6])
        refs = refs[6:]
    else:
        tabs_w = tabs_m = None
    qa_ref, ka_ref, va_ref, qb_ref, kb_ref, vb_ref, qc_ref, kc_ref, vc_ref = refs[:9]
    st = refs[9:]

    g = g_ref[...]
    mod = mod_ref[0]
    hb = _norm_mod(x_ref[...], g[0:1], mod[1:2], mod[0:1]).astype(BF16)

    def proj(a, b):
        return _dot(hb, w_ref[:, a:b])

    b64 = b64_ref[...]

    def head_norm(z, bmat, gain):
        return z * lax.rsqrt(_gsum(z * z, bmat) + EPS) * gain

    qa = head_norm(proj(C_QA, C_KA), b64, g[1:2, 0:512])
    ka = head_norm(proj(C_KA, C_VA), b64[0:128, 0:128], g[1:2, 512:640])
    va = proj(C_VA, C_QB)
    if states:
        st[0][...] = ka
        st[1][...] = va
    if rope:
        qa = _rotate(qa, *tabs_w, 16)
        ka = _rotate(ka, *tabs_w, 16)
    qa_ref[...] = qa.astype(BF16)
    ka_ref[...] = ka.astype(BF16)
    va_ref[...] = va.astype(BF16)

    qb = head_norm(proj(C_QB, C_KB), b64, g[2:3, 0:512])
    kb = head_norm(proj(C_KB, C_VB), b64, g[2:3, 512:1024])
    vb = proj(C_VB, C_CQ)
    if states:
        st[2][...] = kb
        st[3][...] = vb
    qb_ref[...] = qb.astype(BF16)
    kb_ref[...] = kb.astype(BF16)
    vb_ref[...] = vb.astype(BF16)

    cq = proj(C_CQ, C_CKV)
    cqn = cq * lax.rsqrt(jnp.mean(cq * cq, axis=-1, keepdims=True) + EPS) * g[3:4, 0:256]
    qq = _dot(cqn.astype(BF16), wuq_ref[...])
    qn, qr = qq[:, 0:512], qq[:, 512:768]
    qn2, qr2 = qn * qn, qr * qr
    bnn, brn, bnr, brr = bnn_ref[...], brn_ref[...], bnr_ref[...], brr_ref[...]
    ssn = (_gsum(qn2, bnn) + _gsum(qr2, brn)) * (1.0 / QK_DIM)
    ssr = (_gsum(qn2, bnr) + _gsum(qr2, brr)) * (1.0 / QK_DIM)
    qn = qn * lax.rsqrt(ssn + EPS) * g[4:5, 0:512]
    qr = qr * lax.rsqrt(ssr + EPS) * g[4:5, 512:768]
    if rope:
        qr = _rotate(qr, *tabs_m, 8)
    for p in range(4):
        qc_ref[:, 256 * p:256 * p + 128] = qn[:, 128 * p:128 * p + 128].astype(BF16)
        q4 = 128 * (p // 2)
        qc_ref[:, 256 * p + 128:256 * p + 256] = qr[:, q4:q4 + 128].astype(BF16)

    ckv = proj(C_CKV, C_KR)
    ckvn = ckv * lax.rsqrt(jnp.mean(ckv * ckv, axis=-1, keepdims=True) + EPS) * g[3:4, 256:384]
    kr_t = proj(C_KR, C_END)
    if states:
        st[4][...] = ckvn
        st[5][...] = kr_t[:, 0:ROPE_DIM]
    _mla_key_tail(ckvn.astype(BF16), kr_t, g, wukv_ref, bnn, bnr, brr, tabs_m, kc_ref, vc_ref)


def _const_spec(shape):
    nd = len(shape)
    return pl.BlockSpec(shape, lambda i, _nd=nd: (0,) * _nd)


def _inproj(x, mods, rows_per_mod, gains, w_a, w_uq, w_ukv, mats, rope_tabs, seq_len, states):
    n = x.shape[0]
    tm = min(512, n)
    rope = rope_tabs is not None
    row = lambda w: pl.BlockSpec((tm, w), lambda i: (i, 0))
    in_specs = [row(D_MODEL),
                pl.BlockSpec((1, 8, D_MODEL), lambda i: ((i * tm) // rows_per_mod, 0, 0)),
                _const_spec(gains.shape), _const_spec(w_a.shape), _const_spec(w_uq.shape),
                _const_spec(w_ukv.shape)] + [_const_spec(m.shape) for m in mats]
    args = [x, mods, gains, w_a, w_uq, w_ukv, *mats]
    if rope:
        tiles_per_seq = seq_len // tm
        in_specs += [pl.BlockSpec((tm, LANES), lambda i: (i % tiles_per_seq, 0))] * 6
        args += list(rope_tabs)
    widths = [512, 128, 128, 512, 512, 512, 1024, 1024, 512]
    out_shape = [jax.ShapeDtypeStruct((n, w), BF16) for w in widths]
    out_specs = [row(w) for w in widths]
    if states:
        swidths = [128, 128, 512, 512, KV_LORA, ROPE_DIM]
        out_shape += [jax.ShapeDtypeStruct((n, w), F32) for w in swidths]
        out_specs += [row(w) for w in swidths]
    return pl.pallas_call(
        functools.partial(_inproj_body, rope, states),
        grid=(n // tm,), in_specs=in_specs, out_specs=out_specs, out_shape=out_shape,
        compiler_params=_cparams(("parallel",)),
        name="inproj_lat" if rope else "inproj_ctx",
    )(*args)


def _mla_cache_body(ckv_ref, kr_ref, g_ref, wukv_ref, bnn_ref, bnr_ref, brr_ref, kc_ref, vc_ref):
    _mla_key_tail(ckv_ref[...].astype(BF16), kr_ref[...], g_ref[...], wukv_ref, bnn_ref[...],
                  bnr_ref[...], brr_ref[...], None, kc_ref, vc_ref)


def _mla_cache_keys(ckv, kr_t, gains, w_ukv, bnn, bnr, brr):
    n = ckv.shape[0]
    tm = min(512, n)
    row = lambda w: pl.BlockSpec((tm, w), lambda i: (i, 0))
    return pl.pallas_call(
        _mla_cache_body,
        grid=(n // tm,),
        in_specs=[row(KV_LORA), row(256), _const_spec(gains.shape), _const_spec(w_ukv.shape),
                  _const_spec(bnn.shape), _const_spec(bnr.shape), _const_spec(brr.shape)],
        out_specs=[row(1024), row(512)],
        out_shape=[jax.ShapeDtypeStruct((n, 1024), BF16), jax.ShapeDtypeStruct((n, 512), BF16)],
        compiler_params=_cparams(("parallel",)),
        name="mla_cache_keys",
    )(ckv, kr_t, gains, w_ukv, bnn, bnr, brr)


def _lane_mask(width, ranges):
    lane = lax.broadcasted_iota(jnp.int32, (1, width), 1)
    m = None
    for lo, hi in ranges:
        c = (lane >= lo) & (lane < hi)
        m = c if m is None else (m | c)
    return jnp.where(m, 1.0, 0.0).astype(BF16)


def _stack_heads(q, mask0, mask1):
    return jnp.concatenate([q * mask0, q * mask1], axis=0)


def _lane_tiles(x):
    return [x[:, j:j + LANES] for j in range(0, x.shape[1], LANES)]


def _softmax_pv(scores, values, sink=None):
    rows = scores[0].shape[0]
    mp = None
    for s in scores:
        for t in _lane_tiles(s):
            mp = t if mp is None else jnp.maximum(mp, t)
    base = sink if sink is not None else jnp.full((rows, LANES), NEG, F32)
    m = jnp.maximum(base, jnp.max(mp, axis=-1, keepdims=True))
    lp = None
    ps = []
    for s in scores:
        p = jnp.exp2(s - _tile_lanes(m, s.shape[1]))
        for t in _lane_tiles(p):
            lp = t if lp is None else lp + t
        ps.append(p.astype(BF16))
    p_all = ps[0] if len(ps) == 1 else jnp.concatenate(ps, axis=-1)
    v_all = values[0] if len(values) == 1 else jnp.concatenate(values, axis=0)
    o = _dot(p_all, v_all)
    if sink is not None:
        lane = lax.broadcasted_iota(jnp.int32, (rows, LANES), 1)
        lp = lp + jnp.where(lane == 0, jnp.exp2(sink - m), 0.0)
    return o / jnp.sum(lp, axis=-1, keepdims=True)


def _merge_heads(o, tq):
    lane = lax.broadcasted_iota(jnp.int32, (tq, LANES), 1)
    return jnp.where(lane < HEAD_DIM, o[0:tq], o[tq:2 * tq])


def _mla_masks(p_mod2):
    lane = lax.broadcasted_iota(jnp.int32, (1, 256), 1)
    r0 = 128 + 32 * (2 * p_mod2)
    m0 = (lane < 64) | ((lane >= r0) & (lane < r0 + 32))
    m1 = ((lane >= 64) & (lane < 128)) | ((lane >= r0 + 32) & (lane < r0 + 64))
    return (jnp.where(m0, 1.0, 0.0).astype(BF16), jnp.where(m1, 1.0, 0.0).astype(BF16))


def _sink_col(sink_ref, layer, h0, h1, tq):
    row = lax.broadcasted_iota(jnp.int32, (2 * tq, LANES), 0)
    return jnp.where(row < tq, sink_ref[layer, h0], sink_ref[layer, h1])


def _ctx_attn_body(layer, sink_ref, qa_ref, ka_ref, va_ref, qb_ref, kb_ref, vb_ref, qc_ref, kc_ref,
                   vc_ref, oa_ref, ob_ref, oc_ref):
    tq = qa_ref.shape[0]
    lo = _lane_mask(LANES, [(0, 64)])
    hi = _lane_mask(LANES, [(64, 128)])
    ka, va = ka_ref[...], va_ref[...]
    for j in range(4):
        sl = slice(128 * j, 128 * j + 128)
        qs = _stack_heads(qa_ref[:, sl], lo, hi)
        sink = _sink_col(sink_ref, layer, j, 4 + j, tq)
        o = _softmax_pv([_dot_nt(qs, ka)], [va], sink)
        oa_ref[:, sl] = _merge_heads(o, tq).astype(BF16)

        qs = _stack_heads(qb_ref[:, sl], lo, hi)
        o = _softmax_pv([_dot_nt(qs, kb_ref[:, sl])], [vb_ref[:, sl]])
        ob_ref[:, sl] = _merge_heads(o, tq).astype(BF16)

        m0, m1 = _mla_masks(j % 2)
        s2 = slice(256 * j, 256 * j + 256)
        qs = _stack_heads(qc_ref[:, s2], m0, m1)
        o = _softmax_pv([_dot_nt(qs, kc_ref[:, s2])], [vc_ref[:, sl]])
        oc_ref[:, sl] = _merge_heads(o, tq).astype(BF16)


def _ctx_attn(layer, sink, seq, qa, ka, va, qb, kb, vb, qc, kc, vc):
    n = qa.shape[0]
    row = lambda w: pl.BlockSpec((seq, w), lambda b: (b, 0))
    ins = [qa, ka, va, qb, kb, vb, qc, kc, vc]
    return pl.pallas_call(
        functools.partial(_ctx_attn_body, layer),
        grid=(n // seq,),
        in_specs=[pl.BlockSpec(memory_space=pltpu.SMEM)] + [row(a.shape[1]) for a in ins],
        out_specs=[row(512)] * 3,
        out_shape=[jax.ShapeDtypeStruct((n, 512), BF16)] * 3,
        compiler_params=_cparams(("parallel",)),
        name="ctx_attn",
    )(sink, *ins)


def _win_body(layer, sink_ref, q_ref, k_ref, v_ref, kc_ref, vc_ref, o_ref):
    tq = q_ref.shape[1]
    seq = k_ref.shape[1]
    kw = 3 * tq
    i = pl.program_id(1)
    kstart = pl.multiple_of(jnp.clip((i - 1) * tq, 0, seq - kw), tq)
    k_all = jnp.concatenate([k_ref[0, pl.ds(kstart, kw), :], kc_ref[0, 0]], axis=0)
    v_all = jnp.concatenate([v_ref[0, pl.ds(kstart, kw), :], vc_ref[0, 0]], axis=0)
    q_pos = i * tq + lax.broadcasted_iota(jnp.int32, (2 * tq, kw), 0) % tq
    k_pos = kstart + lax.broadcasted_iota(jnp.int32, (2 * tq, kw), 1)
    band = jnp.abs(q_pos - k_pos) <= WINDOW
    lo = _lane_mask(LANES, [(0, 64)])
    hi = _lane_mask(LANES, [(64, 128)])
    qs = jnp.concatenate([_stack_heads(q_ref[0, :, 128 * j:128 * j + 128], lo, hi) for j in range(4)],
                         axis=0)
    s = _dot_nt(qs, k_all)
    for j in range(4):
        sb = s[2 * tq * j:2 * tq * (j + 1)]
        s_band = jnp.where(band, sb[:, 0:kw], NEG)
        sink = _sink_col(sink_ref, layer, j, 4 + j, tq)
        o = _softmax_pv([s_band, sb[:, kw:]], [v_all], sink)
        o_ref[0, :, 128 * j:128 * j + 128] = _merge_heads(o, tq).astype(BF16)


def _win_attn(layer, sink, q, k, v, kc, vc):
    b, seq, _ = q.shape
    tq = 128
    past = kc.shape[2]
    return pl.pallas_call(
        functools.partial(_win_body, layer),
        grid=(b, seq // tq),
        in_specs=[pl.BlockSpec(memory_space=pltpu.SMEM),
                  pl.BlockSpec((1, tq, 512), lambda bi, i: (bi, i, 0)),
                  pl.BlockSpec((1, seq, 128), lambda bi, i: (bi, 0, 0)),
                  pl.BlockSpec((1, seq, 128), lambda bi, i: (bi, 0, 0)),
                  pl.BlockSpec((1, 1, past, 128), lambda bi, i: (bi, layer, 0, 0)),
                  pl.BlockSpec((1, 1, past, 128), lambda bi, i: (bi, layer, 0, 0))],
        out_specs=pl.BlockSpec((1, tq, 512), lambda bi, i: (bi, i, 0)),
        out_shape=jax.ShapeDtypeStruct((b, seq, 512), BF16),
        compiler_params=_cparams(("parallel", "arbitrary")),
        name="win_attn",
    )(sink, q, k, v, kc, vc)


NBR_TILE_ROWS = 4
NBR_WIN_ROWS = NBR_TILE_ROWS + NBR_ROWS
NBR_TAB_PAD = NBR_WIN_ROWS - NBR_ROWS


def _nbr_body(rows, q_ref, k_ref, v_ref, kc_ref, vc_ref, tab_ref, o_ref):
    tq = NBR_TILE_ROWS * GRID_W
    kw = NBR_WIN_ROWS * GRID_W
    i = pl.program_id(1)
    r0 = NBR_TILE_ROWS * i
    ws = jnp.clip(r0 - NBR_ROWS // 2, 0, rows - NBR_WIN_ROWS)
    kstart = pl.multiple_of(ws * GRID_W, LANES)
    q_row = r0 + lax.broadcasted_iota(jnp.int32, (2 * tq, kw), 0) % tq // GRID_W
    k_row = ws + lax.broadcasted_iota(jnp.int32, (2 * tq, kw), 1) // GRID_W
    rs = jnp.clip(q_row - NBR_ROWS // 2, 0, rows - NBR_ROWS)
    valid = (k_row >= rs) & (k_row < rs + NBR_ROWS)
    lo = _lane_mask(LANES, [(0, 64)])
    hi = _lane_mask(LANES, [(64, 128)])
    for j in range(4):
        sl = slice(128 * j, 128 * j + 128)
        qs = _stack_heads(q_ref[0, :, sl], lo, hi)
        k = k_ref[0, pl.ds(kstart, kw), sl]
        v = v_ref[0, pl.ds(kstart, kw), sl]
        pieces = []
        for h in (2 * j, 2 * j + 1):
            for ql in range(NBR_TILE_ROWS):
                d0 = ws - r0 - ql + (NBR_ROWS - 1) + NBR_TAB_PAD
                pieces.append(jnp.concatenate(
                    [tab_ref[h, d0 + 2 * m] for m in range(NBR_WIN_ROWS // 2)], axis=-1))
        bias = jnp.concatenate(pieces, axis=0)
        s_nb = jnp.where(valid, _dot_nt(qs, k) + bias, NEG)
        s_ctx = _dot_nt(qs, kc_ref[0, 0, :, sl])
        o = _softmax_pv([s_nb, s_ctx], [v, vc_ref[0, 0, :, sl]])
        o_ref[0, :, sl] = _merge_heads(o, tq).astype(BF16)


def _nbr_attn(layer, q, k, v, kc, vc, table):
    b, seq, _ = q.shape
    rows = seq // GRID_W
    tq = NBR_TILE_ROWS * GRID_W
    past = kc.shape[2]
    return pl.pallas_call(
        functools.partial(_nbr_body, rows),
        grid=(b, seq // tq),
        in_specs=[pl.BlockSpec((1, tq, 512), lambda bi, i: (bi, i, 0)),
                  pl.BlockSpec((1, seq, 512), lambda bi, i: (bi, 0, 0)),
                  pl.BlockSpec((1, seq, 512), lambda bi, i: (bi, 0, 0)),
                  pl.BlockSpec((1, 1, past, 512), lambda bi, i: (bi, layer, 0, 0)),
                  pl.BlockSpec((1, 1, past, 512), lambda bi, i: (bi, layer, 0, 0)),
                  pl.BlockSpec(table.shape, lambda bi, i: (0, 0, 0, 0))],
        out_specs=pl.BlockSpec((1, tq, 512), lambda bi, i: (bi, i, 0)),
        out_shape=jax.ShapeDtypeStruct((b, seq, 512), BF16),
        compiler_params=_cparams(("parallel", "arbitrary")),
        name="nbr_attn",
    )(q, k, v, kc, vc, table)


def _nbr_bias_table(rel_bias):
    col = np.arange(GRID_W)
    cs = np.clip(col - NBR_COLS // 2, 0, GRID_W - NBR_COLS)
    kc = np.arange(GRID_W)
    ok = (kc[None, :] >= cs[:, None]) & (kc[None, :] < cs[:, None] + NBR_COLS)
    dc = np.clip(kc[None, :] - col[:, None] + (NBR_COLS - 1), 0, 2 * NBR_COLS - 2)
    t = rel_bias.astype(F32)[:, :, dc] * LOG2E
    t = jnp.where(jnp.asarray(ok)[None, None], t, NEG)
    t = jnp.pad(t, ((0, 0), (NBR_TAB_PAD, NBR_TAB_PAD), (0, 0), (0, 0)))
    return jnp.concatenate([t[:, :-1], t[:, 1:]], axis=-1)


MLA_KEY_CHUNK = 512
MLA_ROW_BLOCK = 256


def _mla_body(q_ref, kl_ref, kc_ref, vl_ref, vc_ref, o_ref, qs_sc, m_sc, l_sc, acc_sc):
    tq = q_ref.shape[1]
    rows = 2 * tq
    seq = kl_ref.shape[1]
    past = kc_ref.shape[1]
    tk = min(MLA_KEY_CHUNK, past)
    rb = min(MLA_ROW_BLOCK, rows)
    m0, m1 = _mla_masks(pl.program_id(1) % 2)
    qs_sc[...] = _stack_heads(q_ref[0], m0, m1)
    m_sc[...] = jnp.full(m_sc.shape, NEG, F32)
    l_sc[...] = jnp.zeros(l_sc.shape, F32)
    acc_sc[...] = jnp.zeros(acc_sc.shape, F32)

    def step(k, v):
        s = _dot_nt(qs_sc[...], k)
        for r in range(rows // rb):
            sl = slice(r * rb, (r + 1) * rb)
            sb = s[sl]
            m_prev = m_sc[sl]
            m_new = jnp.maximum(m_prev, jnp.max(sb, axis=-1, keepdims=True))
            alpha = jnp.exp2(m_prev - m_new)
            p = jnp.exp2(sb - _tile_lanes(m_new, tk))
            psum = p[:, 0:LANES]
            for j in range(1, tk // LANES):
                psum = psum + p[:, j * LANES:(j + 1) * LANES]
            l_sc[sl] = alpha * l_sc[sl] + psum
            acc_sc[sl] = alpha * acc_sc[sl] + _dot(p.astype(BF16), v)
            m_sc[sl] = m_new

    for c in range(seq // tk):
        step(kl_ref[0, c * tk:(c + 1) * tk, :], vl_ref[0, c * tk:(c + 1) * tk, :])
    for c in range(past // tk):
        step(kc_ref[0, c * tk:(c + 1) * tk, :], vc_ref[0, c * tk:(c + 1) * tk, :])
    l = jnp.sum(l_sc[...], axis=-1, keepdims=True)
    o_ref[0] = _merge_heads(acc_sc[...] / l, tq).astype(BF16)


def _mla_attn(q, kl, vl, kc, vc):
    b, seq, _ = q.shape
    past = kc.shape[1]
    tq = min(512, seq)
    return pl.pallas_call(
        _mla_body,
        grid=(b, 4, seq // tq),
        in_specs=[pl.BlockSpec((1, tq, 256), lambda bi, p, qi: (bi, qi, p)),
                  pl.BlockSpec((1, seq, 256), lambda bi, p, qi: (bi, 0, p)),
                  pl.BlockSpec((1, past, 256), lambda bi, p, qi: (bi, 0, p)),
                  pl.BlockSpec((1, seq, 128), lambda bi, p, qi: (bi, 0, p)),
                  pl.BlockSpec((1, past, 128), lambda bi, p, qi: (bi, 0, p))],
        out_specs=pl.BlockSpec((1, tq, 128), lambda bi, p, qi: (bi, qi, p)),
        out_shape=jax.ShapeDtypeStruct((b, seq, 512), BF16),
        scratch_shapes=[pltpu.VMEM((2 * tq, 256), BF16), pltpu.VMEM((2 * tq, LANES), F32),
                        pltpu.VMEM((2 * tq, LANES), F32), pltpu.VMEM((2 * tq, LANES), F32)],
        compiler_params=_cparams(("parallel", "parallel", "arbitrary")),
        name="mla_attn",
    )(q, kl, kc, vl, vc)


def _merge_body(x_ref, oa_ref, ob_ref, oc_ref, mod_ref, g_ref, wg_ref, woa_ref, wob_ref, woc_ref,
                wout_ref, wr_ref, br_ref, x1_ref, route_ref):
    x = x_ref[...]
    g = g_ref[...]
    mod = mod_ref[0]
    hb = _norm_mod(x, g[0:1], mod[1:2], mod[0:1]).astype(BF16)
    m = None
    for br, (o_ref, wo_ref) in enumerate(((oa_ref, woa_ref), (ob_ref, wob_ref), (oc_ref, woc_ref))):
        z = _dot(hb, wg_ref[:, D_MODEL * br:D_MODEL * (br + 1)])
        gate = 1.0 / (1.0 + jnp.exp(-z))
        t = gate * _dot(o_ref[...], wo_ref[...])
        m = t if m is None else m + t
    y = _dot(m.astype(BF16), wout_ref[...])
    x1 = x + mod[2:3] * y
    x1_ref[...] = x1

    h2 = _norm_mod(x1, g[6:7], mod[4:5], mod[3:4])
    h_hi, h_lo = _split(h2)
    wr = wr_ref[...]
    w_hi, w_lo = _split(wr)
    logits = _dot_nt(w_hi, h_hi) + _dot_nt(w_hi, h_lo) + _dot_nt(w_lo, h_hi)
    score = 1.0 / (1.0 + jnp.exp(-logits))
    sel = score + br_ref[...]
    sel_r = [sel[e:e + 1] for e in range(N_EXPERTS)]
    sc_r = [score[e:e + 1] for e in range(N_EXPERTS)]
    picked = []
    for e in range(N_EXPERTS):
        grp, a = divmod(e, EXPERTS_PER_GROUP)
        rank = None
        for bb in range(EXPERTS_PER_GROUP):
            if bb == a:
                continue
            o = sel_r[grp * EXPERTS_PER_GROUP + bb]
            beats = (o >= sel_r[e]) if bb < a else (o > sel_r[e])
            r = jnp.where(beats, 1.0, 0.0)
            rank = r if rank is None else rank + r
        picked.append(rank < 2.0)
    gscore = []
    for grp in range(N_GROUPS):
        tot = None
        for a in range(EXPERTS_PER_GROUP):
            e = grp * EXPERTS_PER_GROUP + a
            t = jnp.where(picked[e], sel_r[e], 0.0)
            tot = t if tot is None else tot + t
        gscore.append(tot)
    best = jnp.zeros_like(gscore[0])
    best_v = gscore[0]
    for grp in range(1, N_GROUPS):
        upd = gscore[grp] > best_v
        best = jnp.where(upd, float(grp), best)
        best_v = jnp.where(upd, gscore[grp], best_v)
    cw = []
    for a in range(EXPERTS_PER_GROUP):
        tot = None
        for grp in range(N_GROUPS):
            e = grp * EXPERTS_PER_GROUP + a
            t = jnp.where((best == float(grp)) & picked[e], sc_r[e], 0.0)
            tot = t if tot is None else tot + t
        cw.append(tot)
    den = cw[0] + cw[1] + cw[2] + cw[3]
    for a in range(EXPERTS_PER_GROUP):
        route_ref[a:a + 1, :] = cw[a] / den
    route_ref[4:5, :] = best
    route_ref[5:8, :] = jnp.zeros((3, best.shape[1]), F32)


def _merge(x, oa, ob, oc, mods, rows_per_mod, gains, w_gate, wo_a, wo_b, wo_c, w_out, w_r_t, b_r):
    n = x.shape[0]
    tm = min(512, n)
    row = lambda w: pl.BlockSpec((tm, w), lambda i: (i, 0))
    consts = [gains, w_gate, wo_a, wo_b, wo_c, w_out, w_r_t, b_r]
    return pl.pallas_call(
        _merge_body,
        grid=(n // tm,),
        in_specs=[row(D_MODEL), row(512), row(512), row(512),
                  pl.BlockSpec((1, 8, D_MODEL), lambda i: ((i * tm) // rows_per_mod, 0, 0))]
                 + [_const_spec(c.shape) for c in consts],
        out_specs=[row(D_MODEL), pl.BlockSpec((8, tm), lambda i: (0, i))],
        out_shape=[jax.ShapeDtypeStruct((n, D_MODEL), F32), jax.ShapeDtypeStruct((8, n), F32)],
        compiler_params=_cparams(("parallel",)),
        name="merge",
    )(x, oa, ob, oc, mods, *consts)


def _moe_body(max_chunks, cg_ref, nch_ref, x1_ref, side_ref, idxr_ref, idxc_ref, mod_ref, g_ref,
              wg_ref, wu_ref, wd_ref, o_ref, h2_sc):
    t = pl.program_id(0)
    c = pl.program_id(1)
    tile = x1_ref.shape[0]
    chunk = idxr_ref.shape[2]

    @pl.when(c == 0)
    def _():
        mod = mod_ref[0]
        h2_sc[...] = _norm_mod(x1_ref[...], g_ref[6:7, :], mod[4:5], mod[3:4]).astype(BF16)
        o_ref[...] = jnp.zeros(o_ref.shape, F32)

    @pl.when(c < nch_ref[t])
    def _():
        idx_col = _tile_lanes(idxc_ref[0], tile)
        sel = jnp.where(idx_col == lax.broadcasted_iota(jnp.int32, (chunk, tile), 1), 1.0, 0.0)
        sel = sel.astype(BF16)
        idx_row = idxr_ref[0]
        sel_t = jnp.where(idx_row == lax.broadcasted_iota(jnp.int32, (tile, chunk), 0), 1.0, 0.0)
        sel_t = sel_t.astype(BF16)
        xg = _dot(sel, h2_sc[...]).astype(BF16)
        cwg = _dot(sel, side_ref[...])
        y = None
        for a in range(EXPERTS_PER_GROUP):
            w = cwg[:, a:a + 1] + cwg[:, 4 + a:5 + a]
            zg = _dot(xg, wg_ref[0, a])
            act = zg * (1.0 / (1.0 + jnp.exp(-zg))) * _dot(xg, wu_ref[0, a])
            ya = _dot((act * w).astype(BF16), wd_ref[0, a])
            y = ya if y is None else y + ya
        o_ref[...] += _dot(sel_t, y.astype(BF16))

    @pl.when(c == max_chunks - 1)
    def _():
        o_ref[...] = x1_ref[...] + mod_ref[0][5:6] * o_ref[...]


def _moe(x1, route, mods, rows_per_mod, gains, wg, wu, wd):
    n = x1.shape[0]
    tile = min(MOE_TILE, n)
    chunk = min(MOE_CHUNK, tile)
    nt = n // tile
    max_chunks = (tile + N_GROUPS * (chunk - 1)) // chunk

    cw = route[0:4].T
    gid = route[4].astype(jnp.int32).reshape(nt, tile)
    hi = cw.astype(BF16)
    lo = (cw - hi.astype(F32)).astype(BF16)
    side = jnp.pad(jnp.concatenate([hi, lo], axis=1), ((0, 0), (0, LANES - 8)))
    key = jnp.sort(gid * tile + jnp.arange(tile, dtype=jnp.int32)[None], axis=-1)
    stok = key % tile
    counts = jnp.sum(gid[:, :, None] == jnp.arange(N_GROUPS)[None, None], axis=1).astype(jnp.int32)
    starts = jnp.cumsum(counts, axis=-1) - counts
    nch_g = (counts + chunk - 1) // chunk
    cend = jnp.cumsum(nch_g, axis=-1)
    total = cend[:, -1]
    cidx = jnp.arange(max_chunks, dtype=jnp.int32)
    g_c = jnp.sum(cidx[None, :, None] >= cend[:, None, :], axis=-1).astype(jnp.int32)
    last = jnp.take_along_axis(g_c, jnp.maximum(total - 1, 0)[:, None], axis=1)
    valid_c = cidx[None] < total[:, None]
    g_c = jnp.where(valid_c, jnp.minimum(g_c, N_GROUPS - 1), last)
    k_in_g = cidx[None] - jnp.take_along_axis(cend - nch_g, g_c, axis=1)
    base = jnp.take_along_axis(starts, g_c, axis=1) + k_in_g * chunk
    cnt = jnp.take_along_axis(counts, g_c, axis=1)
    off = jnp.arange(chunk, dtype=jnp.int32)
    ok = valid_c[:, :, None] & ((k_in_g * chunk)[:, :, None] + off[None, None] < cnt[:, :, None])
    pos = jnp.clip(base[:, :, None] + off[None, None], 0, tile - 1).reshape(nt, max_chunks * chunk)
    idx = jnp.where(ok, jnp.take_along_axis(stok, pos, axis=1).reshape(nt, max_chunks, chunk), -1)
    idx_row = idx.reshape(nt * max_chunks, 1, chunk)
    idx_col = jnp.broadcast_to(idx.reshape(nt * max_chunks, chunk, 1), (nt * max_chunks, chunk, LANES))

    wspec = lambda shape: pl.BlockSpec(
        shape, lambda t, c, cg, nc: (cg[t * max_chunks + c], 0, 0, 0))
    grid_spec = pltpu.PrefetchScalarGridSpec(
        num_scalar_prefetch=2,
        grid=(nt, max_chunks),
        in_specs=[pl.BlockSpec((tile, D_MODEL), lambda t, c, cg, nc: (t, 0)),
                  pl.BlockSpec((tile, LANES), lambda t, c, cg, nc: (t, 0)),
                  pl.BlockSpec((1, 1, chunk), lambda t, c, cg, nc: (t * max_chunks + c, 0, 0)),
                  pl.BlockSpec((1, chunk, LANES), lambda t, c, cg, nc: (t * max_chunks + c, 0, 0)),
                  pl.BlockSpec((1, 8, D_MODEL),
                               lambda t, c, cg, nc: ((t * tile) // rows_per_mod, 0, 0)),
                  pl.BlockSpec(gains.shape, lambda t, c, cg, nc: (0, 0)),
                  wspec((1, EXPERTS_PER_GROUP, D_MODEL, D_FF)),
                  wspec((1, EXPERTS_PER_GROUP, D_MODEL, D_FF)),
                  wspec((1, EXPERTS_PER_GROUP, D_FF, D_MODEL))],
        out_specs=pl.BlockSpec((tile, D_MODEL), lambda t, c, cg, nc: (t, 0)),
        scratch_shapes=[pltpu.VMEM((tile, D_MODEL), BF16)])
    return pl.pallas_call(
        functools.partial(_moe_body, max_chunks),
        grid_spec=grid_spec,
        out_shape=jax.ShapeDtypeStruct((n, D_MODEL), F32),
        compiler_params=_cparams(("parallel", "arbitrary")),
        name="moe",
    )(g_c.reshape(-1), total.astype(jnp.int32), x1, side, idx_row, idx_col, mods, gains, wg, wu, wd)


def _block_ones(n_in, g_in, n_out, g_out, value=1.0):
    r = np.arange(n_in)[:, None] // g_in
    c = np.arange(n_out)[None, :] // g_out
    return jnp.asarray(np.where(r == c, value, 0.0), dtype=BF16)


def _rope_tables(seq, head_w):
    pos = np.arange(seq)
    rows, cols = pos // GRID_W, pos % GRID_W
    a = head_w // 2
    half = a // 2
    freqs = (ROPE_BASE ** (-np.arange(half, dtype=np.float32) / half)).astype(np.float32)
    lane = np.arange(LANES) % head_w
    within = lane % a
    first = within < half
    p = np.where((lane // a == 0)[None, :], rows[:, None], cols[:, None]).astype(np.float32)
    ang = (p * freqs[within % half][None, :]).astype(np.float32)
    cos, sin = np.cos(ang), np.sin(ang)
    return (jnp.asarray(cos, F32), jnp.asarray(np.where(first[None], -sin, 0.0), F32),
            jnp.asarray(np.where(first[None], 0.0, sin), F32))


def _tile_to(v, width):
    return jnp.tile(v, width // v.shape[0])


def _layer_params(i, p):
    w_in = p["w_in"][i]
    sp = np.cumsum((512, 128, 128, 512, 512, 512, Q_LORA, KV_LORA, ROPE_DIM))
    qa, ka, va, qb, kb, vb, cq, ckv, kr, gates = jnp.split(w_in, [int(s) for s in sp], axis=1)
    qa = qa.reshape(D_MODEL, WIN_HEADS, HEAD_DIM)[:, WIN_Q_ORDER, :].reshape(D_MODEL, 512)
    w_a = jnp.concatenate([qa, ka, va, qb, kb, vb, cq, ckv, jnp.tile(kr, (1, MLA_HEADS))],
                          axis=1).astype(BF16)
    w_uq = p["w_uq"][i].reshape(Q_LORA, MLA_HEADS, QK_DIM)
    w_uq = jnp.concatenate([w_uq[:, :, :NOPE_DIM].reshape(Q_LORA, 512),
                            w_uq[:, :, NOPE_DIM:].reshape(Q_LORA, 256)], axis=1).astype(BF16)
    w_ukv = p["w_ukv"][i].reshape(KV_LORA, MLA_HEADS, NOPE_DIM + V_DIM)
    w_ukv = jnp.concatenate([w_ukv[:, :, :NOPE_DIM].reshape(KV_LORA, 512),
                             w_ukv[:, :, NOPE_DIM:].reshape(KV_LORA, 512)], axis=1).astype(BF16)
    z = jnp.zeros((D_MODEL,), F32)
    row = lambda *parts: jnp.concatenate(list(parts) + [z])[:D_MODEL]
    q_scale = HEAD_DIM ** -0.5 * LOG2E
    c_scale = QK_DIM ** -0.5 * LOG2E
    g_mla = p["g_qk_mla"][i]
    gains = jnp.stack([
        p["g_norm_mix"][i],
        row(_tile_to(p["g_qk_win"][i, 0], 512) * q_scale, _tile_to(p["g_qk_win"][i, 1], 128)),
        row(_tile_to(p["g_qk_nbr"][i, 0], 512) * q_scale, _tile_to(p["g_qk_nbr"][i, 1], 512)),
        row(p["g_q_lora"][i], p["g_kv_lora"][i]),
        row(_tile_to(g_mla[0, :NOPE_DIM], 512) * c_scale, _tile_to(g_mla[0, NOPE_DIM:], 256) * c_scale),
        row(_tile_to(g_mla[1, :NOPE_DIM], 512), _tile_to(g_mla[1, NOPE_DIM:], 256)),
        p["g_norm_ffn"][i],
        z]).astype(F32)
    wo_a = p["w_o_win"][i].reshape(WIN_HEADS, HEAD_DIM, D_MODEL)[WIN_Q_ORDER, :, :].reshape(512, D_MODEL)
    grp = lambda w: w.astype(BF16).reshape((N_GROUPS, EXPERTS_PER_GROUP) + w.shape[1:])
    return dict(
        w_a=w_a, w_uq=w_uq, w_ukv=w_ukv, gains=gains, w_gate=gates.astype(BF16),
        wo_a=wo_a.astype(BF16), wo_b=p["w_o_nbr"][i].astype(BF16), wo_c=p["w_o_mla"][i].astype(BF16),
        w_out=p["w_out"][i].astype(BF16),
        wg=grp(p["w_exp_gate"][i]), wu=grp(p["w_exp_up"][i]), wd=grp(p["w_exp_down"][i]),
        nbr_table=_nbr_bias_table(p["nbr_rel_bias"][i]))


def kernel(x_prompt, x_sample, cache_win_k, cache_win_v, cache_nbr_k, cache_nbr_v, cache_mla_ckv, cache_mla_krope, c, c_ctx, g_norm_mix, g_norm_ffn, w_ada, b_ada, w_in, g_qk_win, win_sink, g_qk_nbr, nbr_rel_bias, g_q_lora, g_kv_lora, w_uq, w_ukv, g_qk_mla, w_o_win, w_o_nbr, w_o_mla, w_out, w_router, b_router, w_exp_gate, w_exp_up, w_exp_down):
    p = dict(g_norm_mix=g_norm_mix, g_norm_ffn=g_norm_ffn, w_in=w_in, g_qk_win=g_qk_win,
             g_qk_nbr=g_qk_nbr, nbr_rel_bias=nbr_rel_bias, g_q_lora=g_q_lora, g_kv_lora=g_kv_lora,
             w_uq=w_uq, w_ukv=w_ukv, g_qk_mla=g_qk_mla, w_o_win=w_o_win, w_o_nbr=w_o_nbr,
             w_o_mla=w_o_mla, w_out=w_out, w_exp_gate=w_exp_gate, w_exp_up=w_exp_up,
             w_exp_down=w_exp_down)
    depth = w_in.shape[0]
    batch, seq, _ = x_prompt.shape
    dec_batch, dec_seq, _ = x_sample.shape
    past = cache_win_k.shape[2]

    n_c = 1 + dec_batch
    c_rows = -(-n_c // 8) * 8
    c_all = jnp.concatenate([c_ctx[None], c, jnp.zeros((c_rows - n_c, D_MODEL), F32)], axis=0)
    mods = _ada(c_all, w_ada, b_ada).reshape(depth, c_rows, 6, D_MODEL)
    mods = jnp.pad(mods, ((0, 0), (0, 0), (0, 2), (0, 0)))

    mats = (_block_ones(512, 64, 512, 64, 1.0 / HEAD_DIM), _block_ones(512, 64, 512, 64),
            _block_ones(256, 32, 512, 64), _block_ones(512, 64, 256, 32), _block_ones(256, 32, 256, 32))
    tabs = _rope_tables(dec_seq, 64) + _rope_tables(dec_seq, 32)
    sink = win_sink.astype(F32) * LOG2E
    w_r_t = w_router.T.astype(F32)
    b_r = b_router.astype(F32).reshape(N_EXPERTS, 1)
    layers = [_layer_params(i, p) for i in range(depth)]

    def ffn(x1_route, mod, rows_per_mod, lp):
        x1, route = x1_route
        return _moe(x1, route, mod, rows_per_mod, lp["gains"], lp["wg"], lp["wu"], lp["wd"])

    def merge(x, oa, ob, oc, mod, rows_per_mod, lp):
        return _merge(x, oa, ob, oc, mod, rows_per_mod, lp["gains"], lp["w_gate"], lp["wo_a"],
                      lp["wo_b"], lp["wo_c"], lp["w_out"], w_r_t, b_r)

    n_ctx = batch * seq
    x = x_prompt.reshape(n_ctx, D_MODEL)
    states = []
    for i, lp in enumerate(layers):
        mod = mods[i, 0:1]
        outs = _inproj(x, mod, n_ctx, lp["gains"], lp["w_a"], lp["w_uq"], lp["w_ukv"], mats, None,
                       seq, True)
        oa, ob, oc = _ctx_attn(i, sink, seq, *outs[:9])
        states.append(outs[9:])
        x = ffn(merge(x, oa, ob, oc, mod, n_ctx, lp), mod, n_ctx, lp)
    y_prompt = x.reshape(batch, seq, D_MODEL)

    n_lat = dec_batch * dec_seq
    x = x_sample.reshape(n_lat, D_MODEL)
    cwk = cache_win_k.reshape(dec_batch, depth, past, 128).astype(BF16)
    cwv = cache_win_v.reshape(dec_batch, depth, past, 128).astype(BF16)
    cnk = cache_nbr_k.reshape(dec_batch, depth, past, 512).astype(BF16)
    cnv = cache_nbr_v.reshape(dec_batch, depth, past, 512).astype(BF16)
    for i, lp in enumerate(layers):
        mod = mods[i, 1:1 + dec_batch]
        qa, ka, va, qb, kb, vb, qc, kc, vc = _inproj(
            x, mod, dec_seq, lp["gains"], lp["w_a"], lp["w_uq"], lp["w_ukv"], mats, tabs, dec_seq, False)
        kr_t = jnp.tile(cache_mla_krope[:, i].reshape(dec_batch * past, ROPE_DIM), (1, MLA_HEADS))
        kc_c, vc_c = _mla_cache_keys(cache_mla_ckv[:, i].reshape(dec_batch * past, KV_LORA), kr_t,
                                     lp["gains"], lp["w_ukv"], mats[1], mats[3], mats[4])
        r3 = lambda a: a.reshape(dec_batch, dec_seq, a.shape[-1])
        oa = _win_attn(i, sink, r3(qa), r3(ka), r3(va), cwk, cwv)
        ob = _nbr_attn(i, r3(qb), r3(kb), r3(vb), cnk, cnv, lp["nbr_table"])
        oc = _mla_attn(r3(qc), r3(kc), r3(vc), kc_c.reshape(dec_batch, past, 1024),
                       vc_c.reshape(dec_batch, past, 512))
        flat = lambda a: a.reshape(n_lat, 512)
        x = ffn(merge(x, flat(oa), flat(ob), flat(oc), mod, dec_seq, lp), mod, dec_seq, lp)
    y_sample = x.reshape(dec_batch, dec_seq, D_MODEL)

    def stack(k, shape):
        return jnp.stack([s[k].reshape((batch, seq) + shape) for s in states], axis=1)

    return (y_prompt, y_sample,
            stack(0, (WIN_KV_HEADS, HEAD_DIM)), stack(1, (WIN_KV_HEADS, HEAD_DIM)),
            stack(2, (NBR_HEADS, HEAD_DIM)), stack(3, (NBR_HEADS, HEAD_DIM)),
            stack(4, (KV_LORA,)), stack(5, (ROPE_DIM,)))
```

```python
import functools

import numpy as np
import jax
import jax.numpy as jnp
from jax import lax
from jax.experimental import pallas as pl
from jax.experimental.pallas import tpu as pltpu
from jax.experimental.pallas import tpu_sc as plsc

D_MODEL = 1024
GRID_W = 64
HEAD_DIM = 64
WIN_HEADS = 8
WIN_KV_HEADS = 2
WINDOW = 128
NBR_HEADS = 8
NBR_ROWS = 8
NBR_COLS = 16
MLA_HEADS = 8
Q_LORA = 256
KV_LORA = 128
NOPE_DIM = 64
ROPE_DIM = 32
V_DIM = 64
QK_DIM = NOPE_DIM + ROPE_DIM
N_EXPERTS = 16
N_GROUPS = 4
EXPERTS_PER_GROUP = 4
D_FF = 512
ROPE_BASE = 10000.0
EPS = 1e-6

LANES = 128
LOG2E = 1.4426950408889634
NEG = -1e30
VMEM_LIMIT = 56 * 1024 * 1024

F32 = jnp.float32
BF16 = jnp.bfloat16

C_QA, C_KA, C_VA, C_QB, C_KB, C_VB, C_CQ, C_CKV, C_KR, C_END = (
    0, 512, 640, 768, 1280, 1792, 2304, 2560, 2688, 2944)
WIN_Q_ORDER = (0, 4, 1, 5, 2, 6, 3, 7)


def _cparams(sem):
    return pltpu.CompilerParams(dimension_semantics=sem, vmem_limit_bytes=VMEM_LIMIT)


def _dot(a, b):
    return jnp.dot(a, b, preferred_element_type=F32)


def _dot_nt(a, b):
    return lax.dot_general(a, b, (((1,), (1,)), ((), ())), preferred_element_type=F32)


def _split(x):
    hi = x.astype(BF16)
    lo = (x - hi.astype(F32)).astype(BF16)
    return hi, lo


def _gsum(x2, bmat):
    return _dot(x2.astype(BF16), bmat)


def _tile_lanes(t, width):
    reps = width // t.shape[-1]
    return t if reps == 1 else jnp.concatenate([t] * reps, axis=-1)


def _rotate(x, cos, sin_a, sin_b, half):
    w = x.shape[-1]
    up = pltpu.roll(x, w - half, 1)
    dn = pltpu.roll(x, half, 1)
    return (x * _tile_lanes(cos, w) + up * _tile_lanes(sin_a, w) + dn * _tile_lanes(sin_b, w))


def _norm_mod(x, gain, scale, shift):
    ms = jnp.mean(x * x, axis=-1, keepdims=True)
    return (x * lax.rsqrt(ms + EPS) * gain) * (1.0 + scale) + shift


def _ada_body(c_ref, w_ref, b_ref, o_ref):
    c = c_ref[...]
    a = c * (1.0 / (1.0 + jnp.exp(-c)))
    a_hi, a_lo = _split(a)
    w_hi, w_lo = _split(w_ref[0])
    o_ref[0] = _dot(a_hi, w_hi) + _dot(a_hi, w_lo) + _dot(a_lo, w_hi) + b_ref[0]


def _ada(c_all, w_ada, b_ada):
    depth = w_ada.shape[0]
    rows = c_all.shape[0]
    tn = 1536
    return pl.pallas_call(
        _ada_body,
        grid=(depth, 6 * D_MODEL // tn),
        in_specs=[pl.BlockSpec((rows, D_MODEL), lambda l, j: (0, 0)),
                  pl.BlockSpec((1, D_MODEL, tn), lambda l, j: (l, 0, j)),
                  pl.BlockSpec((1, 1, tn), lambda l, j: (l, 0, j))],
        out_specs=pl.BlockSpec((1, rows, tn), lambda l, j: (l, 0, j)),
        out_shape=jax.ShapeDtypeStruct((depth, rows, 6 * D_MODEL), F32),
        compiler_params=_cparams(("parallel", "parallel")),
        name="ada",
    )(c_all, w_ada, b_ada.reshape(depth, 1, 6 * D_MODEL))


def _mla_key_tail(ckvn_b, kr_t, g, wukv_ref, bnn, bnr, brr, rope_tabs, kc_ref, vc_ref):
    kv = _dot(ckvn_b, wukv_ref[...])
    kn = kv[:, 0:512]
    vc_ref[...] = kv[:, 512:1024].astype(BF16)
    kn2 = kn * kn
    kr2 = kr_t * kr_t
    kr_sum32 = _gsum(kr2, brr)
    ssn = (_gsum(kn2, bnn) + jnp.concatenate([kr_sum32, kr_sum32], axis=-1)) * (1.0 / QK_DIM)
    ssr = (_gsum(kn2, bnr) + kr_sum32) * (1.0 / QK_DIM)
    kn = kn * lax.rsqrt(ssn + EPS) * g[5:6, 0:512]
    kr = kr_t * lax.rsqrt(ssr + EPS) * g[5:6, 512:768]
    if rope_tabs is not None:
        kr = _rotate(kr, *rope_tabs, 8)
    for p in range(4):
        kc_ref[:, 256 * p:256 * p + 128] = kn[:, 128 * p:128 * p + 128].astype(BF16)
        q4 = 128 * (p // 2)
        kc_ref[:, 256 * p + 128:256 * p + 256] = kr[:, q4:q4 + 128].astype(BF16)


def _inproj_body(rope, states, *refs):
    (x_ref, mod_ref, g_ref, w_ref, wuq_ref, wukv_ref, b64_ref, bnn_ref, brn_ref, bnr_ref,
     brr_ref) = refs[:11]
    refs = refs[11:]
    if rope:
        tabs_w = tuple(r[...] for r in refs[0:3])
        tabs_m = tuple(r[...] for r in refs[3:6])
        refs = refs[6:]
    else:
        tabs_w = tabs_m = None
    qa_ref, ka_ref, va_ref, qb_ref, kb_ref, vb_ref, qc_ref, kc_ref, vc_ref = refs[:9]
    st = refs[9:]

    g = g_ref[...]
    mod = mod_ref[0]
    hb = _norm_mod(x_ref[...], g[0:1], mod[1:2], mod[0:1]).astype(BF16)

    def proj(a, b):
        return _dot(hb, w_ref[:, a:b])

    b64 = b64_ref[...]

    def head_norm(z, bmat, gain):
        return z * lax.rsqrt(_gsum(z * z, bmat) + EPS) * gain

    qa = head_norm(proj(C_QA, C_KA), b64, g[1:2, 0:512])
    ka = head_norm(proj(C_KA, C_VA), b64[0:128, 0:128], g[1:2, 512:640])
    va = proj(C_VA, C_QB)
    if states:
        st[0][...] = ka
        st[1][...] = va
    if rope:
        qa = _rotate(qa, *tabs_w, 16)
        ka = _rotate(ka, *tabs_w, 16)
    qa_ref[...] = qa.astype(BF16)
    ka_ref[...] = ka.astype(BF16)
    va_ref[...] = va.astype(BF16)

    qb = head_norm(proj(C_QB, C_KB), b64, g[2:3, 0:512])
    kb = head_norm(proj(C_KB, C_VB), b64, g[2:3, 512:1024])
    vb = proj(C_VB, C_CQ)
    if states:
        st[2][...] = kb
        st[3][...] = vb
    qb_ref[...] = qb.astype(BF16)
    kb_ref[...] = kb.astype(BF16)
    vb_ref[...] = vb.astype(BF16)

    cq = proj(C_CQ, C_CKV)
    cqn = cq * lax.rsqrt(jnp.mean(cq * cq, axis=-1, keepdims=True) + EPS) * g[3:4, 0:256]
    qq = _dot(cqn.astype(BF16), wuq_ref[...])
    qn, qr = qq[:, 0:512], qq[:, 512:768]
    qn2, qr2 = qn * qn, qr * qr
    bnn, brn, bnr, brr = bnn_ref[...], brn_ref[...], bnr_ref[...], brr_ref[...]
    ssn = (_gsum(qn2, bnn) + _gsum(qr2, brn)) * (1.0 / QK_DIM)
    ssr = (_gsum(qn2, bnr) + _gsum(qr2, brr)) * (1.0 / QK_DIM)
    qn = qn * lax.rsqrt(ssn + EPS) * g[4:5, 0:512]
    qr = qr * lax.rsqrt(ssr + EPS) * g[4:5, 512:768]
    if rope:
        qr = _rotate(qr, *tabs_m, 8)
    for p in range(4):
        qc_ref[:, 256 * p:256 * p + 128] = qn[:, 128 * p:128 * p + 128].astype(BF16)
        q4 = 128 * (p // 2)
        qc_ref[:, 256 * p + 128:256 * p + 256] = qr[:, q4:q4 + 128].astype(BF16)

    ckv = proj(C_CKV, C_KR)
    ckvn = ckv * lax.rsqrt(jnp.mean(ckv * ckv, axis=-1, keepdims=True) + EPS) * g[3:4, 256:384]
    kr_t = proj(C_KR, C_END)
    if states:
        st[4][...] = ckvn
        st[5][...] = kr_t[:, 0:ROPE_DIM]
    _mla_key_tail(ckvn.astype(BF16), kr_t, g, wukv_ref, bnn, bnr, brr, tabs_m, kc_ref, vc_ref)


def _const_spec(shape):
    nd = len(shape)
    return pl.BlockSpec(shape, lambda i, _nd=nd: (0,) * _nd)


def _inproj(x, mods, rows_per_mod, gains, w_a, w_uq, w_ukv, mats, rope_tabs, seq_len, states):
    n = x.shape[0]
    tm = min(512, n)
    rope = rope_tabs is not None
    row = lambda w: pl.BlockSpec((tm, w), lambda i: (i, 0))
    in_specs = [row(D_MODEL),
                pl.BlockSpec((1, 8, D_MODEL), lambda i: ((i * tm) // rows_per_mod, 0, 0)),
                _const_spec(gains.shape), _const_spec(w_a.shape), _const_spec(w_uq.shape),
                _const_spec(w_ukv.shape)] + [_const_spec(m.shape) for m in mats]
    args = [x, mods, gains, w_a, w_uq, w_ukv, *mats]
    if rope:
        tiles_per_seq = seq_len // tm
        in_specs += [pl.BlockSpec((tm, LANES), lambda i: (i % tiles_per_seq, 0))] * 6
        args += list(rope_tabs)
    widths = [512, 128, 128, 512, 512, 512, 1024, 1024, 512]
    out_shape = [jax.ShapeDtypeStruct((n, w), BF16) for w in widths]
    out_specs = [row(w) for w in widths]
    if states:
        swidths = [128, 128, 512, 512, KV_LORA, ROPE_DIM]
        out_shape += [jax.ShapeDtypeStruct((n, w), F32) for w in swidths]
        out_specs += [row(w) for w in swidths]
    return pl.pallas_call(
        functools.partial(_inproj_body, rope, states),
        grid=(n // tm,), in_specs=in_specs, out_specs=out_specs, out_shape=out_shape,
        compiler_params=_cparams(("parallel",)),
        name="inproj_lat" if rope else "inproj_ctx",
    )(*args)


def _mla_cache_body(ckv_ref, kr_ref, g_ref, wukv_ref, bnn_ref, bnr_ref, brr_ref, kc_ref, vc_ref):
    _mla_key_tail(ckv_ref[...].astype(BF16), kr_ref[...], g_ref[...], wukv_ref, bnn_ref[...],
                  bnr_ref[...], brr_ref[...], None, kc_ref, vc_ref)


def _mla_cache_keys(ckv, kr_t, gains, w_ukv, bnn, bnr, brr):
    n = ckv.shape[0]
    tm = min(512, n)
    row = lambda w: pl.BlockSpec((tm, w), lambda i: (i, 0))
    return pl.pallas_call(
        _mla_cache_body,
        grid=(n // tm,),
        in_specs=[row(KV_LORA), row(256), _const_spec(gains.shape), _const_spec(w_ukv.shape),
                  _const_spec(bnn.shape), _const_spec(bnr.shape), _const_spec(brr.shape)],
        out_specs=[row(1024), row(512)],
        out_shape=[jax.ShapeDtypeStruct((n, 1024), BF16), jax.ShapeDtypeStruct((n, 512), BF16)],
        compiler_params=_cparams(("parallel",)),
        name="mla_cache_keys",
    )(ckv, kr_t, gains, w_ukv, bnn, bnr, brr)


def _lane_mask(width, ranges):
    lane = lax.broadcasted_iota(jnp.int32, (1, width), 1)
    m = None
    for lo, hi in ranges:
        c = (lane >= lo) & (lane < hi)
        m = c if m is None else (m | c)
    return jnp.where(m, 1.0, 0.0).astype(BF16)


def _stack_heads(q, mask0, mask1):
    return jnp.concatenate([q * mask0, q * mask1], axis=0)


def _lane_tiles(x):
    return [x[:, j:j + LANES] for j in range(0, x.shape[1], LANES)]


def _softmax_pv(scores, values, sink=None):
    rows = scores[0].shape[0]
    mp = None
    for s in scores:
        for t in _lane_tiles(s):
            mp = t if mp is None else jnp.maximum(mp, t)
    base = sink if sink is not None else jnp.full((rows, LANES), NEG, F32)
    m = jnp.maximum(base, jnp.max(mp, axis=-1, keepdims=True))
    lp = None
    ps = []
    for s in scores:
        p = jnp.exp2(s - _tile_lanes(m, s.shape[1]))
        for t in _lane_tiles(p):
            lp = t if lp is None else lp + t
        ps.append(p.astype(BF16))
    p_all = ps[0] if len(ps) == 1 else jnp.concatenate(ps, axis=-1)
    v_all = values[0] if len(values) == 1 else jnp.concatenate(values, axis=0)
    o = _dot(p_all, v_all)
    if sink is not None:
        lane = lax.broadcasted_iota(jnp.int32, (rows, LANES), 1)
        lp = lp + jnp.where(lane == 0, jnp.exp2(sink - m), 0.0)
    return o / jnp.sum(lp, axis=-1, keepdims=True)


def _merge_heads(o, tq):
    lane = lax.broadcasted_iota(jnp.int32, (tq, LANES), 1)
    return jnp.where(lane < HEAD_DIM, o[0:tq], o[tq:2 * tq])


def _mla_masks(p_mod2):
    lane = lax.broadcasted_iota(jnp.int32, (1, 256), 1)
    r0 = 128 + 32 * (2 * p_mod2)
    m0 = (lane < 64) | ((lane >= r0) & (lane < r0 + 32))
    m1 = ((lane >= 64) & (lane < 128)) | ((lane >= r0 + 32) & (lane < r0 + 64))
    return (jnp.where(m0, 1.0, 0.0).astype(BF16), jnp.where(m1, 1.0, 0.0).astype(BF16))


def _sink_col(sink_ref, layer, h0, h1, tq):
    row = lax.broadcasted_iota(jnp.int32, (2 * tq, LANES), 0)
    return jnp.where(row < tq, sink_ref[layer, h0], sink_ref[layer, h1])


def _ctx_attn_body(layer, sink_ref, qa_ref, ka_ref, va_ref, qb_ref, kb_ref, vb_ref, qc_ref, kc_ref,
                   vc_ref, oa_ref, ob_ref, oc_ref):
    tq = qa_ref.shape[0]
    lo = _lane_mask(LANES, [(0, 64)])
    hi = _lane_mask(LANES, [(64, 128)])
    ka, va = ka_ref[...], va_ref[...]
    for j in range(4):
        sl = slice(128 * j, 128 * j + 128)
        qs = _stack_heads(qa_ref[:, sl], lo, hi)
        sink = _sink_col(sink_ref, layer, j, 4 + j, tq)
        o = _softmax_pv([_dot_nt(qs, ka)], [va], sink)
        oa_ref[:, sl] = _merge_heads(o, tq).astype(BF16)

        qs = _stack_heads(qb_ref[:, sl], lo, hi)
        o = _softmax_pv([_dot_nt(qs, kb_ref[:, sl])], [vb_ref[:, sl]])
        ob_ref[:, sl] = _merge_heads(o, tq).astype(BF16)

        m0, m1 = _mla_masks(j % 2)
        s2 = slice(256 * j, 256 * j + 256)
        qs = _stack_heads(qc_ref[:, s2], m0, m1)
        o = _softmax_pv([_dot_nt(qs, kc_ref[:, s2])], [vc_ref[:, sl]])
        oc_ref[:, sl] = _merge_heads(o, tq).astype(BF16)


def _ctx_attn(layer, sink, seq, qa, ka, va, qb, kb, vb, qc, kc, vc):
    n = qa.shape[0]
    row = lambda w: pl.BlockSpec((seq, w), lambda b: (b, 0))
    ins = [qa, ka, va, qb, kb, vb, qc, kc, vc]
    return pl.pallas_call(
        functools.partial(_ctx_attn_body, layer),
        grid=(n // seq,),
        in_specs=[pl.BlockSpec(memory_space=pltpu.SMEM)] + [row(a.shape[1]) for a in ins],
        out_specs=[row(512)] * 3,
        out_shape=[jax.ShapeDtypeStruct((n, 512), BF16)] * 3,
        compiler_params=_cparams(("parallel",)),
        name="ctx_attn",
    )(sink, *ins)


def _win_body(layer, sink_ref, q_ref, k_ref, v_ref, kc_ref, vc_ref, o_ref):
    tq = q_ref.shape[1]
    seq = k_ref.shape[1]
    kw = 3 * tq
    i = pl.program_id(1)
    kstart = pl.multiple_of(jnp.clip((i - 1) * tq, 0, seq - kw), tq)
    k_all = jnp.concatenate([k_ref[0, pl.ds(kstart, kw), :], kc_ref[0, 0]], axis=0)
    v_all = jnp.concatenate([v_ref[0, pl.ds(kstart, kw), :], vc_ref[0, 0]], axis=0)
    q_pos = i * tq + lax.broadcasted_iota(jnp.int32, (2 * tq, kw), 0) % tq
    k_pos = kstart + lax.broadcasted_iota(jnp.int32, (2 * tq, kw), 1)
    band = jnp.abs(q_pos - k_pos) <= WINDOW
    lo = _lane_mask(LANES, [(0, 64)])
    hi = _lane_mask(LANES, [(64, 128)])
    qs = jnp.concatenate([_stack_heads(q_ref[0, :, 128 * j:128 * j + 128], lo, hi) for j in range(4)],
                         axis=0)
    s = _dot_nt(qs, k_all)
    for j in range(4):
        sb = s[2 * tq * j:2 * tq * (j + 1)]
        s_band = jnp.where(band, sb[:, 0:kw], NEG)
        sink = _sink_col(sink_ref, layer, j, 4 + j, tq)
        o = _softmax_pv([s_band, sb[:, kw:]], [v_all], sink)
        o_ref[0, :, 128 * j:128 * j + 128] = _merge_heads(o, tq).astype(BF16)


def _win_attn(layer, sink, q, k, v, kc, vc):
    b, seq, _ = q.shape
    tq = 128
    past = kc.shape[2]
    return pl.pallas_call(
        functools.partial(_win_body, layer),
        grid=(b, seq // tq),
        in_specs=[pl.BlockSpec(memory_space=pltpu.SMEM),
                  pl.BlockSpec((1, tq, 512), lambda bi, i: (bi, i, 0)),
                  pl.BlockSpec((1, seq, 128), lambda bi, i: (bi, 0, 0)),
                  pl.BlockSpec((1, seq, 128), lambda bi, i: (bi, 0, 0)),
                  pl.BlockSpec((1, 1, past, 128), lambda bi, i: (bi, layer, 0, 0)),
                  pl.BlockSpec((1, 1, past, 128), lambda bi, i: (bi, layer, 0, 0))],
        out_specs=pl.BlockSpec((1, tq, 512), lambda bi, i: (bi, i, 0)),
        out_shape=jax.ShapeDtypeStruct((b, seq, 512), BF16),
        compiler_params=_cparams(("parallel", "arbitrary")),
        name="win_attn",
    )(sink, q, k, v, kc, vc)


NBR_TILE_ROWS = 4
NBR_WIN_ROWS = NBR_TILE_ROWS + NBR_ROWS
NBR_TAB_PAD = NBR_WIN_ROWS - NBR_ROWS


def _nbr_body(rows, q_ref, k_ref, v_ref, kc_ref, vc_ref, tab_ref, o_ref):
    tq = NBR_TILE_ROWS * GRID_W
    kw = NBR_WIN_ROWS * GRID_W
    i = pl.program_id(1)
    r0 = NBR_TILE_ROWS * i
    ws = jnp.clip(r0 - NBR_ROWS // 2, 0, rows - NBR_WIN_ROWS)
    kstart = pl.multiple_of(ws * GRID_W, LANES)
    q_row = r0 + lax.broadcasted_iota(jnp.int32, (2 * tq, kw), 0) % tq // GRID_W
    k_row = ws + lax.broadcasted_iota(jnp.int32, (2 * tq, kw), 1) // GRID_W
    rs = jnp.clip(q_row - NBR_ROWS // 2, 0, rows - NBR_ROWS)
    valid = (k_row >= rs) & (k_row < rs + NBR_ROWS)
    lo = _lane_mask(LANES, [(0, 64)])
    hi = _lane_mask(LANES, [(64, 128)])
    for j in range(4):
        sl = slice(128 * j, 128 * j + 128)
        qs = _stack_heads(q_ref[0, :, sl], lo, hi)
        k = k_ref[0, pl.ds(kstart, kw), sl]
        v = v_ref[0, pl.ds(kstart, kw), sl]
        pieces = []
        for h in (2 * j, 2 * j + 1):
            for ql in range(NBR_TILE_ROWS):
                d0 = ws - r0 - ql + (NBR_ROWS - 1) + NBR_TAB_PAD
                pieces.append(jnp.concatenate(
                    [tab_ref[h, d0 + 2 * m] for m in range(NBR_WIN_ROWS // 2)], axis=-1))
        bias = jnp.concatenate(pieces, axis=0)
        s_nb = jnp.where(valid, _dot_nt(qs, k) + bias, NEG)
        s_ctx = _dot_nt(qs, kc_ref[0, 0, :, sl])
        o = _softmax_pv([s_nb, s_ctx], [v, vc_ref[0, 0, :, sl]])
        o_ref[0, :, sl] = _merge_heads(o, tq).astype(BF16)


def _nbr_attn(layer, q, k, v, kc, vc, table):
    b, seq, _ = q.shape
    rows = seq // GRID_W
    tq = NBR_TILE_ROWS * GRID_W
    past = kc.shape[2]
    return pl.pallas_call(
        functools.partial(_nbr_body, rows),
        grid=(b, seq // tq),
        in_specs=[pl.BlockSpec((1, tq, 512), lambda bi, i: (bi, i, 0)),
                  pl.BlockSpec((1, seq, 512), lambda bi, i: (bi, 0, 0)),
                  pl.BlockSpec((1, seq, 512), lambda bi, i: (bi, 0, 0)),
                  pl.BlockSpec((1, 1, past, 512), lambda bi, i: (bi, layer, 0, 0)),
                  pl.BlockSpec((1, 1, past, 512), lambda bi, i: (bi, layer, 0, 0)),
                  pl.BlockSpec(table.shape, lambda bi, i: (0, 0, 0, 0))],
        out_specs=pl.BlockSpec((1, tq, 512), lambda bi, i: (bi, i, 0)),
        out_shape=jax.ShapeDtypeStruct((b, seq, 512), BF16),
        compiler_params=_cparams(("parallel", "arbitrary")),
        name="nbr_attn",
    )(q, k, v, kc, vc, table)


def _nbr_bias_table(rel_bias):
    col = np.arange(GRID_W)
    cs = np.clip(col - NBR_COLS // 2, 0, GRID_W - NBR_COLS)
    kc = np.arange(GRID_W)
    ok = (kc[None, :] >= cs[:, None]) & (kc[None, :] < cs[:, None] + NBR_COLS)
    dc = np.clip(kc[None, :] - col[:, None] + (NBR_COLS - 1), 0, 2 * NBR_COLS - 2)
    t = rel_bias.astype(F32)[:, :, dc] * LOG2E
    t = jnp.where(jnp.asarray(ok)[None, None], t, NEG)
    t = jnp.pad(t, ((0, 0), (NBR_TAB_PAD, NBR_TAB_PAD), (0, 0), (0, 0)))
    return jnp.concatenate([t[:, :-1], t[:, 1:]], axis=-1)


MLA_KEY_CHUNK = 512
MLA_ROW_BLOCK = 256


def _mla_body(q_ref, kl_ref, kc_ref, vl_ref, vc_ref, o_ref, qs_sc, m_sc, l_sc, acc_sc):
    tq = q_ref.shape[1]
    rows = 2 * tq
    seq = kl_ref.shape[1]
    past = kc_ref.shape[1]
    tk = min(MLA_KEY_CHUNK, past)
    rb = min(MLA_ROW_BLOCK, rows)
    m0, m1 = _mla_masks(pl.program_id(1) % 2)
    qs_sc[...] = _stack_heads(q_ref[0], m0, m1)
    m_sc[...] = jnp.full(m_sc.shape, NEG, F32)
    l_sc[...] = jnp.zeros(l_sc.shape, F32)
    acc_sc[...] = jnp.zeros(acc_sc.shape, F32)

    def step(k, v):
        s = _dot_nt(qs_sc[...], k)
        for r in range(rows // rb):
            sl = slice(r * rb, (r + 1) * rb)
            sb = s[sl]
            m_prev = m_sc[sl]
            m_new = jnp.maximum(m_prev, jnp.max(sb, axis=-1, keepdims=True))
            alpha = jnp.exp2(m_prev - m_new)
            p = jnp.exp2(sb - _tile_lanes(m_new, tk))
            psum = p[:, 0:LANES]
            for j in range(1, tk // LANES):
                psum = psum + p[:, j * LANES:(j + 1) * LANES]
            l_sc[sl] = alpha * l_sc[sl] + psum
            acc_sc[sl] = alpha * acc_sc[sl] + _dot(p.astype(BF16), v)
            m_sc[sl] = m_new

    for c in range(seq // tk):
        step(kl_ref[0, c * tk:(c + 1) * tk, :], vl_ref[0, c * tk:(c + 1) * tk, :])
    for c in range(past // tk):
        step(kc_ref[0, c * tk:(c + 1) * tk, :], vc_ref[0, c * tk:(c + 1) * tk, :])
    l = jnp.sum(l_sc[...], axis=-1, keepdims=True)
    o_ref[0] = _merge_heads(acc_sc[...] / l, tq).astype(BF16)


def _mla_attn(q, kl, vl, kc, vc):
    b, seq, _ = q.shape
    past = kc.shape[1]
    tq = min(512, seq)
    return pl.pallas_call(
        _mla_body,
        grid=(b, 4, seq // tq),
        in_specs=[pl.BlockSpec((1, tq, 256), lambda bi, p, qi: (bi, qi, p)),
                  pl.BlockSpec((1, seq, 256), lambda bi, p, qi: (bi, 0, p)),
                  pl.BlockSpec((1, past, 256), lambda bi, p, qi: (bi, 0, p)),
                  pl.BlockSpec((1, seq, 128), lambda bi, p, qi: (bi, 0, p)),
                  pl.BlockSpec((1, past, 128), lambda bi, p, qi: (bi, 0, p))],
        out_specs=pl.BlockSpec((1, tq, 128), lambda bi, p, qi: (bi, qi, p)),
        out_shape=jax.ShapeDtypeStruct((b, seq, 512), BF16),
        scratch_shapes=[pltpu.VMEM((2 * tq, 256), BF16), pltpu.VMEM((2 * tq, LANES), F32),
                        pltpu.VMEM((2 * tq, LANES), F32), pltpu.VMEM((2 * tq, LANES), F32)],
        compiler_params=_cparams(("parallel", "parallel", "arbitrary")),
        name="mla_attn",
    )(q, kl, kc, vl, vc)


def _pack_pairs(x):
    w = x.shape[1] // 2
    hi = lax.bitcast_convert_type(x[:, :w].astype(BF16).astype(F32), jnp.int32)
    lo = lax.bitcast_convert_type(x[:, w:].astype(BF16).astype(F32), jnp.int32)
    return (hi & jnp.int32(-65536)) | lax.shift_right_logical(lo, jnp.int32(16))


def _unpack_pairs(p):
    hi = lax.bitcast_convert_type(p & jnp.int32(-65536), F32)
    lo = lax.bitcast_convert_type(lax.shift_left(p, jnp.int32(16)), F32)
    return jnp.concatenate([hi, lo], axis=-1)


def _merge_body(x_ref, oa_ref, ob_ref, oc_ref, mod_ref, g_ref, wg_ref, woa_ref, wob_ref, woc_ref,
                wout_ref, wr_ref, br_ref, tri_ref, x1_ref, route_ref, h2p_ref, count_ref, count_sc):
    x = x_ref[...]
    g = g_ref[...]
    mod = mod_ref[0]
    hb = _norm_mod(x, g[0:1], mod[1:2], mod[0:1]).astype(BF16)
    m = None
    for br, (o_ref, wo_ref) in enumerate(((oa_ref, woa_ref), (ob_ref, wob_ref), (oc_ref, woc_ref))):
        z = _dot(hb, wg_ref[:, D_MODEL * br:D_MODEL * (br + 1)])
        gate = 1.0 / (1.0 + jnp.exp(-z))
        t = gate * _dot(o_ref[...], wo_ref[...])
        m = t if m is None else m + t
    y = _dot(m.astype(BF16), wout_ref[...])
    x1 = x + mod[2:3] * y
    x1_ref[...] = x1

    h2 = _norm_mod(x1, g[6:7], mod[4:5], mod[3:4])
    h_hi, h_lo = _split(h2)
    wr = wr_ref[...]
    w_hi, w_lo = _split(wr)
    logits = _dot_nt(w_hi, h_hi) + _dot_nt(w_hi, h_lo) + _dot_nt(w_lo, h_hi)
    score = 1.0 / (1.0 + jnp.exp(-logits))
    sel = score + br_ref[...]
    sel_r = [sel[e:e + 1] for e in range(N_EXPERTS)]
    sc_r = [score[e:e + 1] for e in range(N_EXPERTS)]
    picked = []
    for e in range(N_EXPERTS):
        grp, a = divmod(e, EXPERTS_PER_GROUP)
        rank = None
        for bb in range(EXPERTS_PER_GROUP):
            if bb == a:
                continue
            o = sel_r[grp * EXPERTS_PER_GROUP + bb]
            beats = (o >= sel_r[e]) if bb < a else (o > sel_r[e])
            r = jnp.where(beats, 1.0, 0.0)
            rank = r if rank is None else rank + r
        picked.append(rank < 2.0)
    gscore = []
    for grp in range(N_GROUPS):
        tot = None
        for a in range(EXPERTS_PER_GROUP):
            e = grp * EXPERTS_PER_GROUP + a
            t = jnp.where(picked[e], sel_r[e], 0.0)
            tot = t if tot is None else tot + t
        gscore.append(tot)
    best = jnp.zeros_like(gscore[0])
    best_v = gscore[0]
    for grp in range(1, N_GROUPS):
        upd = gscore[grp] > best_v
        best = jnp.where(upd, float(grp), best)
        best_v = jnp.where(upd, gscore[grp], best_v)
    cw, pk = [], []
    for a in range(EXPERTS_PER_GROUP):
        tot = flag = None
        for grp in range(N_GROUPS):
            e = grp * EXPERTS_PER_GROUP + a
            f = (best == float(grp)) & picked[e]
            t = jnp.where(f, sc_r[e], 0.0)
            tot = t if tot is None else tot + t
            flag = f if flag is None else (flag | f)
        cw.append(tot)
        pk.append(flag)
    den = cw[0] + cw[1] + cw[2] + cw[3]
    first = jnp.where(pk[0], 0.0, jnp.where(pk[1], 1.0, jnp.where(pk[2], 2.0, 3.0)))
    second = jnp.where(pk[3], 3.0, jnp.where(pk[2], 2.0, jnp.where(pk[1], 1.0, 0.0)))
    slot_e, slot_w = [], []
    for which in (first, second):
        tot = None
        for a in range(EXPERTS_PER_GROUP):
            t = jnp.where(which == float(a), cw[a], 0.0)
            tot = t if tot is None else tot + t
        slot_w.append(tot / den)
        slot_e.append(best * float(EXPERTS_PER_GROUP) + which)

    @pl.when(pl.program_id(0) == 0)
    def _():
        count_sc[...] = jnp.zeros(count_sc.shape, F32)

    tm = x.shape[0]
    eid = lax.broadcasted_iota(jnp.int32, (N_EXPERTS, tm), 0).astype(F32)
    oh = [eid == slot_e[0], eid == slot_e[1]]
    both = jnp.where(oh[0] | oh[1], 1.0, 0.0)
    seen = count_sc[...][:, 0:1] + _dot(both.astype(BF16), tri_ref[...])
    for k in range(2):
        route_ref[k:k + 1, :] = slot_w[k]
        route_ref[2 + k:3 + k, :] = slot_e[k]
        route_ref[4 + k:5 + k, :] = jnp.sum(jnp.where(oh[k], seen, 0.0), axis=0, keepdims=True)
    route_ref[6:8, :] = jnp.zeros((2, tm), F32)
    count_sc[...] = count_sc[...] + jnp.sum(both, axis=-1, keepdims=True)
    count_ref[...] = count_sc[...]
    h2p_ref[...] = _pack_pairs(h2)


def _merge(x, oa, ob, oc, mods, rows_per_mod, gains, w_gate, wo_a, wo_b, wo_c, w_out, w_r_t, b_r):
    n = x.shape[0]
    tm = min(512, n)
    row = lambda w: pl.BlockSpec((tm, w), lambda i: (i, 0))
    tri = jnp.asarray(np.triu(np.ones((tm, tm), np.float32), 1), BF16)
    consts = [gains, w_gate, wo_a, wo_b, wo_c, w_out, w_r_t, b_r, tri]
    return pl.pallas_call(
        _merge_body,
        grid=(n // tm,),
        in_specs=[row(D_MODEL), row(512), row(512), row(512),
                  pl.BlockSpec((1, 8, D_MODEL), lambda i: ((i * tm) // rows_per_mod, 0, 0))]
                 + [_const_spec(c.shape) for c in consts],
        out_specs=[row(D_MODEL), pl.BlockSpec((8, tm), lambda i: (0, i)), row(512),
                   pl.BlockSpec((N_EXPERTS, LANES), lambda i: (0, 0))],
        out_shape=[jax.ShapeDtypeStruct((n, D_MODEL), F32), jax.ShapeDtypeStruct((8, n), F32),
                   jax.ShapeDtypeStruct((n, 512), jnp.int32),
                   jax.ShapeDtypeStruct((N_EXPERTS, LANES), F32)],
        scratch_shapes=[pltpu.VMEM((N_EXPERTS, LANES), F32)],
        compiler_params=_cparams(("arbitrary",)),
        name="merge",
    )(x, oa, ob, oc, mods, *consts)


EXPERT_TILE = 512
SC_CORES = 2
SC_SUBCORES = 16
SC_WORKERS = SC_CORES * SC_SUBCORES
SC_WINDOW = 128


def _sc_mesh():
    return plsc.VectorSubcoreMesh(core_axis_name="c", subcore_axis_name="s", num_cores=SC_CORES,
                                  num_subcores=SC_SUBCORES)


def _sc_window_base(steps, j):
    wid = lax.axis_index("s") * SC_CORES + lax.axis_index("c")
    return pl.multiple_of((wid * steps + j) * SC_WINDOW, SC_WINDOW)


def _sc_dispatch(rows, pos0, pos1, n_out):
    n, w = rows.shape
    steps = n // (SC_WORKERS * SC_WINDOW)

    @functools.partial(
        pl.kernel, out_type=jax.ShapeDtypeStruct((n_out, w), rows.dtype), mesh=_sc_mesh(),
        scratch_types=[pltpu.VMEM((SC_WINDOW,), jnp.int32), pltpu.VMEM((SC_WINDOW,), jnp.int32),
                       pltpu.VMEM((SC_WINDOW, w), rows.dtype)],
        name="moe_dispatch")
    def run(x_hbm, i0_hbm, i1_hbm, o_hbm, i0_v, i1_v, rows_v):
        @pl.loop(0, steps)
        def _(j):
            base = _sc_window_base(steps, j)
            pltpu.sync_copy(i0_hbm.at[pl.ds(base, SC_WINDOW)], i0_v)
            pltpu.sync_copy(i1_hbm.at[pl.ds(base, SC_WINDOW)], i1_v)
            pltpu.sync_copy(x_hbm.at[pl.ds(base, SC_WINDOW)], rows_v)
            pltpu.sync_copy(rows_v, o_hbm.at[i0_v])
            pltpu.sync_copy(rows_v, o_hbm.at[i1_v])

    return run(rows, pos0, pos1)


def _sc_collect(rows, pos0, pos1):
    n = pos0.shape[0]
    w = rows.shape[1]
    steps = n // (SC_WORKERS * SC_WINDOW)
    out = jax.ShapeDtypeStruct((n, w), rows.dtype)

    @functools.partial(
        pl.kernel, out_type=[out, out], mesh=_sc_mesh(),
        scratch_types=[pltpu.VMEM((SC_WINDOW,), jnp.int32), pltpu.VMEM((SC_WINDOW, w), rows.dtype)],
        name="moe_collect")
    def run(y_hbm, i0_hbm, i1_hbm, o0_hbm, o1_hbm, i_v, rows_v):
        @pl.loop(0, steps)
        def _(j):
            base = _sc_window_base(steps, j)
            for i_hbm, o_hbm in ((i0_hbm, o0_hbm), (i1_hbm, o1_hbm)):
                pltpu.sync_copy(i_hbm.at[pl.ds(base, SC_WINDOW)], i_v)
                pltpu.sync_copy(y_hbm.at[i_v], rows_v)
                pltpu.sync_copy(rows_v, o_hbm.at[pl.ds(base, SC_WINDOW)])

    return run(rows, pos0, pos1)


def _experts_body(te_ref, nv_ref, xs_ref, wg_ref, wu_ref, wd_ref, ys_ref):
    j = pl.program_id(0)

    @pl.when(j < nv_ref[0])
    def _():
        x = _unpack_pairs(xs_ref[...]).astype(BF16)
        zg = _dot(x, wg_ref[0])
        act = zg * (1.0 / (1.0 + jnp.exp(-zg))) * _dot(x, wu_ref[0])
        ys_ref[...] = _pack_pairs(_dot(act.astype(BF16), wd_ref[0]))

    @pl.when(j >= nv_ref[0])
    def _():
        ys_ref[...] = jnp.zeros(ys_ref.shape, ys_ref.dtype)


def _experts(xs, tile_expert, n_valid, wg, wu, wd):
    p = xs.shape[0]
    wspec = lambda shape: pl.BlockSpec(shape, lambda j, te, nv: (te[j], 0, 0))
    grid_spec = pltpu.PrefetchScalarGridSpec(
        num_scalar_prefetch=2,
        grid=(p // EXPERT_TILE,),
        in_specs=[pl.BlockSpec((EXPERT_TILE, 512), lambda j, te, nv: (j, 0)),
                  wspec((1, D_MODEL, D_FF)), wspec((1, D_MODEL, D_FF)), wspec((1, D_FF, D_MODEL))],
        out_specs=pl.BlockSpec((EXPERT_TILE, 512), lambda j, te, nv: (j, 0)))
    return pl.pallas_call(
        _experts_body, grid_spec=grid_spec,
        out_shape=jax.ShapeDtypeStruct((p, 512), jnp.int32),
        compiler_params=_cparams(("arbitrary",)),
        name="experts",
    )(tile_expert, n_valid, xs, wg, wu, wd)


def _combine_body(x1_ref, y0_ref, y1_ref, route_ref, mod_ref, o_ref):
    tm = x1_ref.shape[0]
    eye = jnp.where(lax.broadcasted_iota(jnp.int32, (tm, tm), 0)
                    == lax.broadcasted_iota(jnp.int32, (tm, tm), 1), 1.0, 0.0).astype(BF16)
    r_hi, r_lo = _split(route_ref[...])
    wcol = _dot_nt(eye, r_hi) + _dot_nt(eye, r_lo)
    moe = wcol[:, 0:1] * _unpack_pairs(y0_ref[...]) + wcol[:, 1:2] * _unpack_pairs(y1_ref[...])
    o_ref[...] = x1_ref[...] + mod_ref[0][5:6] * moe


def _combine(x1, y0, y1, route, mods, rows_per_mod):
    n = x1.shape[0]
    tm = min(512, n)
    row = lambda w: pl.BlockSpec((tm, w), lambda i: (i, 0))
    return pl.pallas_call(
        _combine_body,
        grid=(n // tm,),
        in_specs=[row(D_MODEL), row(512), row(512), pl.BlockSpec((8, tm), lambda i: (0, i)),
                  pl.BlockSpec((1, 8, D_MODEL), lambda i: ((i * tm) // rows_per_mod, 0, 0))],
        out_specs=row(D_MODEL),
        out_shape=jax.ShapeDtypeStruct((n, D_MODEL), F32),
        compiler_params=_cparams(("parallel",)),
        name="combine",
    )(x1, y0, y1, route, mods)


def _moe(x1, route, h2p, counts, mods, rows_per_mod, wg, wu, wd):
    n = x1.shape[0]
    p = 2 * n + N_EXPERTS * EXPERT_TILE
    cnt = counts[:, 0].astype(jnp.int32)
    padded = (cnt + EXPERT_TILE - 1) // EXPERT_TILE * EXPERT_TILE
    seg_end = jnp.cumsum(padded)
    seg_off = seg_end - padded
    experts = jnp.arange(N_EXPERTS, dtype=jnp.int32)

    def position(k):
        e = route[2 + k].astype(jnp.int32)
        off = jnp.sum(jnp.where(e[:, None] == experts[None], seg_off[None], 0), axis=1)
        return off + route[4 + k].astype(jnp.int32)

    pos0, pos1 = position(0), position(1)
    tile_start = jnp.arange(p // EXPERT_TILE, dtype=jnp.int32) * EXPERT_TILE
    tile_expert = jnp.sum(tile_start[:, None] >= seg_end[None], axis=1).astype(jnp.int32)
    tile_expert = jnp.minimum(tile_expert, N_EXPERTS - 1)
    n_valid = (seg_end[-1:] // EXPERT_TILE).astype(jnp.int32)

    xs = _sc_dispatch(h2p, pos0, pos1, p)
    ys = _experts(xs, tile_expert, n_valid, wg, wu, wd)
    y0, y1 = _sc_collect(ys, pos0, pos1)
    return _combine(x1, y0, y1, route, mods, rows_per_mod)


def _block_ones(n_in, g_in, n_out, g_out, value=1.0):
    r = np.arange(n_in)[:, None] // g_in
    c = np.arange(n_out)[None, :] // g_out
    return jnp.asarray(np.where(r == c, value, 0.0), dtype=BF16)


def _rope_tables(seq, head_w):
    pos = np.arange(seq)
    rows, cols = pos // GRID_W, pos % GRID_W
    a = head_w // 2
    half = a // 2
    freqs = (ROPE_BASE ** (-np.arange(half, dtype=np.float32) / half)).astype(np.float32)
    lane = np.arange(LANES) % head_w
    within = lane % a
    first = within < half
    p = np.where((lane // a == 0)[None, :], rows[:, None], cols[:, None]).astype(np.float32)
    ang = (p * freqs[within % half][None, :]).astype(np.float32)
    cos, sin = np.cos(ang), np.sin(ang)
    return (jnp.asarray(cos, F32), jnp.asarray(np.where(first[None], -sin, 0.0), F32),
            jnp.asarray(np.where(first[None], 0.0, sin), F32))


def _tile_to(v, width):
    return jnp.tile(v, width // v.shape[0])


def _layer_params(i, p):
    w_in = p["w_in"][i]
    sp = np.cumsum((512, 128, 128, 512, 512, 512, Q_LORA, KV_LORA, ROPE_DIM))
    qa, ka, va, qb, kb, vb, cq, ckv, kr, gates = jnp.split(w_in, [int(s) for s in sp], axis=1)
    qa = qa.reshape(D_MODEL, WIN_HEADS, HEAD_DIM)[:, WIN_Q_ORDER, :].reshape(D_MODEL, 512)
    w_a = jnp.concatenate([qa, ka, va, qb, kb, vb, cq, ckv, jnp.tile(kr, (1, MLA_HEADS))],
                          axis=1).astype(BF16)
    w_uq = p["w_uq"][i].reshape(Q_LORA, MLA_HEADS, QK_DIM)
    w_uq = jnp.concatenate([w_uq[:, :, :NOPE_DIM].reshape(Q_LORA, 512),
                            w_uq[:, :, NOPE_DIM:].reshape(Q_LORA, 256)], axis=1).astype(BF16)
    w_ukv = p["w_ukv"][i].reshape(KV_LORA, MLA_HEADS, NOPE_DIM + V_DIM)
    w_ukv = jnp.concatenate([w_ukv[:, :, :NOPE_DIM].reshape(KV_LORA, 512),
                             w_ukv[:, :, NOPE_DIM:].reshape(KV_LORA, 512)], axis=1).astype(BF16)
    z = jnp.zeros((D_MODEL,), F32)
    row = lambda *parts: jnp.concatenate(list(parts) + [z])[:D_MODEL]
    q_scale = HEAD_DIM ** -0.5 * LOG2E
    c_scale = QK_DIM ** -0.5 * LOG2E
    g_mla = p["g_qk_mla"][i]
    gains = jnp.stack([
        p["g_norm_mix"][i],
        row(_tile_to(p["g_qk_win"][i, 0], 512) * q_scale, _tile_to(p["g_qk_win"][i, 1], 128)),
        row(_tile_to(p["g_qk_nbr"][i, 0], 512) * q_scale, _tile_to(p["g_qk_nbr"][i, 1], 512)),
        row(p["g_q_lora"][i], p["g_kv_lora"][i]),
        row(_tile_to(g_mla[0, :NOPE_DIM], 512) * c_scale, _tile_to(g_mla[0, NOPE_DIM:], 256) * c_scale),
        row(_tile_to(g_mla[1, :NOPE_DIM], 512), _tile_to(g_mla[1, NOPE_DIM:], 256)),
        p["g_norm_ffn"][i],
        z]).astype(F32)
    wo_a = p["w_o_win"][i].reshape(WIN_HEADS, HEAD_DIM, D_MODEL)[WIN_Q_ORDER, :, :].reshape(512, D_MODEL)
    grp = lambda w: w.astype(BF16)
    return dict(
        w_a=w_a, w_uq=w_uq, w_ukv=w_ukv, gains=gains, w_gate=gates.astype(BF16),
        wo_a=wo_a.astype(BF16), wo_b=p["w_o_nbr"][i].astype(BF16), wo_c=p["w_o_mla"][i].astype(BF16),
        w_out=p["w_out"][i].astype(BF16),
        wg=grp(p["w_exp_gate"][i]), wu=grp(p["w_exp_up"][i]), wd=grp(p["w_exp_down"][i]),
        nbr_table=_nbr_bias_table(p["nbr_rel_bias"][i]))


def kernel(x_prompt, x_sample, cache_win_k, cache_win_v, cache_nbr_k, cache_nbr_v, cache_mla_ckv, cache_mla_krope, c, c_ctx, g_norm_mix, g_norm_ffn, w_ada, b_ada, w_in, g_qk_win, win_sink, g_qk_nbr, nbr_rel_bias, g_q_lora, g_kv_lora, w_uq, w_ukv, g_qk_mla, w_o_win, w_o_nbr, w_o_mla, w_out, w_router, b_router, w_exp_gate, w_exp_up, w_exp_down):
    p = dict(g_norm_mix=g_norm_mix, g_norm_ffn=g_norm_ffn, w_in=w_in, g_qk_win=g_qk_win,
             g_qk_nbr=g_qk_nbr, nbr_rel_bias=nbr_rel_bias, g_q_lora=g_q_lora, g_kv_lora=g_kv_lora,
             w_uq=w_uq, w_ukv=w_ukv, g_qk_mla=g_qk_mla, w_o_win=w_o_win, w_o_nbr=w_o_nbr,
             w_o_mla=w_o_mla, w_out=w_out, w_exp_gate=w_exp_gate, w_exp_up=w_exp_up,
             w_exp_down=w_exp_down)
    depth = w_in.shape[0]
    batch, seq, _ = x_prompt.shape
    dec_batch, dec_seq, _ = x_sample.shape
    past = cache_win_k.shape[2]

    n_c = 1 + dec_batch
    c_rows = -(-n_c // 8) * 8
    c_all = jnp.concatenate([c_ctx[None], c, jnp.zeros((c_rows - n_c, D_MODEL), F32)], axis=0)
    mods = _ada(c_all, w_ada, b_ada).reshape(depth, c_rows, 6, D_MODEL)
    mods = jnp.pad(mods, ((0, 0), (0, 0), (0, 2), (0, 0)))

    mats = (_block_ones(512, 64, 512, 64, 1.0 / HEAD_DIM), _block_ones(512, 64, 512, 64),
            _block_ones(256, 32, 512, 64), _block_ones(512, 64, 256, 32), _block_ones(256, 32, 256, 32))
    tabs = _rope_tables(dec_seq, 64) + _rope_tables(dec_seq, 32)
    sink = win_sink.astype(F32) * LOG2E
    w_r_t = w_router.T.astype(F32)
    b_r = b_router.astype(F32).reshape(N_EXPERTS, 1)
    layers = [_layer_params(i, p) for i in range(depth)]

    def ffn(merged, mod, rows_per_mod, lp):
        x1, route, h2p, counts = merged
        return _moe(x1, route, h2p, counts, mod, rows_per_mod, lp["wg"], lp["wu"], lp["wd"])

    def merge(x, oa, ob, oc, mod, rows_per_mod, lp):
        return _merge(x, oa, ob, oc, mod, rows_per_mod, lp["gains"], lp["w_gate"], lp["wo_a"],
                      lp["wo_b"], lp["wo_c"], lp["w_out"], w_r_t, b_r)

    n_ctx = batch * seq
    x = x_prompt.reshape(n_ctx, D_MODEL)
    states = []
    for i, lp in enumerate(layers):
        mod = mods[i, 0:1]
        outs = _inproj(x, mod, n_ctx, lp["gains"], lp["w_a"], lp["w_uq"], lp["w_ukv"], mats, None,
                       seq, True)
        oa, ob, oc = _ctx_attn(i, sink, seq, *outs[:9])
        states.append(outs[9:])
        x = ffn(merge(x, oa, ob, oc, mod, n_ctx, lp), mod, n_ctx, lp)
    y_prompt = x.reshape(batch, seq, D_MODEL)

    n_lat = dec_batch * dec_seq
    x = x_sample.reshape(n_lat, D_MODEL)
    cwk = cache_win_k.reshape(dec_batch, depth, past, 128).astype(BF16)
    cwv = cache_win_v.reshape(dec_batch, depth, past, 128).astype(BF16)
    cnk = cache_nbr_k.reshape(dec_batch, depth, past, 512).astype(BF16)
    cnv = cache_nbr_v.reshape(dec_batch, depth, past, 512).astype(BF16)
    for i, lp in enumerate(layers):
        mod = mods[i, 1:1 + dec_batch]
        qa, ka, va, qb, kb, vb, qc, kc, vc = _inproj(
            x, mod, dec_seq, lp["gains"], lp["w_a"], lp["w_uq"], lp["w_ukv"], mats, tabs, dec_seq, False)
        kr_t = jnp.tile(cache_mla_krope[:, i].reshape(dec_batch * past, ROPE_DIM), (1, MLA_HEADS))
        kc_c, vc_c = _mla_cache_keys(cache_mla_ckv[:, i].reshape(dec_batch * past, KV_LORA), kr_t,
                                     lp["gains"], lp["w_ukv"], mats[1], mats[3], mats[4])
        r3 = lambda a: a.reshape(dec_batch, dec_seq, a.shape[-1])
        oa = _win_attn(i, sink, r3(qa), r3(ka), r3(va), cwk, cwv)
        ob = _nbr_attn(i, r3(qb), r3(kb), r3(vb), cnk, cnv, lp["nbr_table"])
        oc = _mla_attn(r3(qc), r3(kc), r3(vc), kc_c.reshape(dec_batch, past, 1024),
                       vc_c.reshape(dec_batch, past, 512))
        flat = lambda a: a.reshape(n_lat, 512)
        x = ffn(merge(x, flat(oa), flat(ob), flat(oc), mod, dec_seq, lp), mod, dec_seq, lp)
    y_sample = x.reshape(dec_batch, dec_seq, D_MODEL)

    def stack(k, shape):
        return jnp.stack([s[k].reshape((batch, seq) + shape) for s in states], axis=1)

    return (y_prompt, y_sample,
            stack(0, (WIN_KV_HEADS, HEAD_DIM)), stack(1, (WIN_KV_HEADS, HEAD_DIM)),
            stack(2, (NBR_HEADS, HEAD_DIM)), stack(3, (NBR_HEADS, HEAD_DIM)),
            stack(4, (KV_LORA,)), stack(5, (ROPE_DIM,)))
```

```python
import functools

import numpy as np
import jax
import jax.numpy as jnp
from jax import lax
from jax.experimental import pallas as pl
from jax.experimental.pallas import tpu as pltpu
from jax.experimental.pallas import tpu_sc as plsc

D_MODEL = 1024
GRID_W = 64
HEAD_DIM = 64
WIN_HEADS = 8
WIN_KV_HEADS = 2
WINDOW = 128
NBR_HEADS = 8
NBR_ROWS = 8
NBR_COLS = 16
MLA_HEADS = 8
Q_LORA = 256
KV_LORA = 128
NOPE_DIM = 64
ROPE_DIM = 32
V_DIM = 64
QK_DIM = NOPE_DIM + ROPE_DIM
N_EXPERTS = 16
N_GROUPS = 4
EXPERTS_PER_GROUP = 4
D_FF = 512
ROPE_BASE = 10000.0
EPS = 1e-6

LANES = 128
LOG2E = 1.4426950408889634
NEG = -1e30
VMEM_LIMIT = 56 * 1024 * 1024

F32 = jnp.float32
BF16 = jnp.bfloat16

C_QA, C_KA, C_VA, C_QB, C_KB, C_VB, C_CQ, C_CKV, C_KR, C_END = (
    0, 512, 640, 768, 1280, 1792, 2304, 2560, 2688, 2944)
WIN_Q_ORDER = (0, 4, 1, 5, 2, 6, 3, 7)


def _cparams(sem):
    return pltpu.CompilerParams(dimension_semantics=sem, vmem_limit_bytes=VMEM_LIMIT)


def _dot(a, b):
    return jnp.dot(a, b, preferred_element_type=F32)


def _dot_nt(a, b):
    return lax.dot_general(a, b, (((1,), (1,)), ((), ())), preferred_element_type=F32)


def _split(x):
    hi = x.astype(BF16)
    lo = (x - hi.astype(F32)).astype(BF16)
    return hi, lo


def _gsum(x2, bmat):
    return _dot(x2.astype(BF16), bmat)


def _tile_lanes(t, width):
    reps = width // t.shape[-1]
    return t if reps == 1 else jnp.concatenate([t] * reps, axis=-1)


def _rotate(x, cos, sin_a, sin_b, half):
    w = x.shape[-1]
    up = pltpu.roll(x, w - half, 1)
    dn = pltpu.roll(x, half, 1)
    return (x * _tile_lanes(cos, w) + up * _tile_lanes(sin_a, w) + dn * _tile_lanes(sin_b, w))


def _norm_mod(x, gain, scale, shift):
    ms = jnp.mean(x * x, axis=-1, keepdims=True)
    return (x * lax.rsqrt(ms + EPS) * gain) * (1.0 + scale) + shift


def _ada_body(c_ref, w_ref, b_ref, o_ref):
    c = c_ref[...]
    a = c * (1.0 / (1.0 + jnp.exp(-c)))
    a_hi, a_lo = _split(a)
    w_hi, w_lo = _split(w_ref[0])
    o_ref[0] = _dot(a_hi, w_hi) + _dot(a_hi, w_lo) + _dot(a_lo, w_hi) + b_ref[0]


def _ada(c_all, w_ada, b_ada):
    depth = w_ada.shape[0]
    rows = c_all.shape[0]
    tn = 1536
    return pl.pallas_call(
        _ada_body,
        grid=(depth, 6 * D_MODEL // tn),
        in_specs=[pl.BlockSpec((rows, D_MODEL), lambda l, j: (0, 0)),
                  pl.BlockSpec((1, D_MODEL, tn), lambda l, j: (l, 0, j)),
                  pl.BlockSpec((1, 1, tn), lambda l, j: (l, 0, j))],
        out_specs=pl.BlockSpec((1, rows, tn), lambda l, j: (l, 0, j)),
        out_shape=jax.ShapeDtypeStruct((depth, rows, 6 * D_MODEL), F32),
        compiler_params=_cparams(("parallel", "parallel")),
        name="ada",
    )(c_all, w_ada, b_ada.reshape(depth, 1, 6 * D_MODEL))


def _mla_key_tail(ckvn_b, kr_t, g, wukv_ref, bnn, bnr, brr, rope_tabs, kc_ref, vc_ref):
    kv = _dot(ckvn_b, wukv_ref[...])
    kn = kv[:, 0:512]
    vc_ref[...] = kv[:, 512:1024].astype(BF16)
    kn2 = kn * kn
    kr2 = kr_t * kr_t
    kr_sum32 = _gsum(kr2, brr)
    ssn = (_gsum(kn2, bnn) + jnp.concatenate([kr_sum32, kr_sum32], axis=-1)) * (1.0 / QK_DIM)
    ssr = (_gsum(kn2, bnr) + kr_sum32) * (1.0 / QK_DIM)
    kn = kn * lax.rsqrt(ssn + EPS) * g[5:6, 0:512]
    kr = kr_t * lax.rsqrt(ssr + EPS) * g[5:6, 512:768]
    if rope_tabs is not None:
        kr = _rotate(kr, *rope_tabs, 8)
    for p in range(4):
        kc_ref[:, 256 * p:256 * p + 128] = kn[:, 128 * p:128 * p + 128].astype(BF16)
        q4 = 128 * (p // 2)
        kc_ref[:, 256 * p + 128:256 * p + 256] = kr[:, q4:q4 + 128].astype(BF16)


def _inproj_body(rope, states, *refs):
    (x_ref, mod_ref, g_ref, w_ref, wuq_ref, wukv_ref, b64_ref, bnn_ref, brn_ref, bnr_ref,
     brr_ref) = refs[:11]
    refs = refs[11:]
    if rope:
        tabs_w = tuple(r[...] for r in refs[0:3])
        tabs_m = tuple(r[...] for r in refs[3:6])
        refs = refs[6:]
    else:
        tabs_w = tabs_m = None
    qa_ref, ka_ref, va_ref, qb_ref, kb_ref, vb_ref, qc_ref, kc_ref, vc_ref = refs[:9]
    st = refs[9:]

    g = g_ref[...]
    mod = mod_ref[0]
    hb = _norm_mod(x_ref[...], g[0:1], mod[1:2], mod[0:1]).astype(BF16)

    def proj(a, b):
        return _dot(hb, w_ref[:, a:b])

    b64 = b64_ref[...]

    def head_norm(z, bmat, gain):
        return z * lax.rsqrt(_gsum(z * z, bmat) + EPS) * gain

    qa = head_norm(proj(C_QA, C_KA), b64, g[1:2, 0:512])
    ka = head_norm(proj(C_KA, C_VA), b64[0:128, 0:128], g[1:2, 512:640])
    va = proj(C_VA, C_QB)
    if states:
        st[0][...] = ka
        st[1][...] = va
    if rope:
        qa = _rotate(qa, *tabs_w, 16)
        ka = _rotate(ka, *tabs_w, 16)
    qa_ref[...] = qa.astype(BF16)
    ka_ref[...] = ka.astype(BF16)
    va_ref[...] = va.astype(BF16)

    qb = head_norm(proj(C_QB, C_KB), b64, g[2:3, 0:512])
    kb = head_norm(proj(C_KB, C_VB), b64, g[2:3, 512:1024])
    vb = proj(C_VB, C_CQ)
    if states:
        st[2][...] = kb
        st[3][...] = vb
    qb_ref[...] = qb.astype(BF16)
    kb_ref[...] = kb.astype(BF16)
    vb_ref[...] = vb.astype(BF16)

    cq = proj(C_CQ, C_CKV)
    cqn = cq * lax.rsqrt(jnp.mean(cq * cq, axis=-1, keepdims=True) + EPS) * g[3:4, 0:256]
    qq = _dot(cqn.astype(BF16), wuq_ref[...])
    qn, qr = qq[:, 0:512], qq[:, 512:768]
    qn2, qr2 = qn * qn, qr * qr
    bnn, brn, bnr, brr = bnn_ref[...], brn_ref[...], bnr_ref[...], brr_ref[...]
    ssn = (_gsum(qn2, bnn) + _gsum(qr2, brn)) * (1.0 / QK_DIM)
    ssr = (_gsum(qn2, bnr) + _gsum(qr2, brr)) * (1.0 / QK_DIM)
    qn = qn * lax.rsqrt(ssn + EPS) * g[4:5, 0:512]
    qr = qr * lax.rsqrt(ssr + EPS) * g[4:5, 512:768]
    if rope:
        qr = _rotate(qr, *tabs_m, 8)
    for p in range(4):
        qc_ref[:, 256 * p:256 * p + 128] = qn[:, 128 * p:128 * p + 128].astype(BF16)
        q4 = 128 * (p // 2)
        qc_ref[:, 256 * p + 128:256 * p + 256] = qr[:, q4:q4 + 128].astype(BF16)

    ckv = proj(C_CKV, C_KR)
    ckvn = ckv * lax.rsqrt(jnp.mean(ckv * ckv, axis=-1, keepdims=True) + EPS) * g[3:4, 256:384]
    kr_t = proj(C_KR, C_END)
    if states:
        st[4][...] = ckvn
        st[5][...] = kr_t[:, 0:ROPE_DIM]
    _mla_key_tail(ckvn.astype(BF16), kr_t, g, wukv_ref, bnn, bnr, brr, tabs_m, kc_ref, vc_ref)


def _const_spec(shape):
    nd = len(shape)
    return pl.BlockSpec(shape, lambda i, _nd=nd: (0,) * _nd)


def _inproj(x, mods, rows_per_mod, gains, w_a, w_uq, w_ukv, mats, rope_tabs, seq_len, states):
    n = x.shape[0]
    tm = min(512, n)
    rope = rope_tabs is not None
    row = lambda w: pl.BlockSpec((tm, w), lambda i: (i, 0))
    in_specs = [row(D_MODEL),
                pl.BlockSpec((1, 8, D_MODEL), lambda i: ((i * tm) // rows_per_mod, 0, 0)),
                _const_spec(gains.shape), _const_spec(w_a.shape), _const_spec(w_uq.shape),
                _const_spec(w_ukv.shape)] + [_const_spec(m.shape) for m in mats]
    args = [x, mods, gains, w_a, w_uq, w_ukv, *mats]
    if rope:
        tiles_per_seq = seq_len // tm
        in_specs += [pl.BlockSpec((tm, LANES), lambda i: (i % tiles_per_seq, 0))] * 6
        args += list(rope_tabs)
    widths = [512, 128, 128, 512, 512, 512, 1024, 1024, 512]
    out_shape = [jax.ShapeDtypeStruct((n, w), BF16) for w in widths]
    out_specs = [row(w) for w in widths]
    if states:
        swidths = [128, 128, 512, 512, KV_LORA, ROPE_DIM]
        out_shape += [jax.ShapeDtypeStruct((n, w), F32) for w in swidths]
        out_specs += [row(w) for w in swidths]
    return pl.pallas_call(
        functools.partial(_inproj_body, rope, states),
        grid=(n // tm,), in_specs=in_specs, out_specs=out_specs, out_shape=out_shape,
        compiler_params=_cparams(("parallel",)),
        name="inproj_lat" if rope else "inproj_ctx",
    )(*args)


def _mla_cache_body(ckv_ref, kr_ref, g_ref, wukv_ref, bnn_ref, bnr_ref, brr_ref, kc_ref, vc_ref):
    _mla_key_tail(ckv_ref[...].astype(BF16), kr_ref[...], g_ref[...], wukv_ref, bnn_ref[...],
                  bnr_ref[...], brr_ref[...], None, kc_ref, vc_ref)


def _mla_cache_keys(ckv, kr_t, gains, w_ukv, bnn, bnr, brr):
    n = ckv.shape[0]
    tm = min(512, n)
    row = lambda w: pl.BlockSpec((tm, w), lambda i: (i, 0))
    return pl.pallas_call(
        _mla_cache_body,
        grid=(n // tm,),
        in_specs=[row(KV_LORA), row(256), _const_spec(gains.shape), _const_spec(w_ukv.shape),
                  _const_spec(bnn.shape), _const_spec(bnr.shape), _const_spec(brr.shape)],
        out_specs=[row(1024), row(512)],
        out_shape=[jax.ShapeDtypeStruct((n, 1024), BF16), jax.ShapeDtypeStruct((n, 512), BF16)],
        compiler_params=_cparams(("parallel",)),
        name="mla_cache_keys",
    )(ckv, kr_t, gains, w_ukv, bnn, bnr, brr)


def _lane_mask(width, ranges):
    lane = lax.broadcasted_iota(jnp.int32, (1, width), 1)
    m = None
    for lo, hi in ranges:
        c = (lane >= lo) & (lane < hi)
        m = c if m is None else (m | c)
    return jnp.where(m, 1.0, 0.0).astype(BF16)


def _stack_heads(q, mask0, mask1):
    return jnp.concatenate([q * mask0, q * mask1], axis=0)


def _lane_tiles(x):
    return [x[:, j:j + LANES] for j in range(0, x.shape[1], LANES)]


def _softmax_pv(scores, values, sink=None):
    rows = scores[0].shape[0]
    mp = None
    for s in scores:
        for t in _lane_tiles(s):
            mp = t if mp is None else jnp.maximum(mp, t)
    base = sink if sink is not None else jnp.full((rows, LANES), NEG, F32)
    m = jnp.maximum(base, jnp.max(mp, axis=-1, keepdims=True))
    lp = None
    ps = []
    for s in scores:
        p = jnp.exp2(s - _tile_lanes(m, s.shape[1]))
        for t in _lane_tiles(p):
            lp = t if lp is None else lp + t
        ps.append(p.astype(BF16))
    p_all = ps[0] if len(ps) == 1 else jnp.concatenate(ps, axis=-1)
    v_all = values[0] if len(values) == 1 else jnp.concatenate(values, axis=0)
    o = _dot(p_all, v_all)
    if sink is not None:
        lane = lax.broadcasted_iota(jnp.int32, (rows, LANES), 1)
        lp = lp + jnp.where(lane == 0, jnp.exp2(sink - m), 0.0)
    return o / jnp.sum(lp, axis=-1, keepdims=True)


def _merge_heads(o, tq):
    lane = lax.broadcasted_iota(jnp.int32, (tq, LANES), 1)
    return jnp.where(lane < HEAD_DIM, o[0:tq], o[tq:2 * tq])


def _mla_masks(p_mod2):
    lane = lax.broadcasted_iota(jnp.int32, (1, 256), 1)
    r0 = 128 + 32 * (2 * p_mod2)
    m0 = (lane < 64) | ((lane >= r0) & (lane < r0 + 32))
    m1 = ((lane >= 64) & (lane < 128)) | ((lane >= r0 + 32) & (lane < r0 + 64))
    return (jnp.where(m0, 1.0, 0.0).astype(BF16), jnp.where(m1, 1.0, 0.0).astype(BF16))


def _sink_col(sink_ref, layer, h0, h1, tq):
    row = lax.broadcasted_iota(jnp.int32, (2 * tq, LANES), 0)
    return jnp.where(row < tq, sink_ref[layer, h0], sink_ref[layer, h1])


def _ctx_attn_body(layer, sink_ref, qa_ref, ka_ref, va_ref, qb_ref, kb_ref, vb_ref, qc_ref, kc_ref,
                   vc_ref, oa_ref, ob_ref, oc_ref):
    tq = qa_ref.shape[0]
    lo = _lane_mask(LANES, [(0, 64)])
    hi = _lane_mask(LANES, [(64, 128)])
    ka, va = ka_ref[...], va_ref[...]
    for j in range(4):
        sl = slice(128 * j, 128 * j + 128)
        qs = _stack_heads(qa_ref[:, sl], lo, hi)
        sink = _sink_col(sink_ref, layer, j, 4 + j, tq)
        o = _softmax_pv([_dot_nt(qs, ka)], [va], sink)
        oa_ref[:, sl] = _merge_heads(o, tq).astype(BF16)

        qs = _stack_heads(qb_ref[:, sl], lo, hi)
        o = _softmax_pv([_dot_nt(qs, kb_ref[:, sl])], [vb_ref[:, sl]])
        ob_ref[:, sl] = _merge_heads(o, tq).astype(BF16)

        m0, m1 = _mla_masks(j % 2)
        s2 = slice(256 * j, 256 * j + 256)
        qs = _stack_heads(qc_ref[:, s2], m0, m1)
        o = _softmax_pv([_dot_nt(qs, kc_ref[:, s2])], [vc_ref[:, sl]])
        oc_ref[:, sl] = _merge_heads(o, tq).astype(BF16)


def _ctx_attn(layer, sink, seq, qa, ka, va, qb, kb, vb, qc, kc, vc):
    n = qa.shape[0]
    row = lambda w: pl.BlockSpec((seq, w), lambda b: (b, 0))
    ins = [qa, ka, va, qb, kb, vb, qc, kc, vc]
    return pl.pallas_call(
        functools.partial(_ctx_attn_body, layer),
        grid=(n // seq,),
        in_specs=[pl.BlockSpec(memory_space=pltpu.SMEM)] + [row(a.shape[1]) for a in ins],
        out_specs=[row(512)] * 3,
        out_shape=[jax.ShapeDtypeStruct((n, 512), BF16)] * 3,
        compiler_params=_cparams(("parallel",)),
        name="ctx_attn",
    )(sink, *ins)


def _win_body(layer, sink_ref, q_ref, k_ref, v_ref, kc_ref, vc_ref, o_ref):
    tq = q_ref.shape[1]
    seq = k_ref.shape[1]
    kw = 3 * tq
    i = pl.program_id(1)
    kstart = pl.multiple_of(jnp.clip((i - 1) * tq, 0, seq - kw), tq)
    k_all = jnp.concatenate([k_ref[0, pl.ds(kstart, kw), :], kc_ref[0, 0]], axis=0)
    v_all = jnp.concatenate([v_ref[0, pl.ds(kstart, kw), :], vc_ref[0, 0]], axis=0)
    q_pos = i * tq + lax.broadcasted_iota(jnp.int32, (2 * tq, kw), 0) % tq
    k_pos = kstart + lax.broadcasted_iota(jnp.int32, (2 * tq, kw), 1)
    band = jnp.abs(q_pos - k_pos) <= WINDOW
    lo = _lane_mask(LANES, [(0, 64)])
    hi = _lane_mask(LANES, [(64, 128)])
    qs = jnp.concatenate([_stack_heads(q_ref[0, :, 128 * j:128 * j + 128], lo, hi) for j in range(4)],
                         axis=0)
    s = _dot_nt(qs, k_all)
    for j in range(4):
        sb = s[2 * tq * j:2 * tq * (j + 1)]
        s_band = jnp.where(band, sb[:, 0:kw], NEG)
        sink = _sink_col(sink_ref, layer, j, 4 + j, tq)
        o = _softmax_pv([s_band, sb[:, kw:]], [v_all], sink)
        o_ref[0, :, 128 * j:128 * j + 128] = _merge_heads(o, tq).astype(BF16)


def _win_attn(layer, sink, q, k, v, kc, vc):
    b, seq, _ = q.shape
    tq = 128
    past = kc.shape[2]
    return pl.pallas_call(
        functools.partial(_win_body, layer),
        grid=(b, seq // tq),
        in_specs=[pl.BlockSpec(memory_space=pltpu.SMEM),
                  pl.BlockSpec((1, tq, 512), lambda bi, i: (bi, i, 0)),
                  pl.BlockSpec((1, seq, 128), lambda bi, i: (bi, 0, 0)),
                  pl.BlockSpec((1, seq, 128), lambda bi, i: (bi, 0, 0)),
                  pl.BlockSpec((1, 1, past, 128), lambda bi, i: (bi, layer, 0, 0)),
                  pl.BlockSpec((1, 1, past, 128), lambda bi, i: (bi, layer, 0, 0))],
        out_specs=pl.BlockSpec((1, tq, 512), lambda bi, i: (bi, i, 0)),
        out_shape=jax.ShapeDtypeStruct((b, seq, 512), BF16),
        compiler_params=_cparams(("parallel", "arbitrary")),
        name="win_attn",
    )(sink, q, k, v, kc, vc)


NBR_TILE_ROWS = 4
NBR_WIN_ROWS = NBR_TILE_ROWS + NBR_ROWS
NBR_TAB_PAD = NBR_WIN_ROWS - NBR_ROWS


def _nbr_body(rows, q_ref, k_ref, v_ref, kc_ref, vc_ref, tab_ref, o_ref):
    tq = NBR_TILE_ROWS * GRID_W
    kw = NBR_WIN_ROWS * GRID_W
    i = pl.program_id(1)
    r0 = NBR_TILE_ROWS * i
    ws = jnp.clip(r0 - NBR_ROWS // 2, 0, rows - NBR_WIN_ROWS)
    kstart = pl.multiple_of(ws * GRID_W, LANES)
    q_row = r0 + lax.broadcasted_iota(jnp.int32, (2 * tq, kw), 0) % tq // GRID_W
    k_row = ws + lax.broadcasted_iota(jnp.int32, (2 * tq, kw), 1) // GRID_W
    rs = jnp.clip(q_row - NBR_ROWS // 2, 0, rows - NBR_ROWS)
    valid = (k_row >= rs) & (k_row < rs + NBR_ROWS)
    lo = _lane_mask(LANES, [(0, 64)])
    hi = _lane_mask(LANES, [(64, 128)])
    for j in range(4):
        sl = slice(128 * j, 128 * j + 128)
        qs = _stack_heads(q_ref[0, :, sl], lo, hi)
        k = k_ref[0, pl.ds(kstart, kw), sl]
        v = v_ref[0, pl.ds(kstart, kw), sl]
        pieces = []
        for h in (2 * j, 2 * j + 1):
            for ql in range(NBR_TILE_ROWS):
                d0 = ws - r0 - ql + (NBR_ROWS - 1) + NBR_TAB_PAD
                pieces.append(jnp.concatenate(
                    [tab_ref[h, d0 + 2 * m] for m in range(NBR_WIN_ROWS // 2)], axis=-1))
        bias = jnp.concatenate(pieces, axis=0)
        s_nb = jnp.where(valid, _dot_nt(qs, k) + bias, NEG)
        s_ctx = _dot_nt(qs, kc_ref[0, 0, :, sl])
        o = _softmax_pv([s_nb, s_ctx], [v, vc_ref[0, 0, :, sl]])
        o_ref[0, :, sl] = _merge_heads(o, tq).astype(BF16)


def _nbr_attn(layer, q, k, v, kc, vc, table):
    b, seq, _ = q.shape
    rows = seq // GRID_W
    tq = NBR_TILE_ROWS * GRID_W
    past = kc.shape[2]
    return pl.pallas_call(
        functools.partial(_nbr_body, rows),
        grid=(b, seq // tq),
        in_specs=[pl.BlockSpec((1, tq, 512), lambda bi, i: (bi, i, 0)),
                  pl.BlockSpec((1, seq, 512), lambda bi, i: (bi, 0, 0)),
                  pl.BlockSpec((1, seq, 512), lambda bi, i: (bi, 0, 0)),
                  pl.BlockSpec((1, 1, past, 512), lambda bi, i: (bi, layer, 0, 0)),
                  pl.BlockSpec((1, 1, past, 512), lambda bi, i: (bi, layer, 0, 0)),
                  pl.BlockSpec(table.shape, lambda bi, i: (0, 0, 0, 0))],
        out_specs=pl.BlockSpec((1, tq, 512), lambda bi, i: (bi, i, 0)),
        out_shape=jax.ShapeDtypeStruct((b, seq, 512), BF16),
        compiler_params=_cparams(("parallel", "arbitrary")),
        name="nbr_attn",
    )(q, k, v, kc, vc, table)


def _nbr_bias_table(rel_bias):
    col = np.arange(GRID_W)
    cs = np.clip(col - NBR_COLS // 2, 0, GRID_W - NBR_COLS)
    kc = np.arange(GRID_W)
    ok = (kc[None, :] >= cs[:, None]) & (kc[None, :] < cs[:, None] + NBR_COLS)
    dc = kc[None, :] - col[:, None] + (NBR_COLS - 1)
    pick = (dc[:, :, None] == np.arange(2 * NBR_COLS - 1)[None, None, :]) & ok[:, :, None]
    t = jnp.einsum("hdk,qck->hdqc", rel_bias.astype(F32) * LOG2E, jnp.asarray(pick, F32),
                   precision=lax.Precision.HIGHEST)
    t = jnp.where(jnp.asarray(ok)[None, None], t, NEG)
    t = jnp.pad(t, ((0, 0), (NBR_TAB_PAD, NBR_TAB_PAD), (0, 0), (0, 0)))
    return jnp.concatenate([t[:, :-1], t[:, 1:]], axis=-1)


MLA_KEY_CHUNK = 512
MLA_ROW_BLOCK = 256


def _mla_body(q_ref, kl_ref, kc_ref, vl_ref, vc_ref, o_ref, qs_sc, m_sc, l_sc, acc_sc):
    tq = q_ref.shape[1]
    rows = 2 * tq
    seq = kl_ref.shape[1]
    past = kc_ref.shape[1]
    tk = min(MLA_KEY_CHUNK, past)
    rb = min(MLA_ROW_BLOCK, rows)
    m0, m1 = _mla_masks(pl.program_id(1) % 2)
    qs_sc[...] = _stack_heads(q_ref[0], m0, m1)
    m_sc[...] = jnp.full(m_sc.shape, NEG, F32)
    l_sc[...] = jnp.zeros(l_sc.shape, F32)
    acc_sc[...] = jnp.zeros(acc_sc.shape, F32)

    def step(k, v):
        s = _dot_nt(qs_sc[...], k)
        for r in range(rows // rb):
            sl = slice(r * rb, (r + 1) * rb)
            sb = s[sl]
            m_prev = m_sc[sl]
            m_new = jnp.maximum(m_prev, jnp.max(sb, axis=-1, keepdims=True))
            alpha = jnp.exp2(m_prev - m_new)
            p = jnp.exp2(sb - _tile_lanes(m_new, tk))
            psum = p[:, 0:LANES]
            for j in range(1, tk // LANES):
                psum = psum + p[:, j * LANES:(j + 1) * LANES]
            l_sc[sl] = alpha * l_sc[sl] + psum
            acc_sc[sl] = alpha * acc_sc[sl] + _dot(p.astype(BF16), v)
            m_sc[sl] = m_new

    for c in range(seq // tk):
        step(kl_ref[0, c * tk:(c + 1) * tk, :], vl_ref[0, c * tk:(c + 1) * tk, :])
    for c in range(past // tk):
        step(kc_ref[0, c * tk:(c + 1) * tk, :], vc_ref[0, c * tk:(c + 1) * tk, :])
    l = jnp.sum(l_sc[...], axis=-1, keepdims=True)
    o_ref[0] = _merge_heads(acc_sc[...] / l, tq).astype(BF16)


def _mla_attn(q, kl, vl, kc, vc):
    b, seq, _ = q.shape
    past = kc.shape[1]
    tq = min(512, seq)
    return pl.pallas_call(
        _mla_body,
        grid=(b, 4, seq // tq),
        in_specs=[pl.BlockSpec((1, tq, 256), lambda bi, p, qi: (bi, qi, p)),
                  pl.BlockSpec((1, seq, 256), lambda bi, p, qi: (bi, 0, p)),
                  pl.BlockSpec((1, past, 256), lambda bi, p, qi: (bi, 0, p)),
                  pl.BlockSpec((1, seq, 128), lambda bi, p, qi: (bi, 0, p)),
                  pl.BlockSpec((1, past, 128), lambda bi, p, qi: (bi, 0, p))],
        out_specs=pl.BlockSpec((1, tq, 128), lambda bi, p, qi: (bi, qi, p)),
        out_shape=jax.ShapeDtypeStruct((b, seq, 512), BF16),
        scratch_shapes=[pltpu.VMEM((2 * tq, 256), BF16), pltpu.VMEM((2 * tq, LANES), F32),
                        pltpu.VMEM((2 * tq, LANES), F32), pltpu.VMEM((2 * tq, LANES), F32)],
        compiler_params=_cparams(("parallel", "parallel", "arbitrary")),
        name="mla_attn",
    )(q, kl, kc, vl, vc)


def _pack_pairs(x):
    w = x.shape[1] // 2
    hi = lax.bitcast_convert_type(x[:, :w].astype(BF16).astype(F32), jnp.int32)
    lo = lax.bitcast_convert_type(x[:, w:].astype(BF16).astype(F32), jnp.int32)
    return (hi & jnp.int32(-65536)) | lax.shift_right_logical(lo, jnp.int32(16))


def _unpack_pairs(p):
    hi = lax.bitcast_convert_type(p & jnp.int32(-65536), F32)
    lo = lax.bitcast_convert_type(lax.shift_left(p, jnp.int32(16)), F32)
    return jnp.concatenate([hi, lo], axis=-1)


def _merge_body(x_ref, oa_ref, ob_ref, oc_ref, mod_ref, g_ref, wg_ref, woa_ref, wob_ref, woc_ref,
                wout_ref, wr_ref, br_ref, tri_ref, x1_ref, route_ref, h2p_ref, count_ref, count_sc):
    x = x_ref[...]
    g = g_ref[...]
    mod = mod_ref[0]
    hb = _norm_mod(x, g[0:1], mod[1:2], mod[0:1]).astype(BF16)
    m = None
    for br, (o_ref, wo_ref) in enumerate(((oa_ref, woa_ref), (ob_ref, wob_ref), (oc_ref, woc_ref))):
        z = _dot(hb, wg_ref[:, D_MODEL * br:D_MODEL * (br + 1)])
        gate = 1.0 / (1.0 + jnp.exp(-z))
        t = gate * _dot(o_ref[...], wo_ref[...])
        m = t if m is None else m + t
    y = _dot(m.astype(BF16), wout_ref[...])
    x1 = x + mod[2:3] * y
    x1_ref[...] = x1

    h2 = _norm_mod(x1, g[6:7], mod[4:5], mod[3:4])
    h_hi, h_lo = _split(h2)
    w_hi, w_lo = _split(wr_ref[...])
    both_w = _dot_nt(jnp.concatenate([w_hi, w_lo], axis=0), h_hi)
    logits = both_w[0:N_EXPERTS] + both_w[N_EXPERTS:2 * N_EXPERTS] + _dot_nt(w_hi, h_lo)
    score = 1.0 / (1.0 + jnp.exp(-logits))
    sel = score + br_ref[...]
    sel_r = [sel[e:e + 1] for e in range(N_EXPERTS)]
    sc_r = [score[e:e + 1] for e in range(N_EXPERTS)]
    picked = []
    for e in range(N_EXPERTS):
        grp, a = divmod(e, EXPERTS_PER_GROUP)
        rank = None
        for bb in range(EXPERTS_PER_GROUP):
            if bb == a:
                continue
            o = sel_r[grp * EXPERTS_PER_GROUP + bb]
            beats = (o >= sel_r[e]) if bb < a else (o > sel_r[e])
            r = jnp.where(beats, 1.0, 0.0)
            rank = r if rank is None else rank + r
        picked.append(rank < 2.0)
    gscore = []
    for grp in range(N_GROUPS):
        tot = None
        for a in range(EXPERTS_PER_GROUP):
            e = grp * EXPERTS_PER_GROUP + a
            t = jnp.where(picked[e], sel_r[e], 0.0)
            tot = t if tot is None else tot + t
        gscore.append(tot)
    best = jnp.zeros_like(gscore[0])
    best_v = gscore[0]
    for grp in range(1, N_GROUPS):
        upd = gscore[grp] > best_v
        best = jnp.where(upd, float(grp), best)
        best_v = jnp.where(upd, gscore[grp], best_v)
    cw, pk = [], []
    for a in range(EXPERTS_PER_GROUP):
        tot = flag = None
        for grp in range(N_GROUPS):
            e = grp * EXPERTS_PER_GROUP + a
            f = (best == float(grp)) & picked[e]
            t = jnp.where(f, sc_r[e], 0.0)
            tot = t if tot is None else tot + t
            flag = f if flag is None else (flag | f)
        cw.append(tot)
        pk.append(flag)
    den = cw[0] + cw[1] + cw[2] + cw[3]
    first = jnp.where(pk[0], 0.0, jnp.where(pk[1], 1.0, jnp.where(pk[2], 2.0, 3.0)))
    second = jnp.where(pk[3], 3.0, jnp.where(pk[2], 2.0, jnp.where(pk[1], 1.0, 0.0)))
    slot_e, slot_w = [], []
    for which in (first, second):
        tot = None
        for a in range(EXPERTS_PER_GROUP):
            t = jnp.where(which == float(a), cw[a], 0.0)
            tot = t if tot is None else tot + t
        slot_w.append(tot / den)
        slot_e.append(best * float(EXPERTS_PER_GROUP) + which)

    @pl.when(pl.program_id(0) == 0)
    def _():
        count_sc[...] = jnp.zeros(count_sc.shape, F32)

    tm = x.shape[0]
    eid = lax.broadcasted_iota(jnp.int32, (N_EXPERTS, tm), 0).astype(F32)
    oh = [eid == slot_e[0], eid == slot_e[1]]
    both = jnp.where(oh[0] | oh[1], 1.0, 0.0)
    seen = count_sc[...][:, 0:1] + _dot(both.astype(BF16), tri_ref[...])
    for k in range(2):
        route_ref[k:k + 1, :] = slot_w[k]
        route_ref[2 + k:3 + k, :] = slot_e[k]
        route_ref[4 + k:5 + k, :] = jnp.sum(jnp.where(oh[k], seen, 0.0), axis=0, keepdims=True)
    route_ref[6:8, :] = jnp.zeros((2, tm), F32)
    count_sc[...] = count_sc[...] + jnp.sum(both, axis=-1, keepdims=True)
    count_ref[...] = count_sc[...]
    h2p_ref[...] = _pack_pairs(h2)


def _merge(x, oa, ob, oc, mods, rows_per_mod, gains, w_gate, wo_a, wo_b, wo_c, w_out, w_r_t, b_r):
    n = x.shape[0]
    tm = min(512, n)
    row = lambda w: pl.BlockSpec((tm, w), lambda i: (i, 0))
    tri = jnp.asarray(np.triu(np.ones((tm, tm), np.float32), 1), BF16)
    consts = [gains, w_gate, wo_a, wo_b, wo_c, w_out, w_r_t, b_r, tri]
    return pl.pallas_call(
        _merge_body,
        grid=(n // tm,),
        in_specs=[row(D_MODEL), row(512), row(512), row(512),
                  pl.BlockSpec((1, 8, D_MODEL), lambda i: ((i * tm) // rows_per_mod, 0, 0))]
                 + [_const_spec(c.shape) for c in consts],
        out_specs=[row(D_MODEL), pl.BlockSpec((8, tm), lambda i: (0, i)), row(512),
                   pl.BlockSpec((N_EXPERTS, LANES), lambda i: (0, 0))],
        out_shape=[jax.ShapeDtypeStruct((n, D_MODEL), F32), jax.ShapeDtypeStruct((8, n), F32),
                   jax.ShapeDtypeStruct((n, 512), jnp.int32),
                   jax.ShapeDtypeStruct((N_EXPERTS, LANES), F32)],
        scratch_shapes=[pltpu.VMEM((N_EXPERTS, LANES), F32)],
        compiler_params=_cparams(("arbitrary",)),
        name="merge",
    )(x, oa, ob, oc, mods, *consts)


EXPERT_TILE = 512
SC_CORES = 2
SC_SUBCORES = 16
SC_WORKERS = SC_CORES * SC_SUBCORES
SC_WINDOW = 128


def _sc_mesh():
    return plsc.VectorSubcoreMesh(core_axis_name="c", subcore_axis_name="s", num_cores=SC_CORES,
                                  num_subcores=SC_SUBCORES)


def _sc_window_base(steps, j):
    wid = lax.axis_index("s") * SC_CORES + lax.axis_index("c")
    return pl.multiple_of((wid * steps + j) * SC_WINDOW, SC_WINDOW)


def _sc_dispatch(rows, pos0, pos1, n_out):
    n, w = rows.shape
    steps = n // (SC_WORKERS * SC_WINDOW)

    @functools.partial(
        pl.kernel, out_type=jax.ShapeDtypeStruct((n_out, w), rows.dtype), mesh=_sc_mesh(),
        scratch_types=[pltpu.VMEM((SC_WINDOW,), jnp.int32), pltpu.VMEM((SC_WINDOW,), jnp.int32),
                       pltpu.VMEM((SC_WINDOW, w), rows.dtype)],
        name="moe_dispatch")
    def run(x_hbm, i0_hbm, i1_hbm, o_hbm, i0_v, i1_v, rows_v):
        @pl.loop(0, steps)
        def _(j):
            base = _sc_window_base(steps, j)
            pltpu.sync_copy(i0_hbm.at[pl.ds(base, SC_WINDOW)], i0_v)
            pltpu.sync_copy(i1_hbm.at[pl.ds(base, SC_WINDOW)], i1_v)
            pltpu.sync_copy(x_hbm.at[pl.ds(base, SC_WINDOW)], rows_v)
            pltpu.sync_copy(rows_v, o_hbm.at[i0_v])
            pltpu.sync_copy(rows_v, o_hbm.at[i1_v])

    return run(rows, pos0, pos1)


def _sc_collect(rows, pos0, pos1):
    n = pos0.shape[0]
    w = rows.shape[1]
    steps = n // (SC_WORKERS * SC_WINDOW)
    out = jax.ShapeDtypeStruct((n, w), rows.dtype)

    @functools.partial(
        pl.kernel, out_type=[out, out], mesh=_sc_mesh(),
        scratch_types=[pltpu.VMEM((SC_WINDOW,), jnp.int32), pltpu.VMEM((SC_WINDOW, w), rows.dtype)],
        name="moe_collect")
    def run(y_hbm, i0_hbm, i1_hbm, o0_hbm, o1_hbm, i_v, rows_v):
        @pl.loop(0, steps)
        def _(j):
            base = _sc_window_base(steps, j)
            for i_hbm, o_hbm in ((i0_hbm, o0_hbm), (i1_hbm, o1_hbm)):
                pltpu.sync_copy(i_hbm.at[pl.ds(base, SC_WINDOW)], i_v)
                pltpu.sync_copy(y_hbm.at[i_v], rows_v)
                pltpu.sync_copy(rows_v, o_hbm.at[pl.ds(base, SC_WINDOW)])

    return run(rows, pos0, pos1)


def _experts_body(te_ref, nv_ref, xs_ref, wg_ref, wu_ref, wd_ref, ys_ref, wg_sc, wu_sc, wd_sc):
    j = pl.program_id(0)

    @pl.when((j == 0) | (te_ref[j] != te_ref[jnp.maximum(j - 1, 0)]))
    def _():
        wg_sc[...] = wg_ref[0, 0].astype(BF16)
        wu_sc[...] = wu_ref[0, 0].astype(BF16)
        wd_sc[...] = wd_ref[0, 0].astype(BF16)

    @pl.when(j < nv_ref[0])
    def _():
        x = _unpack_pairs(xs_ref[...]).astype(BF16)
        zg = _dot(x, wg_sc[...])
        act = zg * (1.0 / (1.0 + jnp.exp(-zg))) * _dot(x, wu_sc[...])
        ys_ref[...] = _pack_pairs(_dot(act.astype(BF16), wd_sc[...]))

    @pl.when(j >= nv_ref[0])
    def _():
        ys_ref[...] = jnp.zeros(ys_ref.shape, ys_ref.dtype)


def _experts(layer, xs, tile_expert, n_valid, wg, wu, wd):
    p = xs.shape[0]
    wspec = lambda r, c: pl.BlockSpec((1, 1, r, c), lambda j, te, nv: (layer, te[j], 0, 0))
    grid_spec = pltpu.PrefetchScalarGridSpec(
        num_scalar_prefetch=2,
        grid=(p // EXPERT_TILE,),
        in_specs=[pl.BlockSpec((EXPERT_TILE, 512), lambda j, te, nv: (j, 0)),
                  wspec(D_MODEL, D_FF), wspec(D_MODEL, D_FF), wspec(D_FF, D_MODEL)],
        out_specs=pl.BlockSpec((EXPERT_TILE, 512), lambda j, te, nv: (j, 0)),
        scratch_shapes=[pltpu.VMEM((D_MODEL, D_FF), BF16), pltpu.VMEM((D_MODEL, D_FF), BF16),
                        pltpu.VMEM((D_FF, D_MODEL), BF16)])
    return pl.pallas_call(
        _experts_body, grid_spec=grid_spec,
        out_shape=jax.ShapeDtypeStruct((p, 512), jnp.int32),
        compiler_params=_cparams(("arbitrary",)),
        name="experts",
    )(tile_expert, n_valid, xs, wg, wu, wd)


def _combine_body(x1_ref, y0_ref, y1_ref, route_ref, mod_ref, o_ref):
    tm = x1_ref.shape[0]
    eye = jnp.where(lax.broadcasted_iota(jnp.int32, (tm, tm), 0)
                    == lax.broadcasted_iota(jnp.int32, (tm, tm), 1), 1.0, 0.0).astype(BF16)
    r_hi, r_lo = _split(route_ref[...])
    wcol = _dot_nt(eye, r_hi) + _dot_nt(eye, r_lo)
    moe = wcol[:, 0:1] * _unpack_pairs(y0_ref[...]) + wcol[:, 1:2] * _unpack_pairs(y1_ref[...])
    o_ref[...] = x1_ref[...] + mod_ref[0][5:6] * moe


def _combine(x1, y0, y1, route, mods, rows_per_mod):
    n = x1.shape[0]
    tm = min(512, n)
    row = lambda w: pl.BlockSpec((tm, w), lambda i: (i, 0))
    return pl.pallas_call(
        _combine_body,
        grid=(n // tm,),
        in_specs=[row(D_MODEL), row(512), row(512), pl.BlockSpec((8, tm), lambda i: (0, i)),
                  pl.BlockSpec((1, 8, D_MODEL), lambda i: ((i * tm) // rows_per_mod, 0, 0))],
        out_specs=row(D_MODEL),
        out_shape=jax.ShapeDtypeStruct((n, D_MODEL), F32),
        compiler_params=_cparams(("parallel",)),
        name="combine",
    )(x1, y0, y1, route, mods)


def _moe(layer, x1, route, h2p, counts, mods, rows_per_mod, wg, wu, wd):
    n = x1.shape[0]
    p = 2 * n + N_EXPERTS * EXPERT_TILE
    cnt = counts[:, 0].astype(jnp.int32)
    padded = (cnt + EXPERT_TILE - 1) // EXPERT_TILE * EXPERT_TILE
    seg_end = jnp.cumsum(padded)
    seg_off = seg_end - padded
    experts = jnp.arange(N_EXPERTS, dtype=jnp.int32)

    def position(k):
        e = route[2 + k].astype(jnp.int32)
        off = jnp.sum(jnp.where(e[:, None] == experts[None], seg_off[None], 0), axis=1)
        return off + route[4 + k].astype(jnp.int32)

    pos0, pos1 = position(0), position(1)
    tile_start = jnp.arange(p // EXPERT_TILE, dtype=jnp.int32) * EXPERT_TILE
    tile_expert = jnp.sum(tile_start[:, None] >= seg_end[None], axis=1).astype(jnp.int32)
    tile_expert = jnp.minimum(tile_expert, N_EXPERTS - 1)
    n_valid = (seg_end[-1:] // EXPERT_TILE).astype(jnp.int32)

    xs = _sc_dispatch(h2p, pos0, pos1, p)
    ys = _experts(layer, xs, tile_expert, n_valid, wg, wu, wd)
    y0, y1 = _sc_collect(ys, pos0, pos1)
    return _combine(x1, y0, y1, route, mods, rows_per_mod)


def _block_ones(n_in, g_in, n_out, g_out, value=1.0):
    r = np.arange(n_in)[:, None] // g_in
    c = np.arange(n_out)[None, :] // g_out
    return jnp.asarray(np.where(r == c, value, 0.0), dtype=BF16)


def _rope_tables(seq, head_w):
    pos = np.arange(seq)
    rows, cols = pos // GRID_W, pos % GRID_W
    a = head_w // 2
    half = a // 2
    freqs = (ROPE_BASE ** (-np.arange(half, dtype=np.float32) / half)).astype(np.float32)
    lane = np.arange(LANES) % head_w
    within = lane % a
    first = within < half
    p = np.where((lane // a == 0)[None, :], rows[:, None], cols[:, None]).astype(np.float32)
    ang = (p * freqs[within % half][None, :]).astype(np.float32)
    cos, sin = np.cos(ang), np.sin(ang)
    return (jnp.asarray(cos, F32), jnp.asarray(np.where(first[None], -sin, 0.0), F32),
            jnp.asarray(np.where(first[None], 0.0, sin), F32))


def _tile_to(v, width):
    return jnp.tile(v, width // v.shape[0])


def _layer_params(i, p):
    w_in = p["w_in"][i]
    sp = np.cumsum((512, 128, 128, 512, 512, 512, Q_LORA, KV_LORA, ROPE_DIM))
    qa, ka, va, qb, kb, vb, cq, ckv, kr, gates = jnp.split(w_in, [int(s) for s in sp], axis=1)
    qa = qa.reshape(D_MODEL, WIN_HEADS, HEAD_DIM)[:, WIN_Q_ORDER, :].reshape(D_MODEL, 512)
    w_a = jnp.concatenate([qa, ka, va, qb, kb, vb, cq, ckv, jnp.tile(kr, (1, MLA_HEADS))],
                          axis=1).astype(BF16)
    w_uq = p["w_uq"][i].reshape(Q_LORA, MLA_HEADS, QK_DIM)
    w_uq = jnp.concatenate([w_uq[:, :, :NOPE_DIM].reshape(Q_LORA, 512),
                            w_uq[:, :, NOPE_DIM:].reshape(Q_LORA, 256)], axis=1).astype(BF16)
    w_ukv = p["w_ukv"][i].reshape(KV_LORA, MLA_HEADS, NOPE_DIM + V_DIM)
    w_ukv = jnp.concatenate([w_ukv[:, :, :NOPE_DIM].reshape(KV_LORA, 512),
                             w_ukv[:, :, NOPE_DIM:].reshape(KV_LORA, 512)], axis=1).astype(BF16)
    z = jnp.zeros((D_MODEL,), F32)
    row = lambda *parts: jnp.concatenate(list(parts) + [z])[:D_MODEL]
    q_scale = HEAD_DIM ** -0.5 * LOG2E
    c_scale = QK_DIM ** -0.5 * LOG2E
    g_mla = p["g_qk_mla"][i]
    gains = jnp.stack([
        p["g_norm_mix"][i],
        row(_tile_to(p["g_qk_win"][i, 0], 512) * q_scale, _tile_to(p["g_qk_win"][i, 1], 128)),
        row(_tile_to(p["g_qk_nbr"][i, 0], 512) * q_scale, _tile_to(p["g_qk_nbr"][i, 1], 512)),
        row(p["g_q_lora"][i], p["g_kv_lora"][i]),
        row(_tile_to(g_mla[0, :NOPE_DIM], 512) * c_scale, _tile_to(g_mla[0, NOPE_DIM:], 256) * c_scale),
        row(_tile_to(g_mla[1, :NOPE_DIM], 512), _tile_to(g_mla[1, NOPE_DIM:], 256)),
        p["g_norm_ffn"][i],
        z]).astype(F32)
    wo_a = p["w_o_win"][i].reshape(WIN_HEADS, HEAD_DIM, D_MODEL)[WIN_Q_ORDER, :, :].reshape(512, D_MODEL)
    return dict(
        layer=i, w_a=w_a, w_uq=w_uq, w_ukv=w_ukv, gains=gains, w_gate=gates.astype(BF16),
        wo_a=wo_a.astype(BF16), wo_b=p["w_o_nbr"][i].astype(BF16), wo_c=p["w_o_mla"][i].astype(BF16),
        w_out=p["w_out"][i].astype(BF16),
        nbr_table=_nbr_bias_table(p["nbr_rel_bias"][i]))


def kernel(x_prompt, x_sample, cache_win_k, cache_win_v, cache_nbr_k, cache_nbr_v, cache_mla_ckv, cache_mla_krope, c, c_ctx, g_norm_mix, g_norm_ffn, w_ada, b_ada, w_in, g_qk_win, win_sink, g_qk_nbr, nbr_rel_bias, g_q_lora, g_kv_lora, w_uq, w_ukv, g_qk_mla, w_o_win, w_o_nbr, w_o_mla, w_out, w_router, b_router, w_exp_gate, w_exp_up, w_exp_down):
    p = dict(g_norm_mix=g_norm_mix, g_norm_ffn=g_norm_ffn, w_in=w_in, g_qk_win=g_qk_win,
             g_qk_nbr=g_qk_nbr, nbr_rel_bias=nbr_rel_bias, g_q_lora=g_q_lora, g_kv_lora=g_kv_lora,
             w_uq=w_uq, w_ukv=w_ukv, g_qk_mla=g_qk_mla, w_o_win=w_o_win, w_o_nbr=w_o_nbr,
             w_o_mla=w_o_mla, w_out=w_out, w_exp_gate=w_exp_gate, w_exp_up=w_exp_up,
             w_exp_down=w_exp_down)
    depth = w_in.shape[0]
    batch, seq, _ = x_prompt.shape
    dec_batch, dec_seq, _ = x_sample.shape
    past = cache_win_k.shape[2]

    n_c = 1 + dec_batch
    c_rows = -(-n_c // 8) * 8
    c_all = jnp.concatenate([c_ctx[None], c, jnp.zeros((c_rows - n_c, D_MODEL), F32)], axis=0)
    mods = _ada(c_all, w_ada, b_ada).reshape(depth, c_rows, 6, D_MODEL)
    mods = jnp.pad(mods, ((0, 0), (0, 0), (0, 2), (0, 0)))

    mats = (_block_ones(512, 64, 512, 64, 1.0 / HEAD_DIM), _block_ones(512, 64, 512, 64),
            _block_ones(256, 32, 512, 64), _block_ones(512, 64, 256, 32), _block_ones(256, 32, 256, 32))
    tabs = _rope_tables(dec_seq, 64) + _rope_tables(dec_seq, 32)
    sink = win_sink.astype(F32) * LOG2E
    w_r_t = w_router.T.astype(F32)
    b_r = b_router.astype(F32).reshape(N_EXPERTS, 1)
    layers = [_layer_params(i, p) for i in range(depth)]

    def ffn(merged, mod, rows_per_mod, lp):
        x1, route, h2p, counts = merged
        return _moe(lp["layer"], x1, route, h2p, counts, mod, rows_per_mod, w_exp_gate, w_exp_up,
                    w_exp_down)

    def merge(x, oa, ob, oc, mod, rows_per_mod, lp):
        return _merge(x, oa, ob, oc, mod, rows_per_mod, lp["gains"], lp["w_gate"], lp["wo_a"],
                      lp["wo_b"], lp["wo_c"], lp["w_out"], w_r_t, b_r)

    n_ctx = batch * seq
    x = x_prompt.reshape(n_ctx, D_MODEL)
    states = []
    for i, lp in enumerate(layers):
        mod = mods[i, 0:1]
        outs = _inproj(x, mod, n_ctx, lp["gains"], lp["w_a"], lp["w_uq"], lp["w_ukv"], mats, None,
                       seq, True)
        oa, ob, oc = _ctx_attn(i, sink, seq, *outs[:9])
        states.append(outs[9:])
        x = ffn(merge(x, oa, ob, oc, mod, n_ctx, lp), mod, n_ctx, lp)
    y_prompt = x.reshape(batch, seq, D_MODEL)

    n_lat = dec_batch * dec_seq
    x = x_sample.reshape(n_lat, D_MODEL)
    cwk = cache_win_k.reshape(dec_batch, depth, past, 128).astype(BF16)
    cwv = cache_win_v.reshape(dec_batch, depth, past, 128).astype(BF16)
    cnk = cache_nbr_k.reshape(dec_batch, depth, past, 512).astype(BF16)
    cnv = cache_nbr_v.reshape(dec_batch, depth, past, 512).astype(BF16)
    for i, lp in enumerate(layers):
        mod = mods[i, 1:1 + dec_batch]
        qa, ka, va, qb, kb, vb, qc, kc, vc = _inproj(
            x, mod, dec_seq, lp["gains"], lp["w_a"], lp["w_uq"], lp["w_ukv"], mats, tabs, dec_seq, False)
        kr_t = jnp.tile(cache_mla_krope[:, i].reshape(dec_batch * past, ROPE_DIM), (1, MLA_HEADS))
        kc_c, vc_c = _mla_cache_keys(cache_mla_ckv[:, i].reshape(dec_batch * past, KV_LORA), kr_t,
                                     lp["gains"], lp["w_ukv"], mats[1], mats[3], mats[4])
        r3 = lambda a: a.reshape(dec_batch, dec_seq, a.shape[-1])
        oa = _win_attn(i, sink, r3(qa), r3(ka), r3(va), cwk, cwv)
        ob = _nbr_attn(i, r3(qb), r3(kb), r3(vb), cnk, cnv, lp["nbr_table"])
        oc = _mla_attn(r3(qc), r3(kc), r3(vc), kc_c.reshape(dec_batch, past, 1024),
                       vc_c.reshape(dec_batch, past, 512))
        flat = lambda a: a.reshape(n_lat, 512)
        x = ffn(merge(x, flat(oa), flat(ob), flat(oc), mod, dec_seq, lp), mod, dec_seq, lp)
    y_sample = x.reshape(dec_batch, dec_seq, D_MODEL)

    def stack(k, shape):
        return jnp.stack([s[k].reshape((batch, seq) + shape) for s in states], axis=1)

    return (y_prompt, y_sample,
            stack(0, (WIN_KV_HEADS, HEAD_DIM)), stack(1, (WIN_KV_HEADS, HEAD_DIM)),
            stack(2, (NBR_HEADS, HEAD_DIM)), stack(3, (NBR_HEADS, HEAD_DIM)),
            stack(4, (KV_LORA,)), stack(5, (ROPE_DIM,)))
```

```python
import functools

import numpy as np
import jax
import jax.numpy as jnp
from jax import lax
from jax.experimental import pallas as pl
from jax.experimental.pallas import tpu as pltpu
from jax.experimental.pallas import tpu_sc as plsc

D_MODEL = 1024
GRID_W = 64
HEAD_DIM = 64
WIN_HEADS = 8
WIN_KV_HEADS = 2
WINDOW = 128
NBR_HEADS = 8
NBR_ROWS = 8
NBR_COLS = 16
MLA_HEADS = 8
Q_LORA = 256
KV_LORA = 128
NOPE_DIM = 64
ROPE_DIM = 32
V_DIM = 64
QK_DIM = NOPE_DIM + ROPE_DIM
N_EXPERTS = 16
N_GROUPS = 4
EXPERTS_PER_GROUP = 4
D_FF = 512
ROPE_BASE = 10000.0
EPS = 1e-6

LANES = 128
LOG2E = 1.4426950408889634
NEG = -1e30
VMEM_LIMIT = 56 * 1024 * 1024

F32 = jnp.float32
BF16 = jnp.bfloat16

C_QA, C_KA, C_VA, C_QB, C_KB, C_VB, C_CQ, C_CKV, C_KR, C_END = (
    0, 512, 640, 768, 1280, 1792, 2304, 2560, 2688, 2944)
WIN_Q_ORDER = (0, 4, 1, 5, 2, 6, 3, 7)


def _cparams(sem):
    return pltpu.CompilerParams(dimension_semantics=sem, vmem_limit_bytes=VMEM_LIMIT)


def _dot(a, b):
    return jnp.dot(a, b, preferred_element_type=F32)


def _dot_nt(a, b):
    return lax.dot_general(a, b, (((1,), (1,)), ((), ())), preferred_element_type=F32)


def _split(x):
    hi = x.astype(BF16)
    lo = (x - hi.astype(F32)).astype(BF16)
    return hi, lo


def _gsum(x2, bmat):
    return _dot(x2.astype(BF16), bmat)


def _tile_lanes(t, width):
    reps = width // t.shape[-1]
    return t if reps == 1 else jnp.concatenate([t] * reps, axis=-1)


def _rotate(x, cos, sin_a, sin_b, half):
    w = x.shape[-1]
    up = pltpu.roll(x, w - half, 1)
    dn = pltpu.roll(x, half, 1)
    return (x * _tile_lanes(cos, w) + up * _tile_lanes(sin_a, w) + dn * _tile_lanes(sin_b, w))


def _norm_mod(x, gain, scale, shift):
    ms = jnp.mean(x * x, axis=-1, keepdims=True)
    return (x * lax.rsqrt(ms + EPS) * gain) * (1.0 + scale) + shift


def _ada_body(c_ref, w_ref, b_ref, o_ref):
    c = c_ref[...]
    a = c * (1.0 / (1.0 + jnp.exp(-c)))
    a_hi, a_lo = _split(a)
    w_hi, w_lo = _split(w_ref[0])
    o_ref[0] = _dot(a_hi, w_hi) + _dot(a_hi, w_lo) + _dot(a_lo, w_hi) + b_ref[0]


def _ada(c_all, w_ada, b_ada):
    depth = w_ada.shape[0]
    rows = c_all.shape[0]
    tn = 1536
    return pl.pallas_call(
        _ada_body,
        grid=(depth, 6 * D_MODEL // tn),
        in_specs=[pl.BlockSpec((rows, D_MODEL), lambda l, j: (0, 0)),
                  pl.BlockSpec((1, D_MODEL, tn), lambda l, j: (l, 0, j)),
                  pl.BlockSpec((1, 1, tn), lambda l, j: (l, 0, j))],
        out_specs=pl.BlockSpec((1, rows, tn), lambda l, j: (l, 0, j)),
        out_shape=jax.ShapeDtypeStruct((depth, rows, 6 * D_MODEL), F32),
        compiler_params=_cparams(("parallel", "parallel")),
        name="ada",
    )(c_all, w_ada, b_ada.reshape(depth, 1, 6 * D_MODEL))


def _mla_key_tail(ckvn_b, kr_t, g, wukv_ref, bnn, bnr, brr, rope_tabs, kc_ref, vc_ref):
    kv = _dot(ckvn_b, wukv_ref[...])
    kn = kv[:, 0:512]
    vc_ref[...] = kv[:, 512:1024].astype(BF16)
    kn2 = kn * kn
    kr2 = kr_t * kr_t
    kr_sum32 = _gsum(kr2, brr)
    ssn = (_gsum(kn2, bnn) + jnp.concatenate([kr_sum32, kr_sum32], axis=-1)) * (1.0 / QK_DIM)
    ssr = (_gsum(kn2, bnr) + kr_sum32) * (1.0 / QK_DIM)
    kn = kn * lax.rsqrt(ssn + EPS) * g[5:6, 0:512]
    kr = kr_t * lax.rsqrt(ssr + EPS) * g[5:6, 512:768]
    if rope_tabs is not None:
        kr = _rotate(kr, *rope_tabs, 8)
    for p in range(4):
        kc_ref[:, 256 * p:256 * p + 128] = kn[:, 128 * p:128 * p + 128].astype(BF16)
        q4 = 128 * (p // 2)
        kc_ref[:, 256 * p + 128:256 * p + 256] = kr[:, q4:q4 + 128].astype(BF16)


def _inproj_body(rope, states, *refs):
    (x_ref, mod_ref, g_ref, w_ref, wuq_ref, wukv_ref, b64_ref, bnn_ref, brn_ref, bnr_ref,
     brr_ref) = refs[:11]
    refs = refs[11:]
    if rope:
        tabs_w = tuple(r[...] for r in refs[0:3])
        tabs_m = tuple(r[...] for r in refs[3:6])
        refs = refs[6:]
    else:
        tabs_w = tabs_m = None
    qa_ref, ka_ref, va_ref, qb_ref, kb_ref, vb_ref, qc_ref, kc_ref, vc_ref = refs[:9]
    st = refs[9:]

    g = g_ref[...]
    mod = mod_ref[0]
    hb = _norm_mod(x_ref[...], g[0:1], mod[1:2], mod[0:1]).astype(BF16)

    def proj(a, b):
        return _dot(hb, w_ref[:, a:b])

    b64 = b64_ref[...]

    def head_norm(z, bmat, gain):
        return z * lax.rsqrt(_gsum(z * z, bmat) + EPS) * gain

    qa = head_norm(proj(C_QA, C_KA), b64, g[1:2, 0:512])
    ka = head_norm(proj(C_KA, C_VA), b64[0:128, 0:128], g[1:2, 512:640])
    va = proj(C_VA, C_QB)
    if states:
        st[0][...] = ka
        st[1][...] = va
    if rope:
        qa = _rotate(qa, *tabs_w, 16)
        ka = _rotate(ka, *tabs_w, 16)
    qa_ref[...] = qa.astype(BF16)
    ka_ref[...] = ka.astype(BF16)
    va_ref[...] = va.astype(BF16)

    qb = head_norm(proj(C_QB, C_KB), b64, g[2:3, 0:512])
    kb = head_norm(proj(C_KB, C_VB), b64, g[2:3, 512:1024])
    vb = proj(C_VB, C_CQ)
    if states:
        st[2][...] = kb
        st[3][...] = vb
    qb_ref[...] = qb.astype(BF16)
    kb_ref[...] = kb.astype(BF16)
    vb_ref[...] = vb.astype(BF16)

    cq = proj(C_CQ, C_CKV)
    cqn = cq * lax.rsqrt(jnp.mean(cq * cq, axis=-1, keepdims=True) + EPS) * g[3:4, 0:256]
    qq = _dot(cqn.astype(BF16), wuq_ref[...])
    qn, qr = qq[:, 0:512], qq[:, 512:768]
    qn2, qr2 = qn * qn, qr * qr
    bnn, brn, bnr, brr = bnn_ref[...], brn_ref[...], bnr_ref[...], brr_ref[...]
    ssn = (_gsum(qn2, bnn) + _gsum(qr2, brn)) * (1.0 / QK_DIM)
    ssr = (_gsum(qn2, bnr) + _gsum(qr2, brr)) * (1.0 / QK_DIM)
    qn = qn * lax.rsqrt(ssn + EPS) * g[4:5, 0:512]
    qr = qr * lax.rsqrt(ssr + EPS) * g[4:5, 512:768]
    if rope:
        qr = _rotate(qr, *tabs_m, 8)
    for p in range(4):
        qc_ref[:, 256 * p:256 * p + 128] = qn[:, 128 * p:128 * p + 128].astype(BF16)
        q4 = 128 * (p // 2)
        qc_ref[:, 256 * p + 128:256 * p + 256] = qr[:, q4:q4 + 128].astype(BF16)

    ckv = proj(C_CKV, C_KR)
    ckvn = ckv * lax.rsqrt(jnp.mean(ckv * ckv, axis=-1, keepdims=True) + EPS) * g[3:4, 256:384]
    kr_t = proj(C_KR, C_END)
    if states:
        st[4][...] = ckvn
        st[5][...] = kr_t[:, 0:ROPE_DIM]
    _mla_key_tail(ckvn.astype(BF16), kr_t, g, wukv_ref, bnn, bnr, brr, tabs_m, kc_ref, vc_ref)


def _const_spec(shape):
    nd = len(shape)
    return pl.BlockSpec(shape, lambda i, _nd=nd: (0,) * _nd, pipeline_mode=pl.Buffered(1))


def _inproj(x, mods, rows_per_mod, gains, w_a, w_uq, w_ukv, mats, rope_tabs, seq_len, states):
    n = x.shape[0]
    tm = min(1024, n)
    rope = rope_tabs is not None
    row = lambda w: pl.BlockSpec((tm, w), lambda i: (i, 0))
    in_specs = [row(D_MODEL),
                pl.BlockSpec((1, 8, D_MODEL), lambda i: ((i * tm) // rows_per_mod, 0, 0)),
                _const_spec(gains.shape), _const_spec(w_a.shape), _const_spec(w_uq.shape),
                _const_spec(w_ukv.shape)] + [_const_spec(m.shape) for m in mats]
    args = [x, mods, gains, w_a, w_uq, w_ukv, *mats]
    if rope:
        tiles_per_seq = seq_len // tm
        in_specs += [pl.BlockSpec((tm, LANES), lambda i: (i % tiles_per_seq, 0))] * 6
        args += list(rope_tabs)
    widths = [512, 128, 128, 512, 512, 512, 1024, 1024, 512]
    out_shape = [jax.ShapeDtypeStruct((n, w), BF16) for w in widths]
    out_specs = [row(w) for w in widths]
    if states:
        swidths = [128, 128, 512, 512, KV_LORA, ROPE_DIM]
        out_shape += [jax.ShapeDtypeStruct((n, w), F32) for w in swidths]
        out_specs += [row(w) for w in swidths]
    return pl.pallas_call(
        functools.partial(_inproj_body, rope, states),
        grid=(n // tm,), in_specs=in_specs, out_specs=out_specs, out_shape=out_shape,
        compiler_params=_cparams(("parallel",)),
        name="inproj_lat" if rope else "inproj_ctx",
    )(*args)


def _mla_cache_body(ckv_ref, kr_ref, g_ref, wukv_ref, bnn_ref, bnr_ref, brr_ref, kc_ref, vc_ref):
    _mla_key_tail(ckv_ref[...].astype(BF16), kr_ref[...], g_ref[...], wukv_ref, bnn_ref[...],
                  bnr_ref[...], brr_ref[...], None, kc_ref, vc_ref)


def _mla_cache_keys(ckv, kr_t, gains, w_ukv, bnn, bnr, brr):
    n = ckv.shape[0]
    tm = min(512, n)
    row = lambda w: pl.BlockSpec((tm, w), lambda i: (i, 0))
    return pl.pallas_call(
        _mla_cache_body,
        grid=(n // tm,),
        in_specs=[row(KV_LORA), row(256), _const_spec(gains.shape), _const_spec(w_ukv.shape),
                  _const_spec(bnn.shape), _const_spec(bnr.shape), _const_spec(brr.shape)],
        out_specs=[row(1024), row(512)],
        out_shape=[jax.ShapeDtypeStruct((n, 1024), BF16), jax.ShapeDtypeStruct((n, 512), BF16)],
        compiler_params=_cparams(("parallel",)),
        name="mla_cache_keys",
    )(ckv, kr_t, gains, w_ukv, bnn, bnr, brr)


def _lane_mask(width, ranges):
    lane = lax.broadcasted_iota(jnp.int32, (1, width), 1)
    m = None
    for lo, hi in ranges:
        c = (lane >= lo) & (lane < hi)
        m = c if m is None else (m | c)
    return jnp.where(m, 1.0, 0.0).astype(BF16)


def _stack_heads(q, mask0, mask1):
    return jnp.concatenate([q * mask0, q * mask1], axis=0)


def _lane_tiles(x):
    return [x[:, j:j + LANES] for j in range(0, x.shape[1], LANES)]


def _softmax_block(scores, sink=None):
    rows = scores[0].shape[0]
    mp = None
    for s in scores:
        for t in _lane_tiles(s):
            mp = t if mp is None else jnp.maximum(mp, t)
    base = sink if sink is not None else jnp.full((rows, LANES), NEG, F32)
    m = jnp.maximum(base, jnp.max(mp, axis=-1, keepdims=True))
    lp = None
    ps = []
    for s in scores:
        p = jnp.exp2(s - _tile_lanes(m, s.shape[1]))
        for t in _lane_tiles(p):
            lp = t if lp is None else lp + t
        ps.append(p.astype(BF16))
    if sink is not None:
        lane = lax.broadcasted_iota(jnp.int32, (rows, LANES), 1)
        lp = lp + jnp.where(lane == 0, jnp.exp2(sink - m), 0.0)
    p_all = ps[0] if len(ps) == 1 else jnp.concatenate(ps, axis=-1)
    return p_all, jnp.broadcast_to(jnp.sum(lp, axis=-1, keepdims=True), (rows, LANES))


def _softmax_pv(scores, values, sink=None):
    p_all, l = _softmax_block(scores, sink)
    v_all = values[0] if len(values) == 1 else jnp.concatenate(values, axis=0)
    return _dot(p_all, v_all) / l


def _merge_heads(o, tq):
    lane = lax.broadcasted_iota(jnp.int32, (tq, LANES), 1)
    return jnp.where(lane < HEAD_DIM, o[0:tq], o[tq:2 * tq])


def _mla_masks(p_mod2):
    lane = lax.broadcasted_iota(jnp.int32, (1, 256), 1)
    r0 = 128 + 32 * (2 * p_mod2)
    m0 = (lane < 64) | ((lane >= r0) & (lane < r0 + 32))
    m1 = ((lane >= 64) & (lane < 128)) | ((lane >= r0 + 32) & (lane < r0 + 64))
    return (jnp.where(m0, 1.0, 0.0).astype(BF16), jnp.where(m1, 1.0, 0.0).astype(BF16))


def _sink_col(sink_ref, layer, h0, h1, tq):
    row = lax.broadcasted_iota(jnp.int32, (2 * tq, LANES), 0)
    return jnp.where(row < tq, sink_ref[layer, h0], sink_ref[layer, h1])


def _ctx_attn_body(layer, sink_ref, qa_ref, ka_ref, va_ref, qb_ref, kb_ref, vb_ref, qc_ref, kc_ref,
                   vc_ref, oa_ref, ob_ref, oc_ref):
    tq = qa_ref.shape[0]
    lo = _lane_mask(LANES, [(0, 64)])
    hi = _lane_mask(LANES, [(64, 128)])
    ka, va = ka_ref[...], va_ref[...]
    for j in range(4):
        sl = slice(128 * j, 128 * j + 128)
        qs = _stack_heads(qa_ref[:, sl], lo, hi)
        sink = _sink_col(sink_ref, layer, j, 4 + j, tq)
        o = _softmax_pv([_dot_nt(qs, ka)], [va], sink)
        oa_ref[:, sl] = _merge_heads(o, tq).astype(BF16)

        qs = _stack_heads(qb_ref[:, sl], lo, hi)
        o = _softmax_pv([_dot_nt(qs, kb_ref[:, sl])], [vb_ref[:, sl]])
        ob_ref[:, sl] = _merge_heads(o, tq).astype(BF16)

        m0, m1 = _mla_masks(j % 2)
        s2 = slice(256 * j, 256 * j + 256)
        qs = _stack_heads(qc_ref[:, s2], m0, m1)
        o = _softmax_pv([_dot_nt(qs, kc_ref[:, s2])], [vc_ref[:, sl]])
        oc_ref[:, sl] = _merge_heads(o, tq).astype(BF16)


def _ctx_attn(layer, sink, seq, qa, ka, va, qb, kb, vb, qc, kc, vc):
    n = qa.shape[0]
    row = lambda w: pl.BlockSpec((seq, w), lambda b: (b, 0))
    ins = [qa, ka, va, qb, kb, vb, qc, kc, vc]
    return pl.pallas_call(
        functools.partial(_ctx_attn_body, layer),
        grid=(n // seq,),
        in_specs=[pl.BlockSpec(memory_space=pltpu.SMEM)] + [row(a.shape[1]) for a in ins],
        out_specs=[row(512)] * 3,
        out_shape=[jax.ShapeDtypeStruct((n, 512), BF16)] * 3,
        compiler_params=_cparams(("parallel",)),
        name="ctx_attn",
    )(sink, *ins)


WIN_ROW_BLOCK = 32


def _win_body(layer, sink_ref, q_ref, k_ref, v_ref, kc_ref, vc_ref, o_ref, s_sc, p_sc, l_sc):
    tq = q_ref.shape[1]
    seq = k_ref.shape[1]
    kw = 3 * tq
    rb = WIN_ROW_BLOCK
    i = pl.program_id(1)
    kstart = pl.multiple_of(jnp.clip((i - 1) * tq, 0, seq - kw), tq)
    k_all = jnp.concatenate([k_ref[0, pl.ds(kstart, kw), :], kc_ref[0, 0]], axis=0)
    v_all = jnp.concatenate([v_ref[0, pl.ds(kstart, kw), :], vc_ref[0, 0]], axis=0)
    q_pos = i * tq + lax.broadcasted_iota(jnp.int32, (tq, kw), 0)
    k_pos = kstart + lax.broadcasted_iota(jnp.int32, (tq, kw), 1)
    band = jnp.abs(q_pos - k_pos) <= WINDOW
    lo = _lane_mask(LANES, [(0, 64)])
    hi = _lane_mask(LANES, [(64, 128)])
    qs = jnp.concatenate([_stack_heads(q_ref[0, :, 128 * j:128 * j + 128], lo, hi) for j in range(4)],
                         axis=0)
    s_sc[...] = _dot_nt(qs, k_all)
    for j in range(4):
        for r in range(2 * tq * j, 2 * tq * (j + 1), rb):
            head = j if r < 2 * tq * j + tq else 4 + j
            q0 = r % tq
            s_band = jnp.where(band[q0:q0 + rb], s_sc[r:r + rb, 0:kw], NEG)
            sink = jnp.full((rb, LANES), sink_ref[layer, head], F32)
            p, l = _softmax_block([s_band, s_sc[r:r + rb, kw:]], sink)
            p_sc[r:r + rb, :] = p
            l_sc[r:r + rb, :] = l
        rows = slice(2 * tq * j, 2 * tq * (j + 1))
        o = _dot(p_sc[rows, :], v_all) / l_sc[rows, :]
        o_ref[0, :, 128 * j:128 * j + 128] = _merge_heads(o, tq).astype(BF16)


def _win_attn(layer, sink, q, k, v, kc, vc):
    b, seq, _ = q.shape
    tq = 128
    past = kc.shape[2]
    return pl.pallas_call(
        functools.partial(_win_body, layer),
        grid=(b, seq // tq),
        in_specs=[pl.BlockSpec(memory_space=pltpu.SMEM),
                  pl.BlockSpec((1, tq, 512), lambda bi, i: (bi, i, 0)),
                  pl.BlockSpec((1, seq, 128), lambda bi, i: (bi, 0, 0)),
                  pl.BlockSpec((1, seq, 128), lambda bi, i: (bi, 0, 0)),
                  pl.BlockSpec((1, 1, past, 128), lambda bi, i: (bi, layer, 0, 0)),
                  pl.BlockSpec((1, 1, past, 128), lambda bi, i: (bi, layer, 0, 0))],
        out_specs=pl.BlockSpec((1, tq, 512), lambda bi, i: (bi, i, 0)),
        out_shape=jax.ShapeDtypeStruct((b, seq, 512), BF16),
        scratch_shapes=[pltpu.VMEM((8 * tq, 3 * tq + past), F32), pltpu.VMEM((8 * tq, 3 * tq + past), BF16),
                        pltpu.VMEM((8 * tq, LANES), F32)],
        compiler_params=_cparams(("parallel", "arbitrary")),
        name="win_attn",
    )(sink, q, k, v, kc, vc)


NBR_TILE_ROWS = 4
NBR_WIN_ROWS = NBR_TILE_ROWS + NBR_ROWS
NBR_TAB_PAD = NBR_WIN_ROWS - NBR_ROWS
NBR_ROW_BLOCK = 32


def _nbr_body(rows, q_ref, k_ref, v_ref, kc_ref, vc_ref, tab_ref, o_ref, s_sc, p_sc, l_sc):
    tq = NBR_TILE_ROWS * GRID_W
    kw = NBR_WIN_ROWS * GRID_W
    i = pl.program_id(1)
    r0 = NBR_TILE_ROWS * i
    ws = jnp.clip(r0 - NBR_ROWS // 2, 0, rows - NBR_WIN_ROWS)
    kstart = pl.multiple_of(ws * GRID_W, LANES)
    k_row = ws + lax.broadcasted_iota(jnp.int32, (1, kw), 1) // GRID_W
    lo = _lane_mask(LANES, [(0, 64)])
    hi = _lane_mask(LANES, [(64, 128)])
    past = kc_ref.shape[2]
    rb = NBR_ROW_BLOCK
    for j in range(4):
        sl = slice(128 * j, 128 * j + 128)
        qs = _stack_heads(q_ref[0, :, sl], lo, hi)
        s_sc[j, :, 0:kw] = _dot_nt(qs, k_ref[0, pl.ds(kstart, kw), sl])
        s_sc[j, :, kw:kw + past] = _dot_nt(qs, kc_ref[0, 0, :, sl])
    for j in range(4):
        sl = slice(128 * j, 128 * j + 128)
        for b0 in range(0, 2 * tq, rb):
            h = 2 * j + b0 // tq
            ql, sub = divmod(b0 % tq, GRID_W)
            d0 = ws - r0 - ql + (NBR_ROWS - 1) + NBR_TAB_PAD
            bias = jnp.concatenate([tab_ref[h, d0 + 2 * m, sub:sub + rb, :]
                                    for m in range(NBR_WIN_ROWS // 2)], axis=-1)
            rs = jnp.clip(r0 + ql - NBR_ROWS // 2, 0, rows - NBR_ROWS)
            valid = (k_row >= rs) & (k_row < rs + NBR_ROWS)
            s_nb = jnp.where(valid, s_sc[j, b0:b0 + rb, 0:kw] + bias, NEG)
            p, l = _softmax_block([s_nb, s_sc[j, b0:b0 + rb, kw:kw + past]])
            p_sc[j, b0:b0 + rb, :] = p
            l_sc[j, b0:b0 + rb, :] = l
        v_all = jnp.concatenate([v_ref[0, pl.ds(kstart, kw), sl], vc_ref[0, 0, :, sl]], axis=0)
        o = _dot(p_sc[j], v_all) / l_sc[j]
        o_ref[0, :, sl] = _merge_heads(o, tq).astype(BF16)


def _nbr_attn(layer, q, k, v, kc, vc, table):
    b, seq, _ = q.shape
    rows = seq // GRID_W
    tq = NBR_TILE_ROWS * GRID_W
    past = kc.shape[2]
    keys = NBR_WIN_ROWS * GRID_W + past
    return pl.pallas_call(
        functools.partial(_nbr_body, rows),
        grid=(b, seq // tq),
        in_specs=[pl.BlockSpec((1, tq, 512), lambda bi, i: (bi, i, 0)),
                  pl.BlockSpec((1, seq, 512), lambda bi, i: (bi, 0, 0)),
                  pl.BlockSpec((1, seq, 512), lambda bi, i: (bi, 0, 0)),
                  pl.BlockSpec((1, 1, past, 512), lambda bi, i: (bi, layer, 0, 0)),
                  pl.BlockSpec((1, 1, past, 512), lambda bi, i: (bi, layer, 0, 0)),
                  pl.BlockSpec(table.shape, lambda bi, i: (0, 0, 0, 0))],
        out_specs=pl.BlockSpec((1, tq, 512), lambda bi, i: (bi, i, 0)),
        out_shape=jax.ShapeDtypeStruct((b, seq, 512), BF16),
        scratch_shapes=[pltpu.VMEM((4, 2 * tq, keys), F32), pltpu.VMEM((4, 2 * tq, keys), BF16),
                        pltpu.VMEM((4, 2 * tq, LANES), F32)],
        compiler_params=_cparams(("parallel", "arbitrary")),
        name="nbr_attn",
    )(q, k, v, kc, vc, table)


def _nbr_bias_table(rel_bias):
    col = np.arange(GRID_W)
    cs = np.clip(col - NBR_COLS // 2, 0, GRID_W - NBR_COLS)
    kc = np.arange(GRID_W)
    ok = (kc[None, :] >= cs[:, None]) & (kc[None, :] < cs[:, None] + NBR_COLS)
    dc = kc[None, :] - col[:, None] + (NBR_COLS - 1)
    pick = (dc[:, :, None] == np.arange(2 * NBR_COLS - 1)[None, None, :]) & ok[:, :, None]
    t = jnp.einsum("hdk,qck->hdqc", rel_bias.astype(F32) * LOG2E, jnp.asarray(pick, F32),
                   precision=lax.Precision.HIGHEST)
    t = jnp.where(jnp.asarray(ok)[None, None], t, NEG)
    t = jnp.pad(t, ((0, 0), (NBR_TAB_PAD, NBR_TAB_PAD), (0, 0), (0, 0)))
    return jnp.concatenate([t[:, :-1], t[:, 1:]], axis=-1)


MLA_KEY_CHUNK = 512
MLA_ROW_BLOCK = 256


def _mla_body(q_ref, kl_ref, kc_ref, vl_ref, vc_ref, o_ref, qs_sc, m_sc, l_sc, acc_sc):
    tq = q_ref.shape[1]
    rows = 2 * tq
    seq = kl_ref.shape[1]
    past = kc_ref.shape[1]
    tk = min(MLA_KEY_CHUNK, past)
    rb = min(MLA_ROW_BLOCK, rows)
    m0, m1 = _mla_masks(pl.program_id(1) % 2)
    qs_sc[...] = _stack_heads(q_ref[0], m0, m1)
    m_sc[...] = jnp.full(m_sc.shape, NEG, F32)
    l_sc[...] = jnp.zeros(l_sc.shape, F32)
    acc_sc[...] = jnp.zeros(acc_sc.shape, F32)

    def step(k, v):
        s = _dot_nt(qs_sc[...], k)
        for r in range(rows // rb):
            sl = slice(r * rb, (r + 1) * rb)
            sb = s[sl]
            m_prev = m_sc[sl]
            m_new = jnp.maximum(m_prev, jnp.max(sb, axis=-1, keepdims=True))
            alpha = jnp.exp2(m_prev - m_new)
            p = jnp.exp2(sb - _tile_lanes(m_new, tk))
            psum = p[:, 0:LANES]
            for j in range(1, tk // LANES):
                psum = psum + p[:, j * LANES:(j + 1) * LANES]
            l_sc[sl] = alpha * l_sc[sl] + psum
            acc_sc[sl] = alpha * acc_sc[sl] + _dot(p.astype(BF16), v)
            m_sc[sl] = m_new

    for c in range(seq // tk):
        step(kl_ref[0, c * tk:(c + 1) * tk, :], vl_ref[0, c * tk:(c + 1) * tk, :])
    for c in range(past // tk):
        step(kc_ref[0, c * tk:(c + 1) * tk, :], vc_ref[0, c * tk:(c + 1) * tk, :])
    l = jnp.sum(l_sc[...], axis=-1, keepdims=True)
    o_ref[0] = _merge_heads(acc_sc[...] / l, tq).astype(BF16)


def _mla_attn(q, kl, vl, kc, vc):
    b, seq, _ = q.shape
    past = kc.shape[1]
    tq = min(512, seq)
    return pl.pallas_call(
        _mla_body,
        grid=(b, 4, seq // tq),
        in_specs=[pl.BlockSpec((1, tq, 256), lambda bi, p, qi: (bi, qi, p)),
                  pl.BlockSpec((1, seq, 256), lambda bi, p, qi: (bi, 0, p)),
                  pl.BlockSpec((1, past, 256), lambda bi, p, qi: (bi, 0, p)),
                  pl.BlockSpec((1, seq, 128), lambda bi, p, qi: (bi, 0, p)),
                  pl.BlockSpec((1, past, 128), lambda bi, p, qi: (bi, 0, p))],
        out_specs=pl.BlockSpec((1, tq, 128), lambda bi, p, qi: (bi, qi, p)),
        out_shape=jax.ShapeDtypeStruct((b, seq, 512), BF16),
        scratch_shapes=[pltpu.VMEM((2 * tq, 256), BF16), pltpu.VMEM((2 * tq, LANES), F32),
                        pltpu.VMEM((2 * tq, LANES), F32), pltpu.VMEM((2 * tq, LANES), F32)],
        compiler_params=_cparams(("parallel", "parallel", "arbitrary")),
        name="mla_attn",
    )(q, kl, kc, vl, vc)


def _pack_pairs(x):
    w = x.shape[1] // 2
    hi = lax.bitcast_convert_type(x[:, :w].astype(BF16).astype(F32), jnp.int32)
    lo = lax.bitcast_convert_type(x[:, w:].astype(BF16).astype(F32), jnp.int32)
    return (hi & jnp.int32(-65536)) | lax.shift_right_logical(lo, jnp.int32(16))


def _unpack_pairs(p):
    hi = lax.bitcast_convert_type(p & jnp.int32(-65536), F32)
    lo = lax.bitcast_convert_type(lax.shift_left(p, jnp.int32(16)), F32)
    return jnp.concatenate([hi, lo], axis=-1)


def _merge_body(x_ref, oa_ref, ob_ref, oc_ref, mod_ref, g_ref, wg_ref, woa_ref, wob_ref, woc_ref,
                wout_ref, wr_ref, br_ref, tri_ref, x1_ref, route_ref, h2p_ref, count_ref, count_sc):
    x = x_ref[...]
    g = g_ref[...]
    mod = mod_ref[0]
    hb = _norm_mod(x, g[0:1], mod[1:2], mod[0:1]).astype(BF16)
    m = None
    for br, (o_ref, wo_ref) in enumerate(((oa_ref, woa_ref), (ob_ref, wob_ref), (oc_ref, woc_ref))):
        z = _dot(hb, wg_ref[:, D_MODEL * br:D_MODEL * (br + 1)])
        gate = 1.0 / (1.0 + jnp.exp(-z))
        t = gate * _dot(o_ref[...], wo_ref[...])
        m = t if m is None else m + t
    y = _dot(m.astype(BF16), wout_ref[...])
    x1 = x + mod[2:3] * y
    x1_ref[...] = x1

    h2 = _norm_mod(x1, g[6:7], mod[4:5], mod[3:4])
    h_hi, h_lo = _split(h2)
    w_hi, w_lo = _split(wr_ref[...])
    both_w = _dot_nt(jnp.concatenate([w_hi, w_lo], axis=0), h_hi)
    logits = both_w[0:N_EXPERTS] + both_w[N_EXPERTS:2 * N_EXPERTS] + _dot_nt(w_hi, h_lo)
    score = 1.0 / (1.0 + jnp.exp(-logits))
    sel = score + br_ref[...]
    sel_r = [sel[e:e + 1] for e in range(N_EXPERTS)]
    sc_r = [score[e:e + 1] for e in range(N_EXPERTS)]
    picked = []
    for e in range(N_EXPERTS):
        grp, a = divmod(e, EXPERTS_PER_GROUP)
        rank = None
        for bb in range(EXPERTS_PER_GROUP):
            if bb == a:
                continue
            o = sel_r[grp * EXPERTS_PER_GROUP + bb]
            beats = (o >= sel_r[e]) if bb < a else (o > sel_r[e])
            r = jnp.where(beats, 1.0, 0.0)
            rank = r if rank is None else rank + r
        picked.append(rank < 2.0)
    gscore = []
    for grp in range(N_GROUPS):
        tot = None
        for a in range(EXPERTS_PER_GROUP):
            e = grp * EXPERTS_PER_GROUP + a
            t = jnp.where(picked[e], sel_r[e], 0.0)
            tot = t if tot is None else tot + t
        gscore.append(tot)
    best = jnp.zeros_like(gscore[0])
    best_v = gscore[0]
    for grp in range(1, N_GROUPS):
        upd = gscore[grp] > best_v
        best = jnp.where(upd, float(grp), best)
        best_v = jnp.where(upd, gscore[grp], best_v)
    cw, pk = [], []
    for a in range(EXPERTS_PER_GROUP):
        tot = flag = None
        for grp in range(N_GROUPS):
            e = grp * EXPERTS_PER_GROUP + a
            f = (best == float(grp)) & picked[e]
            t = jnp.where(f, sc_r[e], 0.0)
            tot = t if tot is None else tot + t
            flag = f if flag is None else (flag | f)
        cw.append(tot)
        pk.append(flag)
    den = cw[0] + cw[1] + cw[2] + cw[3]
    first = jnp.where(pk[0], 0.0, jnp.where(pk[1], 1.0, jnp.where(pk[2], 2.0, 3.0)))
    second = jnp.where(pk[3], 3.0, jnp.where(pk[2], 2.0, jnp.where(pk[1], 1.0, 0.0)))
    slot_e, slot_w = [], []
    for which in (first, second):
        tot = None
        for a in range(EXPERTS_PER_GROUP):
            t = jnp.where(which == float(a), cw[a], 0.0)
            tot = t if tot is None else tot + t
        slot_w.append(tot / den)
        slot_e.append(best * float(EXPERTS_PER_GROUP) + which)

    @pl.when(pl.program_id(0) == 0)
    def _():
        count_sc[...] = jnp.zeros(count_sc.shape, F32)

    tm = x.shape[0]
    eid = lax.broadcasted_iota(jnp.int32, (N_EXPERTS, tm), 0).astype(F32)
    oh = [eid == slot_e[0], eid == slot_e[1]]
    both = jnp.where(oh[0] | oh[1], 1.0, 0.0)
    seen = count_sc[...][:, 0:1] + _dot(both.astype(BF16), tri_ref[...])
    for k in range(2):
        route_ref[k:k + 1, :] = slot_w[k]
        route_ref[2 + k:3 + k, :] = slot_e[k]
        route_ref[4 + k:5 + k, :] = jnp.sum(jnp.where(oh[k], seen, 0.0), axis=0, keepdims=True)
    route_ref[6:8, :] = jnp.zeros((2, tm), F32)
    count_sc[...] = count_sc[...] + jnp.sum(both, axis=-1, keepdims=True)
    count_ref[...] = count_sc[...]
    h2p_ref[...] = _pack_pairs(h2)


def _merge(x, oa, ob, oc, mods, rows_per_mod, gains, w_gate, wo_a, wo_b, wo_c, w_out, w_r_t, b_r):
    n = x.shape[0]
    tm = min(1024, n)
    row = lambda w: pl.BlockSpec((tm, w), lambda i: (i, 0))
    tri = jnp.asarray(np.triu(np.ones((tm, tm), np.float32), 1), BF16)
    consts = [gains, w_gate, wo_a, wo_b, wo_c, w_out, w_r_t, b_r, tri]
    return pl.pallas_call(
        _merge_body,
        grid=(n // tm,),
        in_specs=[row(D_MODEL), row(512), row(512), row(512),
                  pl.BlockSpec((1, 8, D_MODEL), lambda i: ((i * tm) // rows_per_mod, 0, 0))]
                 + [_const_spec(c.shape) for c in consts],
        out_specs=[row(D_MODEL), pl.BlockSpec((8, tm), lambda i: (0, i)), row(512),
                   pl.BlockSpec((N_EXPERTS, LANES), lambda i: (0, 0))],
        out_shape=[jax.ShapeDtypeStruct((n, D_MODEL), F32), jax.ShapeDtypeStruct((8, n), F32),
                   jax.ShapeDtypeStruct((n, 512), jnp.int32),
                   jax.ShapeDtypeStruct((N_EXPERTS, LANES), F32)],
        scratch_shapes=[pltpu.VMEM((N_EXPERTS, LANES), F32)],
        compiler_params=_cparams(("arbitrary",)),
        name="merge",
    )(x, oa, ob, oc, mods, *consts)


EXPERT_TILE = 512
SC_CORES = 2
SC_SUBCORES = 16
SC_WORKERS = SC_CORES * SC_SUBCORES
SC_WINDOW = 128


def _sc_mesh():
    return plsc.VectorSubcoreMesh(core_axis_name="c", subcore_axis_name="s", num_cores=SC_CORES,
                                  num_subcores=SC_SUBCORES)


def _sc_window_base(steps, j):
    wid = lax.axis_index("s") * SC_CORES + lax.axis_index("c")
    return pl.multiple_of((wid * steps + j) * SC_WINDOW, SC_WINDOW)


def _sc_dispatch(rows, pos0, pos1, n_out):
    n, w = rows.shape
    steps = n // (SC_WORKERS * SC_WINDOW)

    @functools.partial(
        pl.kernel, out_type=jax.ShapeDtypeStruct((n_out, w), rows.dtype), mesh=_sc_mesh(),
        scratch_types=[pltpu.VMEM((SC_WINDOW,), jnp.int32), pltpu.VMEM((SC_WINDOW,), jnp.int32),
                       pltpu.VMEM((SC_WINDOW, w), rows.dtype)],
        name="moe_dispatch")
    def run(x_hbm, i0_hbm, i1_hbm, o_hbm, i0_v, i1_v, rows_v):
        @pl.loop(0, steps)
        def _(j):
            base = _sc_window_base(steps, j)
            pltpu.sync_copy(i0_hbm.at[pl.ds(base, SC_WINDOW)], i0_v)
            pltpu.sync_copy(i1_hbm.at[pl.ds(base, SC_WINDOW)], i1_v)
            pltpu.sync_copy(x_hbm.at[pl.ds(base, SC_WINDOW)], rows_v)
            pltpu.sync_copy(rows_v, o_hbm.at[i0_v])
            pltpu.sync_copy(rows_v, o_hbm.at[i1_v])

    return run(rows, pos0, pos1)


def _sc_collect(rows, pos0, pos1):
    n = pos0.shape[0]
    w = rows.shape[1]
    steps = n // (SC_WORKERS * SC_WINDOW)
    out = jax.ShapeDtypeStruct((n, w), rows.dtype)

    @functools.partial(
        pl.kernel, out_type=[out, out], mesh=_sc_mesh(),
        scratch_types=[pltpu.VMEM((SC_WINDOW,), jnp.int32), pltpu.VMEM((SC_WINDOW, w), rows.dtype)],
        name="moe_collect")
    def run(y_hbm, i0_hbm, i1_hbm, o0_hbm, o1_hbm, i_v, rows_v):
        @pl.loop(0, steps)
        def _(j):
            base = _sc_window_base(steps, j)
            for i_hbm, o_hbm in ((i0_hbm, o0_hbm), (i1_hbm, o1_hbm)):
                pltpu.sync_copy(i_hbm.at[pl.ds(base, SC_WINDOW)], i_v)
                pltpu.sync_copy(y_hbm.at[i_v], rows_v)
                pltpu.sync_copy(rows_v, o_hbm.at[pl.ds(base, SC_WINDOW)])

    return run(rows, pos0, pos1)


def _experts_body(te_ref, nv_ref, xs_ref, wg_ref, wu_ref, wd_ref, ys_ref, wg_sc, wu_sc, wd_sc):
    j = pl.program_id(0)

    @pl.when((j == 0) | (te_ref[j] != te_ref[jnp.maximum(j - 1, 0)]))
    def _():
        wg_sc[...] = wg_ref[0, 0].astype(BF16)
        wu_sc[...] = wu_ref[0, 0].astype(BF16)
        wd_sc[...] = wd_ref[0, 0].astype(BF16)

    @pl.when(j < nv_ref[0])
    def _():
        x = _unpack_pairs(xs_ref[...]).astype(BF16)
        zg = _dot(x, wg_sc[...])
        act = zg * (1.0 / (1.0 + jnp.exp(-zg))) * _dot(x, wu_sc[...])
        ys_ref[...] = _pack_pairs(_dot(act.astype(BF16), wd_sc[...]))

    @pl.when(j >= nv_ref[0])
    def _():
        ys_ref[...] = jnp.zeros(ys_ref.shape, ys_ref.dtype)


def _experts(layer, xs, tile_expert, n_valid, wg, wu, wd):
    p = xs.shape[0]
    wspec = lambda r, c: pl.BlockSpec((1, 1, r, c), lambda j, te, nv: (layer, te[j], 0, 0))
    grid_spec = pltpu.PrefetchScalarGridSpec(
        num_scalar_prefetch=2,
        grid=(p // EXPERT_TILE,),
        in_specs=[pl.BlockSpec((EXPERT_TILE, 512), lambda j, te, nv: (j, 0)),
                  wspec(D_MODEL, D_FF), wspec(D_MODEL, D_FF), wspec(D_FF, D_MODEL)],
        out_specs=pl.BlockSpec((EXPERT_TILE, 512), lambda j, te, nv: (j, 0)),
        scratch_shapes=[pltpu.VMEM((D_MODEL, D_FF), BF16), pltpu.VMEM((D_MODEL, D_FF), BF16),
                        pltpu.VMEM((D_FF, D_MODEL), BF16)])
    return pl.pallas_call(
        _experts_body, grid_spec=grid_spec,
        out_shape=jax.ShapeDtypeStruct((p, 512), jnp.int32),
        compiler_params=_cparams(("arbitrary",)),
        name="experts",
    )(tile_expert, n_valid, xs, wg, wu, wd)


def _combine_body(x1_ref, y0_ref, y1_ref, route_ref, mod_ref, o_ref):
    tm = x1_ref.shape[0]
    eye = jnp.where(lax.broadcasted_iota(jnp.int32, (tm, tm), 0)
                    == lax.broadcasted_iota(jnp.int32, (tm, tm), 1), 1.0, 0.0).astype(BF16)
    r_hi, r_lo = _split(route_ref[...])
    wcol = _dot_nt(eye, r_hi) + _dot_nt(eye, r_lo)
    moe = wcol[:, 0:1] * _unpack_pairs(y0_ref[...]) + wcol[:, 1:2] * _unpack_pairs(y1_ref[...])
    o_ref[...] = x1_ref[...] + mod_ref[0][5:6] * moe


def _combine(x1, y0, y1, route, mods, rows_per_mod):
    n = x1.shape[0]
    tm = min(512, n)
    row = lambda w: pl.BlockSpec((tm, w), lambda i: (i, 0))
    return pl.pallas_call(
        _combine_body,
        grid=(n // tm,),
        in_specs=[row(D_MODEL), row(512), row(512), pl.BlockSpec((8, tm), lambda i: (0, i)),
                  pl.BlockSpec((1, 8, D_MODEL), lambda i: ((i * tm) // rows_per_mod, 0, 0))],
        out_specs=row(D_MODEL),
        out_shape=jax.ShapeDtypeStruct((n, D_MODEL), F32),
        compiler_params=_cparams(("parallel",)),
        name="combine",
    )(x1, y0, y1, route, mods)


def _moe(layer, x1, route, h2p, counts, mods, rows_per_mod, wg, wu, wd):
    n = x1.shape[0]
    p = 2 * n + N_EXPERTS * EXPERT_TILE
    cnt = counts[:, 0].astype(jnp.int32)
    padded = (cnt + EXPERT_TILE - 1) // EXPERT_TILE * EXPERT_TILE
    seg_end = jnp.cumsum(padded)
    seg_off = seg_end - padded
    experts = jnp.arange(N_EXPERTS, dtype=jnp.int32)

    def position(k):
        e = route[2 + k].astype(jnp.int32)
        off = jnp.sum(jnp.where(e[:, None] == experts[None], seg_off[None], 0), axis=1)
        return off + route[4 + k].astype(jnp.int32)

    pos0, pos1 = position(0), position(1)
    tile_start = jnp.arange(p // EXPERT_TILE, dtype=jnp.int32) * EXPERT_TILE
    tile_expert = jnp.sum(tile_start[:, None] >= seg_end[None], axis=1).astype(jnp.int32)
    tile_expert = jnp.minimum(tile_expert, N_EXPERTS - 1)
    n_valid = (seg_end[-1:] // EXPERT_TILE).astype(jnp.int32)

    xs = _sc_dispatch(h2p, pos0, pos1, p)
    ys = _experts(layer, xs, tile_expert, n_valid, wg, wu, wd)
    y0, y1 = _sc_collect(ys, pos0, pos1)
    return _combine(x1, y0, y1, route, mods, rows_per_mod)


def _block_ones(n_in, g_in, n_out, g_out, value=1.0):
    r = np.arange(n_in)[:, None] // g_in
    c = np.arange(n_out)[None, :] // g_out
    return jnp.asarray(np.where(r == c, value, 0.0), dtype=BF16)


def _rope_tables(seq, head_w):
    pos = np.arange(seq)
    rows, cols = pos // GRID_W, pos % GRID_W
    a = head_w // 2
    half = a // 2
    freqs = (ROPE_BASE ** (-np.arange(half, dtype=np.float32) / half)).astype(np.float32)
    lane = np.arange(LANES) % head_w
    within = lane % a
    first = within < half
    p = np.where((lane // a == 0)[None, :], rows[:, None], cols[:, None]).astype(np.float32)
    ang = (p * freqs[within % half][None, :]).astype(np.float32)
    cos, sin = np.cos(ang), np.sin(ang)
    return (jnp.asarray(cos, F32), jnp.asarray(np.where(first[None], -sin, 0.0), F32),
            jnp.asarray(np.where(first[None], 0.0, sin), F32))


def _tile_to(v, width):
    return jnp.tile(v, width // v.shape[0])


def _layer_params(i, p):
    w_in = p["w_in"][i]
    sp = np.cumsum((512, 128, 128, 512, 512, 512, Q_LORA, KV_LORA, ROPE_DIM))
    qa, ka, va, qb, kb, vb, cq, ckv, kr, gates = jnp.split(w_in, [int(s) for s in sp], axis=1)
    qa = qa.reshape(D_MODEL, WIN_HEADS, HEAD_DIM)[:, WIN_Q_ORDER, :].reshape(D_MODEL, 512)
    w_a = jnp.concatenate([qa, ka, va, qb, kb, vb, cq, ckv, jnp.tile(kr, (1, MLA_HEADS))],
                          axis=1).astype(BF16)
    w_uq = p["w_uq"][i].reshape(Q_LORA, MLA_HEADS, QK_DIM)
    w_uq = jnp.concatenate([w_uq[:, :, :NOPE_DIM].reshape(Q_LORA, 512),
                            w_uq[:, :, NOPE_DIM:].reshape(Q_LORA, 256)], axis=1).astype(BF16)
    w_ukv = p["w_ukv"][i].reshape(KV_LORA, MLA_HEADS, NOPE_DIM + V_DIM)
    w_ukv = jnp.concatenate([w_ukv[:, :, :NOPE_DIM].reshape(KV_LORA, 512),
                             w_ukv[:, :, NOPE_DIM:].reshape(KV_LORA, 512)], axis=1).astype(BF16)
    z = jnp.zeros((D_MODEL,), F32)
    row = lambda *parts: jnp.concatenate(list(parts) + [z])[:D_MODEL]
    q_scale = HEAD_DIM ** -0.5 * LOG2E
    c_scale = QK_DIM ** -0.5 * LOG2E
    g_mla = p["g_qk_mla"][i]
    gains = jnp.stack([
        p["g_norm_mix"][i],
        row(_tile_to(p["g_qk_win"][i, 0], 512) * q_scale, _tile_to(p["g_qk_win"][i, 1], 128)),
        row(_tile_to(p["g_qk_nbr"][i, 0], 512) * q_scale, _tile_to(p["g_qk_nbr"][i, 1], 512)),
        row(p["g_q_lora"][i], p["g_kv_lora"][i]),
        row(_tile_to(g_mla[0, :NOPE_DIM], 512) * c_scale, _tile_to(g_mla[0, NOPE_DIM:], 256) * c_scale),
        row(_tile_to(g_mla[1, :NOPE_DIM], 512), _tile_to(g_mla[1, NOPE_DIM:], 256)),
        p["g_norm_ffn"][i],
        z]).astype(F32)
    wo_a = p["w_o_win"][i].reshape(WIN_HEADS, HEAD_DIM, D_MODEL)[WIN_Q_ORDER, :, :].reshape(512, D_MODEL)
    return dict(
        layer=i, w_a=w_a, w_uq=w_uq, w_ukv=w_ukv, gains=gains, w_gate=gates.astype(BF16),
        wo_a=wo_a.astype(BF16), wo_b=p["w_o_nbr"][i].astype(BF16), wo_c=p["w_o_mla"][i].astype(BF16),
        w_out=p["w_out"][i].astype(BF16),
        nbr_table=_nbr_bias_table(p["nbr_rel_bias"][i]))


def kernel(x_prompt, x_sample, cache_win_k, cache_win_v, cache_nbr_k, cache_nbr_v, cache_mla_ckv, cache_mla_krope, c, c_ctx, g_norm_mix, g_norm_ffn, w_ada, b_ada, w_in, g_qk_win, win_sink, g_qk_nbr, nbr_rel_bias, g_q_lora, g_kv_lora, w_uq, w_ukv, g_qk_mla, w_o_win, w_o_nbr, w_o_mla, w_out, w_router, b_router, w_exp_gate, w_exp_up, w_exp_down):
    p = dict(g_norm_mix=g_norm_mix, g_norm_ffn=g_norm_ffn, w_in=w_in, g_qk_win=g_qk_win,
             g_qk_nbr=g_qk_nbr, nbr_rel_bias=nbr_rel_bias, g_q_lora=g_q_lora, g_kv_lora=g_kv_lora,
             w_uq=w_uq, w_ukv=w_ukv, g_qk_mla=g_qk_mla, w_o_win=w_o_win, w_o_nbr=w_o_nbr,
             w_o_mla=w_o_mla, w_out=w_out, w_exp_gate=w_exp_gate, w_exp_up=w_exp_up,
             w_exp_down=w_exp_down)
    depth = w_in.shape[0]
    batch, seq, _ = x_prompt.shape
    dec_batch, dec_seq, _ = x_sample.shape
    past = cache_win_k.shape[2]

    n_c = 1 + dec_batch
    c_rows = -(-n_c // 8) * 8
    c_all = jnp.concatenate([c_ctx[None], c, jnp.zeros((c_rows - n_c, D_MODEL), F32)], axis=0)
    mods = _ada(c_all, w_ada, b_ada).reshape(depth, c_rows, 6, D_MODEL)
    mods = jnp.pad(mods, ((0, 0), (0, 0), (0, 2), (0, 0)))

    mats = (_block_ones(512, 64, 512, 64, 1.0 / HEAD_DIM), _block_ones(512, 64, 512, 64),
            _block_ones(256, 32, 512, 64), _block_ones(512, 64, 256, 32), _block_ones(256, 32, 256, 32))
    tabs = _rope_tables(dec_seq, 64) + _rope_tables(dec_seq, 32)
    sink = win_sink.astype(F32) * LOG2E
    w_r_t = w_router.T.astype(F32)
    b_r = b_router.astype(F32).reshape(N_EXPERTS, 1)
    layers = [_layer_params(i, p) for i in range(depth)]

    def ffn(merged, mod, rows_per_mod, lp):
        x1, route, h2p, counts = merged
        return _moe(lp["layer"], x1, route, h2p, counts, mod, rows_per_mod, w_exp_gate, w_exp_up,
                    w_exp_down)

    def merge(x, oa, ob, oc, mod, rows_per_mod, lp):
        return _merge(x, oa, ob, oc, mod, rows_per_mod, lp["gains"], lp["w_gate"], lp["wo_a"],
                      lp["wo_b"], lp["wo_c"], lp["w_out"], w_r_t, b_r)

    n_ctx = batch * seq
    x = x_prompt.reshape(n_ctx, D_MODEL)
    states = []
    for i, lp in enumerate(layers):
        mod = mods[i, 0:1]
        outs = _inproj(x, mod, n_ctx, lp["gains"], lp["w_a"], lp["w_uq"], lp["w_ukv"], mats, None,
                       seq, True)
        oa, ob, oc = _ctx_attn(i, sink, seq, *outs[:9])
        states.append(outs[9:])
        x = ffn(merge(x, oa, ob, oc, mod, n_ctx, lp), mod, n_ctx, lp)
    y_prompt = x.reshape(batch, seq, D_MODEL)

    n_lat = dec_batch * dec_seq
    x = x_sample.reshape(n_lat, D_MODEL)
    cwk = cache_win_k.reshape(dec_batch, depth, past, 128).astype(BF16)
    cwv = cache_win_v.reshape(dec_batch, depth, past, 128).astype(BF16)
    cnk = cache_nbr_k.reshape(dec_batch, depth, past, 512).astype(BF16)
    cnv = cache_nbr_v.reshape(dec_batch, depth, past, 512).astype(BF16)
    for i, lp in enumerate(layers):
        mod = mods[i, 1:1 + dec_batch]
        qa, ka, va, qb, kb, vb, qc, kc, vc = _inproj(
            x, mod, dec_seq, lp["gains"], lp["w_a"], lp["w_uq"], lp["w_ukv"], mats, tabs, dec_seq, False)
        kr_t = jnp.tile(cache_mla_krope[:, i].reshape(dec_batch * past, ROPE_DIM), (1, MLA_HEADS))
        kc_c, vc_c = _mla_cache_keys(cache_mla_ckv[:, i].reshape(dec_batch * past, KV_LORA), kr_t,
                                     lp["gains"], lp["w_ukv"], mats[1], mats[3], mats[4])
        r3 = lambda a: a.reshape(dec_batch, dec_seq, a.shape[-1])
        oa = _win_attn(i, sink, r3(qa), r3(ka), r3(va), cwk, cwv)
        ob = _nbr_attn(i, r3(qb), r3(kb), r3(vb), cnk, cnv, lp["nbr_table"])
        oc = _mla_attn(r3(qc), r3(kc), r3(vc), kc_c.reshape(dec_batch, past, 1024),
                       vc_c.reshape(dec_batch, past, 512))
        flat = lambda a: a.reshape(n_lat, 512)
        x = ffn(merge(x, flat(oa), flat(ob), flat(oc), mod, dec_seq, lp), mod, dec_seq, lp)
    y_sample = x.reshape(dec_batch, dec_seq, D_MODEL)

    def stack(k, shape):
        return jnp.stack([s[k].reshape((batch, seq) + shape) for s in states], axis=1)

    return (y_prompt, y_sample,
            stack(0, (WIN_KV_HEADS, HEAD_DIM)), stack(1, (WIN_KV_HEADS, HEAD_DIM)),
            stack(2, (NBR_HEADS, HEAD_DIM)), stack(3, (NBR_HEADS, HEAD_DIM)),
            stack(4, (KV_LORA,)), stack(5, (ROPE_DIM,)))
```

```python
import functools

import numpy as np
import jax
import jax.numpy as jnp
from jax import lax
from jax.experimental import pallas as pl
from jax.experimental.pallas import tpu as pltpu
from jax.experimental.pallas import tpu_sc as plsc

D_MODEL = 1024
GRID_W = 64
HEAD_DIM = 64
WIN_HEADS = 8
WIN_KV_HEADS = 2
WINDOW = 128
NBR_HEADS = 8
NBR_ROWS = 8
NBR_COLS = 16
MLA_HEADS = 8
Q_LORA = 256
KV_LORA = 128
NOPE_DIM = 64
ROPE_DIM = 32
V_DIM = 64
QK_DIM = NOPE_DIM + ROPE_DIM
N_EXPERTS = 16
N_GROUPS = 4
EXPERTS_PER_GROUP = 4
D_FF = 512
ROPE_BASE = 10000.0
EPS = 1e-6

LANES = 128
LOG2E = 1.4426950408889634
NEG = -1e30
VMEM_LIMIT = 56 * 1024 * 1024

F32 = jnp.float32
BF16 = jnp.bfloat16

C_QA, C_KA, C_VA, C_QB, C_KB, C_VB, C_CQ, C_CKV, C_KR, C_END = (
    0, 512, 640, 768, 1280, 1792, 2304, 2560, 2688, 2944)
WIN_Q_ORDER = (0, 4, 1, 5, 2, 6, 3, 7)


def _cparams(sem):
    return pltpu.CompilerParams(dimension_semantics=sem, vmem_limit_bytes=VMEM_LIMIT)


def _dot(a, b):
    return jnp.dot(a, b, preferred_element_type=F32)


def _dot_nt(a, b):
    return lax.dot_general(a, b, (((1,), (1,)), ((), ())), preferred_element_type=F32)


def _split(x):
    hi = x.astype(BF16)
    lo = (x - hi.astype(F32)).astype(BF16)
    return hi, lo


def _gsum(x2, bmat):
    return _dot(x2.astype(BF16), bmat)


def _tile_lanes(t, width):
    reps = width // t.shape[-1]
    return t if reps == 1 else jnp.concatenate([t] * reps, axis=-1)


def _rotate(x, cos, sin_a, sin_b, half):
    w = x.shape[-1]
    up = pltpu.roll(x, w - half, 1)
    dn = pltpu.roll(x, half, 1)
    return (x * _tile_lanes(cos, w) + up * _tile_lanes(sin_a, w) + dn * _tile_lanes(sin_b, w))


def _norm_mod(x, gain, scale, shift):
    ms = jnp.mean(x * x, axis=-1, keepdims=True)
    return (x * lax.rsqrt(ms + EPS) * gain) * (1.0 + scale) + shift


def _ada_body(c_ref, w_ref, b_ref, o_ref):
    c = c_ref[...]
    a = c * (1.0 / (1.0 + jnp.exp(-c)))
    a_hi, a_lo = _split(a)
    w_hi, w_lo = _split(w_ref[0])
    o_ref[0] = _dot(a_hi, w_hi) + _dot(a_hi, w_lo) + _dot(a_lo, w_hi) + b_ref[0]


def _ada(c_all, w_ada, b_ada):
    depth = w_ada.shape[0]
    rows = c_all.shape[0]
    tn = 1536
    return pl.pallas_call(
        _ada_body,
        grid=(depth, 6 * D_MODEL // tn),
        in_specs=[pl.BlockSpec((rows, D_MODEL), lambda l, j: (0, 0)),
                  pl.BlockSpec((1, D_MODEL, tn), lambda l, j: (l, 0, j)),
                  pl.BlockSpec((1, 1, tn), lambda l, j: (l, 0, j))],
        out_specs=pl.BlockSpec((1, rows, tn), lambda l, j: (l, 0, j)),
        out_shape=jax.ShapeDtypeStruct((depth, rows, 6 * D_MODEL), F32),
        compiler_params=_cparams(("parallel", "parallel")),
        name="ada",
    )(c_all, w_ada, b_ada.reshape(depth, 1, 6 * D_MODEL))


def _mla_key_tail(ckvn_b, kr_t, g, wukv_ref, bnn, bnr, brr, rope_tabs, kc_ref, vc_ref):
    kv = _dot(ckvn_b, wukv_ref[...])
    kn = kv[:, 0:512]
    vc_ref[...] = kv[:, 512:1024].astype(BF16)
    kn2 = kn * kn
    kr2 = kr_t * kr_t
    kr_sum32 = _gsum(kr2, brr)
    ssn = (_gsum(kn2, bnn) + jnp.concatenate([kr_sum32, kr_sum32], axis=-1)) * (1.0 / QK_DIM)
    ssr = (_gsum(kn2, bnr) + kr_sum32) * (1.0 / QK_DIM)
    kn = kn * lax.rsqrt(ssn + EPS) * g[5:6, 0:512]
    kr = kr_t * lax.rsqrt(ssr + EPS) * g[5:6, 512:768]
    if rope_tabs is not None:
        kr = _rotate(kr, *rope_tabs, 8)
    for p in range(4):
        kc_ref[:, 256 * p:256 * p + 128] = kn[:, 128 * p:128 * p + 128].astype(BF16)
        q4 = 128 * (p // 2)
        kc_ref[:, 256 * p + 128:256 * p + 256] = kr[:, q4:q4 + 128].astype(BF16)


def _inproj_body(rope, states, *refs):
    (x_ref, mod_ref, g_ref, w_ref, wuq_ref, wukv_ref, b64_ref, bnn_ref, brn_ref, bnr_ref,
     brr_ref) = refs[:11]
    refs = refs[11:]
    if rope:
        tabs_w = tuple(r[...] for r in refs[0:3])
        tabs_m = tuple(r[...] for r in refs[3:6])
        refs = refs[6:]
    else:
        tabs_w = tabs_m = None
    qa_ref, ka_ref, va_ref, qb_ref, kb_ref, vb_ref, qc_ref, kc_ref, vc_ref = refs[:9]
    st = refs[9:]

    g = g_ref[...]
    mod = mod_ref[0]
    hb = _norm_mod(x_ref[...], g[0:1], mod[1:2], mod[0:1]).astype(BF16)

    def proj(a, b):
        return _dot(hb, w_ref[:, a:b])

    b64 = b64_ref[...]

    def head_norm(z, bmat, gain):
        return z * lax.rsqrt(_gsum(z * z, bmat) + EPS) * gain

    qa = head_norm(proj(C_QA, C_KA), b64, g[1:2, 0:512])
    ka = head_norm(proj(C_KA, C_VA), b64[0:128, 0:128], g[1:2, 512:640])
    va = proj(C_VA, C_QB)
    if states:
        st[0][...] = ka
        st[1][...] = va
    if rope:
        qa = _rotate(qa, *tabs_w, 16)
        ka = _rotate(ka, *tabs_w, 16)
    qa_ref[...] = qa.astype(BF16)
    ka_ref[...] = ka.astype(BF16)
    va_ref[...] = va.astype(BF16)

    qb = head_norm(proj(C_QB, C_KB), b64, g[2:3, 0:512])
    kb = head_norm(proj(C_KB, C_VB), b64, g[2:3, 512:1024])
    vb = proj(C_VB, C_CQ)
    if states:
        st[2][...] = kb
        st[3][...] = vb
    qb_ref[...] = qb.astype(BF16)
    kb_ref[...] = kb.astype(BF16)
    vb_ref[...] = vb.astype(BF16)

    cq = proj(C_CQ, C_CKV)
    cqn = cq * lax.rsqrt(jnp.mean(cq * cq, axis=-1, keepdims=True) + EPS) * g[3:4, 0:256]
    qq = _dot(cqn.astype(BF16), wuq_ref[...])
    qn, qr = qq[:, 0:512], qq[:, 512:768]
    qn2, qr2 = qn * qn, qr * qr
    bnn, brn, bnr, brr = bnn_ref[...], brn_ref[...], bnr_ref[...], brr_ref[...]
    ssn = (_gsum(qn2, bnn) + _gsum(qr2, brn)) * (1.0 / QK_DIM)
    ssr = (_gsum(qn2, bnr) + _gsum(qr2, brr)) * (1.0 / QK_DIM)
    qn = qn * lax.rsqrt(ssn + EPS) * g[4:5, 0:512]
    qr = qr * lax.rsqrt(ssr + EPS) * g[4:5, 512:768]
    if rope:
        qr = _rotate(qr, *tabs_m, 8)
    for p in range(4):
        qc_ref[:, 256 * p:256 * p + 128] = qn[:, 128 * p:128 * p + 128].astype(BF16)
        q4 = 128 * (p // 2)
        qc_ref[:, 256 * p + 128:256 * p + 256] = qr[:, q4:q4 + 128].astype(BF16)

    ckv = proj(C_CKV, C_KR)
    ckvn = ckv * lax.rsqrt(jnp.mean(ckv * ckv, axis=-1, keepdims=True) + EPS) * g[3:4, 256:384]
    kr_t = proj(C_KR, C_END)
    if states:
        st[4][...] = ckvn
        st[5][...] = kr_t[:, 0:ROPE_DIM]
    _mla_key_tail(ckvn.astype(BF16), kr_t, g, wukv_ref, bnn, bnr, brr, tabs_m, kc_ref, vc_ref)


def _const_spec(shape):
    nd = len(shape)
    return pl.BlockSpec(shape, lambda i, _nd=nd: (0,) * _nd, pipeline_mode=pl.Buffered(1))


def _inproj(x, mods, rows_per_mod, gains, w_a, w_uq, w_ukv, mats, rope_tabs, seq_len, states):
    n = x.shape[0]
    tm = min(1024, n)
    rope = rope_tabs is not None
    row = lambda w: pl.BlockSpec((tm, w), lambda i: (i, 0))
    in_specs = [row(D_MODEL),
                pl.BlockSpec((1, 8, D_MODEL), lambda i: ((i * tm) // rows_per_mod, 0, 0)),
                _const_spec(gains.shape), _const_spec(w_a.shape), _const_spec(w_uq.shape),
                _const_spec(w_ukv.shape)] + [_const_spec(m.shape) for m in mats]
    args = [x, mods, gains, w_a, w_uq, w_ukv, *mats]
    if rope:
        tiles_per_seq = seq_len // tm
        in_specs += [pl.BlockSpec((tm, LANES), lambda i: (i % tiles_per_seq, 0))] * 6
        args += list(rope_tabs)
    widths = [512, 128, 128, 512, 512, 512, 1024, 1024, 512]
    out_shape = [jax.ShapeDtypeStruct((n, w), BF16) for w in widths]
    out_specs = [row(w) for w in widths]
    if states:
        swidths = [128, 128, 512, 512, KV_LORA, ROPE_DIM]
        out_shape += [jax.ShapeDtypeStruct((n, w), F32) for w in swidths]
        out_specs += [row(w) for w in swidths]
    return pl.pallas_call(
        functools.partial(_inproj_body, rope, states),
        grid=(n // tm,), in_specs=in_specs, out_specs=out_specs, out_shape=out_shape,
        compiler_params=_cparams(("parallel",)),
        name="inproj_lat" if rope else "inproj_ctx",
    )(*args)


def _mla_cache_body(ckv_ref, kr_ref, g_ref, wukv_ref, bnn_ref, bnr_ref, brr_ref, kc_ref, vc_ref):
    _mla_key_tail(ckv_ref[...].astype(BF16), kr_ref[...], g_ref[...], wukv_ref, bnn_ref[...],
                  bnr_ref[...], brr_ref[...], None, kc_ref, vc_ref)


def _mla_cache_keys(ckv, kr_t, gains, w_ukv, bnn, bnr, brr):
    n = ckv.shape[0]
    tm = min(512, n)
    row = lambda w: pl.BlockSpec((tm, w), lambda i: (i, 0))
    return pl.pallas_call(
        _mla_cache_body,
        grid=(n // tm,),
        in_specs=[row(KV_LORA), row(256), _const_spec(gains.shape), _const_spec(w_ukv.shape),
                  _const_spec(bnn.shape), _const_spec(bnr.shape), _const_spec(brr.shape)],
        out_specs=[row(1024), row(512)],
        out_shape=[jax.ShapeDtypeStruct((n, 1024), BF16), jax.ShapeDtypeStruct((n, 512), BF16)],
        compiler_params=_cparams(("parallel",)),
        name="mla_cache_keys",
    )(ckv, kr_t, gains, w_ukv, bnn, bnr, brr)


def _lane_mask(width, ranges):
    lane = lax.broadcasted_iota(jnp.int32, (1, width), 1)
    m = None
    for lo, hi in ranges:
        c = (lane >= lo) & (lane < hi)
        m = c if m is None else (m | c)
    return jnp.where(m, 1.0, 0.0).astype(BF16)


def _stack_heads(q, mask0, mask1):
    return jnp.concatenate([q * mask0, q * mask1], axis=0)


def _lane_tiles(x):
    return [x[:, j:j + LANES] for j in range(0, x.shape[1], LANES)]


def _softmax_block(scores, sink=None):
    rows = scores[0].shape[0]
    mp = None
    for s in scores:
        for t in _lane_tiles(s):
            mp = t if mp is None else jnp.maximum(mp, t)
    base = sink if sink is not None else jnp.full((rows, LANES), NEG, F32)
    m = jnp.maximum(base, jnp.max(mp, axis=-1, keepdims=True))
    lp = None
    ps = []
    for s in scores:
        p = jnp.exp2(s - _tile_lanes(m, s.shape[1]))
        for t in _lane_tiles(p):
            lp = t if lp is None else lp + t
        ps.append(p.astype(BF16))
    if sink is not None:
        lane = lax.broadcasted_iota(jnp.int32, (rows, LANES), 1)
        lp = lp + jnp.where(lane == 0, jnp.exp2(sink - m), 0.0)
    p_all = ps[0] if len(ps) == 1 else jnp.concatenate(ps, axis=-1)
    return p_all, jnp.broadcast_to(jnp.sum(lp, axis=-1, keepdims=True), (rows, LANES))


def _softmax_pv(scores, values, sink=None):
    p_all, l = _softmax_block(scores, sink)
    v_all = values[0] if len(values) == 1 else jnp.concatenate(values, axis=0)
    return _dot(p_all, v_all) / l


def _merge_heads(o, tq):
    lane = lax.broadcasted_iota(jnp.int32, (tq, LANES), 1)
    return jnp.where(lane < HEAD_DIM, o[0:tq], o[tq:2 * tq])


def _mla_masks(p_mod2):
    lane = lax.broadcasted_iota(jnp.int32, (1, 256), 1)
    r0 = 128 + 32 * (2 * p_mod2)
    m0 = (lane < 64) | ((lane >= r0) & (lane < r0 + 32))
    m1 = ((lane >= 64) & (lane < 128)) | ((lane >= r0 + 32) & (lane < r0 + 64))
    return (jnp.where(m0, 1.0, 0.0).astype(BF16), jnp.where(m1, 1.0, 0.0).astype(BF16))


def _sink_col(sink_ref, layer, h0, h1, tq):
    row = lax.broadcasted_iota(jnp.int32, (2 * tq, LANES), 0)
    return jnp.where(row < tq, sink_ref[layer, h0], sink_ref[layer, h1])


def _ctx_attn_body(layer, sink_ref, qa_ref, ka_ref, va_ref, qb_ref, kb_ref, vb_ref, qc_ref, kc_ref,
                   vc_ref, oa_ref, ob_ref, oc_ref):
    tq = qa_ref.shape[0]
    lo = _lane_mask(LANES, [(0, 64)])
    hi = _lane_mask(LANES, [(64, 128)])
    ka, va = ka_ref[...], va_ref[...]
    for j in range(4):
        sl = slice(128 * j, 128 * j + 128)
        qs = _stack_heads(qa_ref[:, sl], lo, hi)
        sink = _sink_col(sink_ref, layer, j, 4 + j, tq)
        o = _softmax_pv([_dot_nt(qs, ka)], [va], sink)
        oa_ref[:, sl] = _merge_heads(o, tq).astype(BF16)

        qs = _stack_heads(qb_ref[:, sl], lo, hi)
        o = _softmax_pv([_dot_nt(qs, kb_ref[:, sl])], [vb_ref[:, sl]])
        ob_ref[:, sl] = _merge_heads(o, tq).astype(BF16)

        m0, m1 = _mla_masks(j % 2)
        s2 = slice(256 * j, 256 * j + 256)
        qs = _stack_heads(qc_ref[:, s2], m0, m1)
        o = _softmax_pv([_dot_nt(qs, kc_ref[:, s2])], [vc_ref[:, sl]])
        oc_ref[:, sl] = _merge_heads(o, tq).astype(BF16)


def _ctx_attn(layer, sink, seq, qa, ka, va, qb, kb, vb, qc, kc, vc):
    n = qa.shape[0]
    row = lambda w: pl.BlockSpec((seq, w), lambda b: (b, 0))
    ins = [qa, ka, va, qb, kb, vb, qc, kc, vc]
    return pl.pallas_call(
        functools.partial(_ctx_attn_body, layer),
        grid=(n // seq,),
        in_specs=[pl.BlockSpec(memory_space=pltpu.SMEM)] + [row(a.shape[1]) for a in ins],
        out_specs=[row(512)] * 3,
        out_shape=[jax.ShapeDtypeStruct((n, 512), BF16)] * 3,
        compiler_params=_cparams(("parallel",)),
        name="ctx_attn",
    )(sink, *ins)


WIN_ROW_BLOCK = 32


WIN_Q_BLOCK = 128
WIN_BLOCKS_PER_STEP = 4


def _win_body(layer, sink_ref, q_ref, k_ref, v_ref, kc_ref, vc_ref, o_ref, s_sc, p_sc, l_sc):
    tq = WIN_Q_BLOCK
    seq = k_ref.shape[1]
    kw = 3 * tq
    rb = WIN_ROW_BLOCK
    lo = _lane_mask(LANES, [(0, 64)])
    hi = _lane_mask(LANES, [(64, 128)])
    kc, vc = kc_ref[0, 0], vc_ref[0, 0]
    for u in range(q_ref.shape[1] // tq):
        i = pl.program_id(1) * (q_ref.shape[1] // tq) + u
        kstart = pl.multiple_of(jnp.clip((i - 1) * tq, 0, seq - kw), tq)
        k_all = jnp.concatenate([k_ref[0, pl.ds(kstart, kw), :], kc], axis=0)
        v_all = jnp.concatenate([v_ref[0, pl.ds(kstart, kw), :], vc], axis=0)
        q_pos = i * tq + lax.broadcasted_iota(jnp.int32, (tq, kw), 0)
        k_pos = kstart + lax.broadcasted_iota(jnp.int32, (tq, kw), 1)
        band = jnp.abs(q_pos - k_pos) <= WINDOW
        qs = jnp.concatenate(
            [_stack_heads(q_ref[0, u * tq:(u + 1) * tq, 128 * j:128 * j + 128], lo, hi) for j in range(4)],
            axis=0)
        s_sc[u] = _dot_nt(qs, k_all)
        for j in range(4):
            for r in range(2 * tq * j, 2 * tq * (j + 1), rb):
                head = j if r < 2 * tq * j + tq else 4 + j
                q0 = r % tq
                s_band = jnp.where(band[q0:q0 + rb], s_sc[u, r:r + rb, 0:kw], NEG)
                sink = jnp.full((rb, LANES), sink_ref[layer, head], F32)
                p, l = _softmax_block([s_band, s_sc[u, r:r + rb, kw:]], sink)
                p_sc[u, r:r + rb, :] = p
                l_sc[u, r:r + rb, :] = l
            rows = slice(2 * tq * j, 2 * tq * (j + 1))
            o = _dot(p_sc[u, rows, :], v_all) / l_sc[u, rows, :]
            o_ref[0, u * tq:(u + 1) * tq, 128 * j:128 * j + 128] = _merge_heads(o, tq).astype(BF16)


def _win_attn(layer, sink, q, k, v, kc, vc):
    b, seq, _ = q.shape
    nb = min(WIN_BLOCKS_PER_STEP, seq // WIN_Q_BLOCK)
    tq = nb * WIN_Q_BLOCK
    past = kc.shape[2]
    keys = 3 * WIN_Q_BLOCK + past
    return pl.pallas_call(
        functools.partial(_win_body, layer),
        grid=(b, seq // tq),
        in_specs=[pl.BlockSpec(memory_space=pltpu.SMEM),
                  pl.BlockSpec((1, tq, 512), lambda bi, i: (bi, i, 0)),
                  pl.BlockSpec((1, seq, 128), lambda bi, i: (bi, 0, 0)),
                  pl.BlockSpec((1, seq, 128), lambda bi, i: (bi, 0, 0)),
                  pl.BlockSpec((1, 1, past, 128), lambda bi, i: (bi, layer, 0, 0)),
                  pl.BlockSpec((1, 1, past, 128), lambda bi, i: (bi, layer, 0, 0))],
        out_specs=pl.BlockSpec((1, tq, 512), lambda bi, i: (bi, i, 0)),
        out_shape=jax.ShapeDtypeStruct((b, seq, 512), BF16),
        scratch_shapes=[pltpu.VMEM((nb, 8 * WIN_Q_BLOCK, keys), F32),
                        pltpu.VMEM((nb, 8 * WIN_Q_BLOCK, keys), BF16),
                        pltpu.VMEM((nb, 8 * WIN_Q_BLOCK, LANES), F32)],
        compiler_params=_cparams(("parallel", "arbitrary")),
        name="win_attn",
    )(sink, q, k, v, kc, vc)


NBR_TILE_ROWS = 4
NBR_WIN_ROWS = NBR_TILE_ROWS + NBR_ROWS
NBR_TAB_PAD = NBR_WIN_ROWS - NBR_ROWS
NBR_ROW_BLOCK = 32
NBR_TILES_PER_STEP = 2


def _nbr_body(rows, q_ref, k_ref, v_ref, kc_ref, vc_ref, tab_ref, o_ref, s_sc, p_sc, l_sc):
    tq = NBR_TILE_ROWS * GRID_W
    kw = NBR_WIN_ROWS * GRID_W
    lo = _lane_mask(LANES, [(0, 64)])
    hi = _lane_mask(LANES, [(64, 128)])
    past = kc_ref.shape[2]
    rb = NBR_ROW_BLOCK
    tiles = q_ref.shape[1] // tq
    for u in range(tiles):
        r0 = NBR_TILE_ROWS * (pl.program_id(1) * tiles + u)
        ws = jnp.clip(r0 - NBR_ROWS // 2, 0, rows - NBR_WIN_ROWS)
        kstart = pl.multiple_of(ws * GRID_W, LANES)
        k_row = ws + lax.broadcasted_iota(jnp.int32, (1, kw), 1) // GRID_W
        for j in range(4):
            sl = slice(128 * j, 128 * j + 128)
            qs = _stack_heads(q_ref[0, u * tq:(u + 1) * tq, sl], lo, hi)
            s_sc[u, j, :, 0:kw] = _dot_nt(qs, k_ref[0, pl.ds(kstart, kw), sl])
            s_sc[u, j, :, kw:kw + past] = _dot_nt(qs, kc_ref[0, 0, :, sl])
        for j in range(4):
            sl = slice(128 * j, 128 * j + 128)
            for b0 in range(0, 2 * tq, rb):
                h = 2 * j + b0 // tq
                ql, sub = divmod(b0 % tq, GRID_W)
                d0 = ws - r0 - ql + (NBR_ROWS - 1) + NBR_TAB_PAD
                bias = jnp.concatenate([tab_ref[h, d0 + 2 * m, sub:sub + rb, :]
                                        for m in range(NBR_WIN_ROWS // 2)], axis=-1)
                rs = jnp.clip(r0 + ql - NBR_ROWS // 2, 0, rows - NBR_ROWS)
                valid = (k_row >= rs) & (k_row < rs + NBR_ROWS)
                s_nb = jnp.where(valid, s_sc[u, j, b0:b0 + rb, 0:kw] + bias, NEG)
                p, l = _softmax_block([s_nb, s_sc[u, j, b0:b0 + rb, kw:kw + past]])
                p_sc[u, j, b0:b0 + rb, :] = p
                l_sc[u, j, b0:b0 + rb, :] = l
            v_all = jnp.concatenate([v_ref[0, pl.ds(kstart, kw), sl], vc_ref[0, 0, :, sl]], axis=0)
            o = _dot(p_sc[u, j], v_all) / l_sc[u, j]
            o_ref[0, u * tq:(u + 1) * tq, sl] = _merge_heads(o, tq).astype(BF16)


def _nbr_attn(layer, q, k, v, kc, vc, table):
    b, seq, _ = q.shape
    rows = seq // GRID_W
    tile = NBR_TILE_ROWS * GRID_W
    tq = NBR_TILES_PER_STEP * tile
    past = kc.shape[2]
    keys = NBR_WIN_ROWS * GRID_W + past
    once = pl.Buffered(1)
    return pl.pallas_call(
        functools.partial(_nbr_body, rows),
        grid=(b, seq // tq),
        in_specs=[pl.BlockSpec((1, tq, 512), lambda bi, i: (bi, i, 0)),
                  pl.BlockSpec((1, seq, 512), lambda bi, i: (bi, 0, 0), pipeline_mode=once),
                  pl.BlockSpec((1, seq, 512), lambda bi, i: (bi, 0, 0), pipeline_mode=once),
                  pl.BlockSpec((1, 1, past, 512), lambda bi, i: (bi, layer, 0, 0), pipeline_mode=once),
                  pl.BlockSpec((1, 1, past, 512), lambda bi, i: (bi, layer, 0, 0), pipeline_mode=once),
                  pl.BlockSpec(table.shape, lambda bi, i: (0, 0, 0, 0), pipeline_mode=once)],
        out_specs=pl.BlockSpec((1, tq, 512), lambda bi, i: (bi, i, 0)),
        out_shape=jax.ShapeDtypeStruct((b, seq, 512), BF16),
        scratch_shapes=[pltpu.VMEM((NBR_TILES_PER_STEP, 4, 2 * tile, keys), F32),
                        pltpu.VMEM((NBR_TILES_PER_STEP, 4, 2 * tile, keys), BF16),
                        pltpu.VMEM((NBR_TILES_PER_STEP, 4, 2 * tile, LANES), F32)],
        compiler_params=_cparams(("parallel", "arbitrary")),
        name="nbr_attn",
    )(q, k, v, kc, vc, table)


def _nbr_bias_table(rel_bias):
    col = np.arange(GRID_W)
    cs = np.clip(col - NBR_COLS // 2, 0, GRID_W - NBR_COLS)
    kc = np.arange(GRID_W)
    ok = (kc[None, :] >= cs[:, None]) & (kc[None, :] < cs[:, None] + NBR_COLS)
    dc = kc[None, :] - col[:, None] + (NBR_COLS - 1)
    pick = (dc[:, :, None] == np.arange(2 * NBR_COLS - 1)[None, None, :]) & ok[:, :, None]
    t = jnp.einsum("hdk,qck->hdqc", rel_bias.astype(F32) * LOG2E, jnp.asarray(pick, F32),
                   precision=lax.Precision.HIGHEST)
    t = jnp.where(jnp.asarray(ok)[None, None], t, NEG)
    t = jnp.pad(t, ((0, 0), (NBR_TAB_PAD, NBR_TAB_PAD), (0, 0), (0, 0)))
    return jnp.concatenate([t[:, :-1], t[:, 1:]], axis=-1)


MLA_Q_TILE = 1024
MLA_KEY_CHUNK = 512
MLA_ROW_BLOCK = 64


def _mla_body(q_ref, kl_ref, kc_ref, vl_ref, vc_ref, o_ref, qs_sc, m_sc, l_sc, acc_sc, s_sc, p_sc, a_sc):
    tq = q_ref.shape[1]
    rows = 2 * tq
    seq = kl_ref.shape[1]
    past = kc_ref.shape[1]
    tk = min(MLA_KEY_CHUNK, past)
    rb = MLA_ROW_BLOCK
    m0, m1 = _mla_masks(pl.program_id(1) % 2)
    qs_sc[...] = _stack_heads(q_ref[0], m0, m1)
    m_sc[...] = jnp.full(m_sc.shape, NEG, F32)
    l_sc[...] = jnp.zeros(l_sc.shape, F32)
    acc_sc[...] = jnp.zeros(acc_sc.shape, F32)

    def step(c, k, v):
        slab = c % 2
        s_sc[slab] = _dot_nt(qs_sc[...], k)
        for r in range(0, rows, rb):
            sl = slice(r, r + rb)
            sb = s_sc[slab, sl, :]
            mp = None
            for t in _lane_tiles(sb):
                mp = t if mp is None else jnp.maximum(mp, t)
            m_prev = m_sc[sl]
            m_new = jnp.maximum(m_prev, jnp.max(mp, axis=-1, keepdims=True))
            alpha = jnp.exp2(m_prev - m_new)
            p = jnp.exp2(sb - _tile_lanes(m_new, tk))
            psum = None
            for t in _lane_tiles(p):
                psum = t if psum is None else psum + t
            l_sc[sl] = alpha * l_sc[sl] + psum
            m_sc[sl] = m_new
            a_sc[slab, sl, :] = alpha
            p_sc[slab, sl, :] = p.astype(BF16)
        acc_sc[...] = a_sc[slab] * acc_sc[...] + _dot(p_sc[slab], v)

    n_lat = seq // tk
    for c in range(n_lat):
        step(c, kl_ref[0, c * tk:(c + 1) * tk, :], vl_ref[0, c * tk:(c + 1) * tk, :])
    for c in range(past // tk):
        step(n_lat + c, kc_ref[0, c * tk:(c + 1) * tk, :], vc_ref[0, c * tk:(c + 1) * tk, :])
    l = jnp.sum(l_sc[...], axis=-1, keepdims=True)
    o_ref[0] = _merge_heads(acc_sc[...] / l, tq).astype(BF16)


def _mla_attn(q, kl, vl, kc, vc):
    b, seq, _ = q.shape
    past = kc.shape[1]
    tq = min(MLA_Q_TILE, seq)
    tk = min(MLA_KEY_CHUNK, past)
    return pl.pallas_call(
        _mla_body,
        grid=(b, 4, seq // tq),
        in_specs=[pl.BlockSpec((1, tq, 256), lambda bi, p, qi: (bi, qi, p)),
                  pl.BlockSpec((1, seq, 256), lambda bi, p, qi: (bi, 0, p)),
                  pl.BlockSpec((1, past, 256), lambda bi, p, qi: (bi, 0, p)),
                  pl.BlockSpec((1, seq, 128), lambda bi, p, qi: (bi, 0, p)),
                  pl.BlockSpec((1, past, 128), lambda bi, p, qi: (bi, 0, p))],
        out_specs=pl.BlockSpec((1, tq, 128), lambda bi, p, qi: (bi, qi, p)),
        out_shape=jax.ShapeDtypeStruct((b, seq, 512), BF16),
        scratch_shapes=[pltpu.VMEM((2 * tq, 256), BF16), pltpu.VMEM((2 * tq, LANES), F32),
                        pltpu.VMEM((2 * tq, LANES), F32), pltpu.VMEM((2 * tq, LANES), F32),
                        pltpu.VMEM((2, 2 * tq, tk), F32), pltpu.VMEM((2, 2 * tq, tk), BF16),
                        pltpu.VMEM((2, 2 * tq, LANES), F32)],
        compiler_params=_cparams(("parallel", "parallel", "arbitrary")),
        name="mla_attn",
    )(q, kl, kc, vl, vc)


def _pack_pairs(x):
    w = x.shape[1] // 2
    hi = lax.bitcast_convert_type(x[:, :w].astype(BF16).astype(F32), jnp.int32)
    lo = lax.bitcast_convert_type(x[:, w:].astype(BF16).astype(F32), jnp.int32)
    return (hi & jnp.int32(-65536)) | lax.shift_right_logical(lo, jnp.int32(16))


def _unpack_pairs(p):
    hi = lax.bitcast_convert_type(p & jnp.int32(-65536), F32)
    lo = lax.bitcast_convert_type(lax.shift_left(p, jnp.int32(16)), F32)
    return jnp.concatenate([hi, lo], axis=-1)


def _merge_body(x_ref, oa_ref, ob_ref, oc_ref, mod_ref, g_ref, wg_ref, woa_ref, wob_ref, woc_ref,
                wout_ref, wr_ref, br_ref, tri_ref, x1_ref, route_ref, h2p_ref, count_ref, count_sc):
    x = x_ref[...]
    g = g_ref[...]
    mod = mod_ref[0]
    hb = _norm_mod(x, g[0:1], mod[1:2], mod[0:1]).astype(BF16)
    m = None
    for br, (o_ref, wo_ref) in enumerate(((oa_ref, woa_ref), (ob_ref, wob_ref), (oc_ref, woc_ref))):
        z = _dot(hb, wg_ref[:, D_MODEL * br:D_MODEL * (br + 1)])
        gate = 1.0 / (1.0 + jnp.exp(-z))
        t = gate * _dot(o_ref[...], wo_ref[...])
        m = t if m is None else m + t
    y = _dot(m.astype(BF16), wout_ref[...])
    x1 = x + mod[2:3] * y
    x1_ref[...] = x1

    h2 = _norm_mod(x1, g[6:7], mod[4:5], mod[3:4])
    h_hi, h_lo = _split(h2)
    w_hi, w_lo = _split(wr_ref[...])
    both_w = _dot_nt(jnp.concatenate([w_hi, w_lo], axis=0), h_hi)
    logits = both_w[0:N_EXPERTS] + both_w[N_EXPERTS:2 * N_EXPERTS] + _dot_nt(w_hi, h_lo)
    score = 1.0 / (1.0 + jnp.exp(-logits))
    sel = score + br_ref[...]
    sel_r = [sel[e:e + 1] for e in range(N_EXPERTS)]
    sc_r = [score[e:e + 1] for e in range(N_EXPERTS)]
    picked = []
    for e in range(N_EXPERTS):
        grp, a = divmod(e, EXPERTS_PER_GROUP)
        rank = None
        for bb in range(EXPERTS_PER_GROUP):
            if bb == a:
                continue
            o = sel_r[grp * EXPERTS_PER_GROUP + bb]
            beats = (o >= sel_r[e]) if bb < a else (o > sel_r[e])
            r = jnp.where(beats, 1.0, 0.0)
            rank = r if rank is None else rank + r
        picked.append(rank < 2.0)
    gscore = []
    for grp in range(N_GROUPS):
        tot = None
        for a in range(EXPERTS_PER_GROUP):
            e = grp * EXPERTS_PER_GROUP + a
            t = jnp.where(picked[e], sel_r[e], 0.0)
            tot = t if tot is None else tot + t
        gscore.append(tot)
    best = jnp.zeros_like(gscore[0])
    best_v = gscore[0]
    for grp in range(1, N_GROUPS):
        upd = gscore[grp] > best_v
        best = jnp.where(upd, float(grp), best)
        best_v = jnp.where(upd, gscore[grp], best_v)
    cw, pk = [], []
    for a in range(EXPERTS_PER_GROUP):
        tot = flag = None
        for grp in range(N_GROUPS):
            e = grp * EXPERTS_PER_GROUP + a
            f = (best == float(grp)) & picked[e]
            t = jnp.where(f, sc_r[e], 0.0)
            tot = t if tot is None else tot + t
            flag = f if flag is None else (flag | f)
        cw.append(tot)
        pk.append(flag)
    den = cw[0] + cw[1] + cw[2] + cw[3]
    first = jnp.where(pk[0], 0.0, jnp.where(pk[1], 1.0, jnp.where(pk[2], 2.0, 3.0)))
    second = jnp.where(pk[3], 3.0, jnp.where(pk[2], 2.0, jnp.where(pk[1], 1.0, 0.0)))
    slot_e, slot_w = [], []
    for which in (first, second):
        tot = None
        for a in range(EXPERTS_PER_GROUP):
            t = jnp.where(which == float(a), cw[a], 0.0)
            tot = t if tot is None else tot + t
        slot_w.append(tot / den)
        slot_e.append(best * float(EXPERTS_PER_GROUP) + which)

    @pl.when(pl.program_id(0) == 0)
    def _():
        count_sc[...] = jnp.zeros(count_sc.shape, F32)

    tm = x.shape[0]
    eid = lax.broadcasted_iota(jnp.int32, (N_EXPERTS, tm), 0).astype(F32)
    oh = [eid == slot_e[0], eid == slot_e[1]]
    both = jnp.where(oh[0] | oh[1], 1.0, 0.0)
    seen = count_sc[...][:, 0:1] + _dot(both.astype(BF16), tri_ref[...])
    for k in range(2):
        route_ref[k:k + 1, :] = slot_w[k]
        route_ref[2 + k:3 + k, :] = slot_e[k]
        route_ref[4 + k:5 + k, :] = jnp.sum(jnp.where(oh[k], seen, 0.0), axis=0, keepdims=True)
    route_ref[6:8, :] = jnp.zeros((2, tm), F32)
    count_sc[...] = count_sc[...] + jnp.sum(both, axis=-1, keepdims=True)
    count_ref[...] = count_sc[...]
    h2p_ref[...] = _pack_pairs(h2)


def _merge(x, oa, ob, oc, mods, rows_per_mod, gains, w_gate, wo_a, wo_b, wo_c, w_out, w_r_t, b_r):
    n = x.shape[0]
    tm = min(1024, n)
    row = lambda w: pl.BlockSpec((tm, w), lambda i: (i, 0))
    tri = jnp.asarray(np.triu(np.ones((tm, tm), np.float32), 1), BF16)
    consts = [gains, w_gate, wo_a, wo_b, wo_c, w_out, w_r_t, b_r, tri]
    return pl.pallas_call(
        _merge_body,
        grid=(n // tm,),
        in_specs=[row(D_MODEL), row(512), row(512), row(512),
                  pl.BlockSpec((1, 8, D_MODEL), lambda i: ((i * tm) // rows_per_mod, 0, 0))]
                 + [_const_spec(c.shape) for c in consts],
        out_specs=[row(D_MODEL), pl.BlockSpec((8, tm), lambda i: (0, i)), row(512),
                   pl.BlockSpec((N_EXPERTS, LANES), lambda i: (0, 0))],
        out_shape=[jax.ShapeDtypeStruct((n, D_MODEL), F32), jax.ShapeDtypeStruct((8, n), F32),
                   jax.ShapeDtypeStruct((n, 512), jnp.int32),
                   jax.ShapeDtypeStruct((N_EXPERTS, LANES), F32)],
        scratch_shapes=[pltpu.VMEM((N_EXPERTS, LANES), F32)],
        compiler_params=_cparams(("arbitrary",)),
        name="merge",
    )(x, oa, ob, oc, mods, *consts)


EXPERT_TILE = 512
SC_CORES = 2
SC_SUBCORES = 16
SC_WORKERS = SC_CORES * SC_SUBCORES
SC_WINDOW = 128


def _sc_mesh():
    return plsc.VectorSubcoreMesh(core_axis_name="c", subcore_axis_name="s", num_cores=SC_CORES,
                                  num_subcores=SC_SUBCORES)


def _sc_window_base(steps, j):
    wid = lax.axis_index("s") * SC_CORES + lax.axis_index("c")
    return pl.multiple_of((wid * steps + j) * SC_WINDOW, SC_WINDOW)


def _sc_dispatch(rows, pos0, pos1, n_out):
    n, w = rows.shape
    steps = n // (SC_WORKERS * SC_WINDOW)

    @functools.partial(
        pl.kernel, out_type=jax.ShapeDtypeStruct((n_out, w), rows.dtype), mesh=_sc_mesh(),
        scratch_types=[pltpu.VMEM((SC_WINDOW,), jnp.int32), pltpu.VMEM((SC_WINDOW,), jnp.int32),
                       pltpu.VMEM((SC_WINDOW, w), rows.dtype)],
        name="moe_dispatch")
    def run(x_hbm, i0_hbm, i1_hbm, o_hbm, i0_v, i1_v, rows_v):
        @pl.loop(0, steps)
        def _(j):
            base = _sc_window_base(steps, j)
            pltpu.sync_copy(i0_hbm.at[pl.ds(base, SC_WINDOW)], i0_v)
            pltpu.sync_copy(i1_hbm.at[pl.ds(base, SC_WINDOW)], i1_v)
            pltpu.sync_copy(x_hbm.at[pl.ds(base, SC_WINDOW)], rows_v)
            pltpu.sync_copy(rows_v, o_hbm.at[i0_v])
            pltpu.sync_copy(rows_v, o_hbm.at[i1_v])

    return run(rows, pos0, pos1)


def _sc_collect(rows, pos0, pos1):
    n = pos0.shape[0]
    w = rows.shape[1]
    steps = n // (SC_WORKERS * SC_WINDOW)
    out = jax.ShapeDtypeStruct((n, w), rows.dtype)

    @functools.partial(
        pl.kernel, out_type=[out, out], mesh=_sc_mesh(),
        scratch_types=[pltpu.VMEM((SC_WINDOW,), jnp.int32), pltpu.VMEM((SC_WINDOW, w), rows.dtype)],
        name="moe_collect")
    def run(y_hbm, i0_hbm, i1_hbm, o0_hbm, o1_hbm, i_v, rows_v):
        @pl.loop(0, steps)
        def _(j):
            base = _sc_window_base(steps, j)
            for i_hbm, o_hbm in ((i0_hbm, o0_hbm), (i1_hbm, o1_hbm)):
                pltpu.sync_copy(i_hbm.at[pl.ds(base, SC_WINDOW)], i_v)
                pltpu.sync_copy(y_hbm.at[i_v], rows_v)
                pltpu.sync_copy(rows_v, o_hbm.at[pl.ds(base, SC_WINDOW)])

    return run(rows, pos0, pos1)


def _experts_body(te_ref, nv_ref, xs_ref, wg_ref, wu_ref, wd_ref, ys_ref, wg_sc, wu_sc, wd_sc):
    j = pl.program_id(0)

    @pl.when((j == 0) | (te_ref[j] != te_ref[jnp.maximum(j - 1, 0)]))
    def _():
        wg_sc[...] = wg_ref[0, 0].astype(BF16)
        wu_sc[...] = wu_ref[0, 0].astype(BF16)
        wd_sc[...] = wd_ref[0, 0].astype(BF16)

    @pl.when(j < nv_ref[0])
    def _():
        x = _unpack_pairs(xs_ref[...]).astype(BF16)
        zg = _dot(x, wg_sc[...])
        act = zg * (1.0 / (1.0 + jnp.exp(-zg))) * _dot(x, wu_sc[...])
        ys_ref[...] = _pack_pairs(_dot(act.astype(BF16), wd_sc[...]))

    @pl.when(j >= nv_ref[0])
    def _():
        ys_ref[...] = jnp.zeros(ys_ref.shape, ys_ref.dtype)


def _experts(layer, xs, tile_expert, n_valid, wg, wu, wd):
    p = xs.shape[0]
    wspec = lambda r, c: pl.BlockSpec((1, 1, r, c), lambda j, te, nv: (layer, te[j], 0, 0))
    grid_spec = pltpu.PrefetchScalarGridSpec(
        num_scalar_prefetch=2,
        grid=(p // EXPERT_TILE,),
        in_specs=[pl.BlockSpec((EXPERT_TILE, 512), lambda j, te, nv: (j, 0)),
                  wspec(D_MODEL, D_FF), wspec(D_MODEL, D_FF), wspec(D_FF, D_MODEL)],
        out_specs=pl.BlockSpec((EXPERT_TILE, 512), lambda j, te, nv: (j, 0)),
        scratch_shapes=[pltpu.VMEM((D_MODEL, D_FF), BF16), pltpu.VMEM((D_MODEL, D_FF), BF16),
                        pltpu.VMEM((D_FF, D_MODEL), BF16)])
    return pl.pallas_call(
        _experts_body, grid_spec=grid_spec,
        out_shape=jax.ShapeDtypeStruct((p, 512), jnp.int32),
        compiler_params=_cparams(("arbitrary",)),
        name="experts",
    )(tile_expert, n_valid, xs, wg, wu, wd)


def _combine_body(x1_ref, y0_ref, y1_ref, route_ref, mod_ref, o_ref):
    tm = x1_ref.shape[0]
    eye = jnp.where(lax.broadcasted_iota(jnp.int32, (tm, tm), 0)
                    == lax.broadcasted_iota(jnp.int32, (tm, tm), 1), 1.0, 0.0).astype(BF16)
    r_hi, r_lo = _split(route_ref[...])
    wcol = _dot_nt(eye, r_hi) + _dot_nt(eye, r_lo)
    moe = wcol[:, 0:1] * _unpack_pairs(y0_ref[...]) + wcol[:, 1:2] * _unpack_pairs(y1_ref[...])
    o_ref[...] = x1_ref[...] + mod_ref[0][5:6] * moe


def _combine(x1, y0, y1, route, mods, rows_per_mod):
    n = x1.shape[0]
    tm = min(512, n)
    row = lambda w: pl.BlockSpec((tm, w), lambda i: (i, 0))
    return pl.pallas_call(
        _combine_body,
        grid=(n // tm,),
        in_specs=[row(D_MODEL), row(512), row(512), pl.BlockSpec((8, tm), lambda i: (0, i)),
                  pl.BlockSpec((1, 8, D_MODEL), lambda i: ((i * tm) // rows_per_mod, 0, 0))],
        out_specs=row(D_MODEL),
        out_shape=jax.ShapeDtypeStruct((n, D_MODEL), F32),
        compiler_params=_cparams(("parallel",)),
        name="combine",
    )(x1, y0, y1, route, mods)


def _moe(layer, x1, route, h2p, counts, mods, rows_per_mod, wg, wu, wd):
    n = x1.shape[0]
    p = 2 * n + N_EXPERTS * EXPERT_TILE
    cnt = counts[:, 0].astype(jnp.int32)
    padded = (cnt + EXPERT_TILE - 1) // EXPERT_TILE * EXPERT_TILE
    seg_end = jnp.cumsum(padded)
    seg_off = seg_end - padded
    experts = jnp.arange(N_EXPERTS, dtype=jnp.int32)

    def position(k):
        e = route[2 + k].astype(jnp.int32)
        off = jnp.sum(jnp.where(e[:, None] == experts[None], seg_off[None], 0), axis=1)
        return off + route[4 + k].astype(jnp.int32)

    pos0, pos1 = position(0), position(1)
    tile_start = jnp.arange(p // EXPERT_TILE, dtype=jnp.int32) * EXPERT_TILE
    tile_expert = jnp.sum(tile_start[:, None] >= seg_end[None], axis=1).astype(jnp.int32)
    tile_expert = jnp.minimum(tile_expert, N_EXPERTS - 1)
    n_valid = (seg_end[-1:] // EXPERT_TILE).astype(jnp.int32)

    xs = _sc_dispatch(h2p, pos0, pos1, p)
    ys = _experts(layer, xs, tile_expert, n_valid, wg, wu, wd)
    y0, y1 = _sc_collect(ys, pos0, pos1)
    return _combine(x1, y0, y1, route, mods, rows_per_mod)


def _block_ones(n_in, g_in, n_out, g_out, value=1.0):
    r = np.arange(n_in)[:, None] // g_in
    c = np.arange(n_out)[None, :] // g_out
    return jnp.asarray(np.where(r == c, value, 0.0), dtype=BF16)


def _rope_tables(seq, head_w):
    pos = np.arange(seq)
    rows, cols = pos // GRID_W, pos % GRID_W
    a = head_w // 2
    half = a // 2
    freqs = (ROPE_BASE ** (-np.arange(half, dtype=np.float32) / half)).astype(np.float32)
    lane = np.arange(LANES) % head_w
    within = lane % a
    first = within < half
    p = np.where((lane // a == 0)[None, :], rows[:, None], cols[:, None]).astype(np.float32)
    ang = (p * freqs[within % half][None, :]).astype(np.float32)
    cos, sin = np.cos(ang), np.sin(ang)
    return (jnp.asarray(cos, F32), jnp.asarray(np.where(first[None], -sin, 0.0), F32),
            jnp.asarray(np.where(first[None], 0.0, sin), F32))


def _tile_to(v, width):
    return jnp.tile(v, width // v.shape[0])


def _layer_params(i, p):
    w_in = p["w_in"][i]
    sp = np.cumsum((512, 128, 128, 512, 512, 512, Q_LORA, KV_LORA, ROPE_DIM))
    qa, ka, va, qb, kb, vb, cq, ckv, kr, gates = jnp.split(w_in, [int(s) for s in sp], axis=1)
    qa = qa.reshape(D_MODEL, WIN_HEADS, HEAD_DIM)[:, WIN_Q_ORDER, :].reshape(D_MODEL, 512)
    w_a = jnp.concatenate([qa, ka, va, qb, kb, vb, cq, ckv, jnp.tile(kr, (1, MLA_HEADS))],
                          axis=1).astype(BF16)
    w_uq = p["w_uq"][i].reshape(Q_LORA, MLA_HEADS, QK_DIM)
    w_uq = jnp.concatenate([w_uq[:, :, :NOPE_DIM].reshape(Q_LORA, 512),
                            w_uq[:, :, NOPE_DIM:].reshape(Q_LORA, 256)], axis=1).astype(BF16)
    w_ukv = p["w_ukv"][i].reshape(KV_LORA, MLA_HEADS, NOPE_DIM + V_DIM)
    w_ukv = jnp.concatenate([w_ukv[:, :, :NOPE_DIM].reshape(KV_LORA, 512),
                             w_ukv[:, :, NOPE_DIM:].reshape(KV_LORA, 512)], axis=1).astype(BF16)
    z = jnp.zeros((D_MODEL,), F32)
    row = lambda *parts: jnp.concatenate(list(parts) + [z])[:D_MODEL]
    q_scale = HEAD_DIM ** -0.5 * LOG2E
    c_scale = QK_DIM ** -0.5 * LOG2E
    g_mla = p["g_qk_mla"][i]
    gains = jnp.stack([
        p["g_norm_mix"][i],
        row(_tile_to(p["g_qk_win"][i, 0], 512) * q_scale, _tile_to(p["g_qk_win"][i, 1], 128)),
        row(_tile_to(p["g_qk_nbr"][i, 0], 512) * q_scale, _tile_to(p["g_qk_nbr"][i, 1], 512)),
        row(p["g_q_lora"][i], p["g_kv_lora"][i]),
        row(_tile_to(g_mla[0, :NOPE_DIM], 512) * c_scale, _tile_to(g_mla[0, NOPE_DIM:], 256) * c_scale),
        row(_tile_to(g_mla[1, :NOPE_DIM], 512), _tile_to(g_mla[1, NOPE_DIM:], 256)),
        p["g_norm_ffn"][i],
        z]).astype(F32)
    wo_a = p["w_o_win"][i].reshape(WIN_HEADS, HEAD_DIM, D_MODEL)[WIN_Q_ORDER, :, :].reshape(512, D_MODEL)
    return dict(
        layer=i, w_a=w_a, w_uq=w_uq, w_ukv=w_ukv, gains=gains, w_gate=gates.astype(BF16),
        wo_a=wo_a.astype(BF16), wo_b=p["w_o_nbr"][i].astype(BF16), wo_c=p["w_o_mla"][i].astype(BF16),
        w_out=p["w_out"][i].astype(BF16),
        nbr_table=_nbr_bias_table(p["nbr_rel_bias"][i]))


def kernel(x_prompt, x_sample, cache_win_k, cache_win_v, cache_nbr_k, cache_nbr_v, cache_mla_ckv, cache_mla_krope, c, c_ctx, g_norm_mix, g_norm_ffn, w_ada, b_ada, w_in, g_qk_win, win_sink, g_qk_nbr, nbr_rel_bias, g_q_lora, g_kv_lora, w_uq, w_ukv, g_qk_mla, w_o_win, w_o_nbr, w_o_mla, w_out, w_router, b_router, w_exp_gate, w_exp_up, w_exp_down):
    p = dict(g_norm_mix=g_norm_mix, g_norm_ffn=g_norm_ffn, w_in=w_in, g_qk_win=g_qk_win,
             g_qk_nbr=g_qk_nbr, nbr_rel_bias=nbr_rel_bias, g_q_lora=g_q_lora, g_kv_lora=g_kv_lora,
             w_uq=w_uq, w_ukv=w_ukv, g_qk_mla=g_qk_mla, w_o_win=w_o_win, w_o_nbr=w_o_nbr,
             w_o_mla=w_o_mla, w_out=w_out, w_exp_gate=w_exp_gate, w_exp_up=w_exp_up,
             w_exp_down=w_exp_down)
    depth = w_in.shape[0]
    batch, seq, _ = x_prompt.shape
    dec_batch, dec_seq, _ = x_sample.shape
    past = cache_win_k.shape[2]

    n_c = 1 + dec_batch
    c_rows = -(-n_c // 8) * 8
    c_all = jnp.concatenate([c_ctx[None], c, jnp.zeros((c_rows - n_c, D_MODEL), F32)], axis=0)
    mods = _ada(c_all, w_ada, b_ada).reshape(depth, c_rows, 6, D_MODEL)
    mods = jnp.pad(mods, ((0, 0), (0, 0), (0, 2), (0, 0)))

    mats = (_block_ones(512, 64, 512, 64, 1.0 / HEAD_DIM), _block_ones(512, 64, 512, 64),
            _block_ones(256, 32, 512, 64), _block_ones(512, 64, 256, 32), _block_ones(256, 32, 256, 32))
    tabs = _rope_tables(dec_seq, 64) + _rope_tables(dec_seq, 32)
    sink = win_sink.astype(F32) * LOG2E
    w_r_t = w_router.T.astype(F32)
    b_r = b_router.astype(F32).reshape(N_EXPERTS, 1)
    layers = [_layer_params(i, p) for i in range(depth)]

    def ffn(merged, mod, rows_per_mod, lp):
        x1, route, h2p, counts = merged
        return _moe(lp["layer"], x1, route, h2p, counts, mod, rows_per_mod, w_exp_gate, w_exp_up,
                    w_exp_down)

    def merge(x, oa, ob, oc, mod, rows_per_mod, lp):
        return _merge(x, oa, ob, oc, mod, rows_per_mod, lp["gains"], lp["w_gate"], lp["wo_a"],
                      lp["wo_b"], lp["wo_c"], lp["w_out"], w_r_t, b_r)

    n_ctx = batch * seq
    x = x_prompt.reshape(n_ctx, D_MODEL)
    states = []
    for i, lp in enumerate(layers):
        mod = mods[i, 0:1]
        outs = _inproj(x, mod, n_ctx, lp["gains"], lp["w_a"], lp["w_uq"], lp["w_ukv"], mats, None,
                       seq, True)
        oa, ob, oc = _ctx_attn(i, sink, seq, *outs[:9])
        states.append(outs[9:])
        x = ffn(merge(x, oa, ob, oc, mod, n_ctx, lp), mod, n_ctx, lp)
    y_prompt = x.reshape(batch, seq, D_MODEL)

    n_lat = dec_batch * dec_seq
    x = x_sample.reshape(n_lat, D_MODEL)
    cwk = cache_win_k.reshape(dec_batch, depth, past, 128).astype(BF16)
    cwv = cache_win_v.reshape(dec_batch, depth, past, 128).astype(BF16)
    cnk = cache_nbr_k.reshape(dec_batch, depth, past, 512).astype(BF16)
    cnv = cache_nbr_v.reshape(dec_batch, depth, past, 512).astype(BF16)
    for i, lp in enumerate(layers):
        mod = mods[i, 1:1 + dec_batch]
        qa, ka, va, qb, kb, vb, qc, kc, vc = _inproj(
            x, mod, dec_seq, lp["gains"], lp["w_a"], lp["w_uq"], lp["w_ukv"], mats, tabs, dec_seq, False)
        kr_t = jnp.tile(cache_mla_krope[:, i].reshape(dec_batch * past, ROPE_DIM), (1, MLA_HEADS))
        kc_c, vc_c = _mla_cache_keys(cache_mla_ckv[:, i].reshape(dec_batch * past, KV_LORA), kr_t,
                                     lp["gains"], lp["w_ukv"], mats[1], mats[3], mats[4])
        r3 = lambda a: a.reshape(dec_batch, dec_seq, a.shape[-1])
        oa = _win_attn(i, sink, r3(qa), r3(ka), r3(va), cwk, cwv)
        ob = _nbr_attn(i, r3(qb), r3(kb), r3(vb), cnk, cnv, lp["nbr_table"])
        oc = _mla_attn(r3(qc), r3(kc), r3(vc), kc_c.reshape(dec_batch, past, 1024),
                       vc_c.reshape(dec_batch, past, 512))
        flat = lambda a: a.reshape(n_lat, 512)
        x = ffn(merge(x, flat(oa), flat(ob), flat(oc), mod, dec_seq, lp), mod, dec_seq, lp)
    y_sample = x.reshape(dec_batch, dec_seq, D_MODEL)

    def stack(k, shape):
        return jnp.stack([s[k].reshape((batch, seq) + shape) for s in states], axis=1)

    return (y_prompt, y_sample,
            stack(0, (WIN_KV_HEADS, HEAD_DIM)), stack(1, (WIN_KV_HEADS, HEAD_DIM)),
            stack(2, (NBR_HEADS, HEAD_DIM)), stack(3, (NBR_HEADS, HEAD_DIM)),
            stack(4, (KV_LORA,)), stack(5, (ROPE_DIM,)))
```

```python
import functools

import numpy as np
import jax
import jax.numpy as jnp
from jax import lax
from jax.experimental import pallas as pl
from jax.experimental.pallas import tpu as pltpu
from jax.experimental.pallas import tpu_sc as plsc

D_MODEL = 1024
GRID_W = 64
HEAD_DIM = 64
WIN_HEADS = 8
WIN_KV_HEADS = 2
WINDOW = 128
NBR_HEADS = 8
NBR_ROWS = 8
NBR_COLS = 16
MLA_HEADS = 8
Q_LORA = 256
KV_LORA = 128
NOPE_DIM = 64
ROPE_DIM = 32
V_DIM = 64
QK_DIM = NOPE_DIM + ROPE_DIM
N_EXPERTS = 16
N_GROUPS = 4
EXPERTS_PER_GROUP = 4
D_FF = 512
ROPE_BASE = 10000.0
EPS = 1e-6

LANES = 128
LOG2E = 1.4426950408889634
NEG = -1e30
VMEM_LIMIT = 56 * 1024 * 1024

F32 = jnp.float32
BF16 = jnp.bfloat16

C_QA, C_KA, C_VA, C_QB, C_KB, C_VB, C_CQ, C_CKV, C_KR, C_END = (
    0, 512, 640, 768, 1280, 1792, 2304, 2560, 2688, 2944)
WIN_Q_ORDER = (0, 4, 1, 5, 2, 6, 3, 7)


def _cparams(sem):
    return pltpu.CompilerParams(dimension_semantics=sem, vmem_limit_bytes=VMEM_LIMIT)


def _dot(a, b):
    return jnp.dot(a, b, preferred_element_type=F32)


def _dot_nt(a, b):
    return lax.dot_general(a, b, (((1,), (1,)), ((), ())), preferred_element_type=F32)


def _split(x):
    hi = x.astype(BF16)
    lo = (x - hi.astype(F32)).astype(BF16)
    return hi, lo


def _gsum(x2, bmat):
    return _dot(x2.astype(BF16), bmat)


def _tile_lanes(t, width):
    reps = width // t.shape[-1]
    return t if reps == 1 else jnp.concatenate([t] * reps, axis=-1)


def _rotate(x, cos, sin_a, sin_b, half):
    w = x.shape[-1]
    up = pltpu.roll(x, w - half, 1)
    dn = pltpu.roll(x, half, 1)
    return (x * _tile_lanes(cos, w) + up * _tile_lanes(sin_a, w) + dn * _tile_lanes(sin_b, w))


def _norm_mod(x, gain, scale, shift):
    ms = jnp.mean(x * x, axis=-1, keepdims=True)
    return (x * lax.rsqrt(ms + EPS) * gain) * (1.0 + scale) + shift


def _ada_body(c_ref, w_ref, b_ref, o_ref):
    c = c_ref[...]
    a = c * (1.0 / (1.0 + jnp.exp(-c)))
    a_hi, a_lo = _split(a)
    w_hi, w_lo = _split(w_ref[0])
    o_ref[0] = _dot(a_hi, w_hi) + _dot(a_hi, w_lo) + _dot(a_lo, w_hi) + b_ref[0]


def _ada(c_all, w_ada, b_ada):
    depth = w_ada.shape[0]
    rows = c_all.shape[0]
    tn = 1536
    return pl.pallas_call(
        _ada_body,
        grid=(depth, 6 * D_MODEL // tn),
        in_specs=[pl.BlockSpec((rows, D_MODEL), lambda l, j: (0, 0)),
                  pl.BlockSpec((1, D_MODEL, tn), lambda l, j: (l, 0, j)),
                  pl.BlockSpec((1, 1, tn), lambda l, j: (l, 0, j))],
        out_specs=pl.BlockSpec((1, rows, tn), lambda l, j: (l, 0, j)),
        out_shape=jax.ShapeDtypeStruct((depth, rows, 6 * D_MODEL), F32),
        compiler_params=_cparams(("parallel", "parallel")),
        name="ada",
    )(c_all, w_ada, b_ada.reshape(depth, 1, 6 * D_MODEL))


def _mla_key_tail(ckvn_b, kr_t, g, wukv_ref, bnn, bnr, brr, rope_tabs, kc_ref, vc_ref):
    kv = _dot(ckvn_b, wukv_ref[...])
    kn = kv[:, 0:512]
    vc_ref[...] = kv[:, 512:1024].astype(BF16)
    kn2 = kn * kn
    kr2 = kr_t * kr_t
    kr_sum32 = _gsum(kr2, brr)
    ssn = (_gsum(kn2, bnn) + jnp.concatenate([kr_sum32, kr_sum32], axis=-1)) * (1.0 / QK_DIM)
    ssr = (_gsum(kn2, bnr) + kr_sum32) * (1.0 / QK_DIM)
    kn = kn * lax.rsqrt(ssn + EPS) * g[5:6, 0:512]
    kr = kr_t * lax.rsqrt(ssr + EPS) * g[5:6, 512:768]
    if rope_tabs is not None:
        kr = _rotate(kr, *rope_tabs, 8)
    for p in range(4):
        kc_ref[:, 256 * p:256 * p + 128] = kn[:, 128 * p:128 * p + 128].astype(BF16)
        q4 = 128 * (p // 2)
        kc_ref[:, 256 * p + 128:256 * p + 256] = kr[:, q4:q4 + 128].astype(BF16)


def _inproj_body(rope, states, *refs):
    (x_ref, mod_ref, g_ref, w_ref, wuq_ref, wukv_ref, b64_ref, bnn_ref, brn_ref, bnr_ref,
     brr_ref) = refs[:11]
    refs = refs[11:]
    if rope:
        tabs_w = tuple(r[...] for r in refs[0:3])
        tabs_m = tuple(r[...] for r in refs[3:6])
        refs = refs[6:]
    else:
        tabs_w = tabs_m = None
    qa_ref, ka_ref, va_ref, qb_ref, kb_ref, vb_ref, qc_ref, kc_ref, vc_ref = refs[:9]
    st = refs[9:]

    g = g_ref[...]
    mod = mod_ref[0]
    hb = _norm_mod(x_ref[...], g[0:1], mod[1:2], mod[0:1]).astype(BF16)

    def proj(a, b):
        return _dot(hb, w_ref[:, a:b])

    b64 = b64_ref[...]

    def head_norm(z, bmat, gain):
        return z * lax.rsqrt(_gsum(z * z, bmat) + EPS) * gain

    qa = head_norm(proj(C_QA, C_KA), b64, g[1:2, 0:512])
    ka = head_norm(proj(C_KA, C_VA), b64[0:128, 0:128], g[1:2, 512:640])
    va = proj(C_VA, C_QB)
    if states:
        st[0][...] = ka
        st[1][...] = va
    if rope:
        qa = _rotate(qa, *tabs_w, 16)
        ka = _rotate(ka, *tabs_w, 16)
    qa_ref[...] = qa.astype(BF16)
    ka_ref[...] = ka.astype(BF16)
    va_ref[...] = va.astype(BF16)

    qb = head_norm(proj(C_QB, C_KB), b64, g[2:3, 0:512])
    kb = head_norm(proj(C_KB, C_VB), b64, g[2:3, 512:1024])
    vb = proj(C_VB, C_CQ)
    if states:
        st[2][...] = kb
        st[3][...] = vb
    qb_ref[...] = qb.astype(BF16)
    kb_ref[...] = kb.astype(BF16)
    vb_ref[...] = vb.astype(BF16)

    cq = proj(C_CQ, C_CKV)
    cqn = cq * lax.rsqrt(jnp.mean(cq * cq, axis=-1, keepdims=True) + EPS) * g[3:4, 0:256]
    qq = _dot(cqn.astype(BF16), wuq_ref[...])
    qn, qr = qq[:, 0:512], qq[:, 512:768]
    qn2, qr2 = qn * qn, qr * qr
    bnn, brn, bnr, brr = bnn_ref[...], brn_ref[...], bnr_ref[...], brr_ref[...]
    ssn = (_gsum(qn2, bnn) + _gsum(qr2, brn)) * (1.0 / QK_DIM)
    ssr = (_gsum(qn2, bnr) + _gsum(qr2, brr)) * (1.0 / QK_DIM)
    qn = qn * lax.rsqrt(ssn + EPS) * g[4:5, 0:512]
    qr = qr * lax.rsqrt(ssr + EPS) * g[4:5, 512:768]
    if rope:
        qr = _rotate(qr, *tabs_m, 8)
    for p in range(4):
        qc_ref[:, 256 * p:256 * p + 128] = qn[:, 128 * p:128 * p + 128].astype(BF16)
        q4 = 128 * (p // 2)
        qc_ref[:, 256 * p + 128:256 * p + 256] = qr[:, q4:q4 + 128].astype(BF16)

    ckv = proj(C_CKV, C_KR)
    ckvn = ckv * lax.rsqrt(jnp.mean(ckv * ckv, axis=-1, keepdims=True) + EPS) * g[3:4, 256:384]
    kr_t = proj(C_KR, C_END)
    if states:
        st[4][...] = ckvn
        st[5][...] = kr_t[:, 0:ROPE_DIM]
    _mla_key_tail(ckvn.astype(BF16), kr_t, g, wukv_ref, bnn, bnr, brr, tabs_m, kc_ref, vc_ref)


def _const_spec(shape):
    nd = len(shape)
    return pl.BlockSpec(shape, lambda i, _nd=nd: (0,) * _nd, pipeline_mode=pl.Buffered(1))


def _inproj(x, mods, rows_per_mod, gains, w_a, w_uq, w_ukv, mats, rope_tabs, seq_len, states):
    n = x.shape[0]
    tm = min(1024, n)
    rope = rope_tabs is not None
    row = lambda w: pl.BlockSpec((tm, w), lambda i: (i, 0))
    in_specs = [row(D_MODEL),
                pl.BlockSpec((1, 8, D_MODEL), lambda i: ((i * tm) // rows_per_mod, 0, 0)),
                _const_spec(gains.shape), _const_spec(w_a.shape), _const_spec(w_uq.shape),
                _const_spec(w_ukv.shape)] + [_const_spec(m.shape) for m in mats]
    args = [x, mods, gains, w_a, w_uq, w_ukv, *mats]
    if rope:
        tiles_per_seq = seq_len // tm
        in_specs += [pl.BlockSpec((tm, LANES), lambda i: (i % tiles_per_seq, 0))] * 6
        args += list(rope_tabs)
    widths = [512, 128, 128, 512, 512, 512, 1024, 1024, 512]
    out_shape = [jax.ShapeDtypeStruct((n, w), BF16) for w in widths]
    out_specs = [row(w) for w in widths]
    if states:
        swidths = [128, 128, 512, 512, KV_LORA, ROPE_DIM]
        out_shape += [jax.ShapeDtypeStruct((n, w), F32) for w in swidths]
        out_specs += [row(w) for w in swidths]
    return pl.pallas_call(
        functools.partial(_inproj_body, rope, states),
        grid=(n // tm,), in_specs=in_specs, out_specs=out_specs, out_shape=out_shape,
        compiler_params=_cparams(("parallel",)),
        name="inproj_lat" if rope else "inproj_ctx",
    )(*args)


def _mla_cache_body(ckv_ref, kr_ref, g_ref, wukv_ref, bnn_ref, bnr_ref, brr_ref, kc_ref, vc_ref):
    _mla_key_tail(ckv_ref[...].astype(BF16), kr_ref[...], g_ref[...], wukv_ref, bnn_ref[...],
                  bnr_ref[...], brr_ref[...], None, kc_ref, vc_ref)


def _mla_cache_keys(ckv, kr_t, gains, w_ukv, bnn, bnr, brr):
    n = ckv.shape[0]
    tm = min(512, n)
    row = lambda w: pl.BlockSpec((tm, w), lambda i: (i, 0))
    return pl.pallas_call(
        _mla_cache_body,
        grid=(n // tm,),
        in_specs=[row(KV_LORA), row(256), _const_spec(gains.shape), _const_spec(w_ukv.shape),
                  _const_spec(bnn.shape), _const_spec(bnr.shape), _const_spec(brr.shape)],
        out_specs=[row(1024), row(512)],
        out_shape=[jax.ShapeDtypeStruct((n, 1024), BF16), jax.ShapeDtypeStruct((n, 512), BF16)],
        compiler_params=_cparams(("parallel",)),
        name="mla_cache_keys",
    )(ckv, kr_t, gains, w_ukv, bnn, bnr, brr)


def _lane_mask(width, ranges):
    lane = lax.broadcasted_iota(jnp.int32, (1, width), 1)
    m = None
    for lo, hi in ranges:
        c = (lane >= lo) & (lane < hi)
        m = c if m is None else (m | c)
    return jnp.where(m, 1.0, 0.0).astype(BF16)


def _stack_heads(q, mask0, mask1):
    return jnp.concatenate([q * mask0, q * mask1], axis=0)


def _lane_tiles(x):
    return [x[:, j:j + LANES] for j in range(0, x.shape[1], LANES)]


def _softmax_block(scores, sink=None):
    rows = scores[0].shape[0]
    mp = None
    for s in scores:
        for t in _lane_tiles(s):
            mp = t if mp is None else jnp.maximum(mp, t)
    base = sink if sink is not None else jnp.full((rows, LANES), NEG, F32)
    m = jnp.maximum(base, jnp.max(mp, axis=-1, keepdims=True))
    lp = None
    ps = []
    for s in scores:
        p = jnp.exp2(s - _tile_lanes(m, s.shape[1]))
        for t in _lane_tiles(p):
            lp = t if lp is None else lp + t
        ps.append(p.astype(BF16))
    if sink is not None:
        lane = lax.broadcasted_iota(jnp.int32, (rows, LANES), 1)
        lp = lp + jnp.where(lane == 0, jnp.exp2(sink - m), 0.0)
    p_all = ps[0] if len(ps) == 1 else jnp.concatenate(ps, axis=-1)
    return p_all, jnp.broadcast_to(jnp.sum(lp, axis=-1, keepdims=True), (rows, LANES))


def _softmax_pv(scores, values, sink=None):
    p_all, l = _softmax_block(scores, sink)
    v_all = values[0] if len(values) == 1 else jnp.concatenate(values, axis=0)
    return _dot(p_all, v_all) / l


def _merge_heads(o, tq):
    lane = lax.broadcasted_iota(jnp.int32, (tq, LANES), 1)
    return jnp.where(lane < HEAD_DIM, o[0:tq], o[tq:2 * tq])


def _mla_masks(p_mod2):
    lane = lax.broadcasted_iota(jnp.int32, (1, 256), 1)
    r0 = 128 + 32 * (2 * p_mod2)
    m0 = (lane < 64) | ((lane >= r0) & (lane < r0 + 32))
    m1 = ((lane >= 64) & (lane < 128)) | ((lane >= r0 + 32) & (lane < r0 + 64))
    return (jnp.where(m0, 1.0, 0.0).astype(BF16), jnp.where(m1, 1.0, 0.0).astype(BF16))


def _sink_col(sink_ref, layer, h0, h1, tq):
    row = lax.broadcasted_iota(jnp.int32, (2 * tq, LANES), 0)
    return jnp.where(row < tq, sink_ref[layer, h0], sink_ref[layer, h1])


def _ctx_attn_body(layer, sink_ref, qa_ref, ka_ref, va_ref, qb_ref, kb_ref, vb_ref, qc_ref, kc_ref,
                   vc_ref, oa_ref, ob_ref, oc_ref):
    tq = qa_ref.shape[0]
    lo = _lane_mask(LANES, [(0, 64)])
    hi = _lane_mask(LANES, [(64, 128)])
    ka, va = ka_ref[...], va_ref[...]
    for j in range(4):
        sl = slice(128 * j, 128 * j + 128)
        qs = _stack_heads(qa_ref[:, sl], lo, hi)
        sink = _sink_col(sink_ref, layer, j, 4 + j, tq)
        o = _softmax_pv([_dot_nt(qs, ka)], [va], sink)
        oa_ref[:, sl] = _merge_heads(o, tq).astype(BF16)

        qs = _stack_heads(qb_ref[:, sl], lo, hi)
        o = _softmax_pv([_dot_nt(qs, kb_ref[:, sl])], [vb_ref[:, sl]])
        ob_ref[:, sl] = _merge_heads(o, tq).astype(BF16)

        m0, m1 = _mla_masks(j % 2)
        s2 = slice(256 * j, 256 * j + 256)
        qs = _stack_heads(qc_ref[:, s2], m0, m1)
        o = _softmax_pv([_dot_nt(qs, kc_ref[:, s2])], [vc_ref[:, sl]])
        oc_ref[:, sl] = _merge_heads(o, tq).astype(BF16)


def _ctx_attn(layer, sink, seq, qa, ka, va, qb, kb, vb, qc, kc, vc):
    n = qa.shape[0]
    row = lambda w: pl.BlockSpec((seq, w), lambda b: (b, 0))
    ins = [qa, ka, va, qb, kb, vb, qc, kc, vc]
    return pl.pallas_call(
        functools.partial(_ctx_attn_body, layer),
        grid=(n // seq,),
        in_specs=[pl.BlockSpec(memory_space=pltpu.SMEM)] + [row(a.shape[1]) for a in ins],
        out_specs=[row(512)] * 3,
        out_shape=[jax.ShapeDtypeStruct((n, 512), BF16)] * 3,
        compiler_params=_cparams(("parallel",)),
        name="ctx_attn",
    )(sink, *ins)


WIN_ROW_BLOCK = 32


WIN_Q_BLOCK = 128
WIN_BLOCKS_PER_STEP = 4


def _win_body(layer, sink_ref, q_ref, k_ref, v_ref, kc_ref, vc_ref, o_ref, s_sc, p_sc, l_sc):
    tq = WIN_Q_BLOCK
    seq = k_ref.shape[1]
    kw = 3 * tq
    rb = WIN_ROW_BLOCK
    lo = _lane_mask(LANES, [(0, 64)])
    hi = _lane_mask(LANES, [(64, 128)])
    kc, vc = kc_ref[0, 0], vc_ref[0, 0]
    for u in range(q_ref.shape[1] // tq):
        i = pl.program_id(1) * (q_ref.shape[1] // tq) + u
        kstart = pl.multiple_of(jnp.clip((i - 1) * tq, 0, seq - kw), tq)
        k_all = jnp.concatenate([k_ref[0, pl.ds(kstart, kw), :], kc], axis=0)
        v_all = jnp.concatenate([v_ref[0, pl.ds(kstart, kw), :], vc], axis=0)
        q_pos = i * tq + lax.broadcasted_iota(jnp.int32, (tq, kw), 0)
        k_pos = kstart + lax.broadcasted_iota(jnp.int32, (tq, kw), 1)
        band = jnp.abs(q_pos - k_pos) <= WINDOW
        qs = jnp.concatenate(
            [_stack_heads(q_ref[0, u * tq:(u + 1) * tq, 128 * j:128 * j + 128], lo, hi) for j in range(4)],
            axis=0)
        s_sc[u] = _dot_nt(qs, k_all)
        for j in range(4):
            for r in range(2 * tq * j, 2 * tq * (j + 1), rb):
                head = j if r < 2 * tq * j + tq else 4 + j
                q0 = r % tq
                s_band = jnp.where(band[q0:q0 + rb], s_sc[u, r:r + rb, 0:kw], NEG)
                sink = jnp.full((rb, LANES), sink_ref[layer, head], F32)
                p, l = _softmax_block([s_band, s_sc[u, r:r + rb, kw:]], sink)
                p_sc[u, r:r + rb, :] = p
                l_sc[u, r:r + rb, :] = l
            rows = slice(2 * tq * j, 2 * tq * (j + 1))
            o = _dot(p_sc[u, rows, :], v_all) / l_sc[u, rows, :]
            o_ref[0, u * tq:(u + 1) * tq, 128 * j:128 * j + 128] = _merge_heads(o, tq).astype(BF16)


def _win_attn(layer, sink, q, k, v, kc, vc):
    b, seq, _ = q.shape
    nb = min(WIN_BLOCKS_PER_STEP, seq // WIN_Q_BLOCK)
    tq = nb * WIN_Q_BLOCK
    past = kc.shape[2]
    keys = 3 * WIN_Q_BLOCK + past
    return pl.pallas_call(
        functools.partial(_win_body, layer),
        grid=(b, seq // tq),
        in_specs=[pl.BlockSpec(memory_space=pltpu.SMEM),
                  pl.BlockSpec((1, tq, 512), lambda bi, i: (bi, i, 0)),
                  pl.BlockSpec((1, seq, 128), lambda bi, i: (bi, 0, 0)),
                  pl.BlockSpec((1, seq, 128), lambda bi, i: (bi, 0, 0)),
                  pl.BlockSpec((1, 1, past, 128), lambda bi, i: (bi, layer, 0, 0)),
                  pl.BlockSpec((1, 1, past, 128), lambda bi, i: (bi, layer, 0, 0))],
        out_specs=pl.BlockSpec((1, tq, 512), lambda bi, i: (bi, i, 0)),
        out_shape=jax.ShapeDtypeStruct((b, seq, 512), BF16),
        scratch_shapes=[pltpu.VMEM((nb, 8 * WIN_Q_BLOCK, keys), F32),
                        pltpu.VMEM((nb, 8 * WIN_Q_BLOCK, keys), BF16),
                        pltpu.VMEM((nb, 8 * WIN_Q_BLOCK, LANES), F32)],
        compiler_params=_cparams(("parallel", "arbitrary")),
        name="win_attn",
    )(sink, q, k, v, kc, vc)


NBR_TILE_ROWS = 4
NBR_WIN_ROWS = NBR_TILE_ROWS + NBR_ROWS
NBR_TAB_PAD = NBR_WIN_ROWS - NBR_ROWS
NBR_ROW_BLOCK = 32
NBR_TILES_PER_STEP = 2


def _nbr_body(rows, q_ref, k_ref, v_ref, kc_ref, vc_ref, tab_ref, o_ref, s_sc, p_sc, l_sc):
    tq = NBR_TILE_ROWS * GRID_W
    kw = NBR_WIN_ROWS * GRID_W
    lo = _lane_mask(LANES, [(0, 64)])
    hi = _lane_mask(LANES, [(64, 128)])
    past = kc_ref.shape[2]
    rb = NBR_ROW_BLOCK
    tiles = q_ref.shape[1] // tq
    for u in range(tiles):
        r0 = NBR_TILE_ROWS * (pl.program_id(1) * tiles + u)
        ws = jnp.clip(r0 - NBR_ROWS // 2, 0, rows - NBR_WIN_ROWS)
        kstart = pl.multiple_of(ws * GRID_W, LANES)
        k_row = ws + lax.broadcasted_iota(jnp.int32, (1, kw), 1) // GRID_W
        for j in range(4):
            sl = slice(128 * j, 128 * j + 128)
            qs = _stack_heads(q_ref[0, u * tq:(u + 1) * tq, sl], lo, hi)
            s_sc[u, j, :, 0:kw] = _dot_nt(qs, k_ref[0, pl.ds(kstart, kw), sl])
            s_sc[u, j, :, kw:kw + past] = _dot_nt(qs, kc_ref[0, 0, :, sl])
        for j in range(4):
            sl = slice(128 * j, 128 * j + 128)
            for b0 in range(0, 2 * tq, rb):
                h = 2 * j + b0 // tq
                ql, sub = divmod(b0 % tq, GRID_W)
                d0 = ws - r0 - ql + (NBR_ROWS - 1) + NBR_TAB_PAD
                bias = jnp.concatenate([tab_ref[h, d0 + 2 * m, sub:sub + rb, :]
                                        for m in range(NBR_WIN_ROWS // 2)], axis=-1)
                rs = jnp.clip(r0 + ql - NBR_ROWS // 2, 0, rows - NBR_ROWS)
                valid = (k_row >= rs) & (k_row < rs + NBR_ROWS)
                s_nb = jnp.where(valid, s_sc[u, j, b0:b0 + rb, 0:kw] + bias, NEG)
                p, l = _softmax_block([s_nb, s_sc[u, j, b0:b0 + rb, kw:kw + past]])
                p_sc[u, j, b0:b0 + rb, :] = p
                l_sc[u, j, b0:b0 + rb, :] = l
            v_all = jnp.concatenate([v_ref[0, pl.ds(kstart, kw), sl], vc_ref[0, 0, :, sl]], axis=0)
            o = _dot(p_sc[u, j], v_all) / l_sc[u, j]
            o_ref[0, u * tq:(u + 1) * tq, sl] = _merge_heads(o, tq).astype(BF16)


def _nbr_attn(layer, q, k, v, kc, vc, table):
    b, seq, _ = q.shape
    rows = seq // GRID_W
    tile = NBR_TILE_ROWS * GRID_W
    tq = NBR_TILES_PER_STEP * tile
    past = kc.shape[2]
    keys = NBR_WIN_ROWS * GRID_W + past
    once = pl.Buffered(1)
    return pl.pallas_call(
        functools.partial(_nbr_body, rows),
        grid=(b, seq // tq),
        in_specs=[pl.BlockSpec((1, tq, 512), lambda bi, i: (bi, i, 0)),
                  pl.BlockSpec((1, seq, 512), lambda bi, i: (bi, 0, 0), pipeline_mode=once),
                  pl.BlockSpec((1, seq, 512), lambda bi, i: (bi, 0, 0), pipeline_mode=once),
                  pl.BlockSpec((1, 1, past, 512), lambda bi, i: (bi, layer, 0, 0), pipeline_mode=once),
                  pl.BlockSpec((1, 1, past, 512), lambda bi, i: (bi, layer, 0, 0), pipeline_mode=once),
                  pl.BlockSpec(table.shape, lambda bi, i: (0, 0, 0, 0), pipeline_mode=once)],
        out_specs=pl.BlockSpec((1, tq, 512), lambda bi, i: (bi, i, 0)),
        out_shape=jax.ShapeDtypeStruct((b, seq, 512), BF16),
        scratch_shapes=[pltpu.VMEM((NBR_TILES_PER_STEP, 4, 2 * tile, keys), F32),
                        pltpu.VMEM((NBR_TILES_PER_STEP, 4, 2 * tile, keys), BF16),
                        pltpu.VMEM((NBR_TILES_PER_STEP, 4, 2 * tile, LANES), F32)],
        compiler_params=_cparams(("parallel", "arbitrary")),
        name="nbr_attn",
    )(q, k, v, kc, vc, table)


def _nbr_bias_table(rel_bias):
    col = np.arange(GRID_W)
    cs = np.clip(col - NBR_COLS // 2, 0, GRID_W - NBR_COLS)
    kc = np.arange(GRID_W)
    ok = (kc[None, :] >= cs[:, None]) & (kc[None, :] < cs[:, None] + NBR_COLS)
    dc = kc[None, :] - col[:, None] + (NBR_COLS - 1)
    pick = (dc[:, :, None] == np.arange(2 * NBR_COLS - 1)[None, None, :]) & ok[:, :, None]
    t = jnp.einsum("hdk,qck->hdqc", rel_bias.astype(F32) * LOG2E, jnp.asarray(pick, F32),
                   precision=lax.Precision.HIGHEST)
    t = jnp.where(jnp.asarray(ok)[None, None], t, NEG)
    t = jnp.pad(t, ((0, 0), (NBR_TAB_PAD, NBR_TAB_PAD), (0, 0), (0, 0)))
    return jnp.concatenate([t[:, :-1], t[:, 1:]], axis=-1)


MLA_Q_TILE = 1024
MLA_KEY_CHUNK = 512
MLA_ROW_BLOCK = 64


def _mla_body(q_ref, kl_ref, kc_ref, vl_ref, vc_ref, o_ref, qs_sc, m_sc, l_sc, acc_sc, s_sc, p_sc, a_sc):
    tq = q_ref.shape[1]
    rows = 2 * tq
    seq = kl_ref.shape[1]
    past = kc_ref.shape[1]
    tk = min(MLA_KEY_CHUNK, seq)
    rb = MLA_ROW_BLOCK
    m0, m1 = _mla_masks(pl.program_id(1) % 2)
    qs_sc[...] = _stack_heads(q_ref[0], m0, m1)
    m_sc[...] = jnp.full(m_sc.shape, NEG, F32)
    l_sc[...] = jnp.zeros(l_sc.shape, F32)
    acc_sc[...] = jnp.zeros(acc_sc.shape, F32)

    def step(c, k, v):
        slab = c % 2
        n = k.shape[0]
        s_sc[slab, :, 0:n] = _dot_nt(qs_sc[...], k)
        for r in range(0, rows, rb):
            sl = slice(r, r + rb)
            sb = s_sc[slab, sl, 0:n]
            mp = None
            for t in _lane_tiles(sb):
                mp = t if mp is None else jnp.maximum(mp, t)
            m_prev = m_sc[sl]
            m_new = jnp.maximum(m_prev, jnp.max(mp, axis=-1, keepdims=True))
            alpha = jnp.exp2(m_prev - m_new)
            p = jnp.exp2(sb - _tile_lanes(m_new, n))
            psum = None
            for t in _lane_tiles(p):
                psum = t if psum is None else psum + t
            l_sc[sl] = alpha * l_sc[sl] + psum
            m_sc[sl] = m_new
            a_sc[slab, sl, :] = alpha
            p_sc[slab, sl, 0:n] = p.astype(BF16)
        acc_sc[...] = a_sc[slab] * acc_sc[...] + _dot(p_sc[slab, :, 0:n], v)

    chunks = [(kl_ref, vl_ref, o, min(tk, seq - o)) for o in range(0, seq, tk)]
    chunks += [(kc_ref, vc_ref, o, min(tk, past - o)) for o in range(0, past, tk)]
    for c, (k_ref, v_ref, o, n) in enumerate(chunks):
        step(c, k_ref[0, o:o + n, :], v_ref[0, o:o + n, :])
    l = jnp.sum(l_sc[...], axis=-1, keepdims=True)
    o_ref[0] = _merge_heads(acc_sc[...] / l, tq).astype(BF16)


def _mla_attn(q, kl, vl, kc, vc):
    b, seq, _ = q.shape
    past = kc.shape[1]
    tq = min(MLA_Q_TILE, seq)
    tk = min(MLA_KEY_CHUNK, seq)
    return pl.pallas_call(
        _mla_body,
        grid=(b, 4, seq // tq),
        in_specs=[pl.BlockSpec((1, tq, 256), lambda bi, p, qi: (bi, qi, p)),
                  pl.BlockSpec((1, seq, 256), lambda bi, p, qi: (bi, 0, p)),
                  pl.BlockSpec((1, past, 256), lambda bi, p, qi: (bi, 0, p)),
                  pl.BlockSpec((1, seq, 128), lambda bi, p, qi: (bi, 0, p)),
                  pl.BlockSpec((1, past, 128), lambda bi, p, qi: (bi, 0, p))],
        out_specs=pl.BlockSpec((1, tq, 128), lambda bi, p, qi: (bi, qi, p)),
        out_shape=jax.ShapeDtypeStruct((b, seq, 512), BF16),
        scratch_shapes=[pltpu.VMEM((2 * tq, 256), BF16), pltpu.VMEM((2 * tq, LANES), F32),
                        pltpu.VMEM((2 * tq, LANES), F32), pltpu.VMEM((2 * tq, LANES), F32),
                        pltpu.VMEM((2, 2 * tq, tk), F32), pltpu.VMEM((2, 2 * tq, tk), BF16),
                        pltpu.VMEM((2, 2 * tq, LANES), F32)],
        compiler_params=_cparams(("parallel", "parallel", "arbitrary")),
        name="mla_attn",
    )(q, kl, kc, vl, vc)


def _pack_pairs(x):
    w = x.shape[1] // 2
    hi = lax.bitcast_convert_type(x[:, :w].astype(BF16).astype(F32), jnp.int32)
    lo = lax.bitcast_convert_type(x[:, w:].astype(BF16).astype(F32), jnp.int32)
    return (hi & jnp.int32(-65536)) | lax.shift_right_logical(lo, jnp.int32(16))


def _unpack_pairs(p):
    hi = lax.bitcast_convert_type(p & jnp.int32(-65536), F32)
    lo = lax.bitcast_convert_type(lax.shift_left(p, jnp.int32(16)), F32)
    return jnp.concatenate([hi, lo], axis=-1)


def _merge_body(x_ref, oa_ref, ob_ref, oc_ref, mod_ref, g_ref, wg_ref, woa_ref, wob_ref, woc_ref,
                wout_ref, wr_ref, br_ref, tri_ref, x1_ref, route_ref, h2p_ref, count_ref, count_sc):
    x = x_ref[...]
    g = g_ref[...]
    mod = mod_ref[0]
    hb = _norm_mod(x, g[0:1], mod[1:2], mod[0:1]).astype(BF16)
    m = None
    for br, (o_ref, wo_ref) in enumerate(((oa_ref, woa_ref), (ob_ref, wob_ref), (oc_ref, woc_ref))):
        z = _dot(hb, wg_ref[:, D_MODEL * br:D_MODEL * (br + 1)])
        gate = 1.0 / (1.0 + jnp.exp(-z))
        t = gate * _dot(o_ref[...], wo_ref[...])
        m = t if m is None else m + t
    y = _dot(m.astype(BF16), wout_ref[...])
    x1 = x + mod[2:3] * y
    x1_ref[...] = x1

    h2 = _norm_mod(x1, g[6:7], mod[4:5], mod[3:4])
    h_hi, h_lo = _split(h2)
    w_hi, w_lo = _split(wr_ref[...])
    both_w = _dot_nt(jnp.concatenate([w_hi, w_lo], axis=0), h_hi)
    logits = both_w[0:N_EXPERTS] + both_w[N_EXPERTS:2 * N_EXPERTS] + _dot_nt(w_hi, h_lo)
    score = 1.0 / (1.0 + jnp.exp(-logits))
    sel = score + br_ref[...]
    sel_r = [sel[e:e + 1] for e in range(N_EXPERTS)]
    sc_r = [score[e:e + 1] for e in range(N_EXPERTS)]
    picked = []
    for e in range(N_EXPERTS):
        grp, a = divmod(e, EXPERTS_PER_GROUP)
        rank = None
        for bb in range(EXPERTS_PER_GROUP):
            if bb == a:
                continue
            o = sel_r[grp * EXPERTS_PER_GROUP + bb]
            beats = (o >= sel_r[e]) if bb < a else (o > sel_r[e])
            r = jnp.where(beats, 1.0, 0.0)
            rank = r if rank is None else rank + r
        picked.append(rank < 2.0)
    gscore = []
    for grp in range(N_GROUPS):
        tot = None
        for a in range(EXPERTS_PER_GROUP):
            e = grp * EXPERTS_PER_GROUP + a
            t = jnp.where(picked[e], sel_r[e], 0.0)
            tot = t if tot is None else tot + t
        gscore.append(tot)
    best = jnp.zeros_like(gscore[0])
    best_v = gscore[0]
    for grp in range(1, N_GROUPS):
        upd = gscore[grp] > best_v
        best = jnp.where(upd, float(grp), best)
        best_v = jnp.where(upd, gscore[grp], best_v)
    cw, pk = [], []
    for a in range(EXPERTS_PER_GROUP):
        tot = flag = None
        for grp in range(N_GROUPS):
            e = grp * EXPERTS_PER_GROUP + a
            f = (best == float(grp)) & picked[e]
            t = jnp.where(f, sc_r[e], 0.0)
            tot = t if tot is None else tot + t
            flag = f if flag is None else (flag | f)
        cw.append(tot)
        pk.append(flag)
    den = cw[0] + cw[1] + cw[2] + cw[3]
    first = jnp.where(pk[0], 0.0, jnp.where(pk[1], 1.0, jnp.where(pk[2], 2.0, 3.0)))
    second = jnp.where(pk[3], 3.0, jnp.where(pk[2], 2.0, jnp.where(pk[1], 1.0, 0.0)))
    slot_e, slot_w = [], []
    for which in (first, second):
        tot = None
        for a in range(EXPERTS_PER_GROUP):
            t = jnp.where(which == float(a), cw[a], 0.0)
            tot = t if tot is None else tot + t
        slot_w.append(tot / den)
        slot_e.append(best * float(EXPERTS_PER_GROUP) + which)

    @pl.when(pl.program_id(0) == 0)
    def _():
        count_sc[...] = jnp.zeros(count_sc.shape, F32)

    tm = x.shape[0]
    eid = lax.broadcasted_iota(jnp.int32, (N_EXPERTS, tm), 0).astype(F32)
    oh = [eid == slot_e[0], eid == slot_e[1]]
    both = jnp.where(oh[0] | oh[1], 1.0, 0.0)
    seen = count_sc[...][:, 0:1] + _dot(both.astype(BF16), tri_ref[...])
    for k in range(2):
        route_ref[k:k + 1, :] = slot_w[k]
        route_ref[2 + k:3 + k, :] = slot_e[k]
        route_ref[4 + k:5 + k, :] = jnp.sum(jnp.where(oh[k], seen, 0.0), axis=0, keepdims=True)
    route_ref[6:8, :] = jnp.zeros((2, tm), F32)
    count_sc[...] = count_sc[...] + jnp.sum(both, axis=-1, keepdims=True)
    count_ref[...] = count_sc[...]
    h2p_ref[...] = _pack_pairs(h2)


def _merge(x, oa, ob, oc, mods, rows_per_mod, gains, w_gate, wo_a, wo_b, wo_c, w_out, w_r_t, b_r):
    n = x.shape[0]
    tm = min(1024, n)
    row = lambda w: pl.BlockSpec((tm, w), lambda i: (i, 0))
    tri = jnp.asarray(np.triu(np.ones((tm, tm), np.float32), 1), BF16)
    consts = [gains, w_gate, wo_a, wo_b, wo_c, w_out, w_r_t, b_r, tri]
    return pl.pallas_call(
        _merge_body,
        grid=(n // tm,),
        in_specs=[row(D_MODEL), row(512), row(512), row(512),
                  pl.BlockSpec((1, 8, D_MODEL), lambda i: ((i * tm) // rows_per_mod, 0, 0))]
                 + [_const_spec(c.shape) for c in consts],
        out_specs=[row(D_MODEL), pl.BlockSpec((8, tm), lambda i: (0, i)), row(512),
                   pl.BlockSpec((N_EXPERTS, LANES), lambda i: (0, 0))],
        out_shape=[jax.ShapeDtypeStruct((n, D_MODEL), F32), jax.ShapeDtypeStruct((8, n), F32),
                   jax.ShapeDtypeStruct((n, 512), jnp.int32),
                   jax.ShapeDtypeStruct((N_EXPERTS, LANES), F32)],
        scratch_shapes=[pltpu.VMEM((N_EXPERTS, LANES), F32)],
        compiler_params=_cparams(("arbitrary",)),
        name="merge",
    )(x, oa, ob, oc, mods, *consts)


EXPERT_TILE = 512
SC_CORES = 2
SC_SUBCORES = 16
SC_WORKERS = SC_CORES * SC_SUBCORES
SC_WINDOW = 128


def _sc_mesh():
    return plsc.VectorSubcoreMesh(core_axis_name="c", subcore_axis_name="s", num_cores=SC_CORES,
                                  num_subcores=SC_SUBCORES)


def _sc_window_base(steps, j):
    wid = lax.axis_index("s") * SC_CORES + lax.axis_index("c")
    return pl.multiple_of((wid * steps + j) * SC_WINDOW, SC_WINDOW)


def _sc_dispatch(groups, n_out):
    w = groups[0][0].shape[1]
    dtype = groups[0][0].dtype
    steps = [g[0].shape[0] // (SC_WORKERS * SC_WINDOW) for g in groups]

    @functools.partial(
        pl.kernel, out_type=jax.ShapeDtypeStruct((n_out, w), dtype), mesh=_sc_mesh(),
        scratch_types=[pltpu.VMEM((SC_WINDOW,), jnp.int32), pltpu.VMEM((SC_WINDOW,), jnp.int32),
                       pltpu.VMEM((SC_WINDOW, w), dtype)],
        name="moe_dispatch")
    def run(*refs):
        o_hbm, i0_v, i1_v, rows_v = refs[3 * len(groups):]
        for g, n_steps in enumerate(steps):
            x_hbm, i0_hbm, i1_hbm = refs[3 * g:3 * g + 3]

            @pl.loop(0, n_steps)
            def _(j, x_hbm=x_hbm, i0_hbm=i0_hbm, i1_hbm=i1_hbm, n_steps=n_steps):
                base = _sc_window_base(n_steps, j)
                pltpu.sync_copy(i0_hbm.at[pl.ds(base, SC_WINDOW)], i0_v)
                pltpu.sync_copy(i1_hbm.at[pl.ds(base, SC_WINDOW)], i1_v)
                pltpu.sync_copy(x_hbm.at[pl.ds(base, SC_WINDOW)], rows_v)
                pltpu.sync_copy(rows_v, o_hbm.at[i0_v])
                pltpu.sync_copy(rows_v, o_hbm.at[i1_v])

    return run(*[a for g in groups for a in g])


def _sc_collect(rows, positions):
    w = rows.shape[1]
    steps = [p0.shape[0] // (SC_WORKERS * SC_WINDOW) for p0, _ in positions]
    outs = [jax.ShapeDtypeStruct((p.shape[0], w), rows.dtype) for pair in positions for p in pair]
    n_idx = 2 * len(positions)

    @functools.partial(
        pl.kernel, out_type=outs, mesh=_sc_mesh(),
        scratch_types=[pltpu.VMEM((SC_WINDOW,), jnp.int32), pltpu.VMEM((SC_WINDOW, w), rows.dtype)],
        name="moe_collect")
    def run(y_hbm, *refs):
        i_hbms, o_hbms = refs[:n_idx], refs[n_idx:2 * n_idx]
        i_v, rows_v = refs[2 * n_idx:]
        for g, n_steps in enumerate(steps):
            @pl.loop(0, n_steps)
            def _(j, g=g, n_steps=n_steps):
                base = _sc_window_base(n_steps, j)
                for k in (2 * g, 2 * g + 1):
                    pltpu.sync_copy(i_hbms[k].at[pl.ds(base, SC_WINDOW)], i_v)
                    pltpu.sync_copy(y_hbm.at[i_v], rows_v)
                    pltpu.sync_copy(rows_v, o_hbms[k].at[pl.ds(base, SC_WINDOW)])

    flat = run(rows, *[p for pair in positions for p in pair])
    return [(flat[2 * g], flat[2 * g + 1]) for g in range(len(positions))]


def _experts_body(te_ref, nv_ref, xs_ref, wg_ref, wu_ref, wd_ref, ys_ref, wg_sc, wu_sc, wd_sc):
    j = pl.program_id(0)

    @pl.when((j == 0) | (te_ref[j] != te_ref[jnp.maximum(j - 1, 0)]))
    def _():
        wg_sc[...] = wg_ref[0, 0].astype(BF16)
        wu_sc[...] = wu_ref[0, 0].astype(BF16)
        wd_sc[...] = wd_ref[0, 0].astype(BF16)

    @pl.when(j < nv_ref[0])
    def _():
        x = _unpack_pairs(xs_ref[...]).astype(BF16)
        zg = _dot(x, wg_sc[...])
        act = zg * (1.0 / (1.0 + jnp.exp(-zg))) * _dot(x, wu_sc[...])
        ys_ref[...] = _pack_pairs(_dot(act.astype(BF16), wd_sc[...]))

    @pl.when(j >= nv_ref[0])
    def _():
        ys_ref[...] = jnp.zeros(ys_ref.shape, ys_ref.dtype)


def _experts(layer, xs, tile_expert, n_valid, wg, wu, wd):
    p = xs.shape[0]
    wspec = lambda r, c: pl.BlockSpec((1, 1, r, c), lambda j, te, nv: (layer, te[j], 0, 0))
    grid_spec = pltpu.PrefetchScalarGridSpec(
        num_scalar_prefetch=2,
        grid=(p // EXPERT_TILE,),
        in_specs=[pl.BlockSpec((EXPERT_TILE, 512), lambda j, te, nv: (j, 0)),
                  wspec(D_MODEL, D_FF), wspec(D_MODEL, D_FF), wspec(D_FF, D_MODEL)],
        out_specs=pl.BlockSpec((EXPERT_TILE, 512), lambda j, te, nv: (j, 0)),
        scratch_shapes=[pltpu.VMEM((D_MODEL, D_FF), BF16), pltpu.VMEM((D_MODEL, D_FF), BF16),
                        pltpu.VMEM((D_FF, D_MODEL), BF16)])
    return pl.pallas_call(
        _experts_body, grid_spec=grid_spec,
        out_shape=jax.ShapeDtypeStruct((p, 512), jnp.int32),
        compiler_params=_cparams(("arbitrary",)),
        name="experts",
    )(tile_expert, n_valid, xs, wg, wu, wd)


def _combine_body(x1_ref, y0_ref, y1_ref, route_ref, mod_ref, o_ref):
    tm = x1_ref.shape[0]
    eye = jnp.where(lax.broadcasted_iota(jnp.int32, (tm, tm), 0)
                    == lax.broadcasted_iota(jnp.int32, (tm, tm), 1), 1.0, 0.0).astype(BF16)
    r_hi, r_lo = _split(route_ref[...])
    wcol = _dot_nt(eye, r_hi) + _dot_nt(eye, r_lo)
    moe = wcol[:, 0:1] * _unpack_pairs(y0_ref[...]) + wcol[:, 1:2] * _unpack_pairs(y1_ref[...])
    o_ref[...] = x1_ref[...] + mod_ref[0][5:6] * moe


def _combine(x1, y0, y1, route, mods, rows_per_mod):
    n = x1.shape[0]
    tm = min(512, n)
    row = lambda w: pl.BlockSpec((tm, w), lambda i: (i, 0))
    return pl.pallas_call(
        _combine_body,
        grid=(n // tm,),
        in_specs=[row(D_MODEL), row(512), row(512), pl.BlockSpec((8, tm), lambda i: (0, i)),
                  pl.BlockSpec((1, 8, D_MODEL), lambda i: ((i * tm) // rows_per_mod, 0, 0))],
        out_specs=row(D_MODEL),
        out_shape=jax.ShapeDtypeStruct((n, D_MODEL), F32),
        compiler_params=_cparams(("parallel",)),
        name="combine",
    )(x1, y0, y1, route, mods)


def _moe(layer, groups, wg, wu, wd):
    n_slots = sum(2 * g[0].shape[0] for g in groups)
    p = n_slots + N_EXPERTS * EXPERT_TILE
    cnts = [g[3][:, 0].astype(jnp.int32) for g in groups]
    total = sum(cnts)
    padded = (total + EXPERT_TILE - 1) // EXPERT_TILE * EXPERT_TILE
    seg_end = jnp.cumsum(padded)
    experts = jnp.arange(N_EXPERTS, dtype=jnp.int32)

    positions = []
    start = seg_end - padded
    for (_, route, _, _, _, _), cnt in zip(groups, cnts):
        pos = []
        for k in range(2):
            e = route[2 + k].astype(jnp.int32)
            off = jnp.sum(jnp.where(e[:, None] == experts[None], start[None], 0), axis=1)
            pos.append(off + route[4 + k].astype(jnp.int32))
        positions.append(tuple(pos))
        start = start + cnt
    tile_start = jnp.arange(p // EXPERT_TILE, dtype=jnp.int32) * EXPERT_TILE
    tile_expert = jnp.sum(tile_start[:, None] >= seg_end[None], axis=1).astype(jnp.int32)
    tile_expert = jnp.minimum(tile_expert, N_EXPERTS - 1)
    n_valid = (seg_end[-1:] // EXPERT_TILE).astype(jnp.int32)

    xs = _sc_dispatch([(g[2], pos[0], pos[1]) for g, pos in zip(groups, positions)], p)
    ys = _experts(layer, xs, tile_expert, n_valid, wg, wu, wd)
    collected = _sc_collect(ys, positions)
    return [_combine(x1, y0, y1, route, mods, rows_per_mod)
            for (x1, route, _, _, mods, rows_per_mod), (y0, y1) in zip(groups, collected)]


def _block_ones(n_in, g_in, n_out, g_out, value=1.0):
    r = np.arange(n_in)[:, None] // g_in
    c = np.arange(n_out)[None, :] // g_out
    return jnp.asarray(np.where(r == c, value, 0.0), dtype=BF16)


def _rope_tables(seq, head_w):
    pos = np.arange(seq)
    rows, cols = pos // GRID_W, pos % GRID_W
    a = head_w // 2
    half = a // 2
    freqs = (ROPE_BASE ** (-np.arange(half, dtype=np.float32) / half)).astype(np.float32)
    lane = np.arange(LANES) % head_w
    within = lane % a
    first = within < half
    p = np.where((lane // a == 0)[None, :], rows[:, None], cols[:, None]).astype(np.float32)
    ang = (p * freqs[within % half][None, :]).astype(np.float32)
    cos, sin = np.cos(ang), np.sin(ang)
    return (jnp.asarray(cos, F32), jnp.asarray(np.where(first[None], -sin, 0.0), F32),
            jnp.asarray(np.where(first[None], 0.0, sin), F32))


def _tile_to(v, width):
    return jnp.tile(v, width // v.shape[0])


def _layer_params(i, p):
    w_in = p["w_in"][i]
    sp = np.cumsum((512, 128, 128, 512, 512, 512, Q_LORA, KV_LORA, ROPE_DIM))
    qa, ka, va, qb, kb, vb, cq, ckv, kr, gates = jnp.split(w_in, [int(s) for s in sp], axis=1)
    qa = qa.reshape(D_MODEL, WIN_HEADS, HEAD_DIM)[:, WIN_Q_ORDER, :].reshape(D_MODEL, 512)
    w_a = jnp.concatenate([qa, ka, va, qb, kb, vb, cq, ckv, jnp.tile(kr, (1, MLA_HEADS))],
                          axis=1).astype(BF16)
    w_uq = p["w_uq"][i].reshape(Q_LORA, MLA_HEADS, QK_DIM)
    w_uq = jnp.concatenate([w_uq[:, :, :NOPE_DIM].reshape(Q_LORA, 512),
                            w_uq[:, :, NOPE_DIM:].reshape(Q_LORA, 256)], axis=1).astype(BF16)
    w_ukv = p["w_ukv"][i].reshape(KV_LORA, MLA_HEADS, NOPE_DIM + V_DIM)
    w_ukv = jnp.concatenate([w_ukv[:, :, :NOPE_DIM].reshape(KV_LORA, 512),
                             w_ukv[:, :, NOPE_DIM:].reshape(KV_LORA, 512)], axis=1).astype(BF16)
    z = jnp.zeros((D_MODEL,), F32)
    row = lambda *parts: jnp.concatenate(list(parts) + [z])[:D_MODEL]
    q_scale = HEAD_DIM ** -0.5 * LOG2E
    c_scale = QK_DIM ** -0.5 * LOG2E
    g_mla = p["g_qk_mla"][i]
    gains = jnp.stack([
        p["g_norm_mix"][i],
        row(_tile_to(p["g_qk_win"][i, 0], 512) * q_scale, _tile_to(p["g_qk_win"][i, 1], 128)),
        row(_tile_to(p["g_qk_nbr"][i, 0], 512) * q_scale, _tile_to(p["g_qk_nbr"][i, 1], 512)),
        row(p["g_q_lora"][i], p["g_kv_lora"][i]),
        row(_tile_to(g_mla[0, :NOPE_DIM], 512) * c_scale, _tile_to(g_mla[0, NOPE_DIM:], 256) * c_scale),
        row(_tile_to(g_mla[1, :NOPE_DIM], 512), _tile_to(g_mla[1, NOPE_DIM:], 256)),
        p["g_norm_ffn"][i],
        z]).astype(F32)
    wo_a = p["w_o_win"][i].reshape(WIN_HEADS, HEAD_DIM, D_MODEL)[WIN_Q_ORDER, :, :].reshape(512, D_MODEL)
    return dict(
        w_a=w_a, w_uq=w_uq, w_ukv=w_ukv, gains=gains, w_gate=gates.astype(BF16),
        wo_a=wo_a.astype(BF16), wo_b=p["w_o_nbr"][i].astype(BF16), wo_c=p["w_o_mla"][i].astype(BF16),
        w_out=p["w_out"][i].astype(BF16),
        nbr_table=_nbr_bias_table(p["nbr_rel_bias"][i]))


def kernel(x_prompt, x_sample, cache_win_k, cache_win_v, cache_nbr_k, cache_nbr_v, cache_mla_ckv, cache_mla_krope, c, c_ctx, g_norm_mix, g_norm_ffn, w_ada, b_ada, w_in, g_qk_win, win_sink, g_qk_nbr, nbr_rel_bias, g_q_lora, g_kv_lora, w_uq, w_ukv, g_qk_mla, w_o_win, w_o_nbr, w_o_mla, w_out, w_router, b_router, w_exp_gate, w_exp_up, w_exp_down):
    p = dict(g_norm_mix=g_norm_mix, g_norm_ffn=g_norm_ffn, w_in=w_in, g_qk_win=g_qk_win,
             g_qk_nbr=g_qk_nbr, nbr_rel_bias=nbr_rel_bias, g_q_lora=g_q_lora, g_kv_lora=g_kv_lora,
             w_uq=w_uq, w_ukv=w_ukv, g_qk_mla=g_qk_mla, w_o_win=w_o_win, w_o_nbr=w_o_nbr,
             w_o_mla=w_o_mla, w_out=w_out, w_exp_gate=w_exp_gate, w_exp_up=w_exp_up,
             w_exp_down=w_exp_down)
    depth = w_in.shape[0]
    batch, seq, _ = x_prompt.shape
    dec_batch, dec_seq, _ = x_sample.shape
    past = cache_win_k.shape[2]

    n_c = 1 + dec_batch
    c_rows = -(-n_c // 8) * 8
    c_all = jnp.concatenate([c_ctx[None], c, jnp.zeros((c_rows - n_c, D_MODEL), F32)], axis=0)
    mods = _ada(c_all, w_ada, b_ada).reshape(depth, c_rows, 6, D_MODEL)
    mods = jnp.pad(mods, ((0, 0), (0, 0), (0, 2), (0, 0)))

    mats = (_block_ones(512, 64, 512, 64, 1.0 / HEAD_DIM), _block_ones(512, 64, 512, 64),
            _block_ones(256, 32, 512, 64), _block_ones(512, 64, 256, 32), _block_ones(256, 32, 256, 32))
    tabs = _rope_tables(dec_seq, 64) + _rope_tables(dec_seq, 32)
    sink = win_sink.astype(F32) * LOG2E
    w_r_t = w_router.T.astype(F32)
    b_r = b_router.astype(F32).reshape(N_EXPERTS, 1)
    layers = [_layer_params(i, p) for i in range(depth)]

    def merge(x, oa, ob, oc, mod, rows_per_mod, lp):
        return _merge(x, oa, ob, oc, mod, rows_per_mod, lp["gains"], lp["w_gate"], lp["wo_a"],
                      lp["wo_b"], lp["wo_c"], lp["w_out"], w_r_t, b_r)

    n_ctx = batch * seq
    n_lat = dec_batch * dec_seq
    x_ctx = x_prompt.reshape(n_ctx, D_MODEL)
    x_lat = x_sample.reshape(n_lat, D_MODEL)
    cwk = cache_win_k.reshape(dec_batch, depth, past, 128).astype(BF16)
    cwv = cache_win_v.reshape(dec_batch, depth, past, 128).astype(BF16)
    cnk = cache_nbr_k.reshape(dec_batch, depth, past, 512).astype(BF16)
    cnv = cache_nbr_v.reshape(dec_batch, depth, past, 512).astype(BF16)
    states = []
    for i, lp in enumerate(layers):
        mod_c = mods[i, 0:1]
        outs = _inproj(x_ctx, mod_c, n_ctx, lp["gains"], lp["w_a"], lp["w_uq"], lp["w_ukv"], mats, None,
                       seq, True)
        oa, ob, oc = _ctx_attn(i, sink, seq, *outs[:9])
        states.append(outs[9:])
        merged_c = merge(x_ctx, oa, ob, oc, mod_c, n_ctx, lp)

        mod_l = mods[i, 1:1 + dec_batch]
        qa, ka, va, qb, kb, vb, qc, kc, vc = _inproj(
            x_lat, mod_l, dec_seq, lp["gains"], lp["w_a"], lp["w_uq"], lp["w_ukv"], mats, tabs, dec_seq,
            False)
        kr_t = jnp.tile(cache_mla_krope[:, i].reshape(dec_batch * past, ROPE_DIM), (1, MLA_HEADS))
        kc_c, vc_c = _mla_cache_keys(cache_mla_ckv[:, i].reshape(dec_batch * past, KV_LORA), kr_t,
                                     lp["gains"], lp["w_ukv"], mats[1], mats[3], mats[4])
        r3 = lambda a: a.reshape(dec_batch, dec_seq, a.shape[-1])
        oa = _win_attn(i, sink, r3(qa), r3(ka), r3(va), cwk, cwv)
        ob = _nbr_attn(i, r3(qb), r3(kb), r3(vb), cnk, cnv, lp["nbr_table"])
        oc = _mla_attn(r3(qc), r3(kc), r3(vc), kc_c.reshape(dec_batch, past, 1024),
                       vc_c.reshape(dec_batch, past, 512))
        flat = lambda a: a.reshape(n_lat, 512)
        merged_l = merge(x_lat, flat(oa), flat(ob), flat(oc), mod_l, dec_seq, lp)

        x_ctx, x_lat = _moe(i, [tuple(merged_c) + (mod_c, n_ctx), tuple(merged_l) + (mod_l, dec_seq)],
                            w_exp_gate, w_exp_up, w_exp_down)
    y_prompt = x_ctx.reshape(batch, seq, D_MODEL)
    y_sample = x_lat.reshape(dec_batch, dec_seq, D_MODEL)

    def stack(k, shape):
        return jnp.stack([s[k].reshape((batch, seq) + shape) for s in states], axis=1)

    return (y_prompt, y_sample,
            stack(0, (WIN_KV_HEADS, HEAD_DIM)), stack(1, (WIN_KV_HEADS, HEAD_DIM)),
            stack(2, (NBR_HEADS, HEAD_DIM)), stack(3, (NBR_HEADS, HEAD_DIM)),
            stack(4, (KV_LORA,)), stack(5, (ROPE_DIM,)))
```

```python
import functools

import numpy as np
import jax
import jax.numpy as jnp
from jax import lax
from jax.experimental import pallas as pl
from jax.experimental.pallas import tpu as pltpu
from jax.experimental.pallas import tpu_sc as plsc

D_MODEL = 1024
GRID_W = 64
HEAD_DIM = 64
WIN_HEADS = 8
WIN_KV_HEADS = 2
WINDOW = 128
NBR_HEADS = 8
NBR_ROWS = 8
NBR_COLS = 16
MLA_HEADS = 8
Q_LORA = 256
KV_LORA = 128
NOPE_DIM = 64
ROPE_DIM = 32
V_DIM = 64
QK_DIM = NOPE_DIM + ROPE_DIM
N_EXPERTS = 16
N_GROUPS = 4
EXPERTS_PER_GROUP = 4
D_FF = 512
ROPE_BASE = 10000.0
EPS = 1e-6

LANES = 128
LOG2E = 1.4426950408889634
NEG = -1e30
VMEM_LIMIT = 56 * 1024 * 1024

F32 = jnp.float32
BF16 = jnp.bfloat16

C_QA, C_KA, C_VA, C_QB, C_KB, C_VB, C_CQ, C_CKV, C_KR, C_END = (
    0, 512, 640, 768, 1280, 1792, 2304, 2560, 2688, 2944)
WIN_Q_ORDER = (0, 4, 1, 5, 2, 6, 3, 7)


def _cparams(sem):
    return pltpu.CompilerParams(dimension_semantics=sem, vmem_limit_bytes=VMEM_LIMIT)


def _dot(a, b):
    return jnp.dot(a, b, preferred_element_type=F32)


def _dot_nt(a, b):
    return lax.dot_general(a, b, (((1,), (1,)), ((), ())), preferred_element_type=F32)


def _split(x):
    hi = x.astype(BF16)
    lo = (x - hi.astype(F32)).astype(BF16)
    return hi, lo


def _gsum(x2, bmat):
    return _dot(x2.astype(BF16), bmat)


def _tile_lanes(t, width):
    reps = width // t.shape[-1]
    return t if reps == 1 else jnp.concatenate([t] * reps, axis=-1)


def _rotate(x, cos, sin_a, sin_b, half):
    w = x.shape[-1]
    up = pltpu.roll(x, w - half, 1)
    dn = pltpu.roll(x, half, 1)
    return (x * _tile_lanes(cos, w) + up * _tile_lanes(sin_a, w) + dn * _tile_lanes(sin_b, w))


def _norm_mod(x, gain, scale, shift):
    ms = jnp.mean(x * x, axis=-1, keepdims=True)
    return (x * lax.rsqrt(ms + EPS) * gain) * (1.0 + scale) + shift


def _ada_body(c_ref, w_ref, b_ref, o_ref):
    c = c_ref[...]
    a = c * (1.0 / (1.0 + jnp.exp(-c)))
    a_hi, a_lo = _split(a)
    w_hi, w_lo = _split(w_ref[0])
    o_ref[0] = _dot(a_hi, w_hi) + _dot(a_hi, w_lo) + _dot(a_lo, w_hi) + b_ref[0]


def _ada(c_all, w_ada, b_ada):
    depth = w_ada.shape[0]
    rows = c_all.shape[0]
    tn = 1536
    return pl.pallas_call(
        _ada_body,
        grid=(depth, 6 * D_MODEL // tn),
        in_specs=[pl.BlockSpec((rows, D_MODEL), lambda l, j: (0, 0)),
                  pl.BlockSpec((1, D_MODEL, tn), lambda l, j: (l, 0, j)),
                  pl.BlockSpec((1, 1, tn), lambda l, j: (l, 0, j))],
        out_specs=pl.BlockSpec((1, rows, tn), lambda l, j: (l, 0, j)),
        out_shape=jax.ShapeDtypeStruct((depth, rows, 6 * D_MODEL), F32),
        compiler_params=_cparams(("parallel", "parallel")),
        name="ada",
    )(c_all, w_ada, b_ada.reshape(depth, 1, 6 * D_MODEL))


def _mla_key_tail(ckvn_b, kr_t, g, wukv_ref, bnn, bnr, brr, rope_tabs, kc_ref, vc_ref):
    kv = _dot(ckvn_b, wukv_ref[...])
    kn = kv[:, 0:512]
    vc_ref[...] = kv[:, 512:1024].astype(BF16)
    kn2 = kn * kn
    kr2 = kr_t * kr_t
    kr_sum32 = _gsum(kr2, brr)
    ssn = (_gsum(kn2, bnn) + jnp.concatenate([kr_sum32, kr_sum32], axis=-1)) * (1.0 / QK_DIM)
    ssr = (_gsum(kn2, bnr) + kr_sum32) * (1.0 / QK_DIM)
    kn = kn * lax.rsqrt(ssn + EPS) * g[5:6, 0:512]
    kr = kr_t * lax.rsqrt(ssr + EPS) * g[5:6, 512:768]
    if rope_tabs is not None:
        kr = _rotate(kr, *rope_tabs, 8)
    for p in range(4):
        kc_ref[:, 256 * p:256 * p + 128] = kn[:, 128 * p:128 * p + 128].astype(BF16)
        q4 = 128 * (p // 2)
        kc_ref[:, 256 * p + 128:256 * p + 256] = kr[:, q4:q4 + 128].astype(BF16)


def _inproj_body(rope, states, *refs):
    (x_ref, mod_ref, g_ref, w_ref, wuq_ref, wukv_ref, b64_ref, bnn_ref, brn_ref, bnr_ref,
     brr_ref) = refs[:11]
    refs = refs[11:]
    if rope:
        tabs_w = tuple(r[...] for r in refs[0:3])
        tabs_m = tuple(r[...] for r in refs[3:6])
        refs = refs[6:]
    else:
        tabs_w = tabs_m = None
    qa_ref, ka_ref, va_ref, qb_ref, kb_ref, vb_ref, qc_ref, kc_ref, vc_ref = refs[:9]
    st = refs[9:]

    g = g_ref[...]
    mod = mod_ref[0]
    hb = _norm_mod(x_ref[...], g[0:1], mod[1:2], mod[0:1]).astype(BF16)

    def proj(a, b):
        return _dot(hb, w_ref[:, a:b])

    b64 = b64_ref[...]

    def head_norm(z, bmat, gain):
        return z * lax.rsqrt(_gsum(z * z, bmat) + EPS) * gain

    qa = head_norm(proj(C_QA, C_KA), b64, g[1:2, 0:512])
    ka = head_norm(proj(C_KA, C_VA), b64[0:128, 0:128], g[1:2, 512:640])
    va = proj(C_VA, C_QB)
    if states:
        st[0][...] = ka
        st[1][...] = va
    if rope:
        qa = _rotate(qa, *tabs_w, 16)
        ka = _rotate(ka, *tabs_w, 16)
    qa_ref[...] = qa.astype(BF16)
    ka_ref[...] = ka.astype(BF16)
    va_ref[...] = va.astype(BF16)

    qb = head_norm(proj(C_QB, C_KB), b64, g[2:3, 0:512])
    kb = head_norm(proj(C_KB, C_VB), b64, g[2:3, 512:1024])
    vb = proj(C_VB, C_CQ)
    if states:
        st[2][...] = kb
        st[3][...] = vb
    qb_ref[...] = qb.astype(BF16)
    kb_ref[...] = kb.astype(BF16)
    vb_ref[...] = vb.astype(BF16)

    cq = proj(C_CQ, C_CKV)
    cqn = cq * lax.rsqrt(jnp.mean(cq * cq, axis=-1, keepdims=True) + EPS) * g[3:4, 0:256]
    qq = _dot(cqn.astype(BF16), wuq_ref[...])
    qn, qr = qq[:, 0:512], qq[:, 512:768]
    qn2, qr2 = qn * qn, qr * qr
    bnn, brn, bnr, brr = bnn_ref[...], brn_ref[...], bnr_ref[...], brr_ref[...]
    ssn = (_gsum(qn2, bnn) + _gsum(qr2, brn)) * (1.0 / QK_DIM)
    ssr = (_gsum(qn2, bnr) + _gsum(qr2, brr)) * (1.0 / QK_DIM)
    qn = qn * lax.rsqrt(ssn + EPS) * g[4:5, 0:512]
    qr = qr * lax.rsqrt(ssr + EPS) * g[4:5, 512:768]
    if rope:
        qr = _rotate(qr, *tabs_m, 8)
    for p in range(4):
        qc_ref[:, 256 * p:256 * p + 128] = qn[:, 128 * p:128 * p + 128].astype(BF16)
        q4 = 128 * (p // 2)
        qc_ref[:, 256 * p + 128:256 * p + 256] = qr[:, q4:q4 + 128].astype(BF16)

    ckv = proj(C_CKV, C_KR)
    ckvn = ckv * lax.rsqrt(jnp.mean(ckv * ckv, axis=-1, keepdims=True) + EPS) * g[3:4, 256:384]
    kr_t = proj(C_KR, C_END)
    if states:
        st[4][...] = ckvn
        st[5][...] = kr_t[:, 0:ROPE_DIM]
    _mla_key_tail(ckvn.astype(BF16), kr_t, g, wukv_ref, bnn, bnr, brr, tabs_m, kc_ref, vc_ref)


def _const_spec(shape):
    nd = len(shape)
    return pl.BlockSpec(shape, lambda i, _nd=nd: (0,) * _nd, pipeline_mode=pl.Buffered(1))


def _inproj(x, mods, rows_per_mod, gains, w_a, w_uq, w_ukv, mats, rope_tabs, seq_len, states):
    n = x.shape[0]
    tm = min(1024, n)
    rope = rope_tabs is not None
    row = lambda w: pl.BlockSpec((tm, w), lambda i: (i, 0))
    in_specs = [row(D_MODEL),
                pl.BlockSpec((1, 8, D_MODEL), lambda i: ((i * tm) // rows_per_mod, 0, 0)),
                _const_spec(gains.shape), _const_spec(w_a.shape), _const_spec(w_uq.shape),
                _const_spec(w_ukv.shape)] + [_const_spec(m.shape) for m in mats]
    args = [x, mods, gains, w_a, w_uq, w_ukv, *mats]
    if rope:
        tiles_per_seq = seq_len // tm
        in_specs += [pl.BlockSpec((tm, LANES), lambda i: (i % tiles_per_seq, 0))] * 6
        args += list(rope_tabs)
    widths = [512, 128, 128, 512, 512, 512, 1024, 1024, 512]
    out_shape = [jax.ShapeDtypeStruct((n, w), BF16) for w in widths]
    out_specs = [row(w) for w in widths]
    if states:
        swidths = [128, 128, 512, 512, KV_LORA, ROPE_DIM]
        out_shape += [jax.ShapeDtypeStruct((n, w), F32) for w in swidths]
        out_specs += [row(w) for w in swidths]
    return pl.pallas_call(
        functools.partial(_inproj_body, rope, states),
        grid=(n // tm,), in_specs=in_specs, out_specs=out_specs, out_shape=out_shape,
        compiler_params=_cparams(("parallel",)),
        name="inproj_lat" if rope else "inproj_ctx",
    )(*args)


def _mla_cache_body(ckv_ref, kr_ref, g_ref, wukv_ref, bnn_ref, bnr_ref, brr_ref, kc_ref, vc_ref):
    _mla_key_tail(ckv_ref[...].astype(BF16), kr_ref[...], g_ref[...], wukv_ref, bnn_ref[...],
                  bnr_ref[...], brr_ref[...], None, kc_ref, vc_ref)


def _mla_cache_keys(ckv, kr_t, gains, w_ukv, bnn, bnr, brr):
    n = ckv.shape[0]
    tm = min(512, n)
    row = lambda w: pl.BlockSpec((tm, w), lambda i: (i, 0))
    return pl.pallas_call(
        _mla_cache_body,
        grid=(n // tm,),
        in_specs=[row(KV_LORA), row(256), _const_spec(gains.shape), _const_spec(w_ukv.shape),
                  _const_spec(bnn.shape), _const_spec(bnr.shape), _const_spec(brr.shape)],
        out_specs=[row(1024), row(512)],
        out_shape=[jax.ShapeDtypeStruct((n, 1024), BF16), jax.ShapeDtypeStruct((n, 512), BF16)],
        compiler_params=_cparams(("parallel",)),
        name="mla_cache_keys",
    )(ckv, kr_t, gains, w_ukv, bnn, bnr, brr)


def _lane_mask(width, ranges):
    lane = lax.broadcasted_iota(jnp.int32, (1, width), 1)
    m = None
    for lo, hi in ranges:
        c = (lane >= lo) & (lane < hi)
        m = c if m is None else (m | c)
    return jnp.where(m, 1.0, 0.0).astype(BF16)


def _stack_heads(q, mask0, mask1):
    return jnp.concatenate([q * mask0, q * mask1], axis=0)


def _lane_tiles(x):
    return [x[:, j:j + LANES] for j in range(0, x.shape[1], LANES)]


def _softmax_block(scores, sink=None):
    rows = scores[0].shape[0]
    mp = None
    for s in scores:
        for t in _lane_tiles(s):
            mp = t if mp is None else jnp.maximum(mp, t)
    base = sink if sink is not None else jnp.full((rows, LANES), NEG, F32)
    m = jnp.maximum(base, jnp.max(mp, axis=-1, keepdims=True))
    lp = None
    ps = []
    for s in scores:
        p = jnp.exp2(s - _tile_lanes(m, s.shape[1]))
        for t in _lane_tiles(p):
            lp = t if lp is None else lp + t
        ps.append(p.astype(BF16))
    if sink is not None:
        lane = lax.broadcasted_iota(jnp.int32, (rows, LANES), 1)
        lp = lp + jnp.where(lane == 0, jnp.exp2(sink - m), 0.0)
    p_all = ps[0] if len(ps) == 1 else jnp.concatenate(ps, axis=-1)
    return p_all, jnp.broadcast_to(jnp.sum(lp, axis=-1, keepdims=True), (rows, LANES))


def _softmax_pv(scores, values, sink=None):
    p_all, l = _softmax_block(scores, sink)
    v_all = values[0] if len(values) == 1 else jnp.concatenate(values, axis=0)
    return _dot(p_all, v_all) / l


def _merge_heads(o, tq):
    lane = lax.broadcasted_iota(jnp.int32, (tq, LANES), 1)
    return jnp.where(lane < HEAD_DIM, o[0:tq], o[tq:2 * tq])


def _mla_masks(p_mod2):
    lane = lax.broadcasted_iota(jnp.int32, (1, 256), 1)
    r0 = 128 + 32 * (2 * p_mod2)
    m0 = (lane < 64) | ((lane >= r0) & (lane < r0 + 32))
    m1 = ((lane >= 64) & (lane < 128)) | ((lane >= r0 + 32) & (lane < r0 + 64))
    return (jnp.where(m0, 1.0, 0.0).astype(BF16), jnp.where(m1, 1.0, 0.0).astype(BF16))


def _sink_col(sink_ref, layer, h0, h1, tq):
    row = lax.broadcasted_iota(jnp.int32, (2 * tq, LANES), 0)
    return jnp.where(row < tq, sink_ref[layer, h0], sink_ref[layer, h1])


def _ctx_attn_body(layer, sink_ref, qa_ref, ka_ref, va_ref, qb_ref, kb_ref, vb_ref, qc_ref, kc_ref,
                   vc_ref, oa_ref, ob_ref, oc_ref):
    tq = qa_ref.shape[0]
    lo = _lane_mask(LANES, [(0, 64)])
    hi = _lane_mask(LANES, [(64, 128)])
    ka, va = ka_ref[...], va_ref[...]
    for j in range(4):
        sl = slice(128 * j, 128 * j + 128)
        qs = _stack_heads(qa_ref[:, sl], lo, hi)
        sink = _sink_col(sink_ref, layer, j, 4 + j, tq)
        o = _softmax_pv([_dot_nt(qs, ka)], [va], sink)
        oa_ref[:, sl] = _merge_heads(o, tq).astype(BF16)

        qs = _stack_heads(qb_ref[:, sl], lo, hi)
        o = _softmax_pv([_dot_nt(qs, kb_ref[:, sl])], [vb_ref[:, sl]])
        ob_ref[:, sl] = _merge_heads(o, tq).astype(BF16)

        m0, m1 = _mla_masks(j % 2)
        s2 = slice(256 * j, 256 * j + 256)
        qs = _stack_heads(qc_ref[:, s2], m0, m1)
        o = _softmax_pv([_dot_nt(qs, kc_ref[:, s2])], [vc_ref[:, sl]])
        oc_ref[:, sl] = _merge_heads(o, tq).astype(BF16)


def _ctx_attn(layer, sink, seq, qa, ka, va, qb, kb, vb, qc, kc, vc):
    n = qa.shape[0]
    row = lambda w: pl.BlockSpec((seq, w), lambda b: (b, 0))
    ins = [qa, ka, va, qb, kb, vb, qc, kc, vc]
    return pl.pallas_call(
        functools.partial(_ctx_attn_body, layer),
        grid=(n // seq,),
        in_specs=[pl.BlockSpec(memory_space=pltpu.SMEM)] + [row(a.shape[1]) for a in ins],
        out_specs=[row(512)] * 3,
        out_shape=[jax.ShapeDtypeStruct((n, 512), BF16)] * 3,
        compiler_params=_cparams(("parallel",)),
        name="ctx_attn",
    )(sink, *ins)


WIN_ROW_BLOCK = 32


WIN_Q_BLOCK = 128
WIN_BLOCKS_PER_STEP = 4


def _win_body(layer, sink_ref, q_ref, k_ref, v_ref, kc_ref, vc_ref, o_ref, s_sc, p_sc, l_sc):
    tq = WIN_Q_BLOCK
    seq = k_ref.shape[1]
    kw = 3 * tq
    rb = WIN_ROW_BLOCK
    lo = _lane_mask(LANES, [(0, 64)])
    hi = _lane_mask(LANES, [(64, 128)])
    kc, vc = kc_ref[0, 0], vc_ref[0, 0]
    for u in range(q_ref.shape[1] // tq):
        i = pl.program_id(1) * (q_ref.shape[1] // tq) + u
        kstart = pl.multiple_of(jnp.clip((i - 1) * tq, 0, seq - kw), tq)
        k_all = jnp.concatenate([k_ref[0, pl.ds(kstart, kw), :], kc], axis=0)
        v_all = jnp.concatenate([v_ref[0, pl.ds(kstart, kw), :], vc], axis=0)
        q_pos = i * tq + lax.broadcasted_iota(jnp.int32, (tq, kw), 0)
        k_pos = kstart + lax.broadcasted_iota(jnp.int32, (tq, kw), 1)
        band = jnp.abs(q_pos - k_pos) <= WINDOW
        qs = jnp.concatenate(
            [_stack_heads(q_ref[0, u * tq:(u + 1) * tq, 128 * j:128 * j + 128], lo, hi) for j in range(4)],
            axis=0)
        s_sc[u] = _dot_nt(qs, k_all)
        for j in range(4):
            for r in range(2 * tq * j, 2 * tq * (j + 1), rb):
                head = j if r < 2 * tq * j + tq else 4 + j
                q0 = r % tq
                s_band = jnp.where(band[q0:q0 + rb], s_sc[u, r:r + rb, 0:kw], NEG)
                sink = jnp.full((rb, LANES), sink_ref[layer, head], F32)
                p, l = _softmax_block([s_band, s_sc[u, r:r + rb, kw:]], sink)
                p_sc[u, r:r + rb, :] = p
                l_sc[u, r:r + rb, :] = l
            rows = slice(2 * tq * j, 2 * tq * (j + 1))
            o = _dot(p_sc[u, rows, :], v_all) / l_sc[u, rows, :]
            o_ref[0, u * tq:(u + 1) * tq, 128 * j:128 * j + 128] = _merge_heads(o, tq).astype(BF16)


def _win_attn(layer, sink, q, k, v, kc, vc):
    b, seq, _ = q.shape
    nb = min(WIN_BLOCKS_PER_STEP, seq // WIN_Q_BLOCK)
    tq = nb * WIN_Q_BLOCK
    past = kc.shape[2]
    keys = 3 * WIN_Q_BLOCK + past
    return pl.pallas_call(
        functools.partial(_win_body, layer),
        grid=(b, seq // tq),
        in_specs=[pl.BlockSpec(memory_space=pltpu.SMEM),
                  pl.BlockSpec((1, tq, 512), lambda bi, i: (bi, i, 0)),
                  pl.BlockSpec((1, seq, 128), lambda bi, i: (bi, 0, 0)),
                  pl.BlockSpec((1, seq, 128), lambda bi, i: (bi, 0, 0)),
                  pl.BlockSpec((1, 1, past, 128), lambda bi, i: (bi, layer, 0, 0)),
                  pl.BlockSpec((1, 1, past, 128), lambda bi, i: (bi, layer, 0, 0))],
        out_specs=pl.BlockSpec((1, tq, 512), lambda bi, i: (bi, i, 0)),
        out_shape=jax.ShapeDtypeStruct((b, seq, 512), BF16),
        scratch_shapes=[pltpu.VMEM((nb, 8 * WIN_Q_BLOCK, keys), F32),
                        pltpu.VMEM((nb, 8 * WIN_Q_BLOCK, keys), BF16),
                        pltpu.VMEM((nb, 8 * WIN_Q_BLOCK, LANES), F32)],
        compiler_params=_cparams(("parallel", "arbitrary")),
        name="win_attn",
    )(sink, q, k, v, kc, vc)


NBR_TILE_ROWS = 4
NBR_WIN_ROWS = NBR_TILE_ROWS + NBR_ROWS
NBR_TAB_PAD = NBR_WIN_ROWS - NBR_ROWS
NBR_ROW_BLOCK = 32
NBR_TILES_PER_STEP = 2


def _nbr_body(rows, q_ref, k_ref, v_ref, kc_ref, vc_ref, tab_ref, o_ref, s_sc, p_sc, l_sc):
    tq = NBR_TILE_ROWS * GRID_W
    kw = NBR_WIN_ROWS * GRID_W
    lo = _lane_mask(LANES, [(0, 64)])
    hi = _lane_mask(LANES, [(64, 128)])
    past = kc_ref.shape[2]
    rb = NBR_ROW_BLOCK
    tiles = q_ref.shape[1] // tq
    for u in range(tiles):
        r0 = NBR_TILE_ROWS * (pl.program_id(1) * tiles + u)
        ws = jnp.clip(r0 - NBR_ROWS // 2, 0, rows - NBR_WIN_ROWS)
        kstart = pl.multiple_of(ws * GRID_W, LANES)
        k_row = ws + lax.broadcasted_iota(jnp.int32, (1, kw), 1) // GRID_W
        for j in range(4):
            sl = slice(128 * j, 128 * j + 128)
            qs = _stack_heads(q_ref[0, u * tq:(u + 1) * tq, sl], lo, hi)
            s_sc[u, j, :, 0:kw] = _dot_nt(qs, k_ref[0, pl.ds(kstart, kw), sl])
            s_sc[u, j, :, kw:kw + past] = _dot_nt(qs, kc_ref[0, 0, :, sl])
        for j in range(4):
            sl = slice(128 * j, 128 * j + 128)
            for b0 in range(0, 2 * tq, rb):
                h = 2 * j + b0 // tq
                ql, sub = divmod(b0 % tq, GRID_W)
                d0 = ws - r0 - ql + (NBR_ROWS - 1) + NBR_TAB_PAD
                bias = jnp.concatenate([tab_ref[h, d0 + 2 * m, sub:sub + rb, :]
                                        for m in range(NBR_WIN_ROWS // 2)], axis=-1)
                rs = jnp.clip(r0 + ql - NBR_ROWS // 2, 0, rows - NBR_ROWS)
                valid = (k_row >= rs) & (k_row < rs + NBR_ROWS)
                s_nb = jnp.where(valid, s_sc[u, j, b0:b0 + rb, 0:kw] + bias, NEG)
                p, l = _softmax_block([s_nb, s_sc[u, j, b0:b0 + rb, kw:kw + past]])
                p_sc[u, j, b0:b0 + rb, :] = p
                l_sc[u, j, b0:b0 + rb, :] = l
            v_all = jnp.concatenate([v_ref[0, pl.ds(kstart, kw), sl], vc_ref[0, 0, :, sl]], axis=0)
            o = _dot(p_sc[u, j], v_all) / l_sc[u, j]
            o_ref[0, u * tq:(u + 1) * tq, sl] = _merge_heads(o, tq).astype(BF16)


def _nbr_attn(layer, q, k, v, kc, vc, table):
    b, seq, _ = q.shape
    rows = seq // GRID_W
    tile = NBR_TILE_ROWS * GRID_W
    tq = NBR_TILES_PER_STEP * tile
    past = kc.shape[2]
    keys = NBR_WIN_ROWS * GRID_W + past
    once = pl.Buffered(1)
    return pl.pallas_call(
        functools.partial(_nbr_body, rows),
        grid=(b, seq // tq),
        in_specs=[pl.BlockSpec((1, tq, 512), lambda bi, i: (bi, i, 0)),
                  pl.BlockSpec((1, seq, 512), lambda bi, i: (bi, 0, 0), pipeline_mode=once),
                  pl.BlockSpec((1, seq, 512), lambda bi, i: (bi, 0, 0), pipeline_mode=once),
                  pl.BlockSpec((1, 1, past, 512), lambda bi, i: (bi, layer, 0, 0), pipeline_mode=once),
                  pl.BlockSpec((1, 1, past, 512), lambda bi, i: (bi, layer, 0, 0), pipeline_mode=once),
                  pl.BlockSpec(table.shape, lambda bi, i: (0, 0, 0, 0), pipeline_mode=once)],
        out_specs=pl.BlockSpec((1, tq, 512), lambda bi, i: (bi, i, 0)),
        out_shape=jax.ShapeDtypeStruct((b, seq, 512), BF16),
        scratch_shapes=[pltpu.VMEM((NBR_TILES_PER_STEP, 4, 2 * tile, keys), F32),
                        pltpu.VMEM((NBR_TILES_PER_STEP, 4, 2 * tile, keys), BF16),
                        pltpu.VMEM((NBR_TILES_PER_STEP, 4, 2 * tile, LANES), F32)],
        compiler_params=_cparams(("parallel", "arbitrary")),
        name="nbr_attn",
    )(q, k, v, kc, vc, table)


def _nbr_bias_table(rel_bias):
    col = np.arange(GRID_W)
    cs = np.clip(col - NBR_COLS // 2, 0, GRID_W - NBR_COLS)
    kc = np.arange(GRID_W)
    ok = (kc[None, :] >= cs[:, None]) & (kc[None, :] < cs[:, None] + NBR_COLS)
    dc = kc[None, :] - col[:, None] + (NBR_COLS - 1)
    pick = (dc[:, :, None] == np.arange(2 * NBR_COLS - 1)[None, None, :]) & ok[:, :, None]
    t = jnp.einsum("hdk,qck->hdqc", rel_bias.astype(F32) * LOG2E, jnp.asarray(pick, F32),
                   precision=lax.Precision.HIGHEST)
    t = jnp.where(jnp.asarray(ok)[None, None], t, NEG)
    t = jnp.pad(t, ((0, 0), (NBR_TAB_PAD, NBR_TAB_PAD), (0, 0), (0, 0)))
    return jnp.concatenate([t[:, :-1], t[:, 1:]], axis=-1)


MLA_Q_TILE = 1024
MLA_KEY_CHUNK = 512
MLA_ROW_BLOCK = 64


def _mla_body(q_ref, kl_ref, kc_ref, vl_ref, vc_ref, o_ref, qs_sc, m_sc, l_sc, acc_sc, s_sc, p_sc, a_sc):
    tq = q_ref.shape[1]
    rows = 2 * tq
    seq = kl_ref.shape[1]
    past = kc_ref.shape[1]
    tk = min(MLA_KEY_CHUNK, seq)
    rb = MLA_ROW_BLOCK
    m0, m1 = _mla_masks(pl.program_id(1) % 2)
    qs_sc[...] = _stack_heads(q_ref[0], m0, m1)
    m_sc[...] = jnp.full(m_sc.shape, NEG, F32)
    l_sc[...] = jnp.zeros(l_sc.shape, F32)
    acc_sc[...] = jnp.zeros(acc_sc.shape, F32)

    def step(c, k, v):
        slab = c % 2
        n = k.shape[0]
        s_sc[slab, :, 0:n] = _dot_nt(qs_sc[...], k)
        for r in range(0, rows, rb):
            sl = slice(r, r + rb)
            sb = s_sc[slab, sl, 0:n]
            mp = None
            for t in _lane_tiles(sb):
                mp = t if mp is None else jnp.maximum(mp, t)
            m_prev = m_sc[sl]
            m_new = jnp.maximum(m_prev, jnp.max(mp, axis=-1, keepdims=True))
            alpha = jnp.exp2(m_prev - m_new)
            p = jnp.exp2(sb - _tile_lanes(m_new, n))
            psum = None
            for t in _lane_tiles(p):
                psum = t if psum is None else psum + t
            l_sc[sl] = alpha * l_sc[sl] + psum
            m_sc[sl] = m_new
            a_sc[slab, sl, :] = alpha
            p_sc[slab, sl, 0:n] = p.astype(BF16)
        acc_sc[...] = a_sc[slab] * acc_sc[...] + _dot(p_sc[slab, :, 0:n], v)

    chunks = [(kl_ref, vl_ref, o, min(tk, seq - o)) for o in range(0, seq, tk)]
    chunks += [(kc_ref, vc_ref, o, min(tk, past - o)) for o in range(0, past, tk)]
    for c, (k_ref, v_ref, o, n) in enumerate(chunks):
        step(c, k_ref[0, o:o + n, :], v_ref[0, o:o + n, :])
    l = jnp.sum(l_sc[...], axis=-1, keepdims=True)
    o_ref[0] = _merge_heads(acc_sc[...] / l, tq).astype(BF16)


def _mla_attn(q, kl, vl, kc, vc):
    b, seq, _ = q.shape
    past = kc.shape[1]
    tq = min(MLA_Q_TILE, seq)
    tk = min(MLA_KEY_CHUNK, seq)
    return pl.pallas_call(
        _mla_body,
        grid=(b, 4, seq // tq),
        in_specs=[pl.BlockSpec((1, tq, 256), lambda bi, p, qi: (bi, qi, p)),
                  pl.BlockSpec((1, seq, 256), lambda bi, p, qi: (bi, 0, p)),
                  pl.BlockSpec((1, past, 256), lambda bi, p, qi: (bi, 0, p)),
                  pl.BlockSpec((1, seq, 128), lambda bi, p, qi: (bi, 0, p)),
                  pl.BlockSpec((1, past, 128), lambda bi, p, qi: (bi, 0, p))],
        out_specs=pl.BlockSpec((1, tq, 128), lambda bi, p, qi: (bi, qi, p)),
        out_shape=jax.ShapeDtypeStruct((b, seq, 512), BF16),
        scratch_shapes=[pltpu.VMEM((2 * tq, 256), BF16), pltpu.VMEM((2 * tq, LANES), F32),
                        pltpu.VMEM((2 * tq, LANES), F32), pltpu.VMEM((2 * tq, LANES), F32),
                        pltpu.VMEM((2, 2 * tq, tk), F32), pltpu.VMEM((2, 2 * tq, tk), BF16),
                        pltpu.VMEM((2, 2 * tq, LANES), F32)],
        compiler_params=_cparams(("parallel", "parallel", "arbitrary")),
        name="mla_attn",
    )(q, kl, kc, vl, vc)


def _pack_pairs(x):
    w = x.shape[1] // 2
    hi = lax.bitcast_convert_type(x[:, :w].astype(BF16).astype(F32), jnp.int32)
    lo = lax.bitcast_convert_type(x[:, w:].astype(BF16).astype(F32), jnp.int32)
    return (hi & jnp.int32(-65536)) | lax.shift_right_logical(lo, jnp.int32(16))


def _unpack_pairs(p):
    hi = lax.bitcast_convert_type(p & jnp.int32(-65536), F32)
    lo = lax.bitcast_convert_type(lax.shift_left(p, jnp.int32(16)), F32)
    return jnp.concatenate([hi, lo], axis=-1)


def _merge_body(x_ref, oa_ref, ob_ref, oc_ref, mod_ref, g_ref, wg_ref, woa_ref, wob_ref, woc_ref,
                wout_ref, wr_ref, br_ref, tri_ref, x1_ref, route_ref, h2p_ref, count_ref, count_sc):
    x = x_ref[...]
    g = g_ref[...]
    mod = mod_ref[0]
    hb = _norm_mod(x, g[0:1], mod[1:2], mod[0:1]).astype(BF16)
    m = None
    for br, (o_ref, wo_ref) in enumerate(((oa_ref, woa_ref), (ob_ref, wob_ref), (oc_ref, woc_ref))):
        z = _dot(hb, wg_ref[:, D_MODEL * br:D_MODEL * (br + 1)])
        gate = 1.0 / (1.0 + jnp.exp(-z))
        t = gate * _dot(o_ref[...], wo_ref[...])
        m = t if m is None else m + t
    y = _dot(m.astype(BF16), wout_ref[...])
    x1 = x + mod[2:3] * y
    x1_ref[...] = x1

    h2 = _norm_mod(x1, g[6:7], mod[4:5], mod[3:4])
    h_hi, h_lo = _split(h2)
    w_hi, w_lo = _split(wr_ref[...])
    both_w = _dot_nt(jnp.concatenate([w_hi, w_lo], axis=0), h_hi)
    logits = both_w[0:N_EXPERTS] + both_w[N_EXPERTS:2 * N_EXPERTS] + _dot_nt(w_hi, h_lo)
    score = 1.0 / (1.0 + jnp.exp(-logits))
    sel = score + br_ref[...]
    sel_r = [sel[e:e + 1] for e in range(N_EXPERTS)]
    sc_r = [score[e:e + 1] for e in range(N_EXPERTS)]
    picked = []
    for e in range(N_EXPERTS):
        grp, a = divmod(e, EXPERTS_PER_GROUP)
        rank = None
        for bb in range(EXPERTS_PER_GROUP):
            if bb == a:
                continue
            o = sel_r[grp * EXPERTS_PER_GROUP + bb]
            beats = (o >= sel_r[e]) if bb < a else (o > sel_r[e])
            r = jnp.where(beats, 1.0, 0.0)
            rank = r if rank is None else rank + r
        picked.append(rank < 2.0)
    gscore = []
    for grp in range(N_GROUPS):
        tot = None
        for a in range(EXPERTS_PER_GROUP):
            e = grp * EXPERTS_PER_GROUP + a
            t = jnp.where(picked[e], sel_r[e], 0.0)
            tot = t if tot is None else tot + t
        gscore.append(tot)
    best = jnp.zeros_like(gscore[0])
    best_v = gscore[0]
    for grp in range(1, N_GROUPS):
        upd = gscore[grp] > best_v
        best = jnp.where(upd, float(grp), best)
        best_v = jnp.where(upd, gscore[grp], best_v)
    cw, pk = [], []
    for a in range(EXPERTS_PER_GROUP):
        tot = flag = None
        for grp in range(N_GROUPS):
            e = grp * EXPERTS_PER_GROUP + a
            f = (best == float(grp)) & picked[e]
            t = jnp.where(f, sc_r[e], 0.0)
            tot = t if tot is None else tot + t
            flag = f if flag is None else (flag | f)
        cw.append(tot)
        pk.append(flag)
    den = cw[0] + cw[1] + cw[2] + cw[3]
    first = jnp.where(pk[0], 0.0, jnp.where(pk[1], 1.0, jnp.where(pk[2], 2.0, 3.0)))
    second = jnp.where(pk[3], 3.0, jnp.where(pk[2], 2.0, jnp.where(pk[1], 1.0, 0.0)))
    slot_e, slot_w = [], []
    for which in (first, second):
        tot = None
        for a in range(EXPERTS_PER_GROUP):
            t = jnp.where(which == float(a), cw[a], 0.0)
            tot = t if tot is None else tot + t
        slot_w.append(tot / den)
        slot_e.append(best * float(EXPERTS_PER_GROUP) + which)

    @pl.when(pl.program_id(0) == 0)
    def _():
        count_sc[...] = jnp.zeros(count_sc.shape, F32)

    tm = x.shape[0]
    eid = lax.broadcasted_iota(jnp.int32, (N_EXPERTS, tm), 0).astype(F32)
    oh = [eid == slot_e[0], eid == slot_e[1]]
    both = jnp.where(oh[0] | oh[1], 1.0, 0.0)
    seen = count_sc[...][:, 0:1] + _dot(both.astype(BF16), tri_ref[...])
    for k in range(2):
        route_ref[k:k + 1, :] = slot_w[k]
        route_ref[2 + k:3 + k, :] = slot_e[k]
        route_ref[4 + k:5 + k, :] = jnp.sum(jnp.where(oh[k], seen, 0.0), axis=0, keepdims=True)
    route_ref[6:8, :] = jnp.zeros((2, tm), F32)
    count_sc[...] = count_sc[...] + jnp.sum(both, axis=-1, keepdims=True)
    count_ref[...] = count_sc[...]
    h2p_ref[...] = _pack_pairs(h2)


def _merge(x, oa, ob, oc, mods, rows_per_mod, gains, w_gate, wo_a, wo_b, wo_c, w_out, w_r_t, b_r):
    n = x.shape[0]
    tm = min(1024, n)
    row = lambda w: pl.BlockSpec((tm, w), lambda i: (i, 0))
    tri = jnp.asarray(np.triu(np.ones((tm, tm), np.float32), 1), BF16)
    consts = [gains, w_gate, wo_a, wo_b, wo_c, w_out, w_r_t, b_r, tri]
    return pl.pallas_call(
        _merge_body,
        grid=(n // tm,),
        in_specs=[row(D_MODEL), row(512), row(512), row(512),
                  pl.BlockSpec((1, 8, D_MODEL), lambda i: ((i * tm) // rows_per_mod, 0, 0))]
                 + [_const_spec(c.shape) for c in consts],
        out_specs=[row(D_MODEL), pl.BlockSpec((8, tm), lambda i: (0, i)), row(512),
                   pl.BlockSpec((N_EXPERTS, LANES), lambda i: (0, 0))],
        out_shape=[jax.ShapeDtypeStruct((n, D_MODEL), F32), jax.ShapeDtypeStruct((8, n), F32),
                   jax.ShapeDtypeStruct((n, 512), jnp.int32),
                   jax.ShapeDtypeStruct((N_EXPERTS, LANES), F32)],
        scratch_shapes=[pltpu.VMEM((N_EXPERTS, LANES), F32)],
        compiler_params=_cparams(("arbitrary",)),
        name="merge",
    )(x, oa, ob, oc, mods, *consts)


EXPERT_TILE = 512
SC_CORES = 2
SC_SUBCORES = 16
SC_WORKERS = SC_CORES * SC_SUBCORES
SC_WINDOW = 128


def _sc_mesh():
    return plsc.VectorSubcoreMesh(core_axis_name="c", subcore_axis_name="s", num_cores=SC_CORES,
                                  num_subcores=SC_SUBCORES)


def _sc_window_base(steps, j):
    wid = lax.axis_index("s") * SC_CORES + lax.axis_index("c")
    return pl.multiple_of((wid * steps + j) * SC_WINDOW, SC_WINDOW)


def _sc_dispatch(rows, pos0, pos1, n_out):
    n, w = rows.shape
    steps = n // (SC_WORKERS * SC_WINDOW)

    @functools.partial(
        pl.kernel, out_type=jax.ShapeDtypeStruct((n_out, w), rows.dtype), mesh=_sc_mesh(),
        scratch_types=[pltpu.VMEM((SC_WINDOW,), jnp.int32), pltpu.VMEM((SC_WINDOW,), jnp.int32),
                       pltpu.VMEM((SC_WINDOW, w), rows.dtype)],
        name="moe_dispatch")
    def run(x_hbm, i0_hbm, i1_hbm, o_hbm, i0_v, i1_v, rows_v):
        @pl.loop(0, steps)
        def _(j):
            base = _sc_window_base(steps, j)
            pltpu.sync_copy(i0_hbm.at[pl.ds(base, SC_WINDOW)], i0_v)
            pltpu.sync_copy(i1_hbm.at[pl.ds(base, SC_WINDOW)], i1_v)
            pltpu.sync_copy(x_hbm.at[pl.ds(base, SC_WINDOW)], rows_v)
            pltpu.sync_copy(rows_v, o_hbm.at[i0_v])
            pltpu.sync_copy(rows_v, o_hbm.at[i1_v])

    return run(rows, pos0, pos1)


def _sc_collect(rows, pos0, pos1):
    n = pos0.shape[0]
    w = rows.shape[1]
    steps = n // (SC_WORKERS * SC_WINDOW)
    out = jax.ShapeDtypeStruct((n, w), rows.dtype)

    @functools.partial(
        pl.kernel, out_type=[out, out], mesh=_sc_mesh(),
        scratch_types=[pltpu.VMEM((SC_WINDOW,), jnp.int32), pltpu.VMEM((SC_WINDOW, w), rows.dtype)],
        name="moe_collect")
    def run(y_hbm, i0_hbm, i1_hbm, o0_hbm, o1_hbm, i_v, rows_v):
        @pl.loop(0, steps)
        def _(j):
            base = _sc_window_base(steps, j)
            for i_hbm, o_hbm in ((i0_hbm, o0_hbm), (i1_hbm, o1_hbm)):
                pltpu.sync_copy(i_hbm.at[pl.ds(base, SC_WINDOW)], i_v)
                pltpu.sync_copy(y_hbm.at[i_v], rows_v)
                pltpu.sync_copy(rows_v, o_hbm.at[pl.ds(base, SC_WINDOW)])

    return run(rows, pos0, pos1)


def _experts_body(te_ref, nv_ref, xs_ref, wg_ref, wu_ref, wd_ref, ys_ref, wg_sc, wu_sc, wd_sc):
    j = pl.program_id(0)

    @pl.when((j == 0) | (te_ref[j] != te_ref[jnp.maximum(j - 1, 0)]))
    def _():
        wg_sc[...] = wg_ref[0, 0].astype(BF16)
        wu_sc[...] = wu_ref[0, 0].astype(BF16)
        wd_sc[...] = wd_ref[0, 0].astype(BF16)

    @pl.when(j < nv_ref[0])
    def _():
        x = _unpack_pairs(xs_ref[...]).astype(BF16)
        zg = _dot(x, wg_sc[...])
        act = zg * (1.0 / (1.0 + jnp.exp(-zg))) * _dot(x, wu_sc[...])
        ys_ref[...] = _pack_pairs(_dot(act.astype(BF16), wd_sc[...]))

    @pl.when(j >= nv_ref[0])
    def _():
        ys_ref[...] = jnp.zeros(ys_ref.shape, ys_ref.dtype)


def _experts(layer, xs, tile_expert, n_valid, wg, wu, wd):
    p = xs.shape[0]
    wspec = lambda r, c: pl.BlockSpec((1, 1, r, c), lambda j, te, nv: (layer, te[j], 0, 0))
    grid_spec = pltpu.PrefetchScalarGridSpec(
        num_scalar_prefetch=2,
        grid=(p // EXPERT_TILE,),
        in_specs=[pl.BlockSpec((EXPERT_TILE, 512), lambda j, te, nv: (j, 0)),
                  wspec(D_MODEL, D_FF), wspec(D_MODEL, D_FF), wspec(D_FF, D_MODEL)],
        out_specs=pl.BlockSpec((EXPERT_TILE, 512), lambda j, te, nv: (j, 0)),
        scratch_shapes=[pltpu.VMEM((D_MODEL, D_FF), BF16), pltpu.VMEM((D_MODEL, D_FF), BF16),
                        pltpu.VMEM((D_FF, D_MODEL), BF16)])
    return pl.pallas_call(
        _experts_body, grid_spec=grid_spec,
        out_shape=jax.ShapeDtypeStruct((p, 512), jnp.int32),
        compiler_params=_cparams(("arbitrary",)),
        name="experts",
    )(tile_expert, n_valid, xs, wg, wu, wd)


def _combine_body(x1_ref, y0_ref, y1_ref, route_ref, mod_ref, o_ref):
    tm = x1_ref.shape[0]
    eye = jnp.where(lax.broadcasted_iota(jnp.int32, (tm, tm), 0)
                    == lax.broadcasted_iota(jnp.int32, (tm, tm), 1), 1.0, 0.0).astype(BF16)
    r_hi, r_lo = _split(route_ref[...])
    wcol = _dot_nt(eye, r_hi) + _dot_nt(eye, r_lo)
    moe = wcol[:, 0:1] * _unpack_pairs(y0_ref[...]) + wcol[:, 1:2] * _unpack_pairs(y1_ref[...])
    o_ref[...] = x1_ref[...] + mod_ref[0][5:6] * moe


def _combine(x1, y0, y1, route, mods, rows_per_mod):
    n = x1.shape[0]
    tm = min(512, n)
    row = lambda w: pl.BlockSpec((tm, w), lambda i: (i, 0))
    return pl.pallas_call(
        _combine_body,
        grid=(n // tm,),
        in_specs=[row(D_MODEL), row(512), row(512), pl.BlockSpec((8, tm), lambda i: (0, i)),
                  pl.BlockSpec((1, 8, D_MODEL), lambda i: ((i * tm) // rows_per_mod, 0, 0))],
        out_specs=row(D_MODEL),
        out_shape=jax.ShapeDtypeStruct((n, D_MODEL), F32),
        compiler_params=_cparams(("parallel",)),
        name="combine",
    )(x1, y0, y1, route, mods)


def _moe(layer, x1, route, h2p, counts, mods, rows_per_mod, wg, wu, wd):
    n = x1.shape[0]
    p = 2 * n + N_EXPERTS * EXPERT_TILE
    cnt = counts[:, 0].astype(jnp.int32)
    padded = (cnt + EXPERT_TILE - 1) // EXPERT_TILE * EXPERT_TILE
    seg_end = jnp.cumsum(padded)
    seg_off = seg_end - padded
    experts = jnp.arange(N_EXPERTS, dtype=jnp.int32)

    def position(k):
        e = route[2 + k].astype(jnp.int32)
        off = jnp.sum(jnp.where(e[:, None] == experts[None], seg_off[None], 0), axis=1)
        return off + route[4 + k].astype(jnp.int32)

    pos0, pos1 = position(0), position(1)
    tile_start = jnp.arange(p // EXPERT_TILE, dtype=jnp.int32) * EXPERT_TILE
    tile_expert = jnp.sum(tile_start[:, None] >= seg_end[None], axis=1).astype(jnp.int32)
    tile_expert = jnp.minimum(tile_expert, N_EXPERTS - 1)
    n_valid = (seg_end[-1:] // EXPERT_TILE).astype(jnp.int32)

    xs = _sc_dispatch(h2p, pos0, pos1, p)
    ys = _experts(layer, xs, tile_expert, n_valid, wg, wu, wd)
    y0, y1 = _sc_collect(ys, pos0, pos1)
    return _combine(x1, y0, y1, route, mods, rows_per_mod)


def _block_ones(n_in, g_in, n_out, g_out, value=1.0):
    r = np.arange(n_in)[:, None] // g_in
    c = np.arange(n_out)[None, :] // g_out
    return jnp.asarray(np.where(r == c, value, 0.0), dtype=BF16)


def _rope_tables(seq, head_w):
    pos = np.arange(seq)
    rows, cols = pos // GRID_W, pos % GRID_W
    a = head_w // 2
    half = a // 2
    freqs = (ROPE_BASE ** (-np.arange(half, dtype=np.float32) / half)).astype(np.float32)
    lane = np.arange(LANES) % head_w
    within = lane % a
    first = within < half
    p = np.where((lane // a == 0)[None, :], rows[:, None], cols[:, None]).astype(np.float32)
    ang = (p * freqs[within % half][None, :]).astype(np.float32)
    cos, sin = np.cos(ang), np.sin(ang)
    return (jnp.asarray(cos, F32), jnp.asarray(np.where(first[None], -sin, 0.0), F32),
            jnp.asarray(np.where(first[None], 0.0, sin), F32))


def _tile_to(v, width):
    return jnp.tile(v, width // v.shape[0])


def _layer_params(i, p):
    w_in = p["w_in"][i]
    sp = np.cumsum((512, 128, 128, 512, 512, 512, Q_LORA, KV_LORA, ROPE_DIM))
    qa, ka, va, qb, kb, vb, cq, ckv, kr, gates = jnp.split(w_in, [int(s) for s in sp], axis=1)
    qa = qa.reshape(D_MODEL, WIN_HEADS, HEAD_DIM)[:, WIN_Q_ORDER, :].reshape(D_MODEL, 512)
    w_a = jnp.concatenate([qa, ka, va, qb, kb, vb, cq, ckv, jnp.tile(kr, (1, MLA_HEADS))],
                          axis=1).astype(BF16)
    w_uq = p["w_uq"][i].reshape(Q_LORA, MLA_HEADS, QK_DIM)
    w_uq = jnp.concatenate([w_uq[:, :, :NOPE_DIM].reshape(Q_LORA, 512),
                            w_uq[:, :, NOPE_DIM:].reshape(Q_LORA, 256)], axis=1).astype(BF16)
    w_ukv = p["w_ukv"][i].reshape(KV_LORA, MLA_HEADS, NOPE_DIM + V_DIM)
    w_ukv = jnp.concatenate([w_ukv[:, :, :NOPE_DIM].reshape(KV_LORA, 512),
                             w_ukv[:, :, NOPE_DIM:].reshape(KV_LORA, 512)], axis=1).astype(BF16)
    z = jnp.zeros((D_MODEL,), F32)
    row = lambda *parts: jnp.concatenate(list(parts) + [z])[:D_MODEL]
    q_scale = HEAD_DIM ** -0.5 * LOG2E
    c_scale = QK_DIM ** -0.5 * LOG2E
    g_mla = p["g_qk_mla"][i]
    gains = jnp.stack([
        p["g_norm_mix"][i],
        row(_tile_to(p["g_qk_win"][i, 0], 512) * q_scale, _tile_to(p["g_qk_win"][i, 1], 128)),
        row(_tile_to(p["g_qk_nbr"][i, 0], 512) * q_scale, _tile_to(p["g_qk_nbr"][i, 1], 512)),
        row(p["g_q_lora"][i], p["g_kv_lora"][i]),
        row(_tile_to(g_mla[0, :NOPE_DIM], 512) * c_scale, _tile_to(g_mla[0, NOPE_DIM:], 256) * c_scale),
        row(_tile_to(g_mla[1, :NOPE_DIM], 512), _tile_to(g_mla[1, NOPE_DIM:], 256)),
        p["g_norm_ffn"][i],
        z]).astype(F32)
    wo_a = p["w_o_win"][i].reshape(WIN_HEADS, HEAD_DIM, D_MODEL)[WIN_Q_ORDER, :, :].reshape(512, D_MODEL)
    return dict(
        w_a=w_a, w_uq=w_uq, w_ukv=w_ukv, gains=gains, w_gate=gates.astype(BF16),
        wo_a=wo_a.astype(BF16), wo_b=p["w_o_nbr"][i].astype(BF16), wo_c=p["w_o_mla"][i].astype(BF16),
        w_out=p["w_out"][i].astype(BF16),
        nbr_table=_nbr_bias_table(p["nbr_rel_bias"][i]))


def kernel(x_prompt, x_sample, cache_win_k, cache_win_v, cache_nbr_k, cache_nbr_v, cache_mla_ckv, cache_mla_krope, c, c_ctx, g_norm_mix, g_norm_ffn, w_ada, b_ada, w_in, g_qk_win, win_sink, g_qk_nbr, nbr_rel_bias, g_q_lora, g_kv_lora, w_uq, w_ukv, g_qk_mla, w_o_win, w_o_nbr, w_o_mla, w_out, w_router, b_router, w_exp_gate, w_exp_up, w_exp_down):
    p = dict(g_norm_mix=g_norm_mix, g_norm_ffn=g_norm_ffn, w_in=w_in, g_qk_win=g_qk_win,
             g_qk_nbr=g_qk_nbr, nbr_rel_bias=nbr_rel_bias, g_q_lora=g_q_lora, g_kv_lora=g_kv_lora,
             w_uq=w_uq, w_ukv=w_ukv, g_qk_mla=g_qk_mla, w_o_win=w_o_win, w_o_nbr=w_o_nbr,
             w_o_mla=w_o_mla, w_out=w_out, w_exp_gate=w_exp_gate, w_exp_up=w_exp_up,
             w_exp_down=w_exp_down)
    depth = w_in.shape[0]
    batch, seq, _ = x_prompt.shape
    dec_batch, dec_seq, _ = x_sample.shape
    past = cache_win_k.shape[2]

    n_c = 1 + dec_batch
    c_rows = -(-n_c // 8) * 8
    c_all = jnp.concatenate([c_ctx[None], c, jnp.zeros((c_rows - n_c, D_MODEL), F32)], axis=0)
    mods = _ada(c_all, w_ada, b_ada).reshape(depth, c_rows, 6, D_MODEL)
    mods = jnp.pad(mods, ((0, 0), (0, 0), (0, 2), (0, 0)))

    mats = (_block_ones(512, 64, 512, 64, 1.0 / HEAD_DIM), _block_ones(512, 64, 512, 64),
            _block_ones(256, 32, 512, 64), _block_ones(512, 64, 256, 32), _block_ones(256, 32, 256, 32))
    tabs = _rope_tables(dec_seq, 64) + _rope_tables(dec_seq, 32)
    sink = win_sink.astype(F32) * LOG2E
    w_r_t = w_router.T.astype(F32)
    b_r = b_router.astype(F32).reshape(N_EXPERTS, 1)
    layers = [_layer_params(i, p) for i in range(depth)]

    def merge(x, oa, ob, oc, mod, rows_per_mod, lp):
        return _merge(x, oa, ob, oc, mod, rows_per_mod, lp["gains"], lp["w_gate"], lp["wo_a"],
                      lp["wo_b"], lp["wo_c"], lp["w_out"], w_r_t, b_r)

    n_ctx = batch * seq
    n_lat = dec_batch * dec_seq
    x_ctx = x_prompt.reshape(n_ctx, D_MODEL)
    x_lat = x_sample.reshape(n_lat, D_MODEL)
    cwk = cache_win_k.reshape(dec_batch, depth, past, 128).astype(BF16)
    cwv = cache_win_v.reshape(dec_batch, depth, past, 128).astype(BF16)
    cnk = cache_nbr_k.reshape(dec_batch, depth, past, 512).astype(BF16)
    cnv = cache_nbr_v.reshape(dec_batch, depth, past, 512).astype(BF16)
    states = []
    for i, lp in enumerate(layers):
        mod_c = mods[i, 0:1]
        outs = _inproj(x_ctx, mod_c, n_ctx, lp["gains"], lp["w_a"], lp["w_uq"], lp["w_ukv"], mats, None,
                       seq, True)
        oa, ob, oc = _ctx_attn(i, sink, seq, *outs[:9])
        states.append(outs[9:])
        merged_c = merge(x_ctx, oa, ob, oc, mod_c, n_ctx, lp)

        mod_l = mods[i, 1:1 + dec_batch]
        qa, ka, va, qb, kb, vb, qc, kc, vc = _inproj(
            x_lat, mod_l, dec_seq, lp["gains"], lp["w_a"], lp["w_uq"], lp["w_ukv"], mats, tabs, dec_seq,
            False)
        kr_t = jnp.tile(cache_mla_krope[:, i].reshape(dec_batch * past, ROPE_DIM), (1, MLA_HEADS))
        kc_c, vc_c = _mla_cache_keys(cache_mla_ckv[:, i].reshape(dec_batch * past, KV_LORA), kr_t,
                                     lp["gains"], lp["w_ukv"], mats[1], mats[3], mats[4])
        r3 = lambda a: a.reshape(dec_batch, dec_seq, a.shape[-1])
        oa = _win_attn(i, sink, r3(qa), r3(ka), r3(va), cwk, cwv)
        ob = _nbr_attn(i, r3(qb), r3(kb), r3(vb), cnk, cnv, lp["nbr_table"])
        oc = _mla_attn(r3(qc), r3(kc), r3(vc), kc_c.reshape(dec_batch, past, 1024),
                       vc_c.reshape(dec_batch, past, 512))
        flat = lambda a: a.reshape(n_lat, 512)
        merged_l = merge(x_lat, flat(oa), flat(ob), flat(oc), mod_l, dec_seq, lp)

        x_ctx = _moe(i, *merged_c, mod_c, n_ctx, w_exp_gate, w_exp_up, w_exp_down)
        x_lat = _moe(i, *merged_l, mod_l, dec_seq, w_exp_gate, w_exp_up, w_exp_down)
    y_prompt = x_ctx.reshape(batch, seq, D_MODEL)
    y_sample = x_lat.reshape(dec_batch, dec_seq, D_MODEL)

    def stack(k, shape):
        return jnp.stack([s[k].reshape((batch, seq) + shape) for s in states], axis=1)

    return (y_prompt, y_sample,
            stack(0, (WIN_KV_HEADS, HEAD_DIM)), stack(1, (WIN_KV_HEADS, HEAD_DIM)),
            stack(2, (NBR_HEADS, HEAD_DIM)), stack(3, (NBR_HEADS, HEAD_DIM)),
            stack(4, (KV_LORA,)), stack(5, (ROPE_DIM,)))
```

```python
import functools

import numpy as np
import jax
import jax.numpy as jnp
from jax import lax
from jax.experimental import pallas as pl
from jax.experimental.pallas import tpu as pltpu
from jax.experimental.pallas import tpu_sc as plsc

D_MODEL = 1024
GRID_W = 64
HEAD_DIM = 64
WIN_HEADS = 8
WIN_KV_HEADS = 2
WINDOW = 128
NBR_HEADS = 8
NBR_ROWS = 8
NBR_COLS = 16
MLA_HEADS = 8
Q_LORA = 256
KV_LORA = 128
NOPE_DIM = 64
ROPE_DIM = 32
V_DIM = 64
QK_DIM = NOPE_DIM + ROPE_DIM
N_EXPERTS = 16
N_GROUPS = 4
EXPERTS_PER_GROUP = 4
D_FF = 512
ROPE_BASE = 10000.0
EPS = 1e-6

LANES = 128
LOG2E = 1.4426950408889634
NEG = -1e30
VMEM_LIMIT = 56 * 1024 * 1024

F32 = jnp.float32
BF16 = jnp.bfloat16

C_QA, C_KA, C_VA, C_QB, C_KB, C_VB, C_CQ, C_CKV, C_KR, C_END = (
    0, 512, 640, 768, 1280, 1792, 2304, 2560, 2688, 2944)
WIN_Q_ORDER = (0, 4, 1, 5, 2, 6, 3, 7)


def _cparams(sem):
    return pltpu.CompilerParams(dimension_semantics=sem, vmem_limit_bytes=VMEM_LIMIT)


def _dot(a, b):
    return jnp.dot(a, b, preferred_element_type=F32)


def _dot_nt(a, b):
    return lax.dot_general(a, b, (((1,), (1,)), ((), ())), preferred_element_type=F32)


def _split(x):
    hi = x.astype(BF16)
    lo = (x - hi.astype(F32)).astype(BF16)
    return hi, lo


def _gsum(x2, bmat):
    return _dot(x2.astype(BF16), bmat)


def _tile_lanes(t, width):
    reps = width // t.shape[-1]
    return t if reps == 1 else jnp.concatenate([t] * reps, axis=-1)


def _rotate(x, cos, sin_a, sin_b, half):
    w = x.shape[-1]
    up = pltpu.roll(x, w - half, 1)
    dn = pltpu.roll(x, half, 1)
    return (x * _tile_lanes(cos, w) + up * _tile_lanes(sin_a, w) + dn * _tile_lanes(sin_b, w))


def _norm_mod(x, gain, scale, shift):
    ms = jnp.mean(x * x, axis=-1, keepdims=True)
    return (x * lax.rsqrt(ms + EPS) * gain) * (1.0 + scale) + shift


def _ada_body(c_ref, w_ref, b_ref, o_ref):
    c = c_ref[...]
    a = c * (1.0 / (1.0 + jnp.exp(-c)))
    a_hi, a_lo = _split(a)
    w_hi, w_lo = _split(w_ref[0])
    o_ref[0] = _dot(a_hi, w_hi) + _dot(a_hi, w_lo) + _dot(a_lo, w_hi) + b_ref[0]


def _ada(c_all, w_ada, b_ada):
    depth = w_ada.shape[0]
    rows = c_all.shape[0]
    tn = 1536
    return pl.pallas_call(
        _ada_body,
        grid=(depth, 6 * D_MODEL // tn),
        in_specs=[pl.BlockSpec((rows, D_MODEL), lambda l, j: (0, 0)),
                  pl.BlockSpec((1, D_MODEL, tn), lambda l, j: (l, 0, j)),
                  pl.BlockSpec((1, 1, tn), lambda l, j: (l, 0, j))],
        out_specs=pl.BlockSpec((1, rows, tn), lambda l, j: (l, 0, j)),
        out_shape=jax.ShapeDtypeStruct((depth, rows, 6 * D_MODEL), F32),
        compiler_params=_cparams(("parallel", "parallel")),
        name="ada",
    )(c_all, w_ada, b_ada.reshape(depth, 1, 6 * D_MODEL))


def _mla_key_tail(ckvn_b, kr_t, g, wukv_ref, bnn, bnr, brr, rope_tabs, kc_ref, vc_ref):
    kv = _dot(ckvn_b, wukv_ref[...])
    kn = kv[:, 0:512]
    vc_ref[...] = kv[:, 512:1024].astype(BF16)
    kn2 = kn * kn
    kr2 = kr_t * kr_t
    kr_sum32 = _gsum(kr2, brr)
    ssn = (_gsum(kn2, bnn) + jnp.concatenate([kr_sum32, kr_sum32], axis=-1)) * (1.0 / QK_DIM)
    ssr = (_gsum(kn2, bnr) + kr_sum32) * (1.0 / QK_DIM)
    kn = kn * lax.rsqrt(ssn + EPS) * g[5:6, 0:512]
    kr = kr_t * lax.rsqrt(ssr + EPS) * g[5:6, 512:768]
    if rope_tabs is not None:
        kr = _rotate(kr, *rope_tabs, 8)
    for p in range(4):
        kc_ref[:, 256 * p:256 * p + 128] = kn[:, 128 * p:128 * p + 128].astype(BF16)
        q4 = 128 * (p // 2)
        kc_ref[:, 256 * p + 128:256 * p + 256] = kr[:, q4:q4 + 128].astype(BF16)


def _inproj_body(rope, states, *refs):
    (x_ref, mod_ref, g_ref, w_ref, wuq_ref, wukv_ref, b64_ref, bnn_ref, brn_ref, bnr_ref,
     brr_ref) = refs[:11]
    refs = refs[11:]
    if rope:
        tabs_w = tuple(r[...] for r in refs[0:3])
        tabs_m = tuple(r[...] for r in refs[3:6])
        refs = refs[6:]
    else:
        tabs_w = tabs_m = None
    qa_ref, ka_ref, va_ref, qb_ref, kb_ref, vb_ref, qc_ref, kc_ref, vc_ref = refs[:9]
    st = refs[9:]

    g = g_ref[...]
    mod = mod_ref[0]
    hb = _norm_mod(x_ref[...], g[0:1], mod[1:2], mod[0:1]).astype(BF16)

    def proj(a, b):
        return _dot(hb, w_ref[:, a:b])

    b64 = b64_ref[...]

    def head_norm(z, bmat, gain):
        return z * lax.rsqrt(_gsum(z * z, bmat) + EPS) * gain

    qa = head_norm(proj(C_QA, C_KA), b64, g[1:2, 0:512])
    ka = head_norm(proj(C_KA, C_VA), b64[0:128, 0:128], g[1:2, 512:640])
    va = proj(C_VA, C_QB)
    if states:
        st[0][...] = ka
        st[1][...] = va
    if rope:
        qa = _rotate(qa, *tabs_w, 16)
        ka = _rotate(ka, *tabs_w, 16)
    qa_ref[...] = qa.astype(BF16)
    ka_ref[...] = ka.astype(BF16)
    va_ref[...] = va.astype(BF16)

    qb = head_norm(proj(C_QB, C_KB), b64, g[2:3, 0:512])
    kb = head_norm(proj(C_KB, C_VB), b64, g[2:3, 512:1024])
    vb = proj(C_VB, C_CQ)
    if states:
        st[2][...] = kb
        st[3][...] = vb
    qb_ref[...] = qb.astype(BF16)
    kb_ref[...] = kb.astype(BF16)
    vb_ref[...] = vb.astype(BF16)

    cq = proj(C_CQ, C_CKV)
    cqn = cq * lax.rsqrt(jnp.mean(cq * cq, axis=-1, keepdims=True) + EPS) * g[3:4, 0:256]
    qq = _dot(cqn.astype(BF16), wuq_ref[...])
    qn, qr = qq[:, 0:512], qq[:, 512:768]
    qn2, qr2 = qn * qn, qr * qr
    bnn, brn, bnr, brr = bnn_ref[...], brn_ref[...], bnr_ref[...], brr_ref[...]
    ssn = (_gsum(qn2, bnn) + _gsum(qr2, brn)) * (1.0 / QK_DIM)
    ssr = (_gsum(qn2, bnr) + _gsum(qr2, brr)) * (1.0 / QK_DIM)
    qn = qn * lax.rsqrt(ssn + EPS) * g[4:5, 0:512]
    qr = qr * lax.rsqrt(ssr + EPS) * g[4:5, 512:768]
    if rope:
        qr = _rotate(qr, *tabs_m, 8)
    for p in range(4):
        qc_ref[:, 256 * p:256 * p + 128] = qn[:, 128 * p:128 * p + 128].astype(BF16)
        q4 = 128 * (p // 2)
        qc_ref[:, 256 * p + 128:256 * p + 256] = qr[:, q4:q4 + 128].astype(BF16)

    ckv = proj(C_CKV, C_KR)
    ckvn = ckv * lax.rsqrt(jnp.mean(ckv * ckv, axis=-1, keepdims=True) + EPS) * g[3:4, 256:384]
    kr_t = proj(C_KR, C_END)
    if states:
        st[4][...] = ckvn
        st[5][...] = kr_t[:, 0:ROPE_DIM]
    _mla_key_tail(ckvn.astype(BF16), kr_t, g, wukv_ref, bnn, bnr, brr, tabs_m, kc_ref, vc_ref)


def _const_spec(shape):
    nd = len(shape)
    return pl.BlockSpec(shape, lambda i, _nd=nd: (0,) * _nd, pipeline_mode=pl.Buffered(1))


def _inproj(x, mods, rows_per_mod, gains, w_a, w_uq, w_ukv, mats, rope_tabs, seq_len, states):
    n = x.shape[0]
    tm = min(1024, n)
    rope = rope_tabs is not None
    row = lambda w: pl.BlockSpec((tm, w), lambda i: (i, 0))
    in_specs = [row(D_MODEL),
                pl.BlockSpec((1, 8, D_MODEL), lambda i: ((i * tm) // rows_per_mod, 0, 0)),
                _const_spec(gains.shape), _const_spec(w_a.shape), _const_spec(w_uq.shape),
                _const_spec(w_ukv.shape)] + [_const_spec(m.shape) for m in mats]
    args = [x, mods, gains, w_a, w_uq, w_ukv, *mats]
    if rope:
        tiles_per_seq = seq_len // tm
        in_specs += [pl.BlockSpec((tm, LANES), lambda i: (i % tiles_per_seq, 0))] * 6
        args += list(rope_tabs)
    widths = [512, 128, 128, 512, 512, 512, 1024, 1024, 512]
    out_shape = [jax.ShapeDtypeStruct((n, w), BF16) for w in widths]
    out_specs = [row(w) for w in widths]
    if states:
        swidths = [128, 128, 512, 512, KV_LORA, ROPE_DIM]
        out_shape += [jax.ShapeDtypeStruct((n, w), F32) for w in swidths]
        out_specs += [row(w) for w in swidths]
    return pl.pallas_call(
        functools.partial(_inproj_body, rope, states),
        grid=(n // tm,), in_specs=in_specs, out_specs=out_specs, out_shape=out_shape,
        compiler_params=_cparams(("parallel",)),
        name="inproj_lat" if rope else "inproj_ctx",
    )(*args)


def _mla_cache_body(ckv_ref, kr_ref, g_ref, wukv_ref, bnn_ref, bnr_ref, brr_ref, kc_ref, vc_ref):
    _mla_key_tail(ckv_ref[...].astype(BF16), kr_ref[...], g_ref[...], wukv_ref, bnn_ref[...],
                  bnr_ref[...], brr_ref[...], None, kc_ref, vc_ref)


def _mla_cache_keys(ckv, kr_t, gains, w_ukv, bnn, bnr, brr):
    n = ckv.shape[0]
    tm = min(512, n)
    row = lambda w: pl.BlockSpec((tm, w), lambda i: (i, 0))
    return pl.pallas_call(
        _mla_cache_body,
        grid=(n // tm,),
        in_specs=[row(KV_LORA), row(256), _const_spec(gains.shape), _const_spec(w_ukv.shape),
                  _const_spec(bnn.shape), _const_spec(bnr.shape), _const_spec(brr.shape)],
        out_specs=[row(1024), row(512)],
        out_shape=[jax.ShapeDtypeStruct((n, 1024), BF16), jax.ShapeDtypeStruct((n, 512), BF16)],
        compiler_params=_cparams(("parallel",)),
        name="mla_cache_keys",
    )(ckv, kr_t, gains, w_ukv, bnn, bnr, brr)


def _lane_mask(width, ranges):
    lane = lax.broadcasted_iota(jnp.int32, (1, width), 1)
    m = None
    for lo, hi in ranges:
        c = (lane >= lo) & (lane < hi)
        m = c if m is None else (m | c)
    return jnp.where(m, 1.0, 0.0).astype(BF16)


def _stack_heads(q, mask0, mask1):
    return jnp.concatenate([q * mask0, q * mask1], axis=0)


def _lane_tiles(x):
    return [x[:, j:j + LANES] for j in range(0, x.shape[1], LANES)]


def _softmax_block(scores, sink=None):
    rows = scores[0].shape[0]
    mp = None
    for s in scores:
        for t in _lane_tiles(s):
            mp = t if mp is None else jnp.maximum(mp, t)
    base = sink if sink is not None else jnp.full((rows, LANES), NEG, F32)
    m = jnp.maximum(base, jnp.max(mp, axis=-1, keepdims=True))
    lp = None
    ps = []
    for s in scores:
        p = jnp.exp2(s - _tile_lanes(m, s.shape[1]))
        for t in _lane_tiles(p):
            lp = t if lp is None else lp + t
        ps.append(p.astype(BF16))
    if sink is not None:
        lane = lax.broadcasted_iota(jnp.int32, (rows, LANES), 1)
        lp = lp + jnp.where(lane == 0, jnp.exp2(sink - m), 0.0)
    p_all = ps[0] if len(ps) == 1 else jnp.concatenate(ps, axis=-1)
    return p_all, jnp.broadcast_to(jnp.sum(lp, axis=-1, keepdims=True), (rows, LANES))


def _softmax_pv(scores, values, sink=None):
    p_all, l = _softmax_block(scores, sink)
    v_all = values[0] if len(values) == 1 else jnp.concatenate(values, axis=0)
    return _dot(p_all, v_all) / l


def _merge_heads(o, tq):
    lane = lax.broadcasted_iota(jnp.int32, (tq, LANES), 1)
    return jnp.where(lane < HEAD_DIM, o[0:tq], o[tq:2 * tq])


def _mla_masks(p_mod2):
    lane = lax.broadcasted_iota(jnp.int32, (1, 256), 1)
    r0 = 128 + 32 * (2 * p_mod2)
    m0 = (lane < 64) | ((lane >= r0) & (lane < r0 + 32))
    m1 = ((lane >= 64) & (lane < 128)) | ((lane >= r0 + 32) & (lane < r0 + 64))
    return (jnp.where(m0, 1.0, 0.0).astype(BF16), jnp.where(m1, 1.0, 0.0).astype(BF16))


def _sink_col(sink_ref, layer, h0, h1, tq):
    row = lax.broadcasted_iota(jnp.int32, (2 * tq, LANES), 0)
    return jnp.where(row < tq, sink_ref[layer, h0], sink_ref[layer, h1])


def _ctx_attn_body(layer, sink_ref, qa_ref, ka_ref, va_ref, qb_ref, kb_ref, vb_ref, qc_ref, kc_ref,
                   vc_ref, oa_ref, ob_ref, oc_ref):
    tq = qa_ref.shape[0]
    lo = _lane_mask(LANES, [(0, 64)])
    hi = _lane_mask(LANES, [(64, 128)])
    ka, va = ka_ref[...], va_ref[...]
    for j in range(4):
        sl = slice(128 * j, 128 * j + 128)
        qs = _stack_heads(qa_ref[:, sl], lo, hi)
        sink = _sink_col(sink_ref, layer, j, 4 + j, tq)
        o = _softmax_pv([_dot_nt(qs, ka)], [va], sink)
        oa_ref[:, sl] = _merge_heads(o, tq).astype(BF16)

        qs = _stack_heads(qb_ref[:, sl], lo, hi)
        o = _softmax_pv([_dot_nt(qs, kb_ref[:, sl])], [vb_ref[:, sl]])
        ob_ref[:, sl] = _merge_heads(o, tq).astype(BF16)

        m0, m1 = _mla_masks(j % 2)
        s2 = slice(256 * j, 256 * j + 256)
        qs = _stack_heads(qc_ref[:, s2], m0, m1)
        o = _softmax_pv([_dot_nt(qs, kc_ref[:, s2])], [vc_ref[:, sl]])
        oc_ref[:, sl] = _merge_heads(o, tq).astype(BF16)


def _ctx_attn(layer, sink, seq, qa, ka, va, qb, kb, vb, qc, kc, vc):
    n = qa.shape[0]
    row = lambda w: pl.BlockSpec((seq, w), lambda b: (b, 0))
    ins = [qa, ka, va, qb, kb, vb, qc, kc, vc]
    return pl.pallas_call(
        functools.partial(_ctx_attn_body, layer),
        grid=(n // seq,),
        in_specs=[pl.BlockSpec(memory_space=pltpu.SMEM)] + [row(a.shape[1]) for a in ins],
        out_specs=[row(512)] * 3,
        out_shape=[jax.ShapeDtypeStruct((n, 512), BF16)] * 3,
        compiler_params=_cparams(("parallel",)),
        name="ctx_attn",
    )(sink, *ins)


WIN_ROW_BLOCK = 32


WIN_Q_BLOCK = 128
WIN_BLOCKS_PER_STEP = 4


def _win_body(layer, sink_ref, q_ref, k_ref, v_ref, kc_ref, vc_ref, o_ref, s_sc, p_sc, l_sc):
    tq = WIN_Q_BLOCK
    seq = k_ref.shape[1]
    kw = 3 * tq
    rb = WIN_ROW_BLOCK
    lo = _lane_mask(LANES, [(0, 64)])
    hi = _lane_mask(LANES, [(64, 128)])
    kc, vc = kc_ref[0, 0], vc_ref[0, 0]
    for u in range(q_ref.shape[1] // tq):
        i = pl.program_id(1) * (q_ref.shape[1] // tq) + u
        kstart = pl.multiple_of(jnp.clip((i - 1) * tq, 0, seq - kw), tq)
        k_all = jnp.concatenate([k_ref[0, pl.ds(kstart, kw), :], kc], axis=0)
        v_all = jnp.concatenate([v_ref[0, pl.ds(kstart, kw), :], vc], axis=0)
        q_pos = i * tq + lax.broadcasted_iota(jnp.int32, (tq, kw), 0)
        k_pos = kstart + lax.broadcasted_iota(jnp.int32, (tq, kw), 1)
        band = jnp.abs(q_pos - k_pos) <= WINDOW
        qs = jnp.concatenate(
            [_stack_heads(q_ref[0, u * tq:(u + 1) * tq, 128 * j:128 * j + 128], lo, hi) for j in range(4)],
            axis=0)
        s_sc[u] = _dot_nt(qs, k_all)
        for j in range(4):
            for r in range(2 * tq * j, 2 * tq * (j + 1), rb):
                head = j if r < 2 * tq * j + tq else 4 + j
                q0 = r % tq
                s_band = jnp.where(band[q0:q0 + rb], s_sc[u, r:r + rb, 0:kw], NEG)
                sink = jnp.full((rb, LANES), sink_ref[layer, head], F32)
                p, l = _softmax_block([s_band, s_sc[u, r:r + rb, kw:]], sink)
                p_sc[u, r:r + rb, :] = p
                l_sc[u, r:r + rb, :] = l
            rows = slice(2 * tq * j, 2 * tq * (j + 1))
            o = _dot(p_sc[u, rows, :], v_all) / l_sc[u, rows, :]
            o_ref[0, u * tq:(u + 1) * tq, 128 * j:128 * j + 128] = _merge_heads(o, tq).astype(BF16)


def _win_attn(layer, sink, q, k, v, kc, vc):
    b, seq, _ = q.shape
    nb = min(WIN_BLOCKS_PER_STEP, seq // WIN_Q_BLOCK)
    tq = nb * WIN_Q_BLOCK
    past = kc.shape[2]
    keys = 3 * WIN_Q_BLOCK + past
    return pl.pallas_call(
        functools.partial(_win_body, layer),
        grid=(b, seq // tq),
        in_specs=[pl.BlockSpec(memory_space=pltpu.SMEM),
                  pl.BlockSpec((1, tq, 512), lambda bi, i: (bi, i, 0)),
                  pl.BlockSpec((1, seq, 128), lambda bi, i: (bi, 0, 0)),
                  pl.BlockSpec((1, seq, 128), lambda bi, i: (bi, 0, 0)),
                  pl.BlockSpec((1, 1, past, 128), lambda bi, i: (bi, layer, 0, 0)),
                  pl.BlockSpec((1, 1, past, 128), lambda bi, i: (bi, layer, 0, 0))],
        out_specs=pl.BlockSpec((1, tq, 512), lambda bi, i: (bi, i, 0)),
        out_shape=jax.ShapeDtypeStruct((b, seq, 512), BF16),
        scratch_shapes=[pltpu.VMEM((nb, 8 * WIN_Q_BLOCK, keys), F32),
                        pltpu.VMEM((nb, 8 * WIN_Q_BLOCK, keys), BF16),
                        pltpu.VMEM((nb, 8 * WIN_Q_BLOCK, LANES), F32)],
        compiler_params=_cparams(("parallel", "arbitrary")),
        name="win_attn",
    )(sink, q, k, v, kc, vc)


NBR_TILE_ROWS = 4
NBR_WIN_ROWS = NBR_TILE_ROWS + NBR_ROWS
NBR_TAB_PAD = NBR_WIN_ROWS - NBR_ROWS
NBR_ROW_BLOCK = 32
NBR_TILES_PER_STEP = 2


def _nbr_body(rows, q_ref, k_ref, v_ref, kc_ref, vc_ref, tab_ref, o_ref, s_sc, p_sc, l_sc):
    tq = NBR_TILE_ROWS * GRID_W
    kw = NBR_WIN_ROWS * GRID_W
    lo = _lane_mask(LANES, [(0, 64)])
    hi = _lane_mask(LANES, [(64, 128)])
    past = kc_ref.shape[2]
    rb = NBR_ROW_BLOCK
    tiles = q_ref.shape[1] // tq
    for u in range(tiles):
        r0 = NBR_TILE_ROWS * (pl.program_id(1) * tiles + u)
        ws = jnp.clip(r0 - NBR_ROWS // 2, 0, rows - NBR_WIN_ROWS)
        kstart = pl.multiple_of(ws * GRID_W, LANES)
        k_row = ws + lax.broadcasted_iota(jnp.int32, (1, kw), 1) // GRID_W
        for j in range(4):
            sl = slice(128 * j, 128 * j + 128)
            qs = _stack_heads(q_ref[0, u * tq:(u + 1) * tq, sl], lo, hi)
            s_sc[u, j, :, 0:kw] = _dot_nt(qs, k_ref[0, pl.ds(kstart, kw), sl])
            s_sc[u, j, :, kw:kw + past] = _dot_nt(qs, kc_ref[0, 0, :, sl])
        for j in range(4):
            sl = slice(128 * j, 128 * j + 128)
            for b0 in range(0, 2 * tq, rb):
                h = 2 * j + b0 // tq
                ql, sub = divmod(b0 % tq, GRID_W)
                d0 = ws - r0 - ql + (NBR_ROWS - 1) + NBR_TAB_PAD
                bias = jnp.concatenate([tab_ref[h, d0 + 2 * m, sub:sub + rb, :]
                                        for m in range(NBR_WIN_ROWS // 2)], axis=-1)
                rs = jnp.clip(r0 + ql - NBR_ROWS // 2, 0, rows - NBR_ROWS)
                valid = (k_row >= rs) & (k_row < rs + NBR_ROWS)
                s_nb = jnp.where(valid, s_sc[u, j, b0:b0 + rb, 0:kw] + bias, NEG)
                p, l = _softmax_block([s_nb, s_sc[u, j, b0:b0 + rb, kw:kw + past]])
                p_sc[u, j, b0:b0 + rb, :] = p
                l_sc[u, j, b0:b0 + rb, :] = l
            v_all = jnp.concatenate([v_ref[0, pl.ds(kstart, kw), sl], vc_ref[0, 0, :, sl]], axis=0)
            o = _dot(p_sc[u, j], v_all) / l_sc[u, j]
            o_ref[0, u * tq:(u + 1) * tq, sl] = _merge_heads(o, tq).astype(BF16)


def _nbr_attn(layer, q, k, v, kc, vc, table):
    b, seq, _ = q.shape
    rows = seq // GRID_W
    tile = NBR_TILE_ROWS * GRID_W
    tq = NBR_TILES_PER_STEP * tile
    past = kc.shape[2]
    keys = NBR_WIN_ROWS * GRID_W + past
    once = pl.Buffered(1)
    return pl.pallas_call(
        functools.partial(_nbr_body, rows),
        grid=(b, seq // tq),
        in_specs=[pl.BlockSpec((1, tq, 512), lambda bi, i: (bi, i, 0)),
                  pl.BlockSpec((1, seq, 512), lambda bi, i: (bi, 0, 0), pipeline_mode=once),
                  pl.BlockSpec((1, seq, 512), lambda bi, i: (bi, 0, 0), pipeline_mode=once),
                  pl.BlockSpec((1, 1, past, 512), lambda bi, i: (bi, layer, 0, 0), pipeline_mode=once),
                  pl.BlockSpec((1, 1, past, 512), lambda bi, i: (bi, layer, 0, 0), pipeline_mode=once),
                  pl.BlockSpec(table.shape, lambda bi, i: (0, 0, 0, 0), pipeline_mode=once)],
        out_specs=pl.BlockSpec((1, tq, 512), lambda bi, i: (bi, i, 0)),
        out_shape=jax.ShapeDtypeStruct((b, seq, 512), BF16),
        scratch_shapes=[pltpu.VMEM((NBR_TILES_PER_STEP, 4, 2 * tile, keys), F32),
                        pltpu.VMEM((NBR_TILES_PER_STEP, 4, 2 * tile, keys), BF16),
                        pltpu.VMEM((NBR_TILES_PER_STEP, 4, 2 * tile, LANES), F32)],
        compiler_params=_cparams(("parallel", "arbitrary")),
        name="nbr_attn",
    )(q, k, v, kc, vc, table)


def _nbr_bias_table(rel_bias):
    col = np.arange(GRID_W)
    cs = np.clip(col - NBR_COLS // 2, 0, GRID_W - NBR_COLS)
    kc = np.arange(GRID_W)
    ok = (kc[None, :] >= cs[:, None]) & (kc[None, :] < cs[:, None] + NBR_COLS)
    dc = kc[None, :] - col[:, None] + (NBR_COLS - 1)
    pick = (dc[:, :, None] == np.arange(2 * NBR_COLS - 1)[None, None, :]) & ok[:, :, None]
    t = jnp.einsum("hdk,qck->hdqc", rel_bias.astype(F32) * LOG2E, jnp.asarray(pick, F32),
                   precision=lax.Precision.HIGHEST)
    t = jnp.where(jnp.asarray(ok)[None, None], t, NEG)
    t = jnp.pad(t, ((0, 0), (NBR_TAB_PAD, NBR_TAB_PAD), (0, 0), (0, 0)))
    return jnp.concatenate([t[:, :-1], t[:, 1:]], axis=-1)


MLA_Q_TILE = 1024
MLA_KEY_CHUNK = 512
MLA_ROW_BLOCK = 64


def _mla_body(q_ref, kl_ref, kc_ref, vl_ref, vc_ref, o_ref, qs_sc, m_sc, l_sc, acc_sc, s_sc, p_sc, a_sc):
    tq = q_ref.shape[1]
    rows = 2 * tq
    seq = kl_ref.shape[1]
    past = kc_ref.shape[1]
    tk = min(MLA_KEY_CHUNK, seq)
    rb = MLA_ROW_BLOCK
    m0, m1 = _mla_masks(pl.program_id(1) % 2)
    qs_sc[...] = _stack_heads(q_ref[0], m0, m1)
    m_sc[...] = jnp.full(m_sc.shape, NEG, F32)
    l_sc[...] = jnp.zeros(l_sc.shape, F32)
    acc_sc[...] = jnp.zeros(acc_sc.shape, F32)

    def step(c, k, v):
        slab = c % 2
        n = k.shape[0]
        s_sc[slab, :, 0:n] = _dot_nt(qs_sc[...], k)
        for r in range(0, rows, rb):
            sl = slice(r, r + rb)
            sb = s_sc[slab, sl, 0:n]
            mp = None
            for t in _lane_tiles(sb):
                mp = t if mp is None else jnp.maximum(mp, t)
            m_prev = m_sc[sl]
            m_new = jnp.maximum(m_prev, jnp.max(mp, axis=-1, keepdims=True))
            alpha = jnp.exp2(m_prev - m_new)
            p = jnp.exp2((sb - _tile_lanes(m_new, n)).astype(BF16))
            psum = None
            for t in _lane_tiles(p):
                t = t.astype(F32)
                psum = t if psum is None else psum + t
            l_sc[sl] = alpha * l_sc[sl] + psum
            m_sc[sl] = m_new
            a_sc[slab, sl, :] = alpha
            p_sc[slab, sl, 0:n] = p
        acc_sc[...] = a_sc[slab] * acc_sc[...] + _dot(p_sc[slab, :, 0:n], v)

    chunks = [(kl_ref, vl_ref, o, min(tk, seq - o)) for o in range(0, seq, tk)]
    chunks += [(kc_ref, vc_ref, o, min(tk, past - o)) for o in range(0, past, tk)]
    for c, (k_ref, v_ref, o, n) in enumerate(chunks):
        step(c, k_ref[0, o:o + n, :], v_ref[0, o:o + n, :])
    l = jnp.sum(l_sc[...], axis=-1, keepdims=True)
    o_ref[0] = _merge_heads(acc_sc[...] / l, tq).astype(BF16)


def _mla_attn(q, kl, vl, kc, vc):
    b, seq, _ = q.shape
    past = kc.shape[1]
    tq = min(MLA_Q_TILE, seq)
    tk = min(MLA_KEY_CHUNK, seq)
    return pl.pallas_call(
        _mla_body,
        grid=(b, 4, seq // tq),
        in_specs=[pl.BlockSpec((1, tq, 256), lambda bi, p, qi: (bi, qi, p)),
                  pl.BlockSpec((1, seq, 256), lambda bi, p, qi: (bi, 0, p)),
                  pl.BlockSpec((1, past, 256), lambda bi, p, qi: (bi, 0, p)),
                  pl.BlockSpec((1, seq, 128), lambda bi, p, qi: (bi, 0, p)),
                  pl.BlockSpec((1, past, 128), lambda bi, p, qi: (bi, 0, p))],
        out_specs=pl.BlockSpec((1, tq, 128), lambda bi, p, qi: (bi, qi, p)),
        out_shape=jax.ShapeDtypeStruct((b, seq, 512), BF16),
        scratch_shapes=[pltpu.VMEM((2 * tq, 256), BF16), pltpu.VMEM((2 * tq, LANES), F32),
                        pltpu.VMEM((2 * tq, LANES), F32), pltpu.VMEM((2 * tq, LANES), F32),
                        pltpu.VMEM((2, 2 * tq, tk), F32), pltpu.VMEM((2, 2 * tq, tk), BF16),
                        pltpu.VMEM((2, 2 * tq, LANES), F32)],
        compiler_params=_cparams(("parallel", "parallel", "arbitrary")),
        name="mla_attn",
    )(q, kl, kc, vl, vc)


def _pack_pairs(x):
    w = x.shape[1] // 2
    hi = lax.bitcast_convert_type(x[:, :w].astype(BF16).astype(F32), jnp.int32)
    lo = lax.bitcast_convert_type(x[:, w:].astype(BF16).astype(F32), jnp.int32)
    return (hi & jnp.int32(-65536)) | lax.shift_right_logical(lo, jnp.int32(16))


def _unpack_pairs(p):
    hi = lax.bitcast_convert_type(p & jnp.int32(-65536), F32)
    lo = lax.bitcast_convert_type(lax.shift_left(p, jnp.int32(16)), F32)
    return jnp.concatenate([hi, lo], axis=-1)


def _merge_body(x_ref, oa_ref, ob_ref, oc_ref, mod_ref, g_ref, wg_ref, woa_ref, wob_ref, woc_ref,
                wout_ref, wr_ref, br_ref, tri_ref, x1_ref, route_ref, h2p_ref, count_ref, count_sc):
    x = x_ref[...]
    g = g_ref[...]
    mod = mod_ref[0]
    hb = _norm_mod(x, g[0:1], mod[1:2], mod[0:1]).astype(BF16)
    m = None
    for br, (o_ref, wo_ref) in enumerate(((oa_ref, woa_ref), (ob_ref, wob_ref), (oc_ref, woc_ref))):
        z = _dot(hb, wg_ref[:, D_MODEL * br:D_MODEL * (br + 1)])
        gate = 1.0 / (1.0 + jnp.exp(-z))
        t = gate * _dot(o_ref[...], wo_ref[...])
        m = t if m is None else m + t
    y = _dot(m.astype(BF16), wout_ref[...])
    x1 = x + mod[2:3] * y
    x1_ref[...] = x1

    h2 = _norm_mod(x1, g[6:7], mod[4:5], mod[3:4])
    h_hi, h_lo = _split(h2)
    w_hi, w_lo = _split(wr_ref[...])
    both_w = _dot_nt(jnp.concatenate([w_hi, w_lo], axis=0), h_hi)
    logits = both_w[0:N_EXPERTS] + both_w[N_EXPERTS:2 * N_EXPERTS] + _dot_nt(w_hi, h_lo)
    score = 1.0 / (1.0 + jnp.exp(-logits))
    sel = score + br_ref[...]
    sel_r = [sel[e:e + 1] for e in range(N_EXPERTS)]
    sc_r = [score[e:e + 1] for e in range(N_EXPERTS)]
    picked = []
    for e in range(N_EXPERTS):
        grp, a = divmod(e, EXPERTS_PER_GROUP)
        rank = None
        for bb in range(EXPERTS_PER_GROUP):
            if bb == a:
                continue
            o = sel_r[grp * EXPERTS_PER_GROUP + bb]
            beats = (o >= sel_r[e]) if bb < a else (o > sel_r[e])
            r = jnp.where(beats, 1.0, 0.0)
            rank = r if rank is None else rank + r
        picked.append(rank < 2.0)
    gscore = []
    for grp in range(N_GROUPS):
        tot = None
        for a in range(EXPERTS_PER_GROUP):
            e = grp * EXPERTS_PER_GROUP + a
            t = jnp.where(picked[e], sel_r[e], 0.0)
            tot = t if tot is None else tot + t
        gscore.append(tot)
    best = jnp.zeros_like(gscore[0])
    best_v = gscore[0]
    for grp in range(1, N_GROUPS):
        upd = gscore[grp] > best_v
        best = jnp.where(upd, float(grp), best)
        best_v = jnp.where(upd, gscore[grp], best_v)
    cw, pk = [], []
    for a in range(EXPERTS_PER_GROUP):
        tot = flag = None
        for grp in range(N_GROUPS):
            e = grp * EXPERTS_PER_GROUP + a
            f = (best == float(grp)) & picked[e]
            t = jnp.where(f, sc_r[e], 0.0)
            tot = t if tot is None else tot + t
            flag = f if flag is None else (flag | f)
        cw.append(tot)
        pk.append(flag)
    den = cw[0] + cw[1] + cw[2] + cw[3]
    first = jnp.where(pk[0], 0.0, jnp.where(pk[1], 1.0, jnp.where(pk[2], 2.0, 3.0)))
    second = jnp.where(pk[3], 3.0, jnp.where(pk[2], 2.0, jnp.where(pk[1], 1.0, 0.0)))
    slot_e, slot_w = [], []
    for which in (first, second):
        tot = None
        for a in range(EXPERTS_PER_GROUP):
            t = jnp.where(which == float(a), cw[a], 0.0)
            tot = t if tot is None else tot + t
        slot_w.append(tot / den)
        slot_e.append(best * float(EXPERTS_PER_GROUP) + which)

    @pl.when(pl.program_id(0) == 0)
    def _():
        count_sc[...] = jnp.zeros(count_sc.shape, F32)

    tm = x.shape[0]
    eid = lax.broadcasted_iota(jnp.int32, (N_EXPERTS, tm), 0).astype(F32)
    oh = [eid == slot_e[0], eid == slot_e[1]]
    both = jnp.where(oh[0] | oh[1], 1.0, 0.0)
    seen = count_sc[...][:, 0:1] + _dot(both.astype(BF16), tri_ref[...])
    for k in range(2):
        route_ref[k:k + 1, :] = slot_w[k]
        route_ref[2 + k:3 + k, :] = slot_e[k]
        route_ref[4 + k:5 + k, :] = jnp.sum(jnp.where(oh[k], seen, 0.0), axis=0, keepdims=True)
    route_ref[6:8, :] = jnp.zeros((2, tm), F32)
    count_sc[...] = count_sc[...] + jnp.sum(both, axis=-1, keepdims=True)
    count_ref[...] = count_sc[...]
    h2p_ref[...] = _pack_pairs(h2)


def _merge(x, oa, ob, oc, mods, rows_per_mod, gains, w_gate, wo_a, wo_b, wo_c, w_out, w_r_t, b_r):
    n = x.shape[0]
    tm = min(1024, n)
    row = lambda w: pl.BlockSpec((tm, w), lambda i: (i, 0))
    tri = jnp.asarray(np.triu(np.ones((tm, tm), np.float32), 1), BF16)
    consts = [gains, w_gate, wo_a, wo_b, wo_c, w_out, w_r_t, b_r, tri]
    return pl.pallas_call(
        _merge_body,
        grid=(n // tm,),
        in_specs=[row(D_MODEL), row(512), row(512), row(512),
                  pl.BlockSpec((1, 8, D_MODEL), lambda i: ((i * tm) // rows_per_mod, 0, 0))]
                 + [_const_spec(c.shape) for c in consts],
        out_specs=[row(D_MODEL), pl.BlockSpec((8, tm), lambda i: (0, i)), row(512),
                   pl.BlockSpec((N_EXPERTS, LANES), lambda i: (0, 0))],
        out_shape=[jax.ShapeDtypeStruct((n, D_MODEL), F32), jax.ShapeDtypeStruct((8, n), F32),
                   jax.ShapeDtypeStruct((n, 512), jnp.int32),
                   jax.ShapeDtypeStruct((N_EXPERTS, LANES), F32)],
        scratch_shapes=[pltpu.VMEM((N_EXPERTS, LANES), F32)],
        compiler_params=_cparams(("arbitrary",)),
        name="merge",
    )(x, oa, ob, oc, mods, *consts)


EXPERT_TILE = 512
SC_CORES = 2
SC_SUBCORES = 16
SC_WORKERS = SC_CORES * SC_SUBCORES
SC_WINDOW = 128


def _sc_mesh():
    return plsc.VectorSubcoreMesh(core_axis_name="c", subcore_axis_name="s", num_cores=SC_CORES,
                                  num_subcores=SC_SUBCORES)


def _sc_window_base(steps, j):
    wid = lax.axis_index("s") * SC_CORES + lax.axis_index("c")
    return pl.multiple_of((wid * steps + j) * SC_WINDOW, SC_WINDOW)


def _sc_dispatch(rows, pos0, pos1, n_out):
    n, w = rows.shape
    steps = n // (SC_WORKERS * SC_WINDOW)

    @functools.partial(
        pl.kernel, out_type=jax.ShapeDtypeStruct((n_out, w), rows.dtype), mesh=_sc_mesh(),
        scratch_types=[pltpu.VMEM((SC_WINDOW,), jnp.int32), pltpu.VMEM((SC_WINDOW,), jnp.int32),
                       pltpu.VMEM((SC_WINDOW, w), rows.dtype)],
        name="moe_dispatch")
    def run(x_hbm, i0_hbm, i1_hbm, o_hbm, i0_v, i1_v, rows_v):
        @pl.loop(0, steps)
        def _(j):
            base = _sc_window_base(steps, j)
            pltpu.sync_copy(i0_hbm.at[pl.ds(base, SC_WINDOW)], i0_v)
            pltpu.sync_copy(i1_hbm.at[pl.ds(base, SC_WINDOW)], i1_v)
            pltpu.sync_copy(x_hbm.at[pl.ds(base, SC_WINDOW)], rows_v)
            pltpu.sync_copy(rows_v, o_hbm.at[i0_v])
            pltpu.sync_copy(rows_v, o_hbm.at[i1_v])

    return run(rows, pos0, pos1)


def _sc_collect(rows, pos0, pos1):
    n = pos0.shape[0]
    w = rows.shape[1]
    steps = n // (SC_WORKERS * SC_WINDOW)
    out = jax.ShapeDtypeStruct((n, w), rows.dtype)

    @functools.partial(
        pl.kernel, out_type=[out, out], mesh=_sc_mesh(),
        scratch_types=[pltpu.VMEM((SC_WINDOW,), jnp.int32), pltpu.VMEM((SC_WINDOW, w), rows.dtype)],
        name="moe_collect")
    def run(y_hbm, i0_hbm, i1_hbm, o0_hbm, o1_hbm, i_v, rows_v):
        @pl.loop(0, steps)
        def _(j):
            base = _sc_window_base(steps, j)
            for i_hbm, o_hbm in ((i0_hbm, o0_hbm), (i1_hbm, o1_hbm)):
                pltpu.sync_copy(i_hbm.at[pl.ds(base, SC_WINDOW)], i_v)
                pltpu.sync_copy(y_hbm.at[i_v], rows_v)
                pltpu.sync_copy(rows_v, o_hbm.at[pl.ds(base, SC_WINDOW)])

    return run(rows, pos0, pos1)


def _experts_body(te_ref, nv_ref, xs_ref, wg_ref, wu_ref, wd_ref, ys_ref, wg_sc, wu_sc, wd_sc):
    j = pl.program_id(0)

    @pl.when((j == 0) | (te_ref[j] != te_ref[jnp.maximum(j - 1, 0)]))
    def _():
        wg_sc[...] = wg_ref[0, 0].astype(BF16)
        wu_sc[...] = wu_ref[0, 0].astype(BF16)
        wd_sc[...] = wd_ref[0, 0].astype(BF16)

    @pl.when(j < nv_ref[0])
    def _():
        x = _unpack_pairs(xs_ref[...]).astype(BF16)
        zg = _dot(x, wg_sc[...])
        act = zg * (1.0 / (1.0 + jnp.exp(-zg))) * _dot(x, wu_sc[...])
        ys_ref[...] = _pack_pairs(_dot(act.astype(BF16), wd_sc[...]))

    @pl.when(j >= nv_ref[0])
    def _():
        ys_ref[...] = jnp.zeros(ys_ref.shape, ys_ref.dtype)


def _experts(layer, xs, tile_expert, n_valid, wg, wu, wd):
    p = xs.shape[0]
    wspec = lambda r, c: pl.BlockSpec((1, 1, r, c), lambda j, te, nv: (layer, te[j], 0, 0))
    grid_spec = pltpu.PrefetchScalarGridSpec(
        num_scalar_prefetch=2,
        grid=(p // EXPERT_TILE,),
        in_specs=[pl.BlockSpec((EXPERT_TILE, 512), lambda j, te, nv: (j, 0)),
                  wspec(D_MODEL, D_FF), wspec(D_MODEL, D_FF), wspec(D_FF, D_MODEL)],
        out_specs=pl.BlockSpec((EXPERT_TILE, 512), lambda j, te, nv: (j, 0)),
        scratch_shapes=[pltpu.VMEM((D_MODEL, D_FF), BF16), pltpu.VMEM((D_MODEL, D_FF), BF16),
                        pltpu.VMEM((D_FF, D_MODEL), BF16)])
    return pl.pallas_call(
        _experts_body, grid_spec=grid_spec,
        out_shape=jax.ShapeDtypeStruct((p, 512), jnp.int32),
        compiler_params=_cparams(("arbitrary",)),
        name="experts",
    )(tile_expert, n_valid, xs, wg, wu, wd)


def _combine_body(x1_ref, y0_ref, y1_ref, route_ref, mod_ref, o_ref):
    tm = x1_ref.shape[0]
    eye = jnp.where(lax.broadcasted_iota(jnp.int32, (tm, tm), 0)
                    == lax.broadcasted_iota(jnp.int32, (tm, tm), 1), 1.0, 0.0).astype(BF16)
    r_hi, r_lo = _split(route_ref[...])
    wcol = _dot_nt(eye, r_hi) + _dot_nt(eye, r_lo)
    moe = wcol[:, 0:1] * _unpack_pairs(y0_ref[...]) + wcol[:, 1:2] * _unpack_pairs(y1_ref[...])
    o_ref[...] = x1_ref[...] + mod_ref[0][5:6] * moe


def _combine(x1, y0, y1, route, mods, rows_per_mod):
    n = x1.shape[0]
    tm = min(512, n)
    row = lambda w: pl.BlockSpec((tm, w), lambda i: (i, 0))
    return pl.pallas_call(
        _combine_body,
        grid=(n // tm,),
        in_specs=[row(D_MODEL), row(512), row(512), pl.BlockSpec((8, tm), lambda i: (0, i)),
                  pl.BlockSpec((1, 8, D_MODEL), lambda i: ((i * tm) // rows_per_mod, 0, 0))],
        out_specs=row(D_MODEL),
        out_shape=jax.ShapeDtypeStruct((n, D_MODEL), F32),
        compiler_params=_cparams(("parallel",)),
        name="combine",
    )(x1, y0, y1, route, mods)


def _moe(layer, x1, route, h2p, counts, mods, rows_per_mod, wg, wu, wd):
    n = x1.shape[0]
    p = 2 * n + N_EXPERTS * EXPERT_TILE
    cnt = counts[:, 0].astype(jnp.int32)
    padded = (cnt + EXPERT_TILE - 1) // EXPERT_TILE * EXPERT_TILE
    seg_end = jnp.cumsum(padded)
    seg_off = seg_end - padded
    experts = jnp.arange(N_EXPERTS, dtype=jnp.int32)

    def position(k):
        e = route[2 + k].astype(jnp.int32)
        off = jnp.sum(jnp.where(e[:, None] == experts[None], seg_off[None], 0), axis=1)
        return off + route[4 + k].astype(jnp.int32)

    pos0, pos1 = position(0), position(1)
    tile_start = jnp.arange(p // EXPERT_TILE, dtype=jnp.int32) * EXPERT_TILE
    tile_expert = jnp.sum(tile_start[:, None] >= seg_end[None], axis=1).astype(jnp.int32)
    tile_expert = jnp.minimum(tile_expert, N_EXPERTS - 1)
    n_valid = (seg_end[-1:] // EXPERT_TILE).astype(jnp.int32)

    xs = _sc_dispatch(h2p, pos0, pos1, p)
    ys = _experts(layer, xs, tile_expert, n_valid, wg, wu, wd)
    y0, y1 = _sc_collect(ys, pos0, pos1)
    return _combine(x1, y0, y1, route, mods, rows_per_mod)


def _block_ones(n_in, g_in, n_out, g_out, value=1.0):
    r = np.arange(n_in)[:, None] // g_in
    c = np.arange(n_out)[None, :] // g_out
    return jnp.asarray(np.where(r == c, value, 0.0), dtype=BF16)


def _rope_tables(seq, head_w):
    pos = np.arange(seq)
    rows, cols = pos // GRID_W, pos % GRID_W
    a = head_w // 2
    half = a // 2
    freqs = (ROPE_BASE ** (-np.arange(half, dtype=np.float32) / half)).astype(np.float32)
    lane = np.arange(LANES) % head_w
    within = lane % a
    first = within < half
    p = np.where((lane // a == 0)[None, :], rows[:, None], cols[:, None]).astype(np.float32)
    ang = (p * freqs[within % half][None, :]).astype(np.float32)
    cos, sin = np.cos(ang), np.sin(ang)
    return (jnp.asarray(cos, F32), jnp.asarray(np.where(first[None], -sin, 0.0), F32),
            jnp.asarray(np.where(first[None], 0.0, sin), F32))


def _tile_to(v, width):
    return jnp.tile(v, width // v.shape[0])


def _layer_params(i, p):
    w_in = p["w_in"][i]
    sp = np.cumsum((512, 128, 128, 512, 512, 512, Q_LORA, KV_LORA, ROPE_DIM))
    qa, ka, va, qb, kb, vb, cq, ckv, kr, gates = jnp.split(w_in, [int(s) for s in sp], axis=1)
    qa = qa.reshape(D_MODEL, WIN_HEADS, HEAD_DIM)[:, WIN_Q_ORDER, :].reshape(D_MODEL, 512)
    w_a = jnp.concatenate([qa, ka, va, qb, kb, vb, cq, ckv, jnp.tile(kr, (1, MLA_HEADS))],
                          axis=1).astype(BF16)
    w_uq = p["w_uq"][i].reshape(Q_LORA, MLA_HEADS, QK_DIM)
    w_uq = jnp.concatenate([w_uq[:, :, :NOPE_DIM].reshape(Q_LORA, 512),
                            w_uq[:, :, NOPE_DIM:].reshape(Q_LORA, 256)], axis=1).astype(BF16)
    w_ukv = p["w_ukv"][i].reshape(KV_LORA, MLA_HEADS, NOPE_DIM + V_DIM)
    w_ukv = jnp.concatenate([w_ukv[:, :, :NOPE_DIM].reshape(KV_LORA, 512),
                             w_ukv[:, :, NOPE_DIM:].reshape(KV_LORA, 512)], axis=1).astype(BF16)
    z = jnp.zeros((D_MODEL,), F32)
    row = lambda *parts: jnp.concatenate(list(parts) + [z])[:D_MODEL]
    q_scale = HEAD_DIM ** -0.5 * LOG2E
    c_scale = QK_DIM ** -0.5 * LOG2E
    g_mla = p["g_qk_mla"][i]
    gains = jnp.stack([
        p["g_norm_mix"][i],
        row(_tile_to(p["g_qk_win"][i, 0], 512) * q_scale, _tile_to(p["g_qk_win"][i, 1], 128)),
        row(_tile_to(p["g_qk_nbr"][i, 0], 512) * q_scale, _tile_to(p["g_qk_nbr"][i, 1], 512)),
        row(p["g_q_lora"][i], p["g_kv_lora"][i]),
        row(_tile_to(g_mla[0, :NOPE_DIM], 512) * c_scale, _tile_to(g_mla[0, NOPE_DIM:], 256) * c_scale),
        row(_tile_to(g_mla[1, :NOPE_DIM], 512), _tile_to(g_mla[1, NOPE_DIM:], 256)),
        p["g_norm_ffn"][i],
        z]).astype(F32)
    wo_a = p["w_o_win"][i].reshape(WIN_HEADS, HEAD_DIM, D_MODEL)[WIN_Q_ORDER, :, :].reshape(512, D_MODEL)
    return dict(
        w_a=w_a, w_uq=w_uq, w_ukv=w_ukv, gains=gains, w_gate=gates.astype(BF16),
        wo_a=wo_a.astype(BF16), wo_b=p["w_o_nbr"][i].astype(BF16), wo_c=p["w_o_mla"][i].astype(BF16),
        w_out=p["w_out"][i].astype(BF16),
        nbr_table=_nbr_bias_table(p["nbr_rel_bias"][i]))


def kernel(x_prompt, x_sample, cache_win_k, cache_win_v, cache_nbr_k, cache_nbr_v, cache_mla_ckv, cache_mla_krope, c, c_ctx, g_norm_mix, g_norm_ffn, w_ada, b_ada, w_in, g_qk_win, win_sink, g_qk_nbr, nbr_rel_bias, g_q_lora, g_kv_lora, w_uq, w_ukv, g_qk_mla, w_o_win, w_o_nbr, w_o_mla, w_out, w_router, b_router, w_exp_gate, w_exp_up, w_exp_down):
    p = dict(g_norm_mix=g_norm_mix, g_norm_ffn=g_norm_ffn, w_in=w_in, g_qk_win=g_qk_win,
             g_qk_nbr=g_qk_nbr, nbr_rel_bias=nbr_rel_bias, g_q_lora=g_q_lora, g_kv_lora=g_kv_lora,
             w_uq=w_uq, w_ukv=w_ukv, g_qk_mla=g_qk_mla, w_o_win=w_o_win, w_o_nbr=w_o_nbr,
             w_o_mla=w_o_mla, w_out=w_out, w_exp_gate=w_exp_gate, w_exp_up=w_exp_up,
             w_exp_down=w_exp_down)
    depth = w_in.shape[0]
    batch, seq, _ = x_prompt.shape
    dec_batch, dec_seq, _ = x_sample.shape
    past = cache_win_k.shape[2]

    n_c = 1 + dec_batch
    c_rows = -(-n_c // 8) * 8
    c_all = jnp.concatenate([c_ctx[None], c, jnp.zeros((c_rows - n_c, D_MODEL), F32)], axis=0)
    mods = _ada(c_all, w_ada, b_ada).reshape(depth, c_rows, 6, D_MODEL)
    mods = jnp.pad(mods, ((0, 0), (0, 0), (0, 2), (0, 0)))

    mats = (_block_ones(512, 64, 512, 64, 1.0 / HEAD_DIM), _block_ones(512, 64, 512, 64),
            _block_ones(256, 32, 512, 64), _block_ones(512, 64, 256, 32), _block_ones(256, 32, 256, 32))
    tabs = _rope_tables(dec_seq, 64) + _rope_tables(dec_seq, 32)
    sink = win_sink.astype(F32) * LOG2E
    w_r_t = w_router.T.astype(F32)
    b_r = b_router.astype(F32).reshape(N_EXPERTS, 1)
    layers = [_layer_params(i, p) for i in range(depth)]

    def merge(x, oa, ob, oc, mod, rows_per_mod, lp):
        return _merge(x, oa, ob, oc, mod, rows_per_mod, lp["gains"], lp["w_gate"], lp["wo_a"],
                      lp["wo_b"], lp["wo_c"], lp["w_out"], w_r_t, b_r)

    n_ctx = batch * seq
    n_lat = dec_batch * dec_seq
    x_ctx = x_prompt.reshape(n_ctx, D_MODEL)
    x_lat = x_sample.reshape(n_lat, D_MODEL)
    cwk = cache_win_k.reshape(dec_batch, depth, past, 128).astype(BF16)
    cwv = cache_win_v.reshape(dec_batch, depth, past, 128).astype(BF16)
    cnk = cache_nbr_k.reshape(dec_batch, depth, past, 512).astype(BF16)
    cnv = cache_nbr_v.reshape(dec_batch, depth, past, 512).astype(BF16)
    states = []
    for i, lp in enumerate(layers):
        mod_c = mods[i, 0:1]
        outs = _inproj(x_ctx, mod_c, n_ctx, lp["gains"], lp["w_a"], lp["w_uq"], lp["w_ukv"], mats, None,
                       seq, True)
        oa, ob, oc = _ctx_attn(i, sink, seq, *outs[:9])
        states.append(outs[9:])
        merged_c = merge(x_ctx, oa, ob, oc, mod_c, n_ctx, lp)

        mod_l = mods[i, 1:1 + dec_batch]
        qa, ka, va, qb, kb, vb, qc, kc, vc = _inproj(
            x_lat, mod_l, dec_seq, lp["gains"], lp["w_a"], lp["w_uq"], lp["w_ukv"], mats, tabs, dec_seq,
            False)
        kr_t = jnp.tile(cache_mla_krope[:, i].reshape(dec_batch * past, ROPE_DIM), (1, MLA_HEADS))
        kc_c, vc_c = _mla_cache_keys(cache_mla_ckv[:, i].reshape(dec_batch * past, KV_LORA), kr_t,
                                     lp["gains"], lp["w_ukv"], mats[1], mats[3], mats[4])
        r3 = lambda a: a.reshape(dec_batch, dec_seq, a.shape[-1])
        oa = _win_attn(i, sink, r3(qa), r3(ka), r3(va), cwk, cwv)
        ob = _nbr_attn(i, r3(qb), r3(kb), r3(vb), cnk, cnv, lp["nbr_table"])
        oc = _mla_attn(r3(qc), r3(kc), r3(vc), kc_c.reshape(dec_batch, past, 1024),
                       vc_c.reshape(dec_batch, past, 512))
        flat = lambda a: a.reshape(n_lat, 512)
        merged_l = merge(x_lat, flat(oa), flat(ob), flat(oc), mod_l, dec_seq, lp)

        x_ctx = _moe(i, *merged_c, mod_c, n_ctx, w_exp_gate, w_exp_up, w_exp_down)
        x_lat = _moe(i, *merged_l, mod_l, dec_seq, w_exp_gate, w_exp_up, w_exp_down)
    y_prompt = x_ctx.reshape(batch, seq, D_MODEL)
    y_sample = x_lat.reshape(dec_batch, dec_seq, D_MODEL)

    def stack(k, shape):
        return jnp.stack([s[k].reshape((batch, seq) + shape) for s in states], axis=1)

    return (y_prompt, y_sample,
            stack(0, (WIN_KV_HEADS, HEAD_DIM)), stack(1, (WIN_KV_HEADS, HEAD_DIM)),
            stack(2, (NBR_HEADS, HEAD_DIM)), stack(3, (NBR_HEADS, HEAD_DIM)),
            stack(4, (KV_LORA,)), stack(5, (ROPE_DIM,)))
```

```python
import functools

import numpy as np
import jax
import jax.numpy as jnp
from jax import lax
from jax.experimental import pallas as pl
from jax.experimental.pallas import tpu as pltpu
from jax.experimental.pallas import tpu_sc as plsc

D_MODEL = 1024
GRID_W = 64
HEAD_DIM = 64
WIN_HEADS = 8
WIN_KV_HEADS = 2
WINDOW = 128
NBR_HEADS = 8
NBR_ROWS = 8
NBR_COLS = 16
MLA_HEADS = 8
Q_LORA = 256
KV_LORA = 128
NOPE_DIM = 64
ROPE_DIM = 32
V_DIM = 64
QK_DIM = NOPE_DIM + ROPE_DIM
N_EXPERTS = 16
N_GROUPS = 4
EXPERTS_PER_GROUP = 4
D_FF = 512
ROPE_BASE = 10000.0
EPS = 1e-6

LANES = 128
LOG2E = 1.4426950408889634
NEG = -1e30
VMEM_LIMIT = 56 * 1024 * 1024

F32 = jnp.float32
BF16 = jnp.bfloat16

C_QA, C_KA, C_VA, C_QB, C_KB, C_VB, C_CQ, C_CKV, C_KR, C_END = (
    0, 512, 640, 768, 1280, 1792, 2304, 2560, 2688, 2944)
WIN_Q_ORDER = (0, 4, 1, 5, 2, 6, 3, 7)


def _cparams(sem):
    return pltpu.CompilerParams(dimension_semantics=sem, vmem_limit_bytes=VMEM_LIMIT)


def _dot(a, b):
    return jnp.dot(a, b, preferred_element_type=F32)


def _dot_nt(a, b):
    return lax.dot_general(a, b, (((1,), (1,)), ((), ())), preferred_element_type=F32)


def _split(x):
    hi = x.astype(BF16)
    lo = (x - hi.astype(F32)).astype(BF16)
    return hi, lo


def _gsum(x2, bmat):
    return _dot(x2.astype(BF16), bmat)


def _tile_lanes(t, width):
    reps = width // t.shape[-1]
    return t if reps == 1 else jnp.concatenate([t] * reps, axis=-1)


def _rotate(x, cos, sin_a, sin_b, half):
    w = x.shape[-1]
    up = pltpu.roll(x, w - half, 1)
    dn = pltpu.roll(x, half, 1)
    return (x * _tile_lanes(cos, w) + up * _tile_lanes(sin_a, w) + dn * _tile_lanes(sin_b, w))


def _norm_mod(x, gain, scale, shift):
    ms = jnp.mean(x * x, axis=-1, keepdims=True)
    return (x * lax.rsqrt(ms + EPS) * gain) * (1.0 + scale) + shift


def _ada_body(c_ref, w_ref, b_ref, o_ref):
    c = c_ref[...]
    a = c * (1.0 / (1.0 + jnp.exp(-c)))
    a_hi, a_lo = _split(a)
    w_hi, w_lo = _split(w_ref[0])
    o_ref[0] = _dot(a_hi, w_hi) + _dot(a_hi, w_lo) + _dot(a_lo, w_hi) + b_ref[0]


def _ada(c_all, w_ada, b_ada):
    depth = w_ada.shape[0]
    rows = c_all.shape[0]
    tn = 1536
    return pl.pallas_call(
        _ada_body,
        grid=(depth, 6 * D_MODEL // tn),
        in_specs=[pl.BlockSpec((rows, D_MODEL), lambda l, j: (0, 0)),
                  pl.BlockSpec((1, D_MODEL, tn), lambda l, j: (l, 0, j)),
                  pl.BlockSpec((1, 1, tn), lambda l, j: (l, 0, j))],
        out_specs=pl.BlockSpec((1, rows, tn), lambda l, j: (l, 0, j)),
        out_shape=jax.ShapeDtypeStruct((depth, rows, 6 * D_MODEL), F32),
        compiler_params=_cparams(("parallel", "parallel")),
        name="ada",
    )(c_all, w_ada, b_ada.reshape(depth, 1, 6 * D_MODEL))


def _mla_key_tail(ckvn_b, kr_t, g, wukv_ref, bnn, bnr, brr, rope_tabs, kc_ref, vc_ref):
    kv = _dot(ckvn_b, wukv_ref[...])
    kn = kv[:, 0:512]
    vc_ref[...] = kv[:, 512:1024].astype(BF16)
    kn2 = kn * kn
    kr2 = kr_t * kr_t
    kr_sum32 = _gsum(kr2, brr)
    ssn = (_gsum(kn2, bnn) + jnp.concatenate([kr_sum32, kr_sum32], axis=-1)) * (1.0 / QK_DIM)
    ssr = (_gsum(kn2, bnr) + kr_sum32) * (1.0 / QK_DIM)
    kn = kn * lax.rsqrt(ssn + EPS) * g[5:6, 0:512]
    kr = kr_t * lax.rsqrt(ssr + EPS) * g[5:6, 512:768]
    if rope_tabs is not None:
        kr = _rotate(kr, *rope_tabs, 8)
    for p in range(4):
        kc_ref[:, 256 * p:256 * p + 128] = kn[:, 128 * p:128 * p + 128].astype(BF16)
        q4 = 128 * (p // 2)
        kc_ref[:, 256 * p + 128:256 * p + 256] = kr[:, q4:q4 + 128].astype(BF16)


def _inproj_body(rope, states, *refs):
    (x_ref, mod_ref, g_ref, w_ref, wuq_ref, wukv_ref, b64_ref, bnn_ref, brn_ref, bnr_ref,
     brr_ref) = refs[:11]
    refs = refs[11:]
    if rope:
        tabs_w = tuple(r[...] for r in refs[0:3])
        tabs_m = tuple(r[...] for r in refs[3:6])
        refs = refs[6:]
    else:
        tabs_w = tabs_m = None
    qa_ref, ka_ref, va_ref, qb_ref, kb_ref, vb_ref, qc_ref, kc_ref, vc_ref = refs[:9]
    st = refs[9:]

    g = g_ref[...]
    mod = mod_ref[0]
    hb = _norm_mod(x_ref[...], g[0:1], mod[1:2], mod[0:1]).astype(BF16)

    def proj(a, b):
        return _dot(hb, w_ref[:, a:b])

    b64 = b64_ref[...]

    def head_norm(z, bmat, gain):
        return z * lax.rsqrt(_gsum(z * z, bmat) + EPS) * gain

    qa = head_norm(proj(C_QA, C_KA), b64, g[1:2, 0:512])
    ka = head_norm(proj(C_KA, C_VA), b64[0:128, 0:128], g[1:2, 512:640])
    va = proj(C_VA, C_QB)
    if states:
        st[0][...] = ka
        st[1][...] = va
    if rope:
        qa = _rotate(qa, *tabs_w, 16)
        ka = _rotate(ka, *tabs_w, 16)
    qa_ref[...] = qa.astype(BF16)
    ka_ref[...] = ka.astype(BF16)
    va_ref[...] = va.astype(BF16)

    qb = head_norm(proj(C_QB, C_KB), b64, g[2:3, 0:512])
    kb = head_norm(proj(C_KB, C_VB), b64, g[2:3, 512:1024])
    vb = proj(C_VB, C_CQ)
    if states:
        st[2][...] = kb
        st[3][...] = vb
    qb_ref[...] = qb.astype(BF16)
    kb_ref[...] = kb.astype(BF16)
    vb_ref[...] = vb.astype(BF16)

    cq = proj(C_CQ, C_CKV)
    cqn = cq * lax.rsqrt(jnp.mean(cq * cq, axis=-1, keepdims=True) + EPS) * g[3:4, 0:256]
    qq = _dot(cqn.astype(BF16), wuq_ref[...])
    qn, qr = qq[:, 0:512], qq[:, 512:768]
    qn2, qr2 = qn * qn, qr * qr
    bnn, brn, bnr, brr = bnn_ref[...], brn_ref[...], bnr_ref[...], brr_ref[...]
    ssn = (_gsum(qn2, bnn) + _gsum(qr2, brn)) * (1.0 / QK_DIM)
    ssr = (_gsum(qn2, bnr) + _gsum(qr2, brr)) * (1.0 / QK_DIM)
    qn = qn * lax.rsqrt(ssn + EPS) * g[4:5, 0:512]
    qr = qr * lax.rsqrt(ssr + EPS) * g[4:5, 512:768]
    if rope:
        qr = _rotate(qr, *tabs_m, 8)
    for p in range(4):
        qc_ref[:, 256 * p:256 * p + 128] = qn[:, 128 * p:128 * p + 128].astype(BF16)
        q4 = 128 * (p // 2)
        qc_ref[:, 256 * p + 128:256 * p + 256] = qr[:, q4:q4 + 128].astype(BF16)

    ckv = proj(C_CKV, C_KR)
    ckvn = ckv * lax.rsqrt(jnp.mean(ckv * ckv, axis=-1, keepdims=True) + EPS) * g[3:4, 256:384]
    kr_t = proj(C_KR, C_END)
    if states:
        st[4][...] = ckvn
        st[5][...] = kr_t[:, 0:ROPE_DIM]
    _mla_key_tail(ckvn.astype(BF16), kr_t, g, wukv_ref, bnn, bnr, brr, tabs_m, kc_ref, vc_ref)


def _const_spec(shape):
    nd = len(shape)
    return pl.BlockSpec(shape, lambda i, _nd=nd: (0,) * _nd, pipeline_mode=pl.Buffered(1))


def _inproj(x, mods, rows_per_mod, gains, w_a, w_uq, w_ukv, mats, rope_tabs, seq_len, states):
    n = x.shape[0]
    tm = min(1024, n)
    rope = rope_tabs is not None
    row = lambda w: pl.BlockSpec((tm, w), lambda i: (i, 0))
    in_specs = [row(D_MODEL),
                pl.BlockSpec((1, 8, D_MODEL), lambda i: ((i * tm) // rows_per_mod, 0, 0)),
                _const_spec(gains.shape), _const_spec(w_a.shape), _const_spec(w_uq.shape),
                _const_spec(w_ukv.shape)] + [_const_spec(m.shape) for m in mats]
    args = [x, mods, gains, w_a, w_uq, w_ukv, *mats]
    if rope:
        tiles_per_seq = seq_len // tm
        in_specs += [pl.BlockSpec((tm, LANES), lambda i: (i % tiles_per_seq, 0))] * 6
        args += list(rope_tabs)
    widths = [512, 128, 128, 512, 512, 512, 1024, 1024, 512]
    out_shape = [jax.ShapeDtypeStruct((n, w), BF16) for w in widths]
    out_specs = [row(w) for w in widths]
    if states:
        swidths = [128, 128, 512, 512, KV_LORA, ROPE_DIM]
        out_shape += [jax.ShapeDtypeStruct((n, w), F32) for w in swidths]
        out_specs += [row(w) for w in swidths]
    return pl.pallas_call(
        functools.partial(_inproj_body, rope, states),
        grid=(n // tm,), in_specs=in_specs, out_specs=out_specs, out_shape=out_shape,
        compiler_params=_cparams(("parallel",)),
        name="inproj_lat" if rope else "inproj_ctx",
    )(*args)


def _mla_cache_body(ckv_ref, kr_ref, g_ref, wukv_ref, bnn_ref, bnr_ref, brr_ref, kc_ref, vc_ref):
    _mla_key_tail(ckv_ref[...].astype(BF16), kr_ref[...], g_ref[...], wukv_ref, bnn_ref[...],
                  bnr_ref[...], brr_ref[...], None, kc_ref, vc_ref)


def _mla_cache_keys(ckv, kr_t, gains, w_ukv, bnn, bnr, brr):
    n = ckv.shape[0]
    tm = min(512, n)
    row = lambda w: pl.BlockSpec((tm, w), lambda i: (i, 0))
    return pl.pallas_call(
        _mla_cache_body,
        grid=(n // tm,),
        in_specs=[row(KV_LORA), row(256), _const_spec(gains.shape), _const_spec(w_ukv.shape),
                  _const_spec(bnn.shape), _const_spec(bnr.shape), _const_spec(brr.shape)],
        out_specs=[row(1024), row(512)],
        out_shape=[jax.ShapeDtypeStruct((n, 1024), BF16), jax.ShapeDtypeStruct((n, 512), BF16)],
        compiler_params=_cparams(("parallel",)),
        name="mla_cache_keys",
    )(ckv, kr_t, gains, w_ukv, bnn, bnr, brr)


def _lane_mask(width, ranges):
    lane = lax.broadcasted_iota(jnp.int32, (1, width), 1)
    m = None
    for lo, hi in ranges:
        c = (lane >= lo) & (lane < hi)
        m = c if m is None else (m | c)
    return jnp.where(m, 1.0, 0.0).astype(BF16)


def _stack_heads(q, mask0, mask1):
    return jnp.concatenate([q * mask0, q * mask1], axis=0)


def _lane_tiles(x):
    return [x[:, j:j + LANES] for j in range(0, x.shape[1], LANES)]


def _softmax_block(scores, sink=None):
    rows = scores[0].shape[0]
    mp = None
    for s in scores:
        for t in _lane_tiles(s):
            mp = t if mp is None else jnp.maximum(mp, t)
    base = sink if sink is not None else jnp.full((rows, LANES), NEG, F32)
    m = jnp.maximum(base, jnp.max(mp, axis=-1, keepdims=True))
    lp = None
    ps = []
    for s in scores:
        p = jnp.exp2(s - _tile_lanes(m, s.shape[1]))
        for t in _lane_tiles(p):
            lp = t if lp is None else lp + t
        ps.append(p.astype(BF16))
    if sink is not None:
        lane = lax.broadcasted_iota(jnp.int32, (rows, LANES), 1)
        lp = lp + jnp.where(lane == 0, jnp.exp2(sink - m), 0.0)
    p_all = ps[0] if len(ps) == 1 else jnp.concatenate(ps, axis=-1)
    return p_all, jnp.broadcast_to(jnp.sum(lp, axis=-1, keepdims=True), (rows, LANES))


def _softmax_pv(scores, values, sink=None):
    p_all, l = _softmax_block(scores, sink)
    v_all = values[0] if len(values) == 1 else jnp.concatenate(values, axis=0)
    return _dot(p_all, v_all) / l


def _merge_heads(o, tq):
    lane = lax.broadcasted_iota(jnp.int32, (tq, LANES), 1)
    return jnp.where(lane < HEAD_DIM, o[0:tq], o[tq:2 * tq])


def _mla_masks(p_mod2):
    lane = lax.broadcasted_iota(jnp.int32, (1, 256), 1)
    r0 = 128 + 32 * (2 * p_mod2)
    m0 = (lane < 64) | ((lane >= r0) & (lane < r0 + 32))
    m1 = ((lane >= 64) & (lane < 128)) | ((lane >= r0 + 32) & (lane < r0 + 64))
    return (jnp.where(m0, 1.0, 0.0).astype(BF16), jnp.where(m1, 1.0, 0.0).astype(BF16))


def _sink_col(sink_ref, layer, h0, h1, tq):
    row = lax.broadcasted_iota(jnp.int32, (2 * tq, LANES), 0)
    return jnp.where(row < tq, sink_ref[layer, h0], sink_ref[layer, h1])


def _ctx_attn_body(layer, sink_ref, qa_ref, ka_ref, va_ref, qb_ref, kb_ref, vb_ref, qc_ref, kc_ref,
                   vc_ref, oa_ref, ob_ref, oc_ref):
    tq = qa_ref.shape[0]
    lo = _lane_mask(LANES, [(0, 64)])
    hi = _lane_mask(LANES, [(64, 128)])
    ka, va = ka_ref[...], va_ref[...]
    for j in range(4):
        sl = slice(128 * j, 128 * j + 128)
        qs = _stack_heads(qa_ref[:, sl], lo, hi)
        sink = _sink_col(sink_ref, layer, j, 4 + j, tq)
        o = _softmax_pv([_dot_nt(qs, ka)], [va], sink)
        oa_ref[:, sl] = _merge_heads(o, tq).astype(BF16)

        qs = _stack_heads(qb_ref[:, sl], lo, hi)
        o = _softmax_pv([_dot_nt(qs, kb_ref[:, sl])], [vb_ref[:, sl]])
        ob_ref[:, sl] = _merge_heads(o, tq).astype(BF16)

        m0, m1 = _mla_masks(j % 2)
        s2 = slice(256 * j, 256 * j + 256)
        qs = _stack_heads(qc_ref[:, s2], m0, m1)
        o = _softmax_pv([_dot_nt(qs, kc_ref[:, s2])], [vc_ref[:, sl]])
        oc_ref[:, sl] = _merge_heads(o, tq).astype(BF16)


def _ctx_attn(layer, sink, seq, qa, ka, va, qb, kb, vb, qc, kc, vc):
    n = qa.shape[0]
    row = lambda w: pl.BlockSpec((seq, w), lambda b: (b, 0))
    ins = [qa, ka, va, qb, kb, vb, qc, kc, vc]
    return pl.pallas_call(
        functools.partial(_ctx_attn_body, layer),
        grid=(n // seq,),
        in_specs=[pl.BlockSpec(memory_space=pltpu.SMEM)] + [row(a.shape[1]) for a in ins],
        out_specs=[row(512)] * 3,
        out_shape=[jax.ShapeDtypeStruct((n, 512), BF16)] * 3,
        compiler_params=_cparams(("parallel",)),
        name="ctx_attn",
    )(sink, *ins)


WIN_ROW_BLOCK = 32


WIN_Q_BLOCK = 128
WIN_BLOCKS_PER_STEP = 4


def _win_body(layer, sink_ref, q_ref, k_ref, v_ref, kc_ref, vc_ref, o_ref, s_sc, p_sc, l_sc):
    tq = WIN_Q_BLOCK
    seq = k_ref.shape[1]
    kw = 3 * tq
    rb = WIN_ROW_BLOCK
    lo = _lane_mask(LANES, [(0, 64)])
    hi = _lane_mask(LANES, [(64, 128)])
    kc, vc = kc_ref[0, 0], vc_ref[0, 0]
    for u in range(q_ref.shape[1] // tq):
        i = pl.program_id(1) * (q_ref.shape[1] // tq) + u
        kstart = pl.multiple_of(jnp.clip((i - 1) * tq, 0, seq - kw), tq)
        k_all = jnp.concatenate([k_ref[0, pl.ds(kstart, kw), :], kc], axis=0)
        v_all = jnp.concatenate([v_ref[0, pl.ds(kstart, kw), :], vc], axis=0)
        q_pos = i * tq + lax.broadcasted_iota(jnp.int32, (tq, kw), 0)
        k_pos = kstart + lax.broadcasted_iota(jnp.int32, (tq, kw), 1)
        band = jnp.abs(q_pos - k_pos) <= WINDOW
        qs = jnp.concatenate(
            [_stack_heads(q_ref[0, u * tq:(u + 1) * tq, 128 * j:128 * j + 128], lo, hi) for j in range(4)],
            axis=0)
        s_sc[u] = _dot_nt(qs, k_all)
        for j in range(4):
            for r in range(2 * tq * j, 2 * tq * (j + 1), rb):
                head = j if r < 2 * tq * j + tq else 4 + j
                q0 = r % tq
                s_band = jnp.where(band[q0:q0 + rb], s_sc[u, r:r + rb, 0:kw], NEG)
                sink = jnp.full((rb, LANES), sink_ref[layer, head], F32)
                p, l = _softmax_block([s_band, s_sc[u, r:r + rb, kw:]], sink)
                p_sc[u, r:r + rb, :] = p
                l_sc[u, r:r + rb, :] = l
            rows = slice(2 * tq * j, 2 * tq * (j + 1))
            o = _dot(p_sc[u, rows, :], v_all) / l_sc[u, rows, :]
            o_ref[0, u * tq:(u + 1) * tq, 128 * j:128 * j + 128] = _merge_heads(o, tq).astype(BF16)


def _win_attn(layer, sink, q, k, v, kc, vc):
    b, seq, _ = q.shape
    nb = min(WIN_BLOCKS_PER_STEP, seq // WIN_Q_BLOCK)
    tq = nb * WIN_Q_BLOCK
    past = kc.shape[2]
    keys = 3 * WIN_Q_BLOCK + past
    return pl.pallas_call(
        functools.partial(_win_body, layer),
        grid=(b, seq // tq),
        in_specs=[pl.BlockSpec(memory_space=pltpu.SMEM),
                  pl.BlockSpec((1, tq, 512), lambda bi, i: (bi, i, 0)),
                  pl.BlockSpec((1, seq, 128), lambda bi, i: (bi, 0, 0)),
                  pl.BlockSpec((1, seq, 128), lambda bi, i: (bi, 0, 0)),
                  pl.BlockSpec((1, 1, past, 128), lambda bi, i: (bi, layer, 0, 0)),
                  pl.BlockSpec((1, 1, past, 128), lambda bi, i: (bi, layer, 0, 0))],
        out_specs=pl.BlockSpec((1, tq, 512), lambda bi, i: (bi, i, 0)),
        out_shape=jax.ShapeDtypeStruct((b, seq, 512), BF16),
        scratch_shapes=[pltpu.VMEM((nb, 8 * WIN_Q_BLOCK, keys), F32),
                        pltpu.VMEM((nb, 8 * WIN_Q_BLOCK, keys), BF16),
                        pltpu.VMEM((nb, 8 * WIN_Q_BLOCK, LANES), F32)],
        compiler_params=_cparams(("parallel", "arbitrary")),
        name="win_attn",
    )(sink, q, k, v, kc, vc)


NBR_TILE_ROWS = 4
NBR_WIN_ROWS = NBR_TILE_ROWS + NBR_ROWS
NBR_TAB_PAD = NBR_WIN_ROWS - NBR_ROWS
NBR_ROW_BLOCK = 32
NBR_TILES_PER_STEP = 2


def _nbr_body(rows, q_ref, k_ref, v_ref, kc_ref, vc_ref, tab_ref, o_ref, s_sc, p_sc, l_sc):
    tq = NBR_TILE_ROWS * GRID_W
    kw = NBR_WIN_ROWS * GRID_W
    lo = _lane_mask(LANES, [(0, 64)])
    hi = _lane_mask(LANES, [(64, 128)])
    past = kc_ref.shape[2]
    rb = NBR_ROW_BLOCK
    tiles = q_ref.shape[1] // tq
    for u in range(tiles):
        r0 = NBR_TILE_ROWS * (pl.program_id(1) * tiles + u)
        ws = jnp.clip(r0 - NBR_ROWS // 2, 0, rows - NBR_WIN_ROWS)
        kstart = pl.multiple_of(ws * GRID_W, LANES)
        k_row = ws + lax.broadcasted_iota(jnp.int32, (1, kw), 1) // GRID_W
        for j in range(4):
            sl = slice(128 * j, 128 * j + 128)
            qs = _stack_heads(q_ref[0, u * tq:(u + 1) * tq, sl], lo, hi)
            s_sc[u, j, :, 0:kw] = _dot_nt(qs, k_ref[0, pl.ds(kstart, kw), sl])
            s_sc[u, j, :, kw:kw + past] = _dot_nt(qs, kc_ref[0, 0, :, sl])
        for j in range(4):
            sl = slice(128 * j, 128 * j + 128)
            for b0 in range(0, 2 * tq, rb):
                h = 2 * j + b0 // tq
                ql, sub = divmod(b0 % tq, GRID_W)
                d0 = ws - r0 - ql + (NBR_ROWS - 1) + NBR_TAB_PAD
                bias = jnp.concatenate([tab_ref[h, d0 + 2 * m, sub:sub + rb, :]
                                        for m in range(NBR_WIN_ROWS // 2)], axis=-1)
                rs = jnp.clip(r0 + ql - NBR_ROWS // 2, 0, rows - NBR_ROWS)
                valid = (k_row >= rs) & (k_row < rs + NBR_ROWS)
                s_nb = jnp.where(valid, s_sc[u, j, b0:b0 + rb, 0:kw] + bias, NEG)
                p, l = _softmax_block([s_nb, s_sc[u, j, b0:b0 + rb, kw:kw + past]])
                p_sc[u, j, b0:b0 + rb, :] = p
                l_sc[u, j, b0:b0 + rb, :] = l
            v_all = jnp.concatenate([v_ref[0, pl.ds(kstart, kw), sl], vc_ref[0, 0, :, sl]], axis=0)
            o = _dot(p_sc[u, j], v_all) / l_sc[u, j]
            o_ref[0, u * tq:(u + 1) * tq, sl] = _merge_heads(o, tq).astype(BF16)


def _nbr_attn(layer, q, k, v, kc, vc, table):
    b, seq, _ = q.shape
    rows = seq // GRID_W
    tile = NBR_TILE_ROWS * GRID_W
    tq = NBR_TILES_PER_STEP * tile
    past = kc.shape[2]
    keys = NBR_WIN_ROWS * GRID_W + past
    once = pl.Buffered(1)
    return pl.pallas_call(
        functools.partial(_nbr_body, rows),
        grid=(b, seq // tq),
        in_specs=[pl.BlockSpec((1, tq, 512), lambda bi, i: (bi, i, 0)),
                  pl.BlockSpec((1, seq, 512), lambda bi, i: (bi, 0, 0), pipeline_mode=once),
                  pl.BlockSpec((1, seq, 512), lambda bi, i: (bi, 0, 0), pipeline_mode=once),
                  pl.BlockSpec((1, 1, past, 512), lambda bi, i: (bi, layer, 0, 0), pipeline_mode=once),
                  pl.BlockSpec((1, 1, past, 512), lambda bi, i: (bi, layer, 0, 0), pipeline_mode=once),
                  pl.BlockSpec(table.shape, lambda bi, i: (0, 0, 0, 0), pipeline_mode=once)],
        out_specs=pl.BlockSpec((1, tq, 512), lambda bi, i: (bi, i, 0)),
        out_shape=jax.ShapeDtypeStruct((b, seq, 512), BF16),
        scratch_shapes=[pltpu.VMEM((NBR_TILES_PER_STEP, 4, 2 * tile, keys), F32),
                        pltpu.VMEM((NBR_TILES_PER_STEP, 4, 2 * tile, keys), BF16),
                        pltpu.VMEM((NBR_TILES_PER_STEP, 4, 2 * tile, LANES), F32)],
        compiler_params=_cparams(("parallel", "arbitrary")),
        name="nbr_attn",
    )(q, k, v, kc, vc, table)


def _nbr_bias_table(rel_bias):
    col = np.arange(GRID_W)
    cs = np.clip(col - NBR_COLS // 2, 0, GRID_W - NBR_COLS)
    kc = np.arange(GRID_W)
    ok = (kc[None, :] >= cs[:, None]) & (kc[None, :] < cs[:, None] + NBR_COLS)
    dc = kc[None, :] - col[:, None] + (NBR_COLS - 1)
    pick = (dc[:, :, None] == np.arange(2 * NBR_COLS - 1)[None, None, :]) & ok[:, :, None]
    t = jnp.einsum("hdk,qck->hdqc", rel_bias.astype(F32) * LOG2E, jnp.asarray(pick, F32),
                   precision=lax.Precision.HIGHEST)
    t = jnp.where(jnp.asarray(ok)[None, None], t, NEG)
    t = jnp.pad(t, ((0, 0), (NBR_TAB_PAD, NBR_TAB_PAD), (0, 0), (0, 0)))
    return jnp.concatenate([t[:, :-1], t[:, 1:]], axis=-1)


MLA_Q_TILE = 1024
MLA_KEY_CHUNK = 512
MLA_ROW_BLOCK = 64


def _mla_body(q_ref, kl_ref, kc_ref, vl_ref, vc_ref, o_ref, qs_sc, m_sc, l_sc, acc_sc, s_sc, p_sc, a_sc):
    tq = q_ref.shape[1]
    rows = 2 * tq
    seq = kl_ref.shape[1]
    past = kc_ref.shape[1]
    tk = min(MLA_KEY_CHUNK, seq)
    rb = MLA_ROW_BLOCK
    m0, m1 = _mla_masks(pl.program_id(1) % 2)
    qs_sc[...] = _stack_heads(q_ref[0], m0, m1)
    m_sc[...] = jnp.full(m_sc.shape, NEG, F32)
    l_sc[...] = jnp.zeros(l_sc.shape, F32)
    acc_sc[...] = jnp.zeros(acc_sc.shape, F32)

    def step(c, k, v):
        slab = c % 2
        n = k.shape[0]
        s_sc[slab, :, 0:n] = _dot_nt(qs_sc[...], k)
        for r in range(0, rows, rb):
            sl = slice(r, r + rb)
            sb = s_sc[slab, sl, 0:n]
            mp = None
            for t in _lane_tiles(sb):
                mp = t if mp is None else jnp.maximum(mp, t)
            m_prev = m_sc[sl]
            m_new = jnp.maximum(m_prev, jnp.max(mp, axis=-1, keepdims=True))
            alpha = jnp.exp2(m_prev - m_new)
            p = jnp.exp2(sb - _tile_lanes(m_new, n))
            psum = None
            for t in _lane_tiles(p):
                psum = t if psum is None else psum + t
            l_sc[sl] = alpha * l_sc[sl] + psum
            m_sc[sl] = m_new
            a_sc[slab, sl, :] = alpha
            p_sc[slab, sl, 0:n] = p.astype(BF16)
        acc_sc[...] = a_sc[slab] * acc_sc[...] + _dot(p_sc[slab, :, 0:n], v)

    chunks = [(kl_ref, vl_ref, o, min(tk, seq - o)) for o in range(0, seq, tk)]
    chunks += [(kc_ref, vc_ref, o, min(tk, past - o)) for o in range(0, past, tk)]
    for c, (k_ref, v_ref, o, n) in enumerate(chunks):
        step(c, k_ref[0, o:o + n, :], v_ref[0, o:o + n, :])
    l = jnp.sum(l_sc[...], axis=-1, keepdims=True)
    o_ref[0] = _merge_heads(acc_sc[...] / l, tq).astype(BF16)


def _mla_attn(q, kl, vl, kc, vc):
    b, seq, _ = q.shape
    past = kc.shape[1]
    tq = min(MLA_Q_TILE, seq)
    tk = min(MLA_KEY_CHUNK, seq)
    return pl.pallas_call(
        _mla_body,
        grid=(b, 4, seq // tq),
        in_specs=[pl.BlockSpec((1, tq, 256), lambda bi, p, qi: (bi, qi, p)),
                  pl.BlockSpec((1, seq, 256), lambda bi, p, qi: (bi, 0, p)),
                  pl.BlockSpec((1, past, 256), lambda bi, p, qi: (bi, 0, p)),
                  pl.BlockSpec((1, seq, 128), lambda bi, p, qi: (bi, 0, p)),
                  pl.BlockSpec((1, past, 128), lambda bi, p, qi: (bi, 0, p))],
        out_specs=pl.BlockSpec((1, tq, 128), lambda bi, p, qi: (bi, qi, p)),
        out_shape=jax.ShapeDtypeStruct((b, seq, 512), BF16),
        scratch_shapes=[pltpu.VMEM((2 * tq, 256), BF16), pltpu.VMEM((2 * tq, LANES), F32),
                        pltpu.VMEM((2 * tq, LANES), F32), pltpu.VMEM((2 * tq, LANES), F32),
                        pltpu.VMEM((2, 2 * tq, tk), F32), pltpu.VMEM((2, 2 * tq, tk), BF16),
                        pltpu.VMEM((2, 2 * tq, LANES), F32)],
        compiler_params=_cparams(("parallel", "parallel", "arbitrary")),
        name="mla_attn",
    )(q, kl, kc, vl, vc)


def _pack_pairs(x):
    w = x.shape[1] // 2
    hi = lax.bitcast_convert_type(x[:, :w].astype(BF16).astype(F32), jnp.int32)
    lo = lax.bitcast_convert_type(x[:, w:].astype(BF16).astype(F32), jnp.int32)
    return (hi & jnp.int32(-65536)) | lax.shift_right_logical(lo, jnp.int32(16))


def _unpack_pairs(p):
    hi = lax.bitcast_convert_type(p & jnp.int32(-65536), F32)
    lo = lax.bitcast_convert_type(lax.shift_left(p, jnp.int32(16)), F32)
    return jnp.concatenate([hi, lo], axis=-1)


def _merge_body(x_ref, oa_ref, ob_ref, oc_ref, mod_ref, g_ref, wg_ref, woa_ref, wob_ref, woc_ref,
                wout_ref, wr_ref, br_ref, tri_ref, x1_ref, route_ref, h2p_ref, count_ref, count_sc):
    x = x_ref[...]
    g = g_ref[...]
    mod = mod_ref[0]
    hb = _norm_mod(x, g[0:1], mod[1:2], mod[0:1]).astype(BF16)
    m = None
    for br, (o_ref, wo_ref) in enumerate(((oa_ref, woa_ref), (ob_ref, wob_ref), (oc_ref, woc_ref))):
        z = _dot(hb, wg_ref[:, D_MODEL * br:D_MODEL * (br + 1)])
        gate = 1.0 / (1.0 + jnp.exp(-z))
        t = gate * _dot(o_ref[...], wo_ref[...])
        m = t if m is None else m + t
    y = _dot(m.astype(BF16), wout_ref[...])
    x1 = x + mod[2:3] * y
    x1_ref[...] = x1

    h2 = _norm_mod(x1, g[6:7], mod[4:5], mod[3:4])
    h_hi, h_lo = _split(h2)
    w_hi, w_lo = _split(wr_ref[...])
    both_w = _dot_nt(jnp.concatenate([w_hi, w_lo], axis=0), h_hi)
    logits = both_w[0:N_EXPERTS] + both_w[N_EXPERTS:2 * N_EXPERTS] + _dot_nt(w_hi, h_lo)
    score = 1.0 / (1.0 + jnp.exp(-logits))
    sel = score + br_ref[...]
    sel_r = [sel[e:e + 1] for e in range(N_EXPERTS)]
    sc_r = [score[e:e + 1] for e in range(N_EXPERTS)]
    picked = []
    for e in range(N_EXPERTS):
        grp, a = divmod(e, EXPERTS_PER_GROUP)
        rank = None
        for bb in range(EXPERTS_PER_GROUP):
            if bb == a:
                continue
            o = sel_r[grp * EXPERTS_PER_GROUP + bb]
            beats = (o >= sel_r[e]) if bb < a else (o > sel_r[e])
            r = jnp.where(beats, 1.0, 0.0)
            rank = r if rank is None else rank + r
        picked.append(rank < 2.0)
    gscore = []
    for grp in range(N_GROUPS):
        tot = None
        for a in range(EXPERTS_PER_GROUP):
            e = grp * EXPERTS_PER_GROUP + a
            t = jnp.where(picked[e], sel_r[e], 0.0)
            tot = t if tot is None else tot + t
        gscore.append(tot)
    best = jnp.zeros_like(gscore[0])
    best_v = gscore[0]
    for grp in range(1, N_GROUPS):
        upd = gscore[grp] > best_v
        best = jnp.where(upd, float(grp), best)
        best_v = jnp.where(upd, gscore[grp], best_v)
    cw, pk = [], []
    for a in range(EXPERTS_PER_GROUP):
        tot = flag = None
        for grp in range(N_GROUPS):
            e = grp * EXPERTS_PER_GROUP + a
            f = (best == float(grp)) & picked[e]
            t = jnp.where(f, sc_r[e], 0.0)
            tot = t if tot is None else tot + t
            flag = f if flag is None else (flag | f)
        cw.append(tot)
        pk.append(flag)
    den = cw[0] + cw[1] + cw[2] + cw[3]
    first = jnp.where(pk[0], 0.0, jnp.where(pk[1], 1.0, jnp.where(pk[2], 2.0, 3.0)))
    second = jnp.where(pk[3], 3.0, jnp.where(pk[2], 2.0, jnp.where(pk[1], 1.0, 0.0)))
    slot_e, slot_w = [], []
    for which in (first, second):
        tot = None
        for a in range(EXPERTS_PER_GROUP):
            t = jnp.where(which == float(a), cw[a], 0.0)
            tot = t if tot is None else tot + t
        slot_w.append(tot / den)
        slot_e.append(best * float(EXPERTS_PER_GROUP) + which)

    @pl.when(pl.program_id(0) == 0)
    def _():
        count_sc[...] = jnp.zeros(count_sc.shape, F32)

    tm = x.shape[0]
    eid = lax.broadcasted_iota(jnp.int32, (N_EXPERTS, tm), 0).astype(F32)
    oh = [eid == slot_e[0], eid == slot_e[1]]
    both = jnp.where(oh[0] | oh[1], 1.0, 0.0)
    seen = count_sc[...][:, 0:1] + _dot(both.astype(BF16), tri_ref[...])
    for k in range(2):
        route_ref[k:k + 1, :] = slot_w[k]
        route_ref[2 + k:3 + k, :] = slot_e[k]
        route_ref[4 + k:5 + k, :] = jnp.sum(jnp.where(oh[k], seen, 0.0), axis=0, keepdims=True)
    route_ref[6:8, :] = jnp.zeros((2, tm), F32)
    count_sc[...] = count_sc[...] + jnp.sum(both, axis=-1, keepdims=True)
    count_ref[...] = count_sc[...]
    h2p_ref[...] = _pack_pairs(h2)


def _merge(x, oa, ob, oc, mods, rows_per_mod, gains, w_gate, wo_a, wo_b, wo_c, w_out, w_r_t, b_r):
    n = x.shape[0]
    tm = min(1024, n)
    row = lambda w: pl.BlockSpec((tm, w), lambda i: (i, 0))
    tri = jnp.asarray(np.triu(np.ones((tm, tm), np.float32), 1), BF16)
    consts = [gains, w_gate, wo_a, wo_b, wo_c, w_out, w_r_t, b_r, tri]
    return pl.pallas_call(
        _merge_body,
        grid=(n // tm,),
        in_specs=[row(D_MODEL), row(512), row(512), row(512),
                  pl.BlockSpec((1, 8, D_MODEL), lambda i: ((i * tm) // rows_per_mod, 0, 0))]
                 + [_const_spec(c.shape) for c in consts],
        out_specs=[row(D_MODEL), pl.BlockSpec((8, tm), lambda i: (0, i)), row(512),
                   pl.BlockSpec((N_EXPERTS, LANES), lambda i: (0, 0))],
        out_shape=[jax.ShapeDtypeStruct((n, D_MODEL), F32), jax.ShapeDtypeStruct((8, n), F32),
                   jax.ShapeDtypeStruct((n, 512), jnp.int32),
                   jax.ShapeDtypeStruct((N_EXPERTS, LANES), F32)],
        scratch_shapes=[pltpu.VMEM((N_EXPERTS, LANES), F32)],
        compiler_params=_cparams(("arbitrary",)),
        name="merge",
    )(x, oa, ob, oc, mods, *consts)


EXPERT_TILE = 512
SC_CORES = 2
SC_SUBCORES = 16
SC_WORKERS = SC_CORES * SC_SUBCORES
SC_WINDOW = 128


def _sc_mesh():
    return plsc.VectorSubcoreMesh(core_axis_name="c", subcore_axis_name="s", num_cores=SC_CORES,
                                  num_subcores=SC_SUBCORES)


def _sc_window_base(steps, j):
    wid = lax.axis_index("s") * SC_CORES + lax.axis_index("c")
    return pl.multiple_of((wid * steps + j) * SC_WINDOW, SC_WINDOW)


def _sc_dispatch(rows, pos0, pos1, n_out):
    n, w = rows.shape
    steps = n // (SC_WORKERS * SC_WINDOW)

    @functools.partial(
        pl.kernel, out_type=jax.ShapeDtypeStruct((n_out, w), rows.dtype), mesh=_sc_mesh(),
        scratch_types=[pltpu.VMEM((SC_WINDOW,), jnp.int32), pltpu.VMEM((SC_WINDOW,), jnp.int32),
                       pltpu.VMEM((SC_WINDOW, w), rows.dtype)],
        name="moe_dispatch")
    def run(x_hbm, i0_hbm, i1_hbm, o_hbm, i0_v, i1_v, rows_v):
        @pl.loop(0, steps)
        def _(j):
            base = _sc_window_base(steps, j)
            pltpu.sync_copy(i0_hbm.at[pl.ds(base, SC_WINDOW)], i0_v)
            pltpu.sync_copy(i1_hbm.at[pl.ds(base, SC_WINDOW)], i1_v)
            pltpu.sync_copy(x_hbm.at[pl.ds(base, SC_WINDOW)], rows_v)
            pltpu.sync_copy(rows_v, o_hbm.at[i0_v])
            pltpu.sync_copy(rows_v, o_hbm.at[i1_v])

    return run(rows, pos0, pos1)


def _sc_collect(rows, pos0, pos1):
    n = pos0.shape[0]
    w = rows.shape[1]
    steps = n // (SC_WORKERS * SC_WINDOW)
    out = jax.ShapeDtypeStruct((n, w), rows.dtype)

    @functools.partial(
        pl.kernel, out_type=[out, out], mesh=_sc_mesh(),
        scratch_types=[pltpu.VMEM((SC_WINDOW,), jnp.int32), pltpu.VMEM((SC_WINDOW, w), rows.dtype)],
        name="moe_collect")
    def run(y_hbm, i0_hbm, i1_hbm, o0_hbm, o1_hbm, i_v, rows_v):
        @pl.loop(0, steps)
        def _(j):
            base = _sc_window_base(steps, j)
            for i_hbm, o_hbm in ((i0_hbm, o0_hbm), (i1_hbm, o1_hbm)):
                pltpu.sync_copy(i_hbm.at[pl.ds(base, SC_WINDOW)], i_v)
                pltpu.sync_copy(y_hbm.at[i_v], rows_v)
                pltpu.sync_copy(rows_v, o_hbm.at[pl.ds(base, SC_WINDOW)])

    return run(rows, pos0, pos1)


def _experts_body(te_ref, nv_ref, xs_ref, wg_ref, wu_ref, wd_ref, ys_ref, wg_sc, wu_sc, wd_sc):
    j = pl.program_id(0)

    @pl.when((j == 0) | (te_ref[j] != te_ref[jnp.maximum(j - 1, 0)]))
    def _():
        wg_sc[...] = wg_ref[0, 0].astype(BF16)
        wu_sc[...] = wu_ref[0, 0].astype(BF16)
        wd_sc[...] = wd_ref[0, 0].astype(BF16)

    @pl.when(j < nv_ref[0])
    def _():
        x = _unpack_pairs(xs_ref[...]).astype(BF16)
        zg = _dot(x, wg_sc[...])
        act = zg * (1.0 / (1.0 + jnp.exp(-zg))) * _dot(x, wu_sc[...])
        ys_ref[...] = _pack_pairs(_dot(act.astype(BF16), wd_sc[...]))

    @pl.when(j >= nv_ref[0])
    def _():
        ys_ref[...] = jnp.zeros(ys_ref.shape, ys_ref.dtype)


def _experts(layer, xs, tile_expert, n_valid, wg, wu, wd):
    p = xs.shape[0]
    wspec = lambda r, c: pl.BlockSpec((1, 1, r, c), lambda j, te, nv: (layer, te[j], 0, 0))
    grid_spec = pltpu.PrefetchScalarGridSpec(
        num_scalar_prefetch=2,
        grid=(p // EXPERT_TILE,),
        in_specs=[pl.BlockSpec((EXPERT_TILE, 512), lambda j, te, nv: (j, 0)),
                  wspec(D_MODEL, D_FF), wspec(D_MODEL, D_FF), wspec(D_FF, D_MODEL)],
        out_specs=pl.BlockSpec((EXPERT_TILE, 512), lambda j, te, nv: (j, 0)),
        scratch_shapes=[pltpu.VMEM((D_MODEL, D_FF), BF16), pltpu.VMEM((D_MODEL, D_FF), BF16),
                        pltpu.VMEM((D_FF, D_MODEL), BF16)])
    return pl.pallas_call(
        _experts_body, grid_spec=grid_spec,
        out_shape=jax.ShapeDtypeStruct((p, 512), jnp.int32),
        compiler_params=_cparams(("arbitrary",)),
        name="experts",
    )(tile_expert, n_valid, xs, wg, wu, wd)


def _combine_body(x1_ref, y0_ref, y1_ref, route_ref, mod_ref, o_ref):
    tm = x1_ref.shape[0]
    eye = jnp.where(lax.broadcasted_iota(jnp.int32, (tm, tm), 0)
                    == lax.broadcasted_iota(jnp.int32, (tm, tm), 1), 1.0, 0.0).astype(BF16)
    r_hi, r_lo = _split(route_ref[...])
    wcol = _dot_nt(eye, r_hi) + _dot_nt(eye, r_lo)
    moe = wcol[:, 0:1] * _unpack_pairs(y0_ref[...]) + wcol[:, 1:2] * _unpack_pairs(y1_ref[...])
    o_ref[...] = x1_ref[...] + mod_ref[0][5:6] * moe


def _combine(x1, y0, y1, route, mods, rows_per_mod):
    n = x1.shape[0]
    tm = min(512, n)
    row = lambda w: pl.BlockSpec((tm, w), lambda i: (i, 0))
    return pl.pallas_call(
        _combine_body,
        grid=(n // tm,),
        in_specs=[row(D_MODEL), row(512), row(512), pl.BlockSpec((8, tm), lambda i: (0, i)),
                  pl.BlockSpec((1, 8, D_MODEL), lambda i: ((i * tm) // rows_per_mod, 0, 0))],
        out_specs=row(D_MODEL),
        out_shape=jax.ShapeDtypeStruct((n, D_MODEL), F32),
        compiler_params=_cparams(("parallel",)),
        name="combine",
    )(x1, y0, y1, route, mods)


def _moe(layer, x1, route, h2p, counts, mods, rows_per_mod, wg, wu, wd):
    n = x1.shape[0]
    p = 2 * n + N_EXPERTS * EXPERT_TILE
    cnt = counts[:, 0].astype(jnp.int32)
    padded = (cnt + EXPERT_TILE - 1) // EXPERT_TILE * EXPERT_TILE
    seg_end = jnp.cumsum(padded)
    seg_off = seg_end - padded
    experts = jnp.arange(N_EXPERTS, dtype=jnp.int32)

    def position(k):
        e = route[2 + k].astype(jnp.int32)
        off = jnp.sum(jnp.where(e[:, None] == experts[None], seg_off[None], 0), axis=1)
        return off + route[4 + k].astype(jnp.int32)

    pos0, pos1 = position(0), position(1)
    tile_start = jnp.arange(p // EXPERT_TILE, dtype=jnp.int32) * EXPERT_TILE
    tile_expert = jnp.sum(tile_start[:, None] >= seg_end[None], axis=1).astype(jnp.int32)
    tile_expert = jnp.minimum(tile_expert, N_EXPERTS - 1)
    n_valid = (seg_end[-1:] // EXPERT_TILE).astype(jnp.int32)

    xs = _sc_dispatch(h2p, pos0, pos1, p)
    ys = _experts(layer, xs, tile_expert, n_valid, wg, wu, wd)
    y0, y1 = _sc_collect(ys, pos0, pos1)
    return _combine(x1, y0, y1, route, mods, rows_per_mod)


def _block_ones(n_in, g_in, n_out, g_out, value=1.0):
    r = np.arange(n_in)[:, None] // g_in
    c = np.arange(n_out)[None, :] // g_out
    return jnp.asarray(np.where(r == c, value, 0.0), dtype=BF16)


def _rope_tables(seq, head_w):
    pos = np.arange(seq)
    rows, cols = pos // GRID_W, pos % GRID_W
    a = head_w // 2
    half = a // 2
    freqs = (ROPE_BASE ** (-np.arange(half, dtype=np.float32) / half)).astype(np.float32)
    lane = np.arange(LANES) % head_w
    within = lane % a
    first = within < half
    p = np.where((lane // a == 0)[None, :], rows[:, None], cols[:, None]).astype(np.float32)
    ang = (p * freqs[within % half][None, :]).astype(np.float32)
    cos, sin = np.cos(ang), np.sin(ang)
    return (jnp.asarray(cos, F32), jnp.asarray(np.where(first[None], -sin, 0.0), F32),
            jnp.asarray(np.where(first[None], 0.0, sin), F32))


def _tile_to(v, width):
    return jnp.tile(v, width // v.shape[0])


def _layer_params(i, p):
    w_in = p["w_in"][i]
    sp = np.cumsum((512, 128, 128, 512, 512, 512, Q_LORA, KV_LORA, ROPE_DIM))
    qa, ka, va, qb, kb, vb, cq, ckv, kr, gates = jnp.split(w_in, [int(s) for s in sp], axis=1)
    qa = qa.reshape(D_MODEL, WIN_HEADS, HEAD_DIM)[:, WIN_Q_ORDER, :].reshape(D_MODEL, 512)
    w_a = jnp.concatenate([qa, ka, va, qb, kb, vb, cq, ckv, jnp.tile(kr, (1, MLA_HEADS))],
                          axis=1).astype(BF16)
    w_uq = p["w_uq"][i].reshape(Q_LORA, MLA_HEADS, QK_DIM)
    w_uq = jnp.concatenate([w_uq[:, :, :NOPE_DIM].reshape(Q_LORA, 512),
                            w_uq[:, :, NOPE_DIM:].reshape(Q_LORA, 256)], axis=1).astype(BF16)
    w_ukv = p["w_ukv"][i].reshape(KV_LORA, MLA_HEADS, NOPE_DIM + V_DIM)
    w_ukv = jnp.concatenate([w_ukv[:, :, :NOPE_DIM].reshape(KV_LORA, 512),
                             w_ukv[:, :, NOPE_DIM:].reshape(KV_LORA, 512)], axis=1).astype(BF16)
    z = jnp.zeros((D_MODEL,), F32)
    row = lambda *parts: jnp.concatenate(list(parts) + [z])[:D_MODEL]
    q_scale = HEAD_DIM ** -0.5 * LOG2E
    c_scale = QK_DIM ** -0.5 * LOG2E
    g_mla = p["g_qk_mla"][i]
    gains = jnp.stack([
        p["g_norm_mix"][i],
        row(_tile_to(p["g_qk_win"][i, 0], 512) * q_scale, _tile_to(p["g_qk_win"][i, 1], 128)),
        row(_tile_to(p["g_qk_nbr"][i, 0], 512) * q_scale, _tile_to(p["g_qk_nbr"][i, 1], 512)),
        row(p["g_q_lora"][i], p["g_kv_lora"][i]),
        row(_tile_to(g_mla[0, :NOPE_DIM], 512) * c_scale, _tile_to(g_mla[0, NOPE_DIM:], 256) * c_scale),
        row(_tile_to(g_mla[1, :NOPE_DIM], 512), _tile_to(g_mla[1, NOPE_DIM:], 256)),
        p["g_norm_ffn"][i],
        z]).astype(F32)
    wo_a = p["w_o_win"][i].reshape(WIN_HEADS, HEAD_DIM, D_MODEL)[WIN_Q_ORDER, :, :].reshape(512, D_MODEL)
    return dict(
        w_a=w_a, w_uq=w_uq, w_ukv=w_ukv, gains=gains, w_gate=gates.astype(BF16),
        wo_a=wo_a.astype(BF16), wo_b=p["w_o_nbr"][i].astype(BF16), wo_c=p["w_o_mla"][i].astype(BF16),
        w_out=p["w_out"][i].astype(BF16),
        nbr_table=_nbr_bias_table(p["nbr_rel_bias"][i]))


def kernel(x_prompt, x_sample, cache_win_k, cache_win_v, cache_nbr_k, cache_nbr_v, cache_mla_ckv, cache_mla_krope, c, c_ctx, g_norm_mix, g_norm_ffn, w_ada, b_ada, w_in, g_qk_win, win_sink, g_qk_nbr, nbr_rel_bias, g_q_lora, g_kv_lora, w_uq, w_ukv, g_qk_mla, w_o_win, w_o_nbr, w_o_mla, w_out, w_router, b_router, w_exp_gate, w_exp_up, w_exp_down):
    p = dict(g_norm_mix=g_norm_mix, g_norm_ffn=g_norm_ffn, w_in=w_in, g_qk_win=g_qk_win,
             g_qk_nbr=g_qk_nbr, nbr_rel_bias=nbr_rel_bias, g_q_lora=g_q_lora, g_kv_lora=g_kv_lora,
             w_uq=w_uq, w_ukv=w_ukv, g_qk_mla=g_qk_mla, w_o_win=w_o_win, w_o_nbr=w_o_nbr,
             w_o_mla=w_o_mla, w_out=w_out, w_exp_gate=w_exp_gate, w_exp_up=w_exp_up,
             w_exp_down=w_exp_down)
    depth = w_in.shape[0]
    batch, seq, _ = x_prompt.shape
    dec_batch, dec_seq, _ = x_sample.shape
    past = cache_win_k.shape[2]

    n_c = 1 + dec_batch
    c_rows = -(-n_c // 8) * 8
    c_all = jnp.concatenate([c_ctx[None], c, jnp.zeros((c_rows - n_c, D_MODEL), F32)], axis=0)
    mods = _ada(c_all, w_ada, b_ada).reshape(depth, c_rows, 6, D_MODEL)
    mods = jnp.pad(mods, ((0, 0), (0, 0), (0, 2), (0, 0)))

    mats = (_block_ones(512, 64, 512, 64, 1.0 / HEAD_DIM), _block_ones(512, 64, 512, 64),
            _block_ones(256, 32, 512, 64), _block_ones(512, 64, 256, 32), _block_ones(256, 32, 256, 32))
    tabs = _rope_tables(dec_seq, 64) + _rope_tables(dec_seq, 32)
    sink = win_sink.astype(F32) * LOG2E
    w_r_t = w_router.T.astype(F32)
    b_r = b_router.astype(F32).reshape(N_EXPERTS, 1)
    layers = [_layer_params(i, p) for i in range(depth)]

    def merge(x, oa, ob, oc, mod, rows_per_mod, lp):
        return _merge(x, oa, ob, oc, mod, rows_per_mod, lp["gains"], lp["w_gate"], lp["wo_a"],
                      lp["wo_b"], lp["wo_c"], lp["w_out"], w_r_t, b_r)

    n_ctx = batch * seq
    n_lat = dec_batch * dec_seq
    x_ctx = x_prompt.reshape(n_ctx, D_MODEL)
    x_lat = x_sample.reshape(n_lat, D_MODEL)
    cwk = cache_win_k.reshape(dec_batch, depth, past, 128).astype(BF16)
    cwv = cache_win_v.reshape(dec_batch, depth, past, 128).astype(BF16)
    cnk = cache_nbr_k.reshape(dec_batch, depth, past, 512).astype(BF16)
    cnv = cache_nbr_v.reshape(dec_batch, depth, past, 512).astype(BF16)
    states = []
    for i, lp in enumerate(layers):
        mod_c = mods[i, 0:1]
        outs = _inproj(x_ctx, mod_c, n_ctx, lp["gains"], lp["w_a"], lp["w_uq"], lp["w_ukv"], mats, None,
                       seq, True)
        oa, ob, oc = _ctx_attn(i, sink, seq, *outs[:9])
        states.append(outs[9:])
        merged_c = merge(x_ctx, oa, ob, oc, mod_c, n_ctx, lp)

        mod_l = mods[i, 1:1 + dec_batch]
        qa, ka, va, qb, kb, vb, qc, kc, vc = _inproj(
            x_lat, mod_l, dec_seq, lp["gains"], lp["w_a"], lp["w_uq"], lp["w_ukv"], mats, tabs, dec_seq,
            False)
        kr_t = jnp.tile(cache_mla_krope[:, i].reshape(dec_batch * past, ROPE_DIM), (1, MLA_HEADS))
        kc_c, vc_c = _mla_cache_keys(cache_mla_ckv[:, i].reshape(dec_batch * past, KV_LORA), kr_t,
                                     lp["gains"], lp["w_ukv"], mats[1], mats[3], mats[4])
        r3 = lambda a: a.reshape(dec_batch, dec_seq, a.shape[-1])
        oa = _win_attn(i, sink, r3(qa), r3(ka), r3(va), cwk, cwv)
        ob = _nbr_attn(i, r3(qb), r3(kb), r3(vb), cnk, cnv, lp["nbr_table"])
        oc = _mla_attn(r3(qc), r3(kc), r3(vc), kc_c.reshape(dec_batch, past, 1024),
                       vc_c.reshape(dec_batch, past, 512))
        flat = lambda a: a.reshape(n_lat, 512)
        merged_l = merge(x_lat, flat(oa), flat(ob), flat(oc), mod_l, dec_seq, lp)

        x_ctx = _moe(i, *merged_c, mod_c, n_ctx, w_exp_gate, w_exp_up, w_exp_down)
        x_lat = _moe(i, *merged_l, mod_l, dec_seq, w_exp_gate, w_exp_up, w_exp_down)
    y_prompt = x_ctx.reshape(batch, seq, D_MODEL)
    y_sample = x_lat.reshape(dec_batch, dec_seq, D_MODEL)

    def stack(k, shape):
        return jnp.stack([s[k].reshape((batch, seq) + shape) for s in states], axis=1)

    return (y_prompt, y_sample,
            stack(0, (WIN_KV_HEADS, HEAD_DIM)), stack(1, (WIN_KV_HEADS, HEAD_DIM)),
            stack(2, (NBR_HEADS, HEAD_DIM)), stack(3, (NBR_HEADS, HEAD_DIM)),
            stack(4, (KV_LORA,)), stack(5, (ROPE_DIM,)))
```

```python
import functools

import numpy as np
import jax
import jax.numpy as jnp
from jax import lax
from jax.experimental import pallas as pl
from jax.experimental.pallas import tpu as pltpu
from jax.experimental.pallas import tpu_sc as plsc

D_MODEL = 1024
GRID_W = 64
HEAD_DIM = 64
WIN_HEADS = 8
WIN_KV_HEADS = 2
WINDOW = 128
NBR_HEADS = 8
NBR_ROWS = 8
NBR_COLS = 16
MLA_HEADS = 8
Q_LORA = 256
KV_LORA = 128
NOPE_DIM = 64
ROPE_DIM = 32
V_DIM = 64
QK_DIM = NOPE_DIM + ROPE_DIM
N_EXPERTS = 16
N_GROUPS = 4
EXPERTS_PER_GROUP = 4
D_FF = 512
ROPE_BASE = 10000.0
EPS = 1e-6

LANES = 128
LOG2E = 1.4426950408889634
NEG = -1e30
VMEM_LIMIT = 56 * 1024 * 1024

F32 = jnp.float32
BF16 = jnp.bfloat16

C_QA, C_KA, C_VA, C_QB, C_KB, C_VB, C_CQ, C_CKV, C_KR, C_END = (
    0, 512, 640, 768, 1280, 1792, 2304, 2560, 2688, 2944)
WIN_Q_ORDER = (0, 4, 1, 5, 2, 6, 3, 7)


def _cparams(sem):
    return pltpu.CompilerParams(dimension_semantics=sem, vmem_limit_bytes=VMEM_LIMIT)


def _dot(a, b):
    return jnp.dot(a, b, preferred_element_type=F32)


def _dot_nt(a, b):
    return lax.dot_general(a, b, (((1,), (1,)), ((), ())), preferred_element_type=F32)


def _split(x):
    hi = x.astype(BF16)
    lo = (x - hi.astype(F32)).astype(BF16)
    return hi, lo


def _gsum(x2, bmat):
    return _dot(x2.astype(BF16), bmat)


def _tile_lanes(t, width):
    reps = width // t.shape[-1]
    return t if reps == 1 else jnp.concatenate([t] * reps, axis=-1)


def _rotate(x, cos, sin_a, sin_b, half):
    w = x.shape[-1]
    up = pltpu.roll(x, w - half, 1)
    dn = pltpu.roll(x, half, 1)
    return (x * _tile_lanes(cos, w) + up * _tile_lanes(sin_a, w) + dn * _tile_lanes(sin_b, w))


def _norm_mod(x, gain, scale, shift):
    ms = jnp.mean(x * x, axis=-1, keepdims=True)
    return (x * lax.rsqrt(ms + EPS) * gain) * (1.0 + scale) + shift


def _ada_body(c_ref, w_ref, b_ref, o_ref):
    c = c_ref[...]
    a = c * (1.0 / (1.0 + jnp.exp(-c)))
    a_hi, a_lo = _split(a)
    w_hi, w_lo = _split(w_ref[0])
    o_ref[0] = _dot(a_hi, w_hi) + _dot(a_hi, w_lo) + _dot(a_lo, w_hi) + b_ref[0]


def _ada(c_all, w_ada, b_ada):
    depth = w_ada.shape[0]
    rows = c_all.shape[0]
    tn = 1536
    return pl.pallas_call(
        _ada_body,
        grid=(depth, 6 * D_MODEL // tn),
        in_specs=[pl.BlockSpec((rows, D_MODEL), lambda l, j: (0, 0)),
                  pl.BlockSpec((1, D_MODEL, tn), lambda l, j: (l, 0, j)),
                  pl.BlockSpec((1, 1, tn), lambda l, j: (l, 0, j))],
        out_specs=pl.BlockSpec((1, rows, tn), lambda l, j: (l, 0, j)),
        out_shape=jax.ShapeDtypeStruct((depth, rows, 6 * D_MODEL), F32),
        compiler_params=_cparams(("parallel", "parallel")),
        name="ada",
    )(c_all, w_ada, b_ada.reshape(depth, 1, 6 * D_MODEL))


def _mla_key_tail(ckvn_b, kr_t, g, wukv_ref, bnn, bnr, brr, rope_tabs, kc_ref, vc_ref):
    kv = _dot(ckvn_b, wukv_ref[...])
    kn = kv[:, 0:512]
    vc_ref[...] = kv[:, 512:1024].astype(BF16)
    kn2 = kn * kn
    kr2 = kr_t * kr_t
    kr_sum32 = _gsum(kr2, brr)
    ssn = (_gsum(kn2, bnn) + jnp.concatenate([kr_sum32, kr_sum32], axis=-1)) * (1.0 / QK_DIM)
    ssr = (_gsum(kn2, bnr) + kr_sum32) * (1.0 / QK_DIM)
    kn = kn * lax.rsqrt(ssn + EPS) * g[5:6, 0:512]
    kr = kr_t * lax.rsqrt(ssr + EPS) * g[5:6, 512:768]
    if rope_tabs is not None:
        kr = _rotate(kr, *rope_tabs, 8)
    for p in range(4):
        kc_ref[:, 256 * p:256 * p + 128] = kn[:, 128 * p:128 * p + 128].astype(BF16)
        q4 = 128 * (p // 2)
        kc_ref[:, 256 * p + 128:256 * p + 256] = kr[:, q4:q4 + 128].astype(BF16)


def _inproj_body(rope, states, *refs):
    (x_ref, mod_ref, g_ref, w_ref, wuq_ref, wukv_ref, b64_ref, bnn_ref, brn_ref, bnr_ref,
     brr_ref) = refs[:11]
    refs = refs[11:]
    if rope:
        tabs_w = tuple(r[...] for r in refs[0:3])
        tabs_m = tuple(r[...] for r in refs[3:6])
        refs = refs[6:]
    else:
        tabs_w = tabs_m = None
    qa_ref, ka_ref, va_ref, qb_ref, kb_ref, vb_ref, qc_ref, kc_ref, vc_ref = refs[:9]
    st = refs[9:]

    g = g_ref[...]
    mod = mod_ref[0]
    hb = _norm_mod(x_ref[...], g[0:1], mod[1:2], mod[0:1]).astype(BF16)

    def proj(a, b):
        return _dot(hb, w_ref[:, a:b])

    b64 = b64_ref[...]

    def head_norm(z, bmat, gain):
        return z * lax.rsqrt(_gsum(z * z, bmat) + EPS) * gain

    qa = head_norm(proj(C_QA, C_KA), b64, g[1:2, 0:512])
    ka = head_norm(proj(C_KA, C_VA), b64[0:128, 0:128], g[1:2, 512:640])
    va = proj(C_VA, C_QB)
    if states:
        st[0][...] = ka
        st[1][...] = va
    if rope:
        qa = _rotate(qa, *tabs_w, 16)
        ka = _rotate(ka, *tabs_w, 16)
    qa_ref[...] = qa.astype(BF16)
    ka_ref[...] = ka.astype(BF16)
    va_ref[...] = va.astype(BF16)

    qb = head_norm(proj(C_QB, C_KB), b64, g[2:3, 0:512])
    kb = head_norm(proj(C_KB, C_VB), b64, g[2:3, 512:1024])
    vb = proj(C_VB, C_CQ)
    if states:
        st[2][...] = kb
        st[3][...] = vb
    qb_ref[...] = qb.astype(BF16)
    kb_ref[...] = kb.astype(BF16)
    vb_ref[...] = vb.astype(BF16)

    cq = proj(C_CQ, C_CKV)
    cqn = cq * lax.rsqrt(jnp.mean(cq * cq, axis=-1, keepdims=True) + EPS) * g[3:4, 0:256]
    qq = _dot(cqn.astype(BF16), wuq_ref[...])
    qn, qr = qq[:, 0:512], qq[:, 512:768]
    qn2, qr2 = qn * qn, qr * qr
    bnn, brn, bnr, brr = bnn_ref[...], brn_ref[...], bnr_ref[...], brr_ref[...]
    ssn = (_gsum(qn2, bnn) + _gsum(qr2, brn)) * (1.0 / QK_DIM)
    ssr = (_gsum(qn2, bnr) + _gsum(qr2, brr)) * (1.0 / QK_DIM)
    qn = qn * lax.rsqrt(ssn + EPS) * g[4:5, 0:512]
    qr = qr * lax.rsqrt(ssr + EPS) * g[4:5, 512:768]
    if rope:
        qr = _rotate(qr, *tabs_m, 8)
    for p in range(4):
        qc_ref[:, 256 * p:256 * p + 128] = qn[:, 128 * p:128 * p + 128].astype(BF16)
        q4 = 128 * (p // 2)
        qc_ref[:, 256 * p + 128:256 * p + 256] = qr[:, q4:q4 + 128].astype(BF16)

    ckv = proj(C_CKV, C_KR)
    ckvn = ckv * lax.rsqrt(jnp.mean(ckv * ckv, axis=-1, keepdims=True) + EPS) * g[3:4, 256:384]
    kr_t = proj(C_KR, C_END)
    if states:
        st[4][...] = ckvn
        st[5][...] = kr_t[:, 0:ROPE_DIM]
    _mla_key_tail(ckvn.astype(BF16), kr_t, g, wukv_ref, bnn, bnr, brr, tabs_m, kc_ref, vc_ref)


def _const_spec(shape):
    nd = len(shape)
    return pl.BlockSpec(shape, lambda i, _nd=nd: (0,) * _nd, pipeline_mode=pl.Buffered(1))


def _inproj(x, mods, rows_per_mod, gains, w_a, w_uq, w_ukv, mats, rope_tabs, seq_len, states):
    n = x.shape[0]
    tm = min(1024, n)
    rope = rope_tabs is not None
    assert n % tm == 0 and (rows_per_mod % tm == 0 or rows_per_mod == n), (n, tm, rows_per_mod)
    assert not rope or seq_len % tm == 0, (seq_len, tm)
    row = lambda w: pl.BlockSpec((tm, w), lambda i: (i, 0))
    in_specs = [row(D_MODEL),
                pl.BlockSpec((1, 8, D_MODEL), lambda i: ((i * tm) // rows_per_mod, 0, 0)),
                _const_spec(gains.shape), _const_spec(w_a.shape), _const_spec(w_uq.shape),
                _const_spec(w_ukv.shape)] + [_const_spec(m.shape) for m in mats]
    args = [x, mods, gains, w_a, w_uq, w_ukv, *mats]
    if rope:
        tiles_per_seq = seq_len // tm
        in_specs += [pl.BlockSpec((tm, LANES), lambda i: (i % tiles_per_seq, 0))] * 6
        args += list(rope_tabs)
    widths = [512, 128, 128, 512, 512, 512, 1024, 1024, 512]
    out_shape = [jax.ShapeDtypeStruct((n, w), BF16) for w in widths]
    out_specs = [row(w) for w in widths]
    if states:
        swidths = [128, 128, 512, 512, KV_LORA, ROPE_DIM]
        out_shape += [jax.ShapeDtypeStruct((n, w), F32) for w in swidths]
        out_specs += [row(w) for w in swidths]
    return pl.pallas_call(
        functools.partial(_inproj_body, rope, states),
        grid=(n // tm,), in_specs=in_specs, out_specs=out_specs, out_shape=out_shape,
        compiler_params=_cparams(("parallel",)),
        name="inproj_lat" if rope else "inproj_ctx",
    )(*args)


def _mla_cache_body(ckv_ref, kr_ref, g_ref, wukv_ref, bnn_ref, bnr_ref, brr_ref, kc_ref, vc_ref):
    _mla_key_tail(ckv_ref[...].astype(BF16), kr_ref[...], g_ref[...], wukv_ref, bnn_ref[...],
                  bnr_ref[...], brr_ref[...], None, kc_ref, vc_ref)


def _mla_cache_keys(ckv, kr_t, gains, w_ukv, bnn, bnr, brr):
    n = ckv.shape[0]
    tm = min(512, n)
    row = lambda w: pl.BlockSpec((tm, w), lambda i: (i, 0))
    return pl.pallas_call(
        _mla_cache_body,
        grid=(n // tm,),
        in_specs=[row(KV_LORA), row(256), _const_spec(gains.shape), _const_spec(w_ukv.shape),
                  _const_spec(bnn.shape), _const_spec(bnr.shape), _const_spec(brr.shape)],
        out_specs=[row(1024), row(512)],
        out_shape=[jax.ShapeDtypeStruct((n, 1024), BF16), jax.ShapeDtypeStruct((n, 512), BF16)],
        compiler_params=_cparams(("parallel",)),
        name="mla_cache_keys",
    )(ckv, kr_t, gains, w_ukv, bnn, bnr, brr)


def _lane_mask(width, ranges):
    lane = lax.broadcasted_iota(jnp.int32, (1, width), 1)
    m = None
    for lo, hi in ranges:
        c = (lane >= lo) & (lane < hi)
        m = c if m is None else (m | c)
    return jnp.where(m, 1.0, 0.0).astype(BF16)


def _stack_heads(q, mask0, mask1):
    return jnp.concatenate([q * mask0, q * mask1], axis=0)


def _lane_tiles(x):
    return [x[:, j:j + LANES] for j in range(0, x.shape[1], LANES)]


def _softmax_block(scores, sink=None):
    rows = scores[0].shape[0]
    mp = None
    for s in scores:
        for t in _lane_tiles(s):
            mp = t if mp is None else jnp.maximum(mp, t)
    base = sink if sink is not None else jnp.full((rows, LANES), NEG, F32)
    m = jnp.maximum(base, jnp.max(mp, axis=-1, keepdims=True))
    lp = None
    ps = []
    for s in scores:
        p = jnp.exp2(s - _tile_lanes(m, s.shape[1]))
        for t in _lane_tiles(p):
            lp = t if lp is None else lp + t
        ps.append(p.astype(BF16))
    if sink is not None:
        lane = lax.broadcasted_iota(jnp.int32, (rows, LANES), 1)
        lp = lp + jnp.where(lane == 0, jnp.exp2(sink - m), 0.0)
    p_all = ps[0] if len(ps) == 1 else jnp.concatenate(ps, axis=-1)
    return p_all, jnp.broadcast_to(jnp.sum(lp, axis=-1, keepdims=True), (rows, LANES))


def _softmax_pv(scores, values, sink=None):
    p_all, l = _softmax_block(scores, sink)
    v_all = values[0] if len(values) == 1 else jnp.concatenate(values, axis=0)
    return _dot(p_all, v_all) / l


def _merge_heads(o, tq):
    lane = lax.broadcasted_iota(jnp.int32, (tq, LANES), 1)
    return jnp.where(lane < HEAD_DIM, o[0:tq], o[tq:2 * tq])


def _mla_masks(p_mod2):
    lane = lax.broadcasted_iota(jnp.int32, (1, 256), 1)
    r0 = 128 + 32 * (2 * p_mod2)
    m0 = (lane < 64) | ((lane >= r0) & (lane < r0 + 32))
    m1 = ((lane >= 64) & (lane < 128)) | ((lane >= r0 + 32) & (lane < r0 + 64))
    return (jnp.where(m0, 1.0, 0.0).astype(BF16), jnp.where(m1, 1.0, 0.0).astype(BF16))


def _sink_col(sink_ref, layer, h0, h1, tq):
    row = lax.broadcasted_iota(jnp.int32, (2 * tq, LANES), 0)
    return jnp.where(row < tq, sink_ref[layer, h0], sink_ref[layer, h1])


def _ctx_attn_body(layer, sink_ref, qa_ref, ka_ref, va_ref, qb_ref, kb_ref, vb_ref, qc_ref, kc_ref,
                   vc_ref, oa_ref, ob_ref, oc_ref):
    tq = qa_ref.shape[0]
    lo = _lane_mask(LANES, [(0, 64)])
    hi = _lane_mask(LANES, [(64, 128)])
    ka, va = ka_ref[...], va_ref[...]
    for j in range(4):
        sl = slice(128 * j, 128 * j + 128)
        qs = _stack_heads(qa_ref[:, sl], lo, hi)
        sink = _sink_col(sink_ref, layer, j, 4 + j, tq)
        o = _softmax_pv([_dot_nt(qs, ka)], [va], sink)
        oa_ref[:, sl] = _merge_heads(o, tq).astype(BF16)

        qs = _stack_heads(qb_ref[:, sl], lo, hi)
        o = _softmax_pv([_dot_nt(qs, kb_ref[:, sl])], [vb_ref[:, sl]])
        ob_ref[:, sl] = _merge_heads(o, tq).astype(BF16)

        m0, m1 = _mla_masks(j % 2)
        s2 = slice(256 * j, 256 * j + 256)
        qs = _stack_heads(qc_ref[:, s2], m0, m1)
        o = _softmax_pv([_dot_nt(qs, kc_ref[:, s2])], [vc_ref[:, sl]])
        oc_ref[:, sl] = _merge_heads(o, tq).astype(BF16)


def _ctx_attn(layer, sink, seq, qa, ka, va, qb, kb, vb, qc, kc, vc):
    n = qa.shape[0]
    row = lambda w: pl.BlockSpec((seq, w), lambda b: (b, 0))
    ins = [qa, ka, va, qb, kb, vb, qc, kc, vc]
    return pl.pallas_call(
        functools.partial(_ctx_attn_body, layer),
        grid=(n // seq,),
        in_specs=[pl.BlockSpec(memory_space=pltpu.SMEM)] + [row(a.shape[1]) for a in ins],
        out_specs=[row(512)] * 3,
        out_shape=[jax.ShapeDtypeStruct((n, 512), BF16)] * 3,
        compiler_params=_cparams(("parallel",)),
        name="ctx_attn",
    )(sink, *ins)


WIN_ROW_BLOCK = 32


WIN_Q_BLOCK = 128
WIN_BLOCKS_PER_STEP = 4


def _win_body(layer, sink_ref, q_ref, k_ref, v_ref, kc_ref, vc_ref, o_ref, s_sc, p_sc, l_sc):
    tq = WIN_Q_BLOCK
    seq = k_ref.shape[1]
    kw = 3 * tq
    rb = WIN_ROW_BLOCK
    lo = _lane_mask(LANES, [(0, 64)])
    hi = _lane_mask(LANES, [(64, 128)])
    kc, vc = kc_ref[0, 0], vc_ref[0, 0]
    for u in range(q_ref.shape[1] // tq):
        i = pl.program_id(1) * (q_ref.shape[1] // tq) + u
        kstart = pl.multiple_of(jnp.clip((i - 1) * tq, 0, seq - kw), tq)
        k_all = jnp.concatenate([k_ref[0, pl.ds(kstart, kw), :], kc], axis=0)
        v_all = jnp.concatenate([v_ref[0, pl.ds(kstart, kw), :], vc], axis=0)
        q_pos = i * tq + lax.broadcasted_iota(jnp.int32, (tq, kw), 0)
        k_pos = kstart + lax.broadcasted_iota(jnp.int32, (tq, kw), 1)
        band = jnp.abs(q_pos - k_pos) <= WINDOW
        qs = jnp.concatenate(
            [_stack_heads(q_ref[0, u * tq:(u + 1) * tq, 128 * j:128 * j + 128], lo, hi) for j in range(4)],
            axis=0)
        s_sc[u] = _dot_nt(qs, k_all)
        for j in range(4):
            for r in range(2 * tq * j, 2 * tq * (j + 1), rb):
                head = j if r < 2 * tq * j + tq else 4 + j
                q0 = r % tq
                s_band = jnp.where(band[q0:q0 + rb], s_sc[u, r:r + rb, 0:kw], NEG)
                sink = jnp.full((rb, LANES), sink_ref[layer, head], F32)
                p, l = _softmax_block([s_band, s_sc[u, r:r + rb, kw:]], sink)
                p_sc[u, r:r + rb, :] = p
                l_sc[u, r:r + rb, :] = l
            rows = slice(2 * tq * j, 2 * tq * (j + 1))
            o = _dot(p_sc[u, rows, :], v_all) / l_sc[u, rows, :]
            o_ref[0, u * tq:(u + 1) * tq, 128 * j:128 * j + 128] = _merge_heads(o, tq).astype(BF16)


def _win_attn(layer, sink, q, k, v, kc, vc):
    b, seq, _ = q.shape
    nb = min(WIN_BLOCKS_PER_STEP, seq // WIN_Q_BLOCK)
    tq = nb * WIN_Q_BLOCK
    past = kc.shape[2]
    assert WIN_Q_BLOCK == WINDOW and seq % tq == 0 and seq >= 3 * WIN_Q_BLOCK and past % LANES == 0
    keys = 3 * WIN_Q_BLOCK + past
    return pl.pallas_call(
        functools.partial(_win_body, layer),
        grid=(b, seq // tq),
        in_specs=[pl.BlockSpec(memory_space=pltpu.SMEM),
                  pl.BlockSpec((1, tq, 512), lambda bi, i: (bi, i, 0)),
                  pl.BlockSpec((1, seq, 128), lambda bi, i: (bi, 0, 0)),
                  pl.BlockSpec((1, seq, 128), lambda bi, i: (bi, 0, 0)),
                  pl.BlockSpec((1, 1, past, 128), lambda bi, i: (bi, layer, 0, 0)),
                  pl.BlockSpec((1, 1, past, 128), lambda bi, i: (bi, layer, 0, 0))],
        out_specs=pl.BlockSpec((1, tq, 512), lambda bi, i: (bi, i, 0)),
        out_shape=jax.ShapeDtypeStruct((b, seq, 512), BF16),
        scratch_shapes=[pltpu.VMEM((nb, 8 * WIN_Q_BLOCK, keys), F32),
                        pltpu.VMEM((nb, 8 * WIN_Q_BLOCK, keys), BF16),
                        pltpu.VMEM((nb, 8 * WIN_Q_BLOCK, LANES), F32)],
        compiler_params=_cparams(("parallel", "arbitrary")),
        name="win_attn",
    )(sink, q, k, v, kc, vc)


NBR_TILE_ROWS = 4
NBR_WIN_ROWS = NBR_TILE_ROWS + NBR_ROWS
NBR_TAB_PAD = NBR_WIN_ROWS - NBR_ROWS
NBR_ROW_BLOCK = 32
NBR_TILES_PER_STEP = 2


def _nbr_body(rows, q_ref, k_ref, v_ref, kc_ref, vc_ref, tab_ref, o_ref, s_sc, p_sc, l_sc):
    tq = NBR_TILE_ROWS * GRID_W
    kw = NBR_WIN_ROWS * GRID_W
    lo = _lane_mask(LANES, [(0, 64)])
    hi = _lane_mask(LANES, [(64, 128)])
    past = kc_ref.shape[2]
    rb = NBR_ROW_BLOCK
    tiles = q_ref.shape[1] // tq
    for u in range(tiles):
        r0 = NBR_TILE_ROWS * (pl.program_id(1) * tiles + u)
        ws = jnp.clip(r0 - NBR_ROWS // 2, 0, rows - NBR_WIN_ROWS)
        kstart = pl.multiple_of(ws * GRID_W, LANES)
        k_row = ws + lax.broadcasted_iota(jnp.int32, (1, kw), 1) // GRID_W
        for j in range(4):
            sl = slice(128 * j, 128 * j + 128)
            qs = _stack_heads(q_ref[0, u * tq:(u + 1) * tq, sl], lo, hi)
            s_sc[u, j, :, 0:kw] = _dot_nt(qs, k_ref[0, pl.ds(kstart, kw), sl])
            s_sc[u, j, :, kw:kw + past] = _dot_nt(qs, kc_ref[0, 0, :, sl])
        for j in range(4):
            sl = slice(128 * j, 128 * j + 128)
            for b0 in range(0, 2 * tq, rb):
                h = 2 * j + b0 // tq
                ql, sub = divmod(b0 % tq, GRID_W)
                d0 = ws - r0 - ql + (NBR_ROWS - 1) + NBR_TAB_PAD
                bias = jnp.concatenate([tab_ref[h, d0 + 2 * m, sub:sub + rb, :]
                                        for m in range(NBR_WIN_ROWS // 2)], axis=-1)
                rs = jnp.clip(r0 + ql - NBR_ROWS // 2, 0, rows - NBR_ROWS)
                valid = (k_row >= rs) & (k_row < rs + NBR_ROWS)
                s_nb = jnp.where(valid, s_sc[u, j, b0:b0 + rb, 0:kw] + bias, NEG)
                p, l = _softmax_block([s_nb, s_sc[u, j, b0:b0 + rb, kw:kw + past]])
                p_sc[u, j, b0:b0 + rb, :] = p
                l_sc[u, j, b0:b0 + rb, :] = l
            v_all = jnp.concatenate([v_ref[0, pl.ds(kstart, kw), sl], vc_ref[0, 0, :, sl]], axis=0)
            o = _dot(p_sc[u, j], v_all) / l_sc[u, j]
            o_ref[0, u * tq:(u + 1) * tq, sl] = _merge_heads(o, tq).astype(BF16)


def _nbr_attn(layer, q, k, v, kc, vc, table):
    b, seq, _ = q.shape
    rows = seq // GRID_W
    tile = NBR_TILE_ROWS * GRID_W
    tq = NBR_TILES_PER_STEP * tile
    assert seq % tq == 0 and rows >= NBR_WIN_ROWS and (rows - NBR_WIN_ROWS) % 2 == 0, (seq, rows)
    assert NBR_TILE_ROWS % 2 == 0 and (NBR_ROWS // 2) % 2 == 0 and GRID_W % NBR_ROW_BLOCK == 0
    past = kc.shape[2]
    keys = NBR_WIN_ROWS * GRID_W + past
    once = pl.Buffered(1)
    return pl.pallas_call(
        functools.partial(_nbr_body, rows),
        grid=(b, seq // tq),
        in_specs=[pl.BlockSpec((1, tq, 512), lambda bi, i: (bi, i, 0)),
                  pl.BlockSpec((1, seq, 512), lambda bi, i: (bi, 0, 0), pipeline_mode=once),
                  pl.BlockSpec((1, seq, 512), lambda bi, i: (bi, 0, 0), pipeline_mode=once),
                  pl.BlockSpec((1, 1, past, 512), lambda bi, i: (bi, layer, 0, 0), pipeline_mode=once),
                  pl.BlockSpec((1, 1, past, 512), lambda bi, i: (bi, layer, 0, 0), pipeline_mode=once),
                  pl.BlockSpec(table.shape, lambda bi, i: (0, 0, 0, 0), pipeline_mode=once)],
        out_specs=pl.BlockSpec((1, tq, 512), lambda bi, i: (bi, i, 0)),
        out_shape=jax.ShapeDtypeStruct((b, seq, 512), BF16),
        scratch_shapes=[pltpu.VMEM((NBR_TILES_PER_STEP, 4, 2 * tile, keys), F32),
                        pltpu.VMEM((NBR_TILES_PER_STEP, 4, 2 * tile, keys), BF16),
                        pltpu.VMEM((NBR_TILES_PER_STEP, 4, 2 * tile, LANES), F32)],
        compiler_params=_cparams(("parallel", "arbitrary")),
        name="nbr_attn",
    )(q, k, v, kc, vc, table)


def _nbr_bias_table(rel_bias):
    col = np.arange(GRID_W)
    cs = np.clip(col - NBR_COLS // 2, 0, GRID_W - NBR_COLS)
    kc = np.arange(GRID_W)
    ok = (kc[None, :] >= cs[:, None]) & (kc[None, :] < cs[:, None] + NBR_COLS)
    dc = kc[None, :] - col[:, None] + (NBR_COLS - 1)
    pick = (dc[:, :, None] == np.arange(2 * NBR_COLS - 1)[None, None, :]) & ok[:, :, None]
    t = jnp.einsum("hdk,qck->hdqc", rel_bias.astype(F32) * LOG2E, jnp.asarray(pick, F32),
                   precision=lax.Precision.HIGHEST)
    t = jnp.where(jnp.asarray(ok)[None, None], t, NEG)
    t = jnp.pad(t, ((0, 0), (NBR_TAB_PAD, NBR_TAB_PAD), (0, 0), (0, 0)))
    return jnp.concatenate([t[:, :-1], t[:, 1:]], axis=-1)


MLA_Q_TILE = 1024
MLA_KEY_CHUNK = 512
MLA_ROW_BLOCK = 64


def _mla_body(q_ref, kl_ref, kc_ref, vl_ref, vc_ref, o_ref, qs_sc, m_sc, l_sc, acc_sc, s_sc, p_sc, a_sc):
    tq = q_ref.shape[1]
    rows = 2 * tq
    seq = kl_ref.shape[1]
    past = kc_ref.shape[1]
    tk = min(MLA_KEY_CHUNK, seq)
    rb = MLA_ROW_BLOCK
    m0, m1 = _mla_masks(pl.program_id(1) % 2)
    qs_sc[...] = _stack_heads(q_ref[0], m0, m1)
    m_sc[...] = jnp.full(m_sc.shape, NEG, F32)
    l_sc[...] = jnp.zeros(l_sc.shape, F32)
    acc_sc[...] = jnp.zeros(acc_sc.shape, F32)

    def step(c, k, v):
        slab = c % 2
        n = k.shape[0]
        s_sc[slab, :, 0:n] = _dot_nt(qs_sc[...], k)
        for r in range(0, rows, rb):
            sl = slice(r, r + rb)
            sb = s_sc[slab, sl, 0:n]
            mp = None
            for t in _lane_tiles(sb):
                mp = t if mp is None else jnp.maximum(mp, t)
            m_prev = m_sc[sl]
            m_new = jnp.maximum(m_prev, jnp.max(mp, axis=-1, keepdims=True))
            alpha = jnp.exp2(m_prev - m_new)
            p = jnp.exp2(sb - _tile_lanes(m_new, n))
            psum = None
            for t in _lane_tiles(p):
                psum = t if psum is None else psum + t
            l_sc[sl] = alpha * l_sc[sl] + psum
            m_sc[sl] = m_new
            a_sc[slab, sl, :] = alpha
            p_sc[slab, sl, 0:n] = p.astype(BF16)
        acc_sc[...] = a_sc[slab] * acc_sc[...] + _dot(p_sc[slab, :, 0:n], v)

    chunks = [(kl_ref, vl_ref, o, min(tk, seq - o)) for o in range(0, seq, tk)]
    chunks += [(kc_ref, vc_ref, o, min(tk, past - o)) for o in range(0, past, tk)]
    for c, (k_ref, v_ref, o, n) in enumerate(chunks):
        step(c, k_ref[0, o:o + n, :], v_ref[0, o:o + n, :])
    l = jnp.sum(l_sc[...], axis=-1, keepdims=True)
    o_ref[0] = _merge_heads(acc_sc[...] / l, tq).astype(BF16)


def _mla_attn(q, kl, vl, kc, vc):
    b, seq, _ = q.shape
    past = kc.shape[1]
    tq = min(MLA_Q_TILE, seq)
    tk = min(MLA_KEY_CHUNK, seq)
    assert seq % tq == 0 and seq % LANES == 0 and past % LANES == 0 and (2 * tq) % MLA_ROW_BLOCK == 0
    return pl.pallas_call(
        _mla_body,
        grid=(b, 4, seq // tq),
        in_specs=[pl.BlockSpec((1, tq, 256), lambda bi, p, qi: (bi, qi, p)),
                  pl.BlockSpec((1, seq, 256), lambda bi, p, qi: (bi, 0, p)),
                  pl.BlockSpec((1, past, 256), lambda bi, p, qi: (bi, 0, p)),
                  pl.BlockSpec((1, seq, 128), lambda bi, p, qi: (bi, 0, p)),
                  pl.BlockSpec((1, past, 128), lambda bi, p, qi: (bi, 0, p))],
        out_specs=pl.BlockSpec((1, tq, 128), lambda bi, p, qi: (bi, qi, p)),
        out_shape=jax.ShapeDtypeStruct((b, seq, 512), BF16),
        scratch_shapes=[pltpu.VMEM((2 * tq, 256), BF16), pltpu.VMEM((2 * tq, LANES), F32),
                        pltpu.VMEM((2 * tq, LANES), F32), pltpu.VMEM((2 * tq, LANES), F32),
                        pltpu.VMEM((2, 2 * tq, tk), F32), pltpu.VMEM((2, 2 * tq, tk), BF16),
                        pltpu.VMEM((2, 2 * tq, LANES), F32)],
        compiler_params=_cparams(("parallel", "parallel", "arbitrary")),
        name="mla_attn",
    )(q, kl, kc, vl, vc)


def _pack_pairs(x):
    w = x.shape[1] // 2
    hi = lax.bitcast_convert_type(x[:, :w].astype(BF16).astype(F32), jnp.int32)
    lo = lax.bitcast_convert_type(x[:, w:].astype(BF16).astype(F32), jnp.int32)
    return (hi & jnp.int32(-65536)) | lax.shift_right_logical(lo, jnp.int32(16))


def _unpack_pairs(p):
    hi = lax.bitcast_convert_type(p & jnp.int32(-65536), F32)
    lo = lax.bitcast_convert_type(lax.shift_left(p, jnp.int32(16)), F32)
    return jnp.concatenate([hi, lo], axis=-1)


def _merge_body(x_ref, oa_ref, ob_ref, oc_ref, mod_ref, g_ref, wg_ref, woa_ref, wob_ref, woc_ref,
                wout_ref, wr_ref, br_ref, tri_ref, x1_ref, route_ref, h2p_ref, count_ref, count_sc):
    x = x_ref[...]
    g = g_ref[...]
    mod = mod_ref[0]
    hb = _norm_mod(x, g[0:1], mod[1:2], mod[0:1]).astype(BF16)
    m = None
    for br, (o_ref, wo_ref) in enumerate(((oa_ref, woa_ref), (ob_ref, wob_ref), (oc_ref, woc_ref))):
        z = _dot(hb, wg_ref[:, D_MODEL * br:D_MODEL * (br + 1)])
        gate = 1.0 / (1.0 + jnp.exp(-z))
        t = gate * _dot(o_ref[...], wo_ref[...])
        m = t if m is None else m + t
    y = _dot(m.astype(BF16), wout_ref[...])
    x1 = x + mod[2:3] * y
    x1_ref[...] = x1

    h2 = _norm_mod(x1, g[6:7], mod[4:5], mod[3:4])
    h_hi, h_lo = _split(h2)
    w_hi, w_lo = _split(wr_ref[...])
    both_w = _dot_nt(jnp.concatenate([w_hi, w_lo], axis=0), h_hi)
    logits = both_w[0:N_EXPERTS] + both_w[N_EXPERTS:2 * N_EXPERTS] + _dot_nt(w_hi, h_lo)
    score = 1.0 / (1.0 + jnp.exp(-logits))
    sel = score + br_ref[...]
    sel_r = [sel[e:e + 1] for e in range(N_EXPERTS)]
    sc_r = [score[e:e + 1] for e in range(N_EXPERTS)]
    picked = []
    for e in range(N_EXPERTS):
        grp, a = divmod(e, EXPERTS_PER_GROUP)
        rank = None
        for bb in range(EXPERTS_PER_GROUP):
            if bb == a:
                continue
            o = sel_r[grp * EXPERTS_PER_GROUP + bb]
            beats = (o >= sel_r[e]) if bb < a else (o > sel_r[e])
            r = jnp.where(beats, 1.0, 0.0)
            rank = r if rank is None else rank + r
        picked.append(rank < 2.0)
    gscore = []
    for grp in range(N_GROUPS):
        tot = None
        for a in range(EXPERTS_PER_GROUP):
            e = grp * EXPERTS_PER_GROUP + a
            t = jnp.where(picked[e], sel_r[e], 0.0)
            tot = t if tot is None else tot + t
        gscore.append(tot)
    best = jnp.zeros_like(gscore[0])
    best_v = gscore[0]
    for grp in range(1, N_GROUPS):
        upd = gscore[grp] > best_v
        best = jnp.where(upd, float(grp), best)
        best_v = jnp.where(upd, gscore[grp], best_v)
    cw, pk = [], []
    for a in range(EXPERTS_PER_GROUP):
        tot = flag = None
        for grp in range(N_GROUPS):
            e = grp * EXPERTS_PER_GROUP + a
            f = (best == float(grp)) & picked[e]
            t = jnp.where(f, sc_r[e], 0.0)
            tot = t if tot is None else tot + t
            flag = f if flag is None else (flag | f)
        cw.append(tot)
        pk.append(flag)
    den = cw[0] + cw[1] + cw[2] + cw[3]
    first = jnp.where(pk[0], 0.0, jnp.where(pk[1], 1.0, jnp.where(pk[2], 2.0, 3.0)))
    second = jnp.where(pk[3], 3.0, jnp.where(pk[2], 2.0, jnp.where(pk[1], 1.0, 0.0)))
    slot_e, slot_w = [], []
    for which in (first, second):
        tot = None
        for a in range(EXPERTS_PER_GROUP):
            t = jnp.where(which == float(a), cw[a], 0.0)
            tot = t if tot is None else tot + t
        slot_w.append(tot / den)
        slot_e.append(best * float(EXPERTS_PER_GROUP) + which)

    @pl.when(pl.program_id(0) == 0)
    def _():
        count_sc[...] = jnp.zeros(count_sc.shape, F32)

    tm = x.shape[0]
    eid = lax.broadcasted_iota(jnp.int32, (N_EXPERTS, tm), 0).astype(F32)
    oh = [eid == slot_e[0], eid == slot_e[1]]
    both = jnp.where(oh[0] | oh[1], 1.0, 0.0)
    seen = count_sc[...][:, 0:1] + _dot(both.astype(BF16), tri_ref[...])
    for k in range(2):
        route_ref[k:k + 1, :] = slot_w[k]
        route_ref[2 + k:3 + k, :] = slot_e[k]
        route_ref[4 + k:5 + k, :] = jnp.sum(jnp.where(oh[k], seen, 0.0), axis=0, keepdims=True)
    route_ref[6:8, :] = jnp.zeros((2, tm), F32)
    count_sc[...] = count_sc[...] + jnp.sum(both, axis=-1, keepdims=True)
    count_ref[...] = count_sc[...]
    h2p_ref[...] = _pack_pairs(h2)


def _merge(x, oa, ob, oc, mods, rows_per_mod, gains, w_gate, wo_a, wo_b, wo_c, w_out, w_r_t, b_r):
    n = x.shape[0]
    tm = min(1024, n)
    assert n % tm == 0 and (rows_per_mod % tm == 0 or rows_per_mod == n), (n, tm, rows_per_mod)
    row = lambda w: pl.BlockSpec((tm, w), lambda i: (i, 0))
    tri = jnp.asarray(np.triu(np.ones((tm, tm), np.float32), 1), BF16)
    consts = [gains, w_gate, wo_a, wo_b, wo_c, w_out, w_r_t, b_r, tri]
    return pl.pallas_call(
        _merge_body,
        grid=(n // tm,),
        in_specs=[row(D_MODEL), row(512), row(512), row(512),
                  pl.BlockSpec((1, 8, D_MODEL), lambda i: ((i * tm) // rows_per_mod, 0, 0))]
                 + [_const_spec(c.shape) for c in consts],
        out_specs=[row(D_MODEL), pl.BlockSpec((8, tm), lambda i: (0, i)), row(512),
                   pl.BlockSpec((N_EXPERTS, LANES), lambda i: (0, 0))],
        out_shape=[jax.ShapeDtypeStruct((n, D_MODEL), F32), jax.ShapeDtypeStruct((8, n), F32),
                   jax.ShapeDtypeStruct((n, 512), jnp.int32),
                   jax.ShapeDtypeStruct((N_EXPERTS, LANES), F32)],
        scratch_shapes=[pltpu.VMEM((N_EXPERTS, LANES), F32)],
        compiler_params=_cparams(("arbitrary",)),
        name="merge",
    )(x, oa, ob, oc, mods, *consts)


EXPERT_TILE = 512
SC_CORES = 2
SC_SUBCORES = 16
SC_WORKERS = SC_CORES * SC_SUBCORES
SC_WINDOW = 128


def _sc_mesh():
    return plsc.VectorSubcoreMesh(core_axis_name="c", subcore_axis_name="s", num_cores=SC_CORES,
                                  num_subcores=SC_SUBCORES)


def _sc_window_base(steps, j):
    wid = lax.axis_index("s") * SC_CORES + lax.axis_index("c")
    return pl.multiple_of((wid * steps + j) * SC_WINDOW, SC_WINDOW)


def _sc_dispatch(rows, pos0, pos1, n_out):
    n, w = rows.shape
    assert n % (SC_WORKERS * SC_WINDOW) == 0, n
    steps = n // (SC_WORKERS * SC_WINDOW)

    @functools.partial(
        pl.kernel, out_type=jax.ShapeDtypeStruct((n_out, w), rows.dtype), mesh=_sc_mesh(),
        scratch_types=[pltpu.VMEM((SC_WINDOW,), jnp.int32), pltpu.VMEM((SC_WINDOW,), jnp.int32),
                       pltpu.VMEM((SC_WINDOW, w), rows.dtype)],
        name="moe_dispatch")
    def run(x_hbm, i0_hbm, i1_hbm, o_hbm, i0_v, i1_v, rows_v):
        @pl.loop(0, steps)
        def _(j):
            base = _sc_window_base(steps, j)
            pltpu.sync_copy(i0_hbm.at[pl.ds(base, SC_WINDOW)], i0_v)
            pltpu.sync_copy(i1_hbm.at[pl.ds(base, SC_WINDOW)], i1_v)
            pltpu.sync_copy(x_hbm.at[pl.ds(base, SC_WINDOW)], rows_v)
            pltpu.sync_copy(rows_v, o_hbm.at[i0_v])
            pltpu.sync_copy(rows_v, o_hbm.at[i1_v])

    return run(rows, pos0, pos1)


def _sc_collect(rows, pos0, pos1):
    n = pos0.shape[0]
    w = rows.shape[1]
    assert n % (SC_WORKERS * SC_WINDOW) == 0, n
    steps = n // (SC_WORKERS * SC_WINDOW)
    out = jax.ShapeDtypeStruct((n, w), rows.dtype)

    @functools.partial(
        pl.kernel, out_type=[out, out], mesh=_sc_mesh(),
        scratch_types=[pltpu.VMEM((SC_WINDOW,), jnp.int32), pltpu.VMEM((SC_WINDOW, w), rows.dtype)],
        name="moe_collect")
    def run(y_hbm, i0_hbm, i1_hbm, o0_hbm, o1_hbm, i_v, rows_v):
        @pl.loop(0, steps)
        def _(j):
            base = _sc_window_base(steps, j)
            for i_hbm, o_hbm in ((i0_hbm, o0_hbm), (i1_hbm, o1_hbm)):
                pltpu.sync_copy(i_hbm.at[pl.ds(base, SC_WINDOW)], i_v)
                pltpu.sync_copy(y_hbm.at[i_v], rows_v)
                pltpu.sync_copy(rows_v, o_hbm.at[pl.ds(base, SC_WINDOW)])

    return run(rows, pos0, pos1)


def _experts_body(te_ref, nv_ref, xs_ref, wg_ref, wu_ref, wd_ref, ys_ref, wg_sc, wu_sc, wd_sc):
    j = pl.program_id(0)

    @pl.when((j == 0) | (te_ref[j] != te_ref[jnp.maximum(j - 1, 0)]))
    def _():
        wg_sc[...] = wg_ref[0, 0].astype(BF16)
        wu_sc[...] = wu_ref[0, 0].astype(BF16)
        wd_sc[...] = wd_ref[0, 0].astype(BF16)

    @pl.when(j < nv_ref[0])
    def _():
        x = _unpack_pairs(xs_ref[...]).astype(BF16)
        zg = _dot(x, wg_sc[...])
        act = zg * (1.0 / (1.0 + jnp.exp(-zg))) * _dot(x, wu_sc[...])
        ys_ref[...] = _pack_pairs(_dot(act.astype(BF16), wd_sc[...]))

    @pl.when(j >= nv_ref[0])
    def _():
        ys_ref[...] = jnp.zeros(ys_ref.shape, ys_ref.dtype)


def _experts(layer, xs, tile_expert, n_valid, wg, wu, wd):
    p = xs.shape[0]
    wspec = lambda r, c: pl.BlockSpec((1, 1, r, c), lambda j, te, nv: (layer, te[j], 0, 0))
    grid_spec = pltpu.PrefetchScalarGridSpec(
        num_scalar_prefetch=2,
        grid=(p // EXPERT_TILE,),
        in_specs=[pl.BlockSpec((EXPERT_TILE, 512), lambda j, te, nv: (j, 0)),
                  wspec(D_MODEL, D_FF), wspec(D_MODEL, D_FF), wspec(D_FF, D_MODEL)],
        out_specs=pl.BlockSpec((EXPERT_TILE, 512), lambda j, te, nv: (j, 0)),
        scratch_shapes=[pltpu.VMEM((D_MODEL, D_FF), BF16), pltpu.VMEM((D_MODEL, D_FF), BF16),
                        pltpu.VMEM((D_FF, D_MODEL), BF16)])
    return pl.pallas_call(
        _experts_body, grid_spec=grid_spec,
        out_shape=jax.ShapeDtypeStruct((p, 512), jnp.int32),
        compiler_params=_cparams(("arbitrary",)),
        name="experts",
    )(tile_expert, n_valid, xs, wg, wu, wd)


def _combine_body(x1_ref, y0_ref, y1_ref, route_ref, mod_ref, o_ref):
    tm = x1_ref.shape[0]
    eye = jnp.where(lax.broadcasted_iota(jnp.int32, (tm, tm), 0)
                    == lax.broadcasted_iota(jnp.int32, (tm, tm), 1), 1.0, 0.0).astype(BF16)
    r_hi, r_lo = _split(route_ref[...])
    wcol = _dot_nt(eye, r_hi) + _dot_nt(eye, r_lo)
    moe = wcol[:, 0:1] * _unpack_pairs(y0_ref[...]) + wcol[:, 1:2] * _unpack_pairs(y1_ref[...])
    o_ref[...] = x1_ref[...] + mod_ref[0][5:6] * moe


def _combine(x1, y0, y1, route, mods, rows_per_mod):
    n = x1.shape[0]
    tm = min(512, n)
    row = lambda w: pl.BlockSpec((tm, w), lambda i: (i, 0))
    return pl.pallas_call(
        _combine_body,
        grid=(n // tm,),
        in_specs=[row(D_MODEL), row(512), row(512), pl.BlockSpec((8, tm), lambda i: (0, i)),
                  pl.BlockSpec((1, 8, D_MODEL), lambda i: ((i * tm) // rows_per_mod, 0, 0))],
        out_specs=row(D_MODEL),
        out_shape=jax.ShapeDtypeStruct((n, D_MODEL), F32),
        compiler_params=_cparams(("parallel",)),
        name="combine",
    )(x1, y0, y1, route, mods)


def _moe(layer, x1, route, h2p, counts, mods, rows_per_mod, wg, wu, wd):
    n = x1.shape[0]
    p = 2 * n + N_EXPERTS * EXPERT_TILE
    cnt = counts[:, 0].astype(jnp.int32)
    padded = (cnt + EXPERT_TILE - 1) // EXPERT_TILE * EXPERT_TILE
    seg_end = jnp.cumsum(padded)
    seg_off = seg_end - padded
    experts = jnp.arange(N_EXPERTS, dtype=jnp.int32)

    def position(k):
        e = route[2 + k].astype(jnp.int32)
        off = jnp.sum(jnp.where(e[:, None] == experts[None], seg_off[None], 0), axis=1)
        return off + route[4 + k].astype(jnp.int32)

    pos0, pos1 = position(0), position(1)
    tile_start = jnp.arange(p // EXPERT_TILE, dtype=jnp.int32) * EXPERT_TILE
    tile_expert = jnp.sum(tile_start[:, None] >= seg_end[None], axis=1).astype(jnp.int32)
    tile_expert = jnp.minimum(tile_expert, N_EXPERTS - 1)
    n_valid = (seg_end[-1:] // EXPERT_TILE).astype(jnp.int32)

    xs = _sc_dispatch(h2p, pos0, pos1, p)
    ys = _experts(layer, xs, tile_expert, n_valid, wg, wu, wd)
    y0, y1 = _sc_collect(ys, pos0, pos1)
    return _combine(x1, y0, y1, route, mods, rows_per_mod)


def _block_ones(n_in, g_in, n_out, g_out, value=1.0):
    r = np.arange(n_in)[:, None] // g_in
    c = np.arange(n_out)[None, :] // g_out
    return jnp.asarray(np.where(r == c, value, 0.0), dtype=BF16)


def _rope_tables(seq, head_w):
    pos = np.arange(seq)
    rows, cols = pos // GRID_W, pos % GRID_W
    a = head_w // 2
    half = a // 2
    freqs = (ROPE_BASE ** (-np.arange(half, dtype=np.float32) / half)).astype(np.float32)
    lane = np.arange(LANES) % head_w
    within = lane % a
    first = within < half
    p = np.where((lane // a == 0)[None, :], rows[:, None], cols[:, None]).astype(np.float32)
    ang = (p * freqs[within % half][None, :]).astype(np.float32)
    cos, sin = np.cos(ang), np.sin(ang)
    return (jnp.asarray(cos, F32), jnp.asarray(np.where(first[None], -sin, 0.0), F32),
            jnp.asarray(np.where(first[None], 0.0, sin), F32))


def _tile_to(v, width):
    return jnp.tile(v, width // v.shape[0])


def _layer_params(i, p):
    w_in = p["w_in"][i]
    sp = np.cumsum((512, 128, 128, 512, 512, 512, Q_LORA, KV_LORA, ROPE_DIM))
    qa, ka, va, qb, kb, vb, cq, ckv, kr, gates = jnp.split(w_in, [int(s) for s in sp], axis=1)
    qa = qa.reshape(D_MODEL, WIN_HEADS, HEAD_DIM)[:, WIN_Q_ORDER, :].reshape(D_MODEL, 512)
    w_a = jnp.concatenate([qa, ka, va, qb, kb, vb, cq, ckv, jnp.tile(kr, (1, MLA_HEADS))],
                          axis=1).astype(BF16)
    w_uq = p["w_uq"][i].reshape(Q_LORA, MLA_HEADS, QK_DIM)
    w_uq = jnp.concatenate([w_uq[:, :, :NOPE_DIM].reshape(Q_LORA, 512),
                            w_uq[:, :, NOPE_DIM:].reshape(Q_LORA, 256)], axis=1).astype(BF16)
    w_ukv = p["w_ukv"][i].reshape(KV_LORA, MLA_HEADS, NOPE_DIM + V_DIM)
    w_ukv = jnp.concatenate([w_ukv[:, :, :NOPE_DIM].reshape(KV_LORA, 512),
                             w_ukv[:, :, NOPE_DIM:].reshape(KV_LORA, 512)], axis=1).astype(BF16)
    z = jnp.zeros((D_MODEL,), F32)
    row = lambda *parts: jnp.concatenate(list(parts) + [z])[:D_MODEL]
    q_scale = HEAD_DIM ** -0.5 * LOG2E
    c_scale = QK_DIM ** -0.5 * LOG2E
    g_mla = p["g_qk_mla"][i]
    gains = jnp.stack([
        p["g_norm_mix"][i],
        row(_tile_to(p["g_qk_win"][i, 0], 512) * q_scale, _tile_to(p["g_qk_win"][i, 1], 128)),
        row(_tile_to(p["g_qk_nbr"][i, 0], 512) * q_scale, _tile_to(p["g_qk_nbr"][i, 1], 512)),
        row(p["g_q_lora"][i], p["g_kv_lora"][i]),
        row(_tile_to(g_mla[0, :NOPE_DIM], 512) * c_scale, _tile_to(g_mla[0, NOPE_DIM:], 256) * c_scale),
        row(_tile_to(g_mla[1, :NOPE_DIM], 512), _tile_to(g_mla[1, NOPE_DIM:], 256)),
        p["g_norm_ffn"][i],
        z]).astype(F32)
    wo_a = p["w_o_win"][i].reshape(WIN_HEADS, HEAD_DIM, D_MODEL)[WIN_Q_ORDER, :, :].reshape(512, D_MODEL)
    return dict(
        w_a=w_a, w_uq=w_uq, w_ukv=w_ukv, gains=gains, w_gate=gates.astype(BF16),
        wo_a=wo_a.astype(BF16), wo_b=p["w_o_nbr"][i].astype(BF16), wo_c=p["w_o_mla"][i].astype(BF16),
        w_out=p["w_out"][i].astype(BF16),
        nbr_table=_nbr_bias_table(p["nbr_rel_bias"][i]))


def kernel(x_prompt, x_sample, cache_win_k, cache_win_v, cache_nbr_k, cache_nbr_v, cache_mla_ckv, cache_mla_krope, c, c_ctx, g_norm_mix, g_norm_ffn, w_ada, b_ada, w_in, g_qk_win, win_sink, g_qk_nbr, nbr_rel_bias, g_q_lora, g_kv_lora, w_uq, w_ukv, g_qk_mla, w_o_win, w_o_nbr, w_o_mla, w_out, w_router, b_router, w_exp_gate, w_exp_up, w_exp_down):
    p = dict(g_norm_mix=g_norm_mix, g_norm_ffn=g_norm_ffn, w_in=w_in, g_qk_win=g_qk_win,
             g_qk_nbr=g_qk_nbr, nbr_rel_bias=nbr_rel_bias, g_q_lora=g_q_lora, g_kv_lora=g_kv_lora,
             w_uq=w_uq, w_ukv=w_ukv, g_qk_mla=g_qk_mla, w_o_win=w_o_win, w_o_nbr=w_o_nbr,
             w_o_mla=w_o_mla, w_out=w_out, w_exp_gate=w_exp_gate, w_exp_up=w_exp_up,
             w_exp_down=w_exp_down)
    depth = w_in.shape[0]
    batch, seq, _ = x_prompt.shape
    dec_batch, dec_seq, _ = x_sample.shape
    past = cache_win_k.shape[2]

    n_c = 1 + dec_batch
    c_rows = -(-n_c // 8) * 8
    c_all = jnp.concatenate([c_ctx[None], c, jnp.zeros((c_rows - n_c, D_MODEL), F32)], axis=0)
    mods = _ada(c_all, w_ada, b_ada).reshape(depth, c_rows, 6, D_MODEL)
    mods = jnp.pad(mods, ((0, 0), (0, 0), (0, 2), (0, 0)))

    mats = (_block_ones(512, 64, 512, 64, 1.0 / HEAD_DIM), _block_ones(512, 64, 512, 64),
            _block_ones(256, 32, 512, 64), _block_ones(512, 64, 256, 32), _block_ones(256, 32, 256, 32))
    tabs = _rope_tables(dec_seq, 64) + _rope_tables(dec_seq, 32)
    sink = win_sink.astype(F32) * LOG2E
    w_r_t = w_router.T.astype(F32)
    b_r = b_router.astype(F32).reshape(N_EXPERTS, 1)
    layers = [_layer_params(i, p) for i in range(depth)]

    def merge(x, oa, ob, oc, mod, rows_per_mod, lp):
        return _merge(x, oa, ob, oc, mod, rows_per_mod, lp["gains"], lp["w_gate"], lp["wo_a"],
                      lp["wo_b"], lp["wo_c"], lp["w_out"], w_r_t, b_r)

    n_ctx = batch * seq
    n_lat = dec_batch * dec_seq
    x_ctx = x_prompt.reshape(n_ctx, D_MODEL)
    x_lat = x_sample.reshape(n_lat, D_MODEL)
    cwk = cache_win_k.reshape(dec_batch, depth, past, 128).astype(BF16)
    cwv = cache_win_v.reshape(dec_batch, depth, past, 128).astype(BF16)
    cnk = cache_nbr_k.reshape(dec_batch, depth, past, 512).astype(BF16)
    cnv = cache_nbr_v.reshape(dec_batch, depth, past, 512).astype(BF16)
    states = []
    for i, lp in enumerate(layers):
        mod_c = mods[i, 0:1]
        outs = _inproj(x_ctx, mod_c, n_ctx, lp["gains"], lp["w_a"], lp["w_uq"], lp["w_ukv"], mats, None,
                       seq, True)
        oa, ob, oc = _ctx_attn(i, sink, seq, *outs[:9])
        states.append(outs[9:])
        merged_c = merge(x_ctx, oa, ob, oc, mod_c, n_ctx, lp)

        mod_l = mods[i, 1:1 + dec_batch]
        qa, ka, va, qb, kb, vb, qc, kc, vc = _inproj(
            x_lat, mod_l, dec_seq, lp["gains"], lp["w_a"], lp["w_uq"], lp["w_ukv"], mats, tabs, dec_seq,
            False)
        kr_t = jnp.tile(cache_mla_krope[:, i].reshape(dec_batch * past, ROPE_DIM), (1, MLA_HEADS))
        kc_c, vc_c = _mla_cache_keys(cache_mla_ckv[:, i].reshape(dec_batch * past, KV_LORA), kr_t,
                                     lp["gains"], lp["w_ukv"], mats[1], mats[3], mats[4])
        r3 = lambda a: a.reshape(dec_batch, dec_seq, a.shape[-1])
        oa = _win_attn(i, sink, r3(qa), r3(ka), r3(va), cwk, cwv)
        ob = _nbr_attn(i, r3(qb), r3(kb), r3(vb), cnk, cnv, lp["nbr_table"])
        oc = _mla_attn(r3(qc), r3(kc), r3(vc), kc_c.reshape(dec_batch, past, 1024),
                       vc_c.reshape(dec_batch, past, 512))
        flat = lambda a: a.reshape(n_lat, 512)
        merged_l = merge(x_lat, flat(oa), flat(ob), flat(oc), mod_l, dec_seq, lp)

        x_ctx = _moe(i, *merged_c, mod_c, n_ctx, w_exp_gate, w_exp_up, w_exp_down)
        x_lat = _moe(i, *merged_l, mod_l, dec_seq, w_exp_gate, w_exp_up, w_exp_down)
    y_prompt = x_ctx.reshape(batch, seq, D_MODEL)
    y_sample = x_lat.reshape(dec_batch, dec_seq, D_MODEL)

    def stack(k, shape):
        return jnp.stack([s[k].reshape((batch, seq) + shape) for s in states], axis=1)

    return (y_prompt, y_sample,
            stack(0, (WIN_KV_HEADS, HEAD_DIM)), stack(1, (WIN_KV_HEADS, HEAD_DIM)),
            stack(2, (NBR_HEADS, HEAD_DIM)), stack(3, (NBR_HEADS, HEAD_DIM)),
            stack(4, (KV_LORA,)), stack(5, (ROPE_DIM,)))
```

```python
import functools

import numpy as np
import jax
import jax.numpy as jnp
from jax import lax
from jax.experimental import pallas as pl
from jax.experimental.pallas import tpu as pltpu
from jax.experimental.pallas import tpu_sc as plsc

D_MODEL = 1024
GRID_W = 64
HEAD_DIM = 64
WIN_HEADS = 8
WIN_KV_HEADS = 2
WINDOW = 128
NBR_HEADS = 8
NBR_ROWS = 8
NBR_COLS = 16
MLA_HEADS = 8
Q_LORA = 256
KV_LORA = 128
NOPE_DIM = 64
ROPE_DIM = 32
V_DIM = 64
QK_DIM = NOPE_DIM + ROPE_DIM
N_EXPERTS = 16
N_GROUPS = 4
EXPERTS_PER_GROUP = 4
D_FF = 512
ROPE_BASE = 10000.0
EPS = 1e-6

LANES = 128
LOG2E = 1.4426950408889634
NEG = -1e30
VMEM_LIMIT = 56 * 1024 * 1024

F32 = jnp.float32
BF16 = jnp.bfloat16

C_QA, C_KA, C_VA, C_QB, C_KB, C_VB, C_CQ, C_CKV, C_KR, C_END = (
    0, 512, 640, 768, 1280, 1792, 2304, 2560, 2688, 2944)
WIN_Q_ORDER = (0, 4, 1, 5, 2, 6, 3, 7)


def _cparams(sem):
    return pltpu.CompilerParams(dimension_semantics=sem, vmem_limit_bytes=VMEM_LIMIT)


def _dot(a, b):
    return jnp.dot(a, b, preferred_element_type=F32)


def _dot_nt(a, b):
    return lax.dot_general(a, b, (((1,), (1,)), ((), ())), preferred_element_type=F32)


def _split(x):
    hi = x.astype(BF16)
    lo = (x - hi.astype(F32)).astype(BF16)
    return hi, lo


def _gsum(x2, bmat):
    return _dot(x2.astype(BF16), bmat)


def _tile_lanes(t, width):
    reps = width // t.shape[-1]
    return t if reps == 1 else jnp.concatenate([t] * reps, axis=-1)


def _rotate(x, cos, sin_a, sin_b, half):
    w = x.shape[-1]
    up = pltpu.roll(x, w - half, 1)
    dn = pltpu.roll(x, half, 1)
    return (x * _tile_lanes(cos, w) + up * _tile_lanes(sin_a, w) + dn * _tile_lanes(sin_b, w))


def _norm_mod(x, gain, scale, shift):
    ms = jnp.mean(x * x, axis=-1, keepdims=True)
    return (x * lax.rsqrt(ms + EPS) * gain) * (1.0 + scale) + shift


def _ada_body(c_ref, w_ref, b_ref, o_ref):
    c = c_ref[...]
    a = c * (1.0 / (1.0 + jnp.exp(-c)))
    a_hi, a_lo = _split(a)
    w_hi, w_lo = _split(w_ref[0])
    o_ref[0] = _dot(a_hi, w_hi) + _dot(a_hi, w_lo) + _dot(a_lo, w_hi) + b_ref[0]


def _ada(c_all, w_ada, b_ada):
    depth = w_ada.shape[0]
    rows = c_all.shape[0]
    tn = 1536
    return pl.pallas_call(
        _ada_body,
        grid=(depth, 6 * D_MODEL // tn),
        in_specs=[pl.BlockSpec((rows, D_MODEL), lambda l, j: (0, 0)),
                  pl.BlockSpec((1, D_MODEL, tn), lambda l, j: (l, 0, j)),
                  pl.BlockSpec((1, 1, tn), lambda l, j: (l, 0, j))],
        out_specs=pl.BlockSpec((1, rows, tn), lambda l, j: (l, 0, j)),
        out_shape=jax.ShapeDtypeStruct((depth, rows, 6 * D_MODEL), F32),
        compiler_params=_cparams(("parallel", "parallel")),
        name="ada",
    )(c_all, w_ada, b_ada.reshape(depth, 1, 6 * D_MODEL))


def _mla_key_tail(ckvn_b, kr_t, g, wukv_ref, bnn, bnr, brr, rope_tabs, kc_ref, vc_ref):
    kv = _dot(ckvn_b, wukv_ref[...])
    kn = kv[:, 0:512]
    vc_ref[...] = kv[:, 512:1024].astype(BF16)
    kn2 = kn * kn
    kr2 = kr_t * kr_t
    kr_sum32 = _gsum(kr2, brr)
    ssn = (_gsum(kn2, bnn) + jnp.concatenate([kr_sum32, kr_sum32], axis=-1)) * (1.0 / QK_DIM)
    ssr = (_gsum(kn2, bnr) + kr_sum32) * (1.0 / QK_DIM)
    kn = kn * lax.rsqrt(ssn + EPS) * g[5:6, 0:512]
    kr = kr_t * lax.rsqrt(ssr + EPS) * g[5:6, 512:768]
    if rope_tabs is not None:
        kr = _rotate(kr, *rope_tabs, 8)
    for p in range(4):
        kc_ref[:, 256 * p:256 * p + 128] = kn[:, 128 * p:128 * p + 128].astype(BF16)
        q4 = 128 * (p // 2)
        kc_ref[:, 256 * p + 128:256 * p + 256] = kr[:, q4:q4 + 128].astype(BF16)


def _inproj_body(rope, states, *refs):
    (x_ref, mod_ref, g_ref, w_ref, wuq_ref, wukv_ref, b64_ref, bnn_ref, brn_ref, bnr_ref,
     brr_ref) = refs[:11]
    refs = refs[11:]
    if rope:
        tabs_w = tuple(r[...] for r in refs[0:3])
        tabs_m = tuple(r[...] for r in refs[3:6])
        refs = refs[6:]
    else:
        tabs_w = tabs_m = None
    qa_ref, ka_ref, va_ref, qb_ref, kb_ref, vb_ref, qc_ref, kc_ref, vc_ref = refs[:9]
    st = refs[9:]

    g = g_ref[...]
    mod = mod_ref[0]
    hb = _norm_mod(x_ref[...], g[0:1], mod[1:2], mod[0:1]).astype(BF16)

    def proj(a, b):
        return _dot(hb, w_ref[:, a:b])

    b64 = b64_ref[...]

    def head_norm(z, bmat, gain):
        return z * lax.rsqrt(_gsum(z * z, bmat) + EPS) * gain

    qa = head_norm(proj(C_QA, C_KA), b64, g[1:2, 0:512])
    ka = head_norm(proj(C_KA, C_VA), b64[0:128, 0:128], g[1:2, 512:640])
    va = proj(C_VA, C_QB)
    if states:
        st[0][...] = ka
        st[1][...] = va
    if rope:
        qa = _rotate(qa, *tabs_w, 16)
        ka = _rotate(ka, *tabs_w, 16)
    qa_ref[...] = qa.astype(BF16)
    ka_ref[...] = ka.astype(BF16)
    va_ref[...] = va.astype(BF16)

    qb = head_norm(proj(C_QB, C_KB), b64, g[2:3, 0:512])
    kb = head_norm(proj(C_KB, C_VB), b64, g[2:3, 512:1024])
    vb = proj(C_VB, C_CQ)
    if states:
        st[2][...] = kb
        st[3][...] = vb
    qb_ref[...] = qb.astype(BF16)
    kb_ref[...] = kb.astype(BF16)
    vb_ref[...] = vb.astype(BF16)

    cq = proj(C_CQ, C_CKV)
    cqn = cq * lax.rsqrt(jnp.mean(cq * cq, axis=-1, keepdims=True) + EPS) * g[3:4, 0:256]
    qq = _dot(cqn.astype(BF16), wuq_ref[...])
    qn, qr = qq[:, 0:512], qq[:, 512:768]
    qn2, qr2 = qn * qn, qr * qr
    bnn, brn, bnr, brr = bnn_ref[...], brn_ref[...], bnr_ref[...], brr_ref[...]
    ssn = (_gsum(qn2, bnn) + _gsum(qr2, brn)) * (1.0 / QK_DIM)
    ssr = (_gsum(qn2, bnr) + _gsum(qr2, brr)) * (1.0 / QK_DIM)
    qn = qn * lax.rsqrt(ssn + EPS) * g[4:5, 0:512]
    qr = qr * lax.rsqrt(ssr + EPS) * g[4:5, 512:768]
    if rope:
        qr = _rotate(qr, *tabs_m, 8)
    for p in range(4):
        qc_ref[:, 256 * p:256 * p + 128] = qn[:, 128 * p:128 * p + 128].astype(BF16)
        q4 = 128 * (p // 2)
        qc_ref[:, 256 * p + 128:256 * p + 256] = qr[:, q4:q4 + 128].astype(BF16)

    ckv = proj(C_CKV, C_KR)
    ckvn = ckv * lax.rsqrt(jnp.mean(ckv * ckv, axis=-1, keepdims=True) + EPS) * g[3:4, 256:384]
    kr_t = proj(C_KR, C_END)
    if states:
        st[4][...] = ckvn
        st[5][...] = kr_t[:, 0:ROPE_DIM]
    _mla_key_tail(ckvn.astype(BF16), kr_t, g, wukv_ref, bnn, bnr, brr, tabs_m, kc_ref, vc_ref)


def _const_spec(shape):
    nd = len(shape)
    return pl.BlockSpec(shape, lambda i, _nd=nd: (0,) * _nd, pipeline_mode=pl.Buffered(1))


def _inproj(x, mods, rows_per_mod, gains, w_a, w_uq, w_ukv, mats, rope_tabs, seq_len, states):
    n = x.shape[0]
    tm = min(1024, n)
    rope = rope_tabs is not None
    assert n % tm == 0 and (rows_per_mod % tm == 0 or rows_per_mod == n), (n, tm, rows_per_mod)
    assert not rope or seq_len % tm == 0, (seq_len, tm)
    row = lambda w: pl.BlockSpec((tm, w), lambda i: (i, 0))
    in_specs = [row(D_MODEL),
                pl.BlockSpec((1, 8, D_MODEL), lambda i: ((i * tm) // rows_per_mod, 0, 0)),
                _const_spec(gains.shape), _const_spec(w_a.shape), _const_spec(w_uq.shape),
                _const_spec(w_ukv.shape)] + [_const_spec(m.shape) for m in mats]
    args = [x, mods, gains, w_a, w_uq, w_ukv, *mats]
    if rope:
        tiles_per_seq = seq_len // tm
        in_specs += [pl.BlockSpec((tm, LANES), lambda i: (i % tiles_per_seq, 0))] * 6
        args += list(rope_tabs)
    widths = [512, 128, 128, 512, 512, 512, 1024, 1024, 512]
    out_shape = [jax.ShapeDtypeStruct((n, w), BF16) for w in widths]
    out_specs = [row(w) for w in widths]
    if states:
        swidths = [128, 128, 512, 512, KV_LORA, ROPE_DIM]
        out_shape += [jax.ShapeDtypeStruct((n, w), F32) for w in swidths]
        out_specs += [row(w) for w in swidths]
    return pl.pallas_call(
        functools.partial(_inproj_body, rope, states),
        grid=(n // tm,), in_specs=in_specs, out_specs=out_specs, out_shape=out_shape,
        compiler_params=_cparams(("parallel",)),
        name="inproj_lat" if rope else "inproj_ctx",
    )(*args)


def _mla_cache_body(ckv_ref, kr_ref, g_ref, wukv_ref, bnn_ref, bnr_ref, brr_ref, kc_ref, vc_ref):
    _mla_key_tail(ckv_ref[...].astype(BF16), kr_ref[...], g_ref[...], wukv_ref, bnn_ref[...],
                  bnr_ref[...], brr_ref[...], None, kc_ref, vc_ref)


def _mla_cache_keys(ckv, kr_t, gains, w_ukv, bnn, bnr, brr):
    n = ckv.shape[0]
    tm = min(512, n)
    row = lambda w: pl.BlockSpec((tm, w), lambda i: (i, 0))
    return pl.pallas_call(
        _mla_cache_body,
        grid=(n // tm,),
        in_specs=[row(KV_LORA), row(256), _const_spec(gains.shape), _const_spec(w_ukv.shape),
                  _const_spec(bnn.shape), _const_spec(bnr.shape), _const_spec(brr.shape)],
        out_specs=[row(1024), row(512)],
        out_shape=[jax.ShapeDtypeStruct((n, 1024), BF16), jax.ShapeDtypeStruct((n, 512), BF16)],
        compiler_params=_cparams(("parallel",)),
        name="mla_cache_keys",
    )(ckv, kr_t, gains, w_ukv, bnn, bnr, brr)


def _lane_mask(width, ranges):
    lane = lax.broadcasted_iota(jnp.int32, (1, width), 1)
    m = None
    for lo, hi in ranges:
        c = (lane >= lo) & (lane < hi)
        m = c if m is None else (m | c)
    return jnp.where(m, 1.0, 0.0).astype(BF16)


def _stack_heads(q, mask0, mask1):
    return jnp.concatenate([q * mask0, q * mask1], axis=0)


def _lane_tiles(x):
    return [x[:, j:j + LANES] for j in range(0, x.shape[1], LANES)]


def _softmax_block(scores, sink=None):
    rows = scores[0].shape[0]
    mp = None
    for s in scores:
        for t in _lane_tiles(s):
            mp = t if mp is None else jnp.maximum(mp, t)
    base = sink if sink is not None else jnp.full((rows, LANES), NEG, F32)
    m = jnp.maximum(base, jnp.max(mp, axis=-1, keepdims=True))
    lp = None
    ps = []
    for s in scores:
        p = jnp.exp2(s - _tile_lanes(m, s.shape[1]))
        for t in _lane_tiles(p):
            lp = t if lp is None else lp + t
        ps.append(p.astype(BF16))
    if sink is not None:
        lane = lax.broadcasted_iota(jnp.int32, (rows, LANES), 1)
        lp = lp + jnp.where(lane == 0, jnp.exp2(sink - m), 0.0)
    p_all = ps[0] if len(ps) == 1 else jnp.concatenate(ps, axis=-1)
    return p_all, jnp.broadcast_to(jnp.sum(lp, axis=-1, keepdims=True), (rows, LANES))


def _softmax_pv(scores, values, sink=None):
    p_all, l = _softmax_block(scores, sink)
    v_all = values[0] if len(values) == 1 else jnp.concatenate(values, axis=0)
    return _dot(p_all, v_all) / l


def _merge_heads(o, tq):
    lane = lax.broadcasted_iota(jnp.int32, (tq, LANES), 1)
    return jnp.where(lane < HEAD_DIM, o[0:tq], o[tq:2 * tq])


def _mla_masks(p_mod2):
    lane = lax.broadcasted_iota(jnp.int32, (1, 256), 1)
    r0 = 128 + 32 * (2 * p_mod2)
    m0 = (lane < 64) | ((lane >= r0) & (lane < r0 + 32))
    m1 = ((lane >= 64) & (lane < 128)) | ((lane >= r0 + 32) & (lane < r0 + 64))
    return (jnp.where(m0, 1.0, 0.0).astype(BF16), jnp.where(m1, 1.0, 0.0).astype(BF16))


def _sink_col(sink_ref, layer, h0, h1, tq):
    row = lax.broadcasted_iota(jnp.int32, (2 * tq, LANES), 0)
    return jnp.where(row < tq, sink_ref[layer, h0], sink_ref[layer, h1])


def _ctx_attn_body(layer, sink_ref, qa_ref, ka_ref, va_ref, qb_ref, kb_ref, vb_ref, qc_ref, kc_ref,
                   vc_ref, oa_ref, ob_ref, oc_ref):
    tq = qa_ref.shape[0]
    lo = _lane_mask(LANES, [(0, 64)])
    hi = _lane_mask(LANES, [(64, 128)])
    ka, va = ka_ref[...], va_ref[...]
    for j in range(4):
        sl = slice(128 * j, 128 * j + 128)
        qs = _stack_heads(qa_ref[:, sl], lo, hi)
        sink = _sink_col(sink_ref, layer, j, 4 + j, tq)
        o = _softmax_pv([_dot_nt(qs, ka)], [va], sink)
        oa_ref[:, sl] = _merge_heads(o, tq).astype(BF16)

        qs = _stack_heads(qb_ref[:, sl], lo, hi)
        o = _softmax_pv([_dot_nt(qs, kb_ref[:, sl])], [vb_ref[:, sl]])
        ob_ref[:, sl] = _merge_heads(o, tq).astype(BF16)

        m0, m1 = _mla_masks(j % 2)
        s2 = slice(256 * j, 256 * j + 256)
        qs = _stack_heads(qc_ref[:, s2], m0, m1)
        o = _softmax_pv([_dot_nt(qs, kc_ref[:, s2])], [vc_ref[:, sl]])
        oc_ref[:, sl] = _merge_heads(o, tq).astype(BF16)


def _ctx_attn(layer, sink, seq, qa, ka, va, qb, kb, vb, qc, kc, vc):
    n = qa.shape[0]
    row = lambda w: pl.BlockSpec((seq, w), lambda b: (b, 0))
    ins = [qa, ka, va, qb, kb, vb, qc, kc, vc]
    return pl.pallas_call(
        functools.partial(_ctx_attn_body, layer),
        grid=(n // seq,),
        in_specs=[pl.BlockSpec(memory_space=pltpu.SMEM)] + [row(a.shape[1]) for a in ins],
        out_specs=[row(512)] * 3,
        out_shape=[jax.ShapeDtypeStruct((n, 512), BF16)] * 3,
        compiler_params=_cparams(("parallel",)),
        name="ctx_attn",
    )(sink, *ins)


WIN_ROW_BLOCK = 32


WIN_Q_BLOCK = 128
WIN_BLOCKS_PER_STEP = 4


def _win_body(layer, sink_ref, q_ref, k_ref, v_ref, kc_ref, vc_ref, o_ref, s_sc, p_sc, l_sc):
    tq = WIN_Q_BLOCK
    seq = k_ref.shape[1]
    kw = 3 * tq
    rb = WIN_ROW_BLOCK
    lo = _lane_mask(LANES, [(0, 64)])
    hi = _lane_mask(LANES, [(64, 128)])
    kc, vc = kc_ref[0, 0], vc_ref[0, 0]
    for u in range(q_ref.shape[1] // tq):
        i = pl.program_id(1) * (q_ref.shape[1] // tq) + u
        kstart = pl.multiple_of(jnp.clip((i - 1) * tq, 0, seq - kw), tq)
        k_all = jnp.concatenate([k_ref[0, pl.ds(kstart, kw), :], kc], axis=0)
        v_all = jnp.concatenate([v_ref[0, pl.ds(kstart, kw), :], vc], axis=0)
        q_pos = i * tq + lax.broadcasted_iota(jnp.int32, (tq, kw), 0)
        k_pos = kstart + lax.broadcasted_iota(jnp.int32, (tq, kw), 1)
        band = jnp.abs(q_pos - k_pos) <= WINDOW
        qs = jnp.concatenate(
            [_stack_heads(q_ref[0, u * tq:(u + 1) * tq, 128 * j:128 * j + 128], lo, hi) for j in range(4)],
            axis=0)
        s_sc[u] = _dot_nt(qs, k_all)
        for j in range(4):
            for r in range(2 * tq * j, 2 * tq * (j + 1), rb):
                head = j if r < 2 * tq * j + tq else 4 + j
                q0 = r % tq
                s_band = jnp.where(band[q0:q0 + rb], s_sc[u, r:r + rb, 0:kw], NEG)
                sink = jnp.full((rb, LANES), sink_ref[layer, head], F32)
                p, l = _softmax_block([s_band, s_sc[u, r:r + rb, kw:]], sink)
                p_sc[u, r:r + rb, :] = p
                l_sc[u, r:r + rb, :] = l
            rows = slice(2 * tq * j, 2 * tq * (j + 1))
            o = _dot(p_sc[u, rows, :], v_all) / l_sc[u, rows, :]
            o_ref[0, u * tq:(u + 1) * tq, 128 * j:128 * j + 128] = _merge_heads(o, tq).astype(BF16)


def _win_attn(layer, sink, q, k, v, kc, vc):
    b, seq, _ = q.shape
    nb = min(WIN_BLOCKS_PER_STEP, seq // WIN_Q_BLOCK)
    tq = nb * WIN_Q_BLOCK
    past = kc.shape[2]
    assert WIN_Q_BLOCK == WINDOW and seq % tq == 0 and seq >= 3 * WIN_Q_BLOCK and past % LANES == 0
    keys = 3 * WIN_Q_BLOCK + past
    return pl.pallas_call(
        functools.partial(_win_body, layer),
        grid=(b, seq // tq),
        in_specs=[pl.BlockSpec(memory_space=pltpu.SMEM),
                  pl.BlockSpec((1, tq, 512), lambda bi, i: (bi, i, 0)),
                  pl.BlockSpec((1, seq, 128), lambda bi, i: (bi, 0, 0)),
                  pl.BlockSpec((1, seq, 128), lambda bi, i: (bi, 0, 0)),
                  pl.BlockSpec((1, 1, past, 128), lambda bi, i: (bi, layer, 0, 0)),
                  pl.BlockSpec((1, 1, past, 128), lambda bi, i: (bi, layer, 0, 0))],
        out_specs=pl.BlockSpec((1, tq, 512), lambda bi, i: (bi, i, 0)),
        out_shape=jax.ShapeDtypeStruct((b, seq, 512), BF16),
        scratch_shapes=[pltpu.VMEM((nb, 8 * WIN_Q_BLOCK, keys), F32),
                        pltpu.VMEM((nb, 8 * WIN_Q_BLOCK, keys), BF16),
                        pltpu.VMEM((nb, 8 * WIN_Q_BLOCK, LANES), F32)],
        compiler_params=_cparams(("parallel", "arbitrary")),
        name="win_attn",
    )(sink, q, k, v, kc, vc)


NBR_TILE_ROWS = 4
NBR_WIN_ROWS = NBR_TILE_ROWS + NBR_ROWS
NBR_TAB_PAD = NBR_WIN_ROWS - NBR_ROWS
NBR_ROW_BLOCK = 32
NBR_TILES_PER_STEP = 2


def _nbr_body(rows, q_ref, k_ref, v_ref, kc_ref, vc_ref, tab_ref, o_ref, s_sc, p_sc, l_sc):
    tq = NBR_TILE_ROWS * GRID_W
    kw = NBR_WIN_ROWS * GRID_W
    lo = _lane_mask(LANES, [(0, 64)])
    hi = _lane_mask(LANES, [(64, 128)])
    past = kc_ref.shape[2]
    rb = NBR_ROW_BLOCK
    tiles = q_ref.shape[1] // tq
    for u in range(tiles):
        r0 = NBR_TILE_ROWS * (pl.program_id(1) * tiles + u)
        ws = jnp.clip(r0 - NBR_ROWS // 2, 0, rows - NBR_WIN_ROWS)
        kstart = pl.multiple_of(ws * GRID_W, LANES)
        k_row = ws + lax.broadcasted_iota(jnp.int32, (1, kw), 1) // GRID_W
        for j in range(4):
            sl = slice(128 * j, 128 * j + 128)
            qs = _stack_heads(q_ref[0, u * tq:(u + 1) * tq, sl], lo, hi)
            s_sc[u, j, :, 0:kw] = _dot_nt(qs, k_ref[0, pl.ds(kstart, kw), sl])
            s_sc[u, j, :, kw:kw + past] = _dot_nt(qs, kc_ref[0, 0, :, sl])
        for j in range(4):
            sl = slice(128 * j, 128 * j + 128)
            for b0 in range(0, 2 * tq, rb):
                h = 2 * j + b0 // tq
                ql, sub = divmod(b0 % tq, GRID_W)
                d0 = ws - r0 - ql + (NBR_ROWS - 1) + NBR_TAB_PAD
                bias = jnp.concatenate([tab_ref[h, d0 + 2 * m, sub:sub + rb, :]
                                        for m in range(NBR_WIN_ROWS // 2)], axis=-1)
                rs = jnp.clip(r0 + ql - NBR_ROWS // 2, 0, rows - NBR_ROWS)
                valid = (k_row >= rs) & (k_row < rs + NBR_ROWS)
                s_nb = jnp.where(valid, s_sc[u, j, b0:b0 + rb, 0:kw] + bias, NEG)
                p, l = _softmax_block([s_nb, s_sc[u, j, b0:b0 + rb, kw:kw + past]])
                p_sc[u, j, b0:b0 + rb, :] = p
                l_sc[u, j, b0:b0 + rb, :] = l
            v_all = jnp.concatenate([v_ref[0, pl.ds(kstart, kw), sl], vc_ref[0, 0, :, sl]], axis=0)
            o = _dot(p_sc[u, j], v_all) / l_sc[u, j]
            o_ref[0, u * tq:(u + 1) * tq, sl] = _merge_heads(o, tq).astype(BF16)


def _nbr_attn(layer, q, k, v, kc, vc, table):
    b, seq, _ = q.shape
    rows = seq // GRID_W
    tile = NBR_TILE_ROWS * GRID_W
    tq = NBR_TILES_PER_STEP * tile
    assert seq % tq == 0 and rows >= NBR_WIN_ROWS and (rows - NBR_WIN_ROWS) % 2 == 0, (seq, rows)
    assert NBR_TILE_ROWS % 2 == 0 and (NBR_ROWS // 2) % 2 == 0 and GRID_W % NBR_ROW_BLOCK == 0
    past = kc.shape[2]
    keys = NBR_WIN_ROWS * GRID_W + past
    once = pl.Buffered(1)
    return pl.pallas_call(
        functools.partial(_nbr_body, rows),
        grid=(b, seq // tq),
        in_specs=[pl.BlockSpec((1, tq, 512), lambda bi, i: (bi, i, 0)),
                  pl.BlockSpec((1, seq, 512), lambda bi, i: (bi, 0, 0), pipeline_mode=once),
                  pl.BlockSpec((1, seq, 512), lambda bi, i: (bi, 0, 0), pipeline_mode=once),
                  pl.BlockSpec((1, 1, past, 512), lambda bi, i: (bi, layer, 0, 0), pipeline_mode=once),
                  pl.BlockSpec((1, 1, past, 512), lambda bi, i: (bi, layer, 0, 0), pipeline_mode=once),
                  pl.BlockSpec(table.shape, lambda bi, i: (0, 0, 0, 0), pipeline_mode=once)],
        out_specs=pl.BlockSpec((1, tq, 512), lambda bi, i: (bi, i, 0)),
        out_shape=jax.ShapeDtypeStruct((b, seq, 512), BF16),
        scratch_shapes=[pltpu.VMEM((NBR_TILES_PER_STEP, 4, 2 * tile, keys), F32),
                        pltpu.VMEM((NBR_TILES_PER_STEP, 4, 2 * tile, keys), BF16),
                        pltpu.VMEM((NBR_TILES_PER_STEP, 4, 2 * tile, LANES), F32)],
        compiler_params=_cparams(("parallel", "arbitrary")),
        name="nbr_attn",
    )(q, k, v, kc, vc, table)


def _nbr_bias_table(rel_bias):
    col = np.arange(GRID_W)
    cs = np.clip(col - NBR_COLS // 2, 0, GRID_W - NBR_COLS)
    kc = np.arange(GRID_W)
    ok = (kc[None, :] >= cs[:, None]) & (kc[None, :] < cs[:, None] + NBR_COLS)
    dc = kc[None, :] - col[:, None] + (NBR_COLS - 1)
    pick = (dc[:, :, None] == np.arange(2 * NBR_COLS - 1)[None, None, :]) & ok[:, :, None]
    t = jnp.einsum("hdk,qck->hdqc", rel_bias.astype(F32) * LOG2E, jnp.asarray(pick, F32),
                   precision=lax.Precision.HIGHEST)
    t = jnp.where(jnp.asarray(ok)[None, None], t, NEG)
    t = jnp.pad(t, ((0, 0), (NBR_TAB_PAD, NBR_TAB_PAD), (0, 0), (0, 0)))
    return jnp.concatenate([t[:, :-1], t[:, 1:]], axis=-1)


MLA_Q_TILE = 1024
MLA_KEY_CHUNK = 512
MLA_ROW_BLOCK = 64


def _mla_body(q_ref, kl_ref, kc_ref, vl_ref, vc_ref, o_ref, qs_sc, m_sc, l_sc, acc_sc, s_sc, p_sc, a_sc):
    tq = q_ref.shape[1]
    rows = 2 * tq
    seq = kl_ref.shape[1]
    past = kc_ref.shape[1]
    tk = min(MLA_KEY_CHUNK, seq)
    rb = MLA_ROW_BLOCK
    m0, m1 = _mla_masks(pl.program_id(1) % 2)
    qs_sc[...] = _stack_heads(q_ref[0], m0, m1)
    m_sc[...] = jnp.full(m_sc.shape, NEG, F32)
    l_sc[...] = jnp.zeros(l_sc.shape, F32)
    acc_sc[...] = jnp.zeros(acc_sc.shape, F32)

    def step(c, k, v):
        slab = c % 2
        n = k.shape[0]
        s_sc[slab, :, 0:n] = _dot_nt(qs_sc[...], k)
        for r in range(0, rows, rb):
            sl = slice(r, r + rb)
            sb = s_sc[slab, sl, 0:n]
            mp = None
            for t in _lane_tiles(sb):
                mp = t if mp is None else jnp.maximum(mp, t)
            m_prev = m_sc[sl]
            m_new = jnp.maximum(m_prev, jnp.max(mp, axis=-1, keepdims=True))
            alpha = jnp.exp2(m_prev - m_new)
            p = jnp.exp2(sb - _tile_lanes(m_new, n))
            psum = None
            for t in _lane_tiles(p):
                psum = t if psum is None else psum + t
            l_sc[sl] = alpha * l_sc[sl] + psum
            m_sc[sl] = m_new
            a_sc[slab, sl, :] = alpha
            p_sc[slab, sl, 0:n] = p.astype(BF16)
        acc_sc[...] = a_sc[slab] * acc_sc[...] + _dot(p_sc[slab, :, 0:n], v)

    chunks = [(kl_ref, vl_ref, o, min(tk, seq - o)) for o in range(0, seq, tk)]
    chunks += [(kc_ref, vc_ref, o, min(tk, past - o)) for o in range(0, past, tk)]
    for c, (k_ref, v_ref, o, n) in enumerate(chunks):
        step(c, k_ref[0, o:o + n, :], v_ref[0, o:o + n, :])
    l = jnp.sum(l_sc[...], axis=-1, keepdims=True)
    o_ref[0] = _merge_heads(acc_sc[...] / l, tq).astype(BF16)


def _mla_attn(q, kl, vl, kc, vc):
    b, seq, _ = q.shape
    past = kc.shape[1]
    tq = min(MLA_Q_TILE, seq)
    tk = min(MLA_KEY_CHUNK, seq)
    assert seq % tq == 0 and seq % LANES == 0 and past % LANES == 0 and (2 * tq) % MLA_ROW_BLOCK == 0
    return pl.pallas_call(
        _mla_body,
        grid=(b, 4, seq // tq),
        in_specs=[pl.BlockSpec((1, tq, 256), lambda bi, p, qi: (bi, qi, p)),
                  pl.BlockSpec((1, seq, 256), lambda bi, p, qi: (bi, 0, p)),
                  pl.BlockSpec((1, past, 256), lambda bi, p, qi: (bi, 0, p)),
                  pl.BlockSpec((1, seq, 128), lambda bi, p, qi: (bi, 0, p)),
                  pl.BlockSpec((1, past, 128), lambda bi, p, qi: (bi, 0, p))],
        out_specs=pl.BlockSpec((1, tq, 128), lambda bi, p, qi: (bi, qi, p)),
        out_shape=jax.ShapeDtypeStruct((b, seq, 512), BF16),
        scratch_shapes=[pltpu.VMEM((2 * tq, 256), BF16), pltpu.VMEM((2 * tq, LANES), F32),
                        pltpu.VMEM((2 * tq, LANES), F32), pltpu.VMEM((2 * tq, LANES), F32),
                        pltpu.VMEM((2, 2 * tq, tk), F32), pltpu.VMEM((2, 2 * tq, tk), BF16),
                        pltpu.VMEM((2, 2 * tq, LANES), F32)],
        compiler_params=_cparams(("parallel", "parallel", "arbitrary")),
        name="mla_attn",
    )(q, kl, kc, vl, vc)


def _pack_pairs(x):
    w = x.shape[1] // 2
    hi = lax.bitcast_convert_type(x[:, :w].astype(BF16).astype(F32), jnp.int32)
    lo = lax.bitcast_convert_type(x[:, w:].astype(BF16).astype(F32), jnp.int32)
    return (hi & jnp.int32(-65536)) | lax.shift_right_logical(lo, jnp.int32(16))


def _unpack_pairs(p):
    hi = lax.bitcast_convert_type(p & jnp.int32(-65536), F32)
    lo = lax.bitcast_convert_type(lax.shift_left(p, jnp.int32(16)), F32)
    return jnp.concatenate([hi, lo], axis=-1)


def _merge_body(x_ref, oa_ref, ob_ref, oc_ref, mod_ref, g_ref, wg_ref, woa_ref, wob_ref, woc_ref,
                wout_ref, wr_ref, br_ref, tri_ref, x1_ref, route_ref, h2p_ref, count_ref, count_sc):
    x = x_ref[...]
    g = g_ref[...]
    mod = mod_ref[0]
    hb = _norm_mod(x, g[0:1], mod[1:2], mod[0:1]).astype(BF16)
    m = None
    for br, (o_ref, wo_ref) in enumerate(((oa_ref, woa_ref), (ob_ref, wob_ref), (oc_ref, woc_ref))):
        z = _dot(hb, wg_ref[:, D_MODEL * br:D_MODEL * (br + 1)])
        gate = 1.0 / (1.0 + jnp.exp(-z))
        t = gate * _dot(o_ref[...], wo_ref[...])
        m = t if m is None else m + t
    y = _dot(m.astype(BF16), wout_ref[...])
    x1 = x + mod[2:3] * y
    x1_ref[...] = x1

    h2 = _norm_mod(x1, g[6:7], mod[4:5], mod[3:4])
    h_hi, h_lo = _split(h2)
    w_hi, w_lo = _split(wr_ref[...])
    both_w = _dot_nt(jnp.concatenate([w_hi, w_lo], axis=0), h_hi)
    logits = both_w[0:N_EXPERTS] + both_w[N_EXPERTS:2 * N_EXPERTS] + _dot_nt(w_hi, h_lo)
    score = 1.0 / (1.0 + jnp.exp(-logits))
    sel = score + br_ref[...]
    sel_r = [sel[e:e + 1] for e in range(N_EXPERTS)]
    sc_r = [score[e:e + 1] for e in range(N_EXPERTS)]
    picked = []
    for e in range(N_EXPERTS):
        grp, a = divmod(e, EXPERTS_PER_GROUP)
        rank = None
        for bb in range(EXPERTS_PER_GROUP):
            if bb == a:
                continue
            o = sel_r[grp * EXPERTS_PER_GROUP + bb]
            beats = (o >= sel_r[e]) if bb < a else (o > sel_r[e])
            r = jnp.where(beats, 1.0, 0.0)
            rank = r if rank is None else rank + r
        picked.append(rank < 2.0)
    gscore = []
    for grp in range(N_GROUPS):
        tot = None
        for a in range(EXPERTS_PER_GROUP):
            e = grp * EXPERTS_PER_GROUP + a
            t = jnp.where(picked[e], sel_r[e], 0.0)
            tot = t if tot is None else tot + t
        gscore.append(tot)
    best = jnp.zeros_like(gscore[0])
    best_v = gscore[0]
    for grp in range(1, N_GROUPS):
        upd = gscore[grp] > best_v
        best = jnp.where(upd, float(grp), best)
        best_v = jnp.where(upd, gscore[grp], best_v)
    cw, pk = [], []
    for a in range(EXPERTS_PER_GROUP):
        tot = flag = None
        for grp in range(N_GROUPS):
            e = grp * EXPERTS_PER_GROUP + a
            f = (best == float(grp)) & picked[e]
            t = jnp.where(f, sc_r[e], 0.0)
            tot = t if tot is None else tot + t
            flag = f if flag is None else (flag | f)
        cw.append(tot)
        pk.append(flag)
    den = cw[0] + cw[1] + cw[2] + cw[3]
    first = jnp.where(pk[0], 0.0, jnp.where(pk[1], 1.0, jnp.where(pk[2], 2.0, 3.0)))
    second = jnp.where(pk[3], 3.0, jnp.where(pk[2], 2.0, jnp.where(pk[1], 1.0, 0.0)))
    slot_e, slot_w = [], []
    for which in (first, second):
        tot = None
        for a in range(EXPERTS_PER_GROUP):
            t = jnp.where(which == float(a), cw[a], 0.0)
            tot = t if tot is None else tot + t
        slot_w.append(tot / den)
        slot_e.append(best * float(EXPERTS_PER_GROUP) + which)

    @pl.when(pl.program_id(0) == 0)
    def _():
        count_sc[...] = jnp.zeros(count_sc.shape, F32)

    tm = x.shape[0]
    eid = lax.broadcasted_iota(jnp.int32, (N_EXPERTS, tm), 0).astype(F32)
    oh = [eid == slot_e[0], eid == slot_e[1]]
    both = jnp.where(oh[0] | oh[1], 1.0, 0.0)
    seen = count_sc[...][:, 0:1] + _dot(both.astype(BF16), tri_ref[...])
    for k in range(2):
        route_ref[k:k + 1, :] = slot_w[k]
        route_ref[2 + k:3 + k, :] = slot_e[k]
        route_ref[4 + k:5 + k, :] = jnp.sum(jnp.where(oh[k], seen, 0.0), axis=0, keepdims=True)
    route_ref[6:8, :] = jnp.zeros((2, tm), F32)
    count_sc[...] = count_sc[...] + jnp.sum(both, axis=-1, keepdims=True)
    count_ref[...] = count_sc[...]
    h2p_ref[...] = _pack_pairs(h2)


def _merge(x, oa, ob, oc, mods, rows_per_mod, gains, w_gate, wo_a, wo_b, wo_c, w_out, w_r_t, b_r):
    n = x.shape[0]
    tm = min(1024, n)
    assert n % tm == 0 and (rows_per_mod % tm == 0 or rows_per_mod == n), (n, tm, rows_per_mod)
    row = lambda w: pl.BlockSpec((tm, w), lambda i: (i, 0))
    tri = jnp.asarray(np.triu(np.ones((tm, tm), np.float32), 1), BF16)
    consts = [gains, w_gate, wo_a, wo_b, wo_c, w_out, w_r_t, b_r, tri]
    return pl.pallas_call(
        _merge_body,
        grid=(n // tm,),
        in_specs=[row(D_MODEL), row(512), row(512), row(512),
                  pl.BlockSpec((1, 8, D_MODEL), lambda i: ((i * tm) // rows_per_mod, 0, 0))]
                 + [_const_spec(c.shape) for c in consts],
        out_specs=[row(D_MODEL), pl.BlockSpec((8, tm), lambda i: (0, i)), row(512),
                   pl.BlockSpec((N_EXPERTS, LANES), lambda i: (0, 0))],
        out_shape=[jax.ShapeDtypeStruct((n, D_MODEL), F32), jax.ShapeDtypeStruct((8, n), F32),
                   jax.ShapeDtypeStruct((n, 512), jnp.int32),
                   jax.ShapeDtypeStruct((N_EXPERTS, LANES), F32)],
        scratch_shapes=[pltpu.VMEM((N_EXPERTS, LANES), F32)],
        compiler_params=_cparams(("arbitrary",)),
        name="merge",
    )(x, oa, ob, oc, mods, *consts)


EXPERT_TILE = 512
SC_CORES = 2
SC_SUBCORES = 16
SC_WORKERS = SC_CORES * SC_SUBCORES
SC_WINDOW = 128


def _sc_mesh():
    return plsc.VectorSubcoreMesh(core_axis_name="c", subcore_axis_name="s", num_cores=SC_CORES,
                                  num_subcores=SC_SUBCORES)


def _sc_window_base(steps, j):
    wid = lax.axis_index("s") * SC_CORES + lax.axis_index("c")
    return pl.multiple_of((wid * steps + j) * SC_WINDOW, SC_WINDOW)


def _sc_dispatch(rows, pos0, pos1, n_out):
    n, w = rows.shape
    assert n % (SC_WORKERS * SC_WINDOW) == 0, n
    steps = n // (SC_WORKERS * SC_WINDOW)

    @functools.partial(
        pl.kernel, out_type=jax.ShapeDtypeStruct((n_out, w), rows.dtype), mesh=_sc_mesh(),
        scratch_types=[pltpu.VMEM((SC_WINDOW,), jnp.int32), pltpu.VMEM((SC_WINDOW,), jnp.int32),
                       pltpu.VMEM((SC_WINDOW, w), rows.dtype)],
        name="moe_dispatch")
    def run(x_hbm, i0_hbm, i1_hbm, o_hbm, i0_v, i1_v, rows_v):
        @pl.loop(0, steps)
        def _(j):
            base = _sc_window_base(steps, j)
            pltpu.sync_copy(i0_hbm.at[pl.ds(base, SC_WINDOW)], i0_v)
            pltpu.sync_copy(i1_hbm.at[pl.ds(base, SC_WINDOW)], i1_v)
            pltpu.sync_copy(x_hbm.at[pl.ds(base, SC_WINDOW)], rows_v)
            pltpu.sync_copy(rows_v, o_hbm.at[i0_v])
            pltpu.sync_copy(rows_v, o_hbm.at[i1_v])

    return run(rows, pos0, pos1)


def _sc_collect(rows, pos0, pos1):
    n = pos0.shape[0]
    w = rows.shape[1]
    assert n % (SC_WORKERS * SC_WINDOW) == 0, n
    steps = n // (SC_WORKERS * SC_WINDOW)
    out = jax.ShapeDtypeStruct((n, w), rows.dtype)

    @functools.partial(
        pl.kernel, out_type=[out, out], mesh=_sc_mesh(),
        scratch_types=[pltpu.VMEM((SC_WINDOW,), jnp.int32), pltpu.VMEM((SC_WINDOW, w), rows.dtype)],
        name="moe_collect")
    def run(y_hbm, i0_hbm, i1_hbm, o0_hbm, o1_hbm, i_v, rows_v):
        @pl.loop(0, steps)
        def _(j):
            base = _sc_window_base(steps, j)
            for i_hbm, o_hbm in ((i0_hbm, o0_hbm), (i1_hbm, o1_hbm)):
                pltpu.sync_copy(i_hbm.at[pl.ds(base, SC_WINDOW)], i_v)
                pltpu.sync_copy(y_hbm.at[i_v], rows_v)
                pltpu.sync_copy(rows_v, o_hbm.at[pl.ds(base, SC_WINDOW)])

    return run(rows, pos0, pos1)


def _experts_body(layer, te_ref, nv_ref, first_ref, next_ref, slot_ref, xs_ref, wg_hbm, wu_hbm, wd_hbm,
                  ys_ref, stage_g, stage_u, stage_d, wg_sc, wu_sc, wd_sc, sem):
    j = pl.program_id(0)

    def copies(expert, slot):
        return [pltpu.make_async_copy(w.at[layer, expert], st.at[slot], sem.at[slot, i])
                for i, (w, st) in enumerate(((wg_hbm, stage_g), (wu_hbm, stage_u), (wd_hbm, stage_d)))]

    @pl.when(j == 0)
    def _():
        for c in copies(te_ref[0], slot_ref[0]):
            c.start()

    @pl.when(first_ref[j] == 1)
    def _():
        slot = slot_ref[j]
        for c in copies(te_ref[j], slot):
            c.wait()

        @pl.when(next_ref[j] >= 0)
        def _():
            for c in copies(next_ref[j], 1 - slot):
                c.start()

        wg_sc[...] = stage_g[slot].astype(BF16)
        wu_sc[...] = stage_u[slot].astype(BF16)
        wd_sc[...] = stage_d[slot].astype(BF16)

    @pl.when(j < nv_ref[0])
    def _():
        x = _unpack_pairs(xs_ref[...]).astype(BF16)
        zg = _dot(x, wg_sc[...])
        act = zg * (1.0 / (1.0 + jnp.exp(-zg))) * _dot(x, wu_sc[...])
        ys_ref[...] = _pack_pairs(_dot(act.astype(BF16), wd_sc[...]))

    @pl.when(j >= nv_ref[0])
    def _():
        ys_ref[...] = jnp.zeros(ys_ref.shape, ys_ref.dtype)


def _experts(layer, xs, tile_expert, n_valid, wg, wu, wd):
    p = xs.shape[0]
    n_tiles = p // EXPERT_TILE
    assert p % EXPERT_TILE == 0
    tile = jnp.arange(n_tiles, dtype=jnp.int32)
    last = jnp.take(tile_expert, jnp.maximum(n_valid[0] - 1, 0))
    te = jnp.where(tile < n_valid[0], tile_expert, last)
    first = jnp.concatenate([jnp.ones((1,), jnp.int32), (te[1:] != te[:-1]).astype(jnp.int32)])
    slot = (jnp.cumsum(first) - 1) % 2
    later_first = (tile[None, :] > tile[:, None]) & (first[None, :] == 1)
    nxt_tile = jnp.min(jnp.where(later_first, tile[None, :], n_tiles), axis=1)
    nxt = jnp.where(nxt_tile < n_tiles, jnp.take(te, jnp.minimum(nxt_tile, n_tiles - 1)), -1)

    any_spec = pl.BlockSpec(memory_space=pl.ANY)
    idx = lambda j, *_: (j, 0)
    grid_spec = pltpu.PrefetchScalarGridSpec(
        num_scalar_prefetch=5,
        grid=(n_tiles,),
        in_specs=[pl.BlockSpec((EXPERT_TILE, 512), idx), any_spec, any_spec, any_spec],
        out_specs=pl.BlockSpec((EXPERT_TILE, 512), idx),
        scratch_shapes=[pltpu.VMEM((2, D_MODEL, D_FF), F32), pltpu.VMEM((2, D_MODEL, D_FF), F32),
                        pltpu.VMEM((2, D_FF, D_MODEL), F32),
                        pltpu.VMEM((D_MODEL, D_FF), BF16), pltpu.VMEM((D_MODEL, D_FF), BF16),
                        pltpu.VMEM((D_FF, D_MODEL), BF16), pltpu.SemaphoreType.DMA((2, 3))])
    return pl.pallas_call(
        functools.partial(_experts_body, layer), grid_spec=grid_spec,
        out_shape=jax.ShapeDtypeStruct((p, 512), jnp.int32),
        compiler_params=_cparams(("arbitrary",)),
        name="experts",
    )(te, n_valid, first, nxt.astype(jnp.int32), slot.astype(jnp.int32), xs, wg, wu, wd)


def _combine_body(x1_ref, y0_ref, y1_ref, route_ref, mod_ref, o_ref):
    tm = x1_ref.shape[0]
    eye = jnp.where(lax.broadcasted_iota(jnp.int32, (tm, tm), 0)
                    == lax.broadcasted_iota(jnp.int32, (tm, tm), 1), 1.0, 0.0).astype(BF16)
    r_hi, r_lo = _split(route_ref[...])
    wcol = _dot_nt(eye, r_hi) + _dot_nt(eye, r_lo)
    moe = wcol[:, 0:1] * _unpack_pairs(y0_ref[...]) + wcol[:, 1:2] * _unpack_pairs(y1_ref[...])
    o_ref[...] = x1_ref[...] + mod_ref[0][5:6] * moe


def _combine(x1, y0, y1, route, mods, rows_per_mod):
    n = x1.shape[0]
    tm = min(512, n)
    row = lambda w: pl.BlockSpec((tm, w), lambda i: (i, 0))
    return pl.pallas_call(
        _combine_body,
        grid=(n // tm,),
        in_specs=[row(D_MODEL), row(512), row(512), pl.BlockSpec((8, tm), lambda i: (0, i)),
                  pl.BlockSpec((1, 8, D_MODEL), lambda i: ((i * tm) // rows_per_mod, 0, 0))],
        out_specs=row(D_MODEL),
        out_shape=jax.ShapeDtypeStruct((n, D_MODEL), F32),
        compiler_params=_cparams(("parallel",)),
        name="combine",
    )(x1, y0, y1, route, mods)


def _moe(layer, x1, route, h2p, counts, mods, rows_per_mod, wg, wu, wd):
    n = x1.shape[0]
    p = 2 * n + N_EXPERTS * EXPERT_TILE
    cnt = counts[:, 0].astype(jnp.int32)
    padded = (cnt + EXPERT_TILE - 1) // EXPERT_TILE * EXPERT_TILE
    seg_end = jnp.cumsum(padded)
    seg_off = seg_end - padded
    experts = jnp.arange(N_EXPERTS, dtype=jnp.int32)

    def position(k):
        e = route[2 + k].astype(jnp.int32)
        off = jnp.sum(jnp.where(e[:, None] == experts[None], seg_off[None], 0), axis=1)
        return off + route[4 + k].astype(jnp.int32)

    pos0, pos1 = position(0), position(1)
    tile_start = jnp.arange(p // EXPERT_TILE, dtype=jnp.int32) * EXPERT_TILE
    tile_expert = jnp.sum(tile_start[:, None] >= seg_end[None], axis=1).astype(jnp.int32)
    tile_expert = jnp.minimum(tile_expert, N_EXPERTS - 1)
    n_valid = (seg_end[-1:] // EXPERT_TILE).astype(jnp.int32)

    xs = _sc_dispatch(h2p, pos0, pos1, p)
    ys = _experts(layer, xs, tile_expert, n_valid, wg, wu, wd)
    y0, y1 = _sc_collect(ys, pos0, pos1)
    return _combine(x1, y0, y1, route, mods, rows_per_mod)


def _block_ones(n_in, g_in, n_out, g_out, value=1.0):
    r = np.arange(n_in)[:, None] // g_in
    c = np.arange(n_out)[None, :] // g_out
    return jnp.asarray(np.where(r == c, value, 0.0), dtype=BF16)


def _rope_tables(seq, head_w):
    pos = np.arange(seq)
    rows, cols = pos // GRID_W, pos % GRID_W
    a = head_w // 2
    half = a // 2
    freqs = (ROPE_BASE ** (-np.arange(half, dtype=np.float32) / half)).astype(np.float32)
    lane = np.arange(LANES) % head_w
    within = lane % a
    first = within < half
    p = np.where((lane // a == 0)[None, :], rows[:, None], cols[:, None]).astype(np.float32)
    ang = (p * freqs[within % half][None, :]).astype(np.float32)
    cos, sin = np.cos(ang), np.sin(ang)
    return (jnp.asarray(cos, F32), jnp.asarray(np.where(first[None], -sin, 0.0), F32),
            jnp.asarray(np.where(first[None], 0.0, sin), F32))


def _tile_to(v, width):
    return jnp.tile(v, width // v.shape[0])


def _layer_params(i, p):
    w_in = p["w_in"][i]
    sp = np.cumsum((512, 128, 128, 512, 512, 512, Q_LORA, KV_LORA, ROPE_DIM))
    qa, ka, va, qb, kb, vb, cq, ckv, kr, gates = jnp.split(w_in, [int(s) for s in sp], axis=1)
    qa = qa.reshape(D_MODEL, WIN_HEADS, HEAD_DIM)[:, WIN_Q_ORDER, :].reshape(D_MODEL, 512)
    w_a = jnp.concatenate([qa, ka, va, qb, kb, vb, cq, ckv, jnp.tile(kr, (1, MLA_HEADS))],
                          axis=1).astype(BF16)
    w_uq = p["w_uq"][i].reshape(Q_LORA, MLA_HEADS, QK_DIM)
    w_uq = jnp.concatenate([w_uq[:, :, :NOPE_DIM].reshape(Q_LORA, 512),
                            w_uq[:, :, NOPE_DIM:].reshape(Q_LORA, 256)], axis=1).astype(BF16)
    w_ukv = p["w_ukv"][i].reshape(KV_LORA, MLA_HEADS, NOPE_DIM + V_DIM)
    w_ukv = jnp.concatenate([w_ukv[:, :, :NOPE_DIM].reshape(KV_LORA, 512),
                             w_ukv[:, :, NOPE_DIM:].reshape(KV_LORA, 512)], axis=1).astype(BF16)
    z = jnp.zeros((D_MODEL,), F32)
    row = lambda *parts: jnp.concatenate(list(parts) + [z])[:D_MODEL]
    q_scale = HEAD_DIM ** -0.5 * LOG2E
    c_scale = QK_DIM ** -0.5 * LOG2E
    g_mla = p["g_qk_mla"][i]
    gains = jnp.stack([
        p["g_norm_mix"][i],
        row(_tile_to(p["g_qk_win"][i, 0], 512) * q_scale, _tile_to(p["g_qk_win"][i, 1], 128)),
        row(_tile_to(p["g_qk_nbr"][i, 0], 512) * q_scale, _tile_to(p["g_qk_nbr"][i, 1], 512)),
        row(p["g_q_lora"][i], p["g_kv_lora"][i]),
        row(_tile_to(g_mla[0, :NOPE_DIM], 512) * c_scale, _tile_to(g_mla[0, NOPE_DIM:], 256) * c_scale),
        row(_tile_to(g_mla[1, :NOPE_DIM], 512), _tile_to(g_mla[1, NOPE_DIM:], 256)),
        p["g_norm_ffn"][i],
        z]).astype(F32)
    wo_a = p["w_o_win"][i].reshape(WIN_HEADS, HEAD_DIM, D_MODEL)[WIN_Q_ORDER, :, :].reshape(512, D_MODEL)
    return dict(
        w_a=w_a, w_uq=w_uq, w_ukv=w_ukv, gains=gains, w_gate=gates.astype(BF16),
        wo_a=wo_a.astype(BF16), wo_b=p["w_o_nbr"][i].astype(BF16), wo_c=p["w_o_mla"][i].astype(BF16),
        w_out=p["w_out"][i].astype(BF16),
        nbr_table=_nbr_bias_table(p["nbr_rel_bias"][i]))


def kernel(x_prompt, x_sample, cache_win_k, cache_win_v, cache_nbr_k, cache_nbr_v, cache_mla_ckv, cache_mla_krope, c, c_ctx, g_norm_mix, g_norm_ffn, w_ada, b_ada, w_in, g_qk_win, win_sink, g_qk_nbr, nbr_rel_bias, g_q_lora, g_kv_lora, w_uq, w_ukv, g_qk_mla, w_o_win, w_o_nbr, w_o_mla, w_out, w_router, b_router, w_exp_gate, w_exp_up, w_exp_down):
    p = dict(g_norm_mix=g_norm_mix, g_norm_ffn=g_norm_ffn, w_in=w_in, g_qk_win=g_qk_win,
             g_qk_nbr=g_qk_nbr, nbr_rel_bias=nbr_rel_bias, g_q_lora=g_q_lora, g_kv_lora=g_kv_lora,
             w_uq=w_uq, w_ukv=w_ukv, g_qk_mla=g_qk_mla, w_o_win=w_o_win, w_o_nbr=w_o_nbr,
             w_o_mla=w_o_mla, w_out=w_out, w_exp_gate=w_exp_gate, w_exp_up=w_exp_up,
             w_exp_down=w_exp_down)
    depth = w_in.shape[0]
    batch, seq, _ = x_prompt.shape
    dec_batch, dec_seq, _ = x_sample.shape
    past = cache_win_k.shape[2]

    n_c = 1 + dec_batch
    c_rows = -(-n_c // 8) * 8
    c_all = jnp.concatenate([c_ctx[None], c, jnp.zeros((c_rows - n_c, D_MODEL), F32)], axis=0)
    mods = _ada(c_all, w_ada, b_ada).reshape(depth, c_rows, 6, D_MODEL)
    mods = jnp.pad(mods, ((0, 0), (0, 0), (0, 2), (0, 0)))

    mats = (_block_ones(512, 64, 512, 64, 1.0 / HEAD_DIM), _block_ones(512, 64, 512, 64),
            _block_ones(256, 32, 512, 64), _block_ones(512, 64, 256, 32), _block_ones(256, 32, 256, 32))
    tabs = _rope_tables(dec_seq, 64) + _rope_tables(dec_seq, 32)
    sink = win_sink.astype(F32) * LOG2E
    w_r_t = w_router.T.astype(F32)
    b_r = b_router.astype(F32).reshape(N_EXPERTS, 1)
    layers = [_layer_params(i, p) for i in range(depth)]

    def merge(x, oa, ob, oc, mod, rows_per_mod, lp):
        return _merge(x, oa, ob, oc, mod, rows_per_mod, lp["gains"], lp["w_gate"], lp["wo_a"],
                      lp["wo_b"], lp["wo_c"], lp["w_out"], w_r_t, b_r)

    n_ctx = batch * seq
    n_lat = dec_batch * dec_seq
    x_ctx = x_prompt.reshape(n_ctx, D_MODEL)
    x_lat = x_sample.reshape(n_lat, D_MODEL)
    cwk = cache_win_k.reshape(dec_batch, depth, past, 128).astype(BF16)
    cwv = cache_win_v.reshape(dec_batch, depth, past, 128).astype(BF16)
    cnk = cache_nbr_k.reshape(dec_batch, depth, past, 512).astype(BF16)
    cnv = cache_nbr_v.reshape(dec_batch, depth, past, 512).astype(BF16)
    states = []
    for i, lp in enumerate(layers):
        mod_c = mods[i, 0:1]
        outs = _inproj(x_ctx, mod_c, n_ctx, lp["gains"], lp["w_a"], lp["w_uq"], lp["w_ukv"], mats, None,
                       seq, True)
        oa, ob, oc = _ctx_attn(i, sink, seq, *outs[:9])
        states.append(outs[9:])
        merged_c = merge(x_ctx, oa, ob, oc, mod_c, n_ctx, lp)

        mod_l = mods[i, 1:1 + dec_batch]
        qa, ka, va, qb, kb, vb, qc, kc, vc = _inproj(
            x_lat, mod_l, dec_seq, lp["gains"], lp["w_a"], lp["w_uq"], lp["w_ukv"], mats, tabs, dec_seq,
            False)
        kr_t = jnp.tile(cache_mla_krope[:, i].reshape(dec_batch * past, ROPE_DIM), (1, MLA_HEADS))
        kc_c, vc_c = _mla_cache_keys(cache_mla_ckv[:, i].reshape(dec_batch * past, KV_LORA), kr_t,
                                     lp["gains"], lp["w_ukv"], mats[1], mats[3], mats[4])
        r3 = lambda a: a.reshape(dec_batch, dec_seq, a.shape[-1])
        oa = _win_attn(i, sink, r3(qa), r3(ka), r3(va), cwk, cwv)
        ob = _nbr_attn(i, r3(qb), r3(kb), r3(vb), cnk, cnv, lp["nbr_table"])
        oc = _mla_attn(r3(qc), r3(kc), r3(vc), kc_c.reshape(dec_batch, past, 1024),
                       vc_c.reshape(dec_batch, past, 512))
        flat = lambda a: a.reshape(n_lat, 512)
        merged_l = merge(x_lat, flat(oa), flat(ob), flat(oc), mod_l, dec_seq, lp)

        x_ctx = _moe(i, *merged_c, mod_c, n_ctx, w_exp_gate, w_exp_up, w_exp_down)
        x_lat = _moe(i, *merged_l, mod_l, dec_seq, w_exp_gate, w_exp_up, w_exp_down)
    y_prompt = x_ctx.reshape(batch, seq, D_MODEL)
    y_sample = x_lat.reshape(dec_batch, dec_seq, D_MODEL)

    def stack(k, shape):
        return jnp.stack([s[k].reshape((batch, seq) + shape) for s in states], axis=1)

    return (y_prompt, y_sample,
            stack(0, (WIN_KV_HEADS, HEAD_DIM)), stack(1, (WIN_KV_HEADS, HEAD_DIM)),
            stack(2, (NBR_HEADS, HEAD_DIM)), stack(3, (NBR_HEADS, HEAD_DIM)),
            stack(4, (KV_LORA,)), stack(5, (ROPE_DIM,)))
```

```python
import functools

import numpy as np
import jax
import jax.numpy as jnp
from jax import lax
from jax.experimental import pallas as pl
from jax.experimental.pallas import tpu as pltpu
from jax.experimental.pallas import tpu_sc as plsc

D_MODEL = 1024
GRID_W = 64
HEAD_DIM = 64
WIN_HEADS = 8
WIN_KV_HEADS = 2
WINDOW = 128
NBR_HEADS = 8
NBR_ROWS = 8
NBR_COLS = 16
MLA_HEADS = 8
Q_LORA = 256
KV_LORA = 128
NOPE_DIM = 64
ROPE_DIM = 32
V_DIM = 64
QK_DIM = NOPE_DIM + ROPE_DIM
N_EXPERTS = 16
N_GROUPS = 4
EXPERTS_PER_GROUP = 4
D_FF = 512
ROPE_BASE = 10000.0
EPS = 1e-6

LANES = 128
LOG2E = 1.4426950408889634
NEG = -1e30
VMEM_LIMIT = 56 * 1024 * 1024

F32 = jnp.float32
BF16 = jnp.bfloat16

C_QA, C_KA, C_VA, C_QB, C_KB, C_VB, C_CQ, C_CKV, C_KR, C_END = (
    0, 512, 640, 768, 1280, 1792, 2304, 2560, 2688, 2944)
WIN_Q_ORDER = (0, 4, 1, 5, 2, 6, 3, 7)


def _cparams(sem):
    return pltpu.CompilerParams(dimension_semantics=sem, vmem_limit_bytes=VMEM_LIMIT)


def _dot(a, b):
    return jnp.dot(a, b, preferred_element_type=F32)


def _dot_nt(a, b):
    return lax.dot_general(a, b, (((1,), (1,)), ((), ())), preferred_element_type=F32)


def _split(x):
    hi = x.astype(BF16)
    lo = (x - hi.astype(F32)).astype(BF16)
    return hi, lo


def _gsum(x2, bmat):
    return _dot(x2.astype(BF16), bmat)


def _tile_lanes(t, width):
    reps = width // t.shape[-1]
    return t if reps == 1 else jnp.concatenate([t] * reps, axis=-1)


def _rotate(x, cos, sin_a, sin_b, half):
    w = x.shape[-1]
    up = pltpu.roll(x, w - half, 1)
    dn = pltpu.roll(x, half, 1)
    return (x * _tile_lanes(cos, w) + up * _tile_lanes(sin_a, w) + dn * _tile_lanes(sin_b, w))


def _norm_mod(x, gain, scale, shift):
    ms = jnp.mean(x * x, axis=-1, keepdims=True)
    return (x * lax.rsqrt(ms + EPS) * gain) * (1.0 + scale) + shift


def _ada_body(c_ref, w_ref, b_ref, o_ref):
    c = c_ref[...]
    a = c * (1.0 / (1.0 + jnp.exp(-c)))
    a_hi, a_lo = _split(a)
    w_hi, w_lo = _split(w_ref[0])
    o_ref[0] = _dot(a_hi, w_hi) + _dot(a_hi, w_lo) + _dot(a_lo, w_hi) + b_ref[0]


def _ada(c_all, w_ada, b_ada):
    depth = w_ada.shape[0]
    rows = c_all.shape[0]
    tn = 1536
    return pl.pallas_call(
        _ada_body,
        grid=(depth, 6 * D_MODEL // tn),
        in_specs=[pl.BlockSpec((rows, D_MODEL), lambda l, j: (0, 0)),
                  pl.BlockSpec((1, D_MODEL, tn), lambda l, j: (l, 0, j)),
                  pl.BlockSpec((1, 1, tn), lambda l, j: (l, 0, j))],
        out_specs=pl.BlockSpec((1, rows, tn), lambda l, j: (l, 0, j)),
        out_shape=jax.ShapeDtypeStruct((depth, rows, 6 * D_MODEL), F32),
        compiler_params=_cparams(("parallel", "parallel")),
        name="ada",
    )(c_all, w_ada, b_ada.reshape(depth, 1, 6 * D_MODEL))


def _mla_key_tail(ckvn_b, kr_t, g, wukv_ref, bnn, bnr, brr, rope_tabs, kc_ref, vc_ref):
    kv = _dot(ckvn_b, wukv_ref[...])
    kn = kv[:, 0:512]
    vc_ref[...] = kv[:, 512:1024].astype(BF16)
    kn2 = kn * kn
    kr2 = kr_t * kr_t
    kr_sum32 = _gsum(kr2, brr)
    ssn = (_gsum(kn2, bnn) + jnp.concatenate([kr_sum32, kr_sum32], axis=-1)) * (1.0 / QK_DIM)
    ssr = (_gsum(kn2, bnr) + kr_sum32) * (1.0 / QK_DIM)
    kn = kn * lax.rsqrt(ssn + EPS) * g[5:6, 0:512]
    kr = kr_t * lax.rsqrt(ssr + EPS) * g[5:6, 512:768]
    if rope_tabs is not None:
        kr = _rotate(kr, *rope_tabs, 8)
    for p in range(4):
        kc_ref[:, 256 * p:256 * p + 128] = kn[:, 128 * p:128 * p + 128].astype(BF16)
        q4 = 128 * (p // 2)
        kc_ref[:, 256 * p + 128:256 * p + 256] = kr[:, q4:q4 + 128].astype(BF16)


def _inproj_body(rope, states, *refs):
    (x_ref, mod_ref, g_ref, w_ref, wuq_ref, wukv_ref, b64_ref, bnn_ref, brn_ref, bnr_ref,
     brr_ref) = refs[:11]
    refs = refs[11:]
    if rope:
        tabs_w = tuple(r[...] for r in refs[0:3])
        tabs_m = tuple(r[...] for r in refs[3:6])
        refs = refs[6:]
    else:
        tabs_w = tabs_m = None
    qa_ref, ka_ref, va_ref, qb_ref, kb_ref, vb_ref, qc_ref, kc_ref, vc_ref = refs[:9]
    st = refs[9:]

    g = g_ref[...]
    mod = mod_ref[0]
    hb = _norm_mod(x_ref[...], g[0:1], mod[1:2], mod[0:1]).astype(BF16)

    def proj(a, b):
        return _dot(hb, w_ref[:, a:b])

    b64 = b64_ref[...]

    def head_norm(z, bmat, gain):
        return z * lax.rsqrt(_gsum(z * z, bmat) + EPS) * gain

    qa = head_norm(proj(C_QA, C_KA), b64, g[1:2, 0:512])
    ka = head_norm(proj(C_KA, C_VA), b64[0:128, 0:128], g[1:2, 512:640])
    va = proj(C_VA, C_QB)
    if states:
        st[0][...] = ka
        st[1][...] = va
    if rope:
        qa = _rotate(qa, *tabs_w, 16)
        ka = _rotate(ka, *tabs_w, 16)
    qa_ref[...] = qa.astype(BF16)
    ka_ref[...] = ka.astype(BF16)
    va_ref[...] = va.astype(BF16)

    qb = head_norm(proj(C_QB, C_KB), b64, g[2:3, 0:512])
    kb = head_norm(proj(C_KB, C_VB), b64, g[2:3, 512:1024])
    vb = proj(C_VB, C_CQ)
    if states:
        st[2][...] = kb
        st[3][...] = vb
    qb_ref[...] = qb.astype(BF16)
    kb_ref[...] = kb.astype(BF16)
    vb_ref[...] = vb.astype(BF16)

    cq = proj(C_CQ, C_CKV)
    cqn = cq * lax.rsqrt(jnp.mean(cq * cq, axis=-1, keepdims=True) + EPS) * g[3:4, 0:256]
    qq = _dot(cqn.astype(BF16), wuq_ref[...])
    qn, qr = qq[:, 0:512], qq[:, 512:768]
    qn2, qr2 = qn * qn, qr * qr
    bnn, brn, bnr, brr = bnn_ref[...], brn_ref[...], bnr_ref[...], brr_ref[...]
    ssn = (_gsum(qn2, bnn) + _gsum(qr2, brn)) * (1.0 / QK_DIM)
    ssr = (_gsum(qn2, bnr) + _gsum(qr2, brr)) * (1.0 / QK_DIM)
    qn = qn * lax.rsqrt(ssn + EPS) * g[4:5, 0:512]
    qr = qr * lax.rsqrt(ssr + EPS) * g[4:5, 512:768]
    if rope:
        qr = _rotate(qr, *tabs_m, 8)
    for p in range(4):
        qc_ref[:, 256 * p:256 * p + 128] = qn[:, 128 * p:128 * p + 128].astype(BF16)
        q4 = 128 * (p // 2)
        qc_ref[:, 256 * p + 128:256 * p + 256] = qr[:, q4:q4 + 128].astype(BF16)

    ckv = proj(C_CKV, C_KR)
    ckvn = ckv * lax.rsqrt(jnp.mean(ckv * ckv, axis=-1, keepdims=True) + EPS) * g[3:4, 256:384]
    kr_t = proj(C_KR, C_END)
    if states:
        st[4][...] = ckvn
        st[5][...] = kr_t[:, 0:ROPE_DIM]
    _mla_key_tail(ckvn.astype(BF16), kr_t, g, wukv_ref, bnn, bnr, brr, tabs_m, kc_ref, vc_ref)


def _const_spec(shape):
    nd = len(shape)
    return pl.BlockSpec(shape, lambda i, _nd=nd: (0,) * _nd, pipeline_mode=pl.Buffered(1))


def _inproj(x, mods, rows_per_mod, gains, w_a, w_uq, w_ukv, mats, rope_tabs, seq_len, states):
    n = x.shape[0]
    tm = min(1024, n)
    rope = rope_tabs is not None
    assert n % tm == 0 and (rows_per_mod % tm == 0 or rows_per_mod == n), (n, tm, rows_per_mod)
    assert not rope or seq_len % tm == 0, (seq_len, tm)
    row = lambda w: pl.BlockSpec((tm, w), lambda i: (i, 0))
    in_specs = [row(D_MODEL),
                pl.BlockSpec((1, 8, D_MODEL), lambda i: ((i * tm) // rows_per_mod, 0, 0)),
                _const_spec(gains.shape), _const_spec(w_a.shape), _const_spec(w_uq.shape),
                _const_spec(w_ukv.shape)] + [_const_spec(m.shape) for m in mats]
    args = [x, mods, gains, w_a, w_uq, w_ukv, *mats]
    if rope:
        tiles_per_seq = seq_len // tm
        in_specs += [pl.BlockSpec((tm, LANES), lambda i: (i % tiles_per_seq, 0))] * 6
        args += list(rope_tabs)
    widths = [512, 128, 128, 512, 512, 512, 1024, 1024, 512]
    out_shape = [jax.ShapeDtypeStruct((n, w), BF16) for w in widths]
    out_specs = [row(w) for w in widths]
    if states:
        swidths = [128, 128, 512, 512, KV_LORA, ROPE_DIM]
        out_shape += [jax.ShapeDtypeStruct((n, w), F32) for w in swidths]
        out_specs += [row(w) for w in swidths]
    return pl.pallas_call(
        functools.partial(_inproj_body, rope, states),
        grid=(n // tm,), in_specs=in_specs, out_specs=out_specs, out_shape=out_shape,
        compiler_params=_cparams(("parallel",)),
        name="inproj_lat" if rope else "inproj_ctx",
    )(*args)


def _mla_cache_body(ckv_ref, kr_ref, g_ref, wukv_ref, bnn_ref, bnr_ref, brr_ref, kc_ref, vc_ref):
    _mla_key_tail(ckv_ref[...].astype(BF16), kr_ref[...], g_ref[...], wukv_ref, bnn_ref[...],
                  bnr_ref[...], brr_ref[...], None, kc_ref, vc_ref)


def _mla_cache_keys(ckv, kr_t, gains, w_ukv, bnn, bnr, brr):
    n = ckv.shape[0]
    tm = min(512, n)
    row = lambda w: pl.BlockSpec((tm, w), lambda i: (i, 0))
    return pl.pallas_call(
        _mla_cache_body,
        grid=(n // tm,),
        in_specs=[row(KV_LORA), row(256), _const_spec(gains.shape), _const_spec(w_ukv.shape),
                  _const_spec(bnn.shape), _const_spec(bnr.shape), _const_spec(brr.shape)],
        out_specs=[row(1024), row(512)],
        out_shape=[jax.ShapeDtypeStruct((n, 1024), BF16), jax.ShapeDtypeStruct((n, 512), BF16)],
        compiler_params=_cparams(("parallel",)),
        name="mla_cache_keys",
    )(ckv, kr_t, gains, w_ukv, bnn, bnr, brr)


def _lane_mask(width, ranges):
    lane = lax.broadcasted_iota(jnp.int32, (1, width), 1)
    m = None
    for lo, hi in ranges:
        c = (lane >= lo) & (lane < hi)
        m = c if m is None else (m | c)
    return jnp.where(m, 1.0, 0.0).astype(BF16)


def _stack_heads(q, mask0, mask1):
    return jnp.concatenate([q * mask0, q * mask1], axis=0)


def _lane_tiles(x):
    return [x[:, j:j + LANES] for j in range(0, x.shape[1], LANES)]


def _softmax_block(scores, sink=None):
    rows = scores[0].shape[0]
    mp = None
    for s in scores:
        for t in _lane_tiles(s):
            mp = t if mp is None else jnp.maximum(mp, t)
    base = sink if sink is not None else jnp.full((rows, LANES), NEG, F32)
    m = jnp.maximum(base, jnp.max(mp, axis=-1, keepdims=True))
    lp = None
    ps = []
    for s in scores:
        p = jnp.exp2(s - _tile_lanes(m, s.shape[1]))
        for t in _lane_tiles(p):
            lp = t if lp is None else lp + t
        ps.append(p.astype(BF16))
    if sink is not None:
        lane = lax.broadcasted_iota(jnp.int32, (rows, LANES), 1)
        lp = lp + jnp.where(lane == 0, jnp.exp2(sink - m), 0.0)
    p_all = ps[0] if len(ps) == 1 else jnp.concatenate(ps, axis=-1)
    return p_all, jnp.broadcast_to(jnp.sum(lp, axis=-1, keepdims=True), (rows, LANES))


def _softmax_pv(scores, values, sink=None):
    p_all, l = _softmax_block(scores, sink)
    v_all = values[0] if len(values) == 1 else jnp.concatenate(values, axis=0)
    return _dot(p_all, v_all) / l


def _merge_heads(o, tq):
    lane = lax.broadcasted_iota(jnp.int32, (tq, LANES), 1)
    return jnp.where(lane < HEAD_DIM, o[0:tq], o[tq:2 * tq])


def _mla_masks(p_mod2):
    lane = lax.broadcasted_iota(jnp.int32, (1, 256), 1)
    r0 = 128 + 32 * (2 * p_mod2)
    m0 = (lane < 64) | ((lane >= r0) & (lane < r0 + 32))
    m1 = ((lane >= 64) & (lane < 128)) | ((lane >= r0 + 32) & (lane < r0 + 64))
    return (jnp.where(m0, 1.0, 0.0).astype(BF16), jnp.where(m1, 1.0, 0.0).astype(BF16))


def _sink_col(sink_ref, layer, h0, h1, tq):
    row = lax.broadcasted_iota(jnp.int32, (2 * tq, LANES), 0)
    return jnp.where(row < tq, sink_ref[layer, h0], sink_ref[layer, h1])


def _ctx_attn_body(layer, sink_ref, qa_ref, ka_ref, va_ref, qb_ref, kb_ref, vb_ref, qc_ref, kc_ref,
                   vc_ref, oa_ref, ob_ref, oc_ref):
    tq = qa_ref.shape[0]
    lo = _lane_mask(LANES, [(0, 64)])
    hi = _lane_mask(LANES, [(64, 128)])
    ka, va = ka_ref[...], va_ref[...]
    for j in range(4):
        sl = slice(128 * j, 128 * j + 128)
        qs = _stack_heads(qa_ref[:, sl], lo, hi)
        sink = _sink_col(sink_ref, layer, j, 4 + j, tq)
        o = _softmax_pv([_dot_nt(qs, ka)], [va], sink)
        oa_ref[:, sl] = _merge_heads(o, tq).astype(BF16)

        qs = _stack_heads(qb_ref[:, sl], lo, hi)
        o = _softmax_pv([_dot_nt(qs, kb_ref[:, sl])], [vb_ref[:, sl]])
        ob_ref[:, sl] = _merge_heads(o, tq).astype(BF16)

        m0, m1 = _mla_masks(j % 2)
        s2 = slice(256 * j, 256 * j + 256)
        qs = _stack_heads(qc_ref[:, s2], m0, m1)
        o = _softmax_pv([_dot_nt(qs, kc_ref[:, s2])], [vc_ref[:, sl]])
        oc_ref[:, sl] = _merge_heads(o, tq).astype(BF16)


def _ctx_attn(layer, sink, seq, qa, ka, va, qb, kb, vb, qc, kc, vc):
    n = qa.shape[0]
    row = lambda w: pl.BlockSpec((seq, w), lambda b: (b, 0))
    ins = [qa, ka, va, qb, kb, vb, qc, kc, vc]
    return pl.pallas_call(
        functools.partial(_ctx_attn_body, layer),
        grid=(n // seq,),
        in_specs=[pl.BlockSpec(memory_space=pltpu.SMEM)] + [row(a.shape[1]) for a in ins],
        out_specs=[row(512)] * 3,
        out_shape=[jax.ShapeDtypeStruct((n, 512), BF16)] * 3,
        compiler_params=_cparams(("parallel",)),
        name="ctx_attn",
    )(sink, *ins)


WIN_ROW_BLOCK = 32


WIN_Q_BLOCK = 128
WIN_BLOCKS_PER_STEP = 4


def _win_body(layer, sink_ref, q_ref, k_ref, v_ref, kc_ref, vc_ref, o_ref, s_sc, p_sc, l_sc):
    tq = WIN_Q_BLOCK
    seq = k_ref.shape[1]
    kw = 3 * tq
    rb = WIN_ROW_BLOCK
    lo = _lane_mask(LANES, [(0, 64)])
    hi = _lane_mask(LANES, [(64, 128)])
    kc, vc = kc_ref[0, 0], vc_ref[0, 0]
    for u in range(q_ref.shape[1] // tq):
        i = pl.program_id(1) * (q_ref.shape[1] // tq) + u
        kstart = pl.multiple_of(jnp.clip((i - 1) * tq, 0, seq - kw), tq)
        k_all = jnp.concatenate([k_ref[0, pl.ds(kstart, kw), :], kc], axis=0)
        v_all = jnp.concatenate([v_ref[0, pl.ds(kstart, kw), :], vc], axis=0)
        q_pos = i * tq + lax.broadcasted_iota(jnp.int32, (tq, kw), 0)
        k_pos = kstart + lax.broadcasted_iota(jnp.int32, (tq, kw), 1)
        band = jnp.abs(q_pos - k_pos) <= WINDOW
        qs = jnp.concatenate(
            [_stack_heads(q_ref[0, u * tq:(u + 1) * tq, 128 * j:128 * j + 128], lo, hi) for j in range(4)],
            axis=0)
        s_sc[u] = _dot_nt(qs, k_all)
        for j in range(4):
            for r in range(2 * tq * j, 2 * tq * (j + 1), rb):
                head = j if r < 2 * tq * j + tq else 4 + j
                q0 = r % tq
                s_band = jnp.where(band[q0:q0 + rb], s_sc[u, r:r + rb, 0:kw], NEG)
                sink = jnp.full((rb, LANES), sink_ref[layer, head], F32)
                p, l = _softmax_block([s_band, s_sc[u, r:r + rb, kw:]], sink)
                p_sc[u, r:r + rb, :] = p
                l_sc[u, r:r + rb, :] = l
            rows = slice(2 * tq * j, 2 * tq * (j + 1))
            o = _dot(p_sc[u, rows, :], v_all) / l_sc[u, rows, :]
            o_ref[0, u * tq:(u + 1) * tq, 128 * j:128 * j + 128] = _merge_heads(o, tq).astype(BF16)


def _win_attn(layer, sink, q, k, v, kc, vc):
    b, seq, _ = q.shape
    nb = min(WIN_BLOCKS_PER_STEP, seq // WIN_Q_BLOCK)
    tq = nb * WIN_Q_BLOCK
    past = kc.shape[2]
    assert WIN_Q_BLOCK == WINDOW and seq % tq == 0 and seq >= 3 * WIN_Q_BLOCK and past % LANES == 0
    keys = 3 * WIN_Q_BLOCK + past
    return pl.pallas_call(
        functools.partial(_win_body, layer),
        grid=(b, seq // tq),
        in_specs=[pl.BlockSpec(memory_space=pltpu.SMEM),
                  pl.BlockSpec((1, tq, 512), lambda bi, i: (bi, i, 0)),
                  pl.BlockSpec((1, seq, 128), lambda bi, i: (bi, 0, 0)),
                  pl.BlockSpec((1, seq, 128), lambda bi, i: (bi, 0, 0)),
                  pl.BlockSpec((1, 1, past, 128), lambda bi, i: (bi, layer, 0, 0)),
                  pl.BlockSpec((1, 1, past, 128), lambda bi, i: (bi, layer, 0, 0))],
        out_specs=pl.BlockSpec((1, tq, 512), lambda bi, i: (bi, i, 0)),
        out_shape=jax.ShapeDtypeStruct((b, seq, 512), BF16),
        scratch_shapes=[pltpu.VMEM((nb, 8 * WIN_Q_BLOCK, keys), F32),
                        pltpu.VMEM((nb, 8 * WIN_Q_BLOCK, keys), BF16),
                        pltpu.VMEM((nb, 8 * WIN_Q_BLOCK, LANES), F32)],
        compiler_params=_cparams(("parallel", "arbitrary")),
        name="win_attn",
    )(sink, q, k, v, kc, vc)


NBR_TILE_ROWS = 4
NBR_WIN_ROWS = NBR_TILE_ROWS + NBR_ROWS
NBR_TAB_PAD = NBR_WIN_ROWS - NBR_ROWS
NBR_ROW_BLOCK = 32
NBR_TILES_PER_STEP = 2


def _nbr_body(rows, q_ref, k_ref, v_ref, kc_ref, vc_ref, tab_ref, o_ref, s_sc, p_sc, l_sc):
    tq = NBR_TILE_ROWS * GRID_W
    kw = NBR_WIN_ROWS * GRID_W
    lo = _lane_mask(LANES, [(0, 64)])
    hi = _lane_mask(LANES, [(64, 128)])
    past = kc_ref.shape[2]
    rb = NBR_ROW_BLOCK
    tiles = q_ref.shape[1] // tq
    for u in range(tiles):
        r0 = NBR_TILE_ROWS * (pl.program_id(1) * tiles + u)
        ws = jnp.clip(r0 - NBR_ROWS // 2, 0, rows - NBR_WIN_ROWS)
        kstart = pl.multiple_of(ws * GRID_W, LANES)
        k_row = ws + lax.broadcasted_iota(jnp.int32, (1, kw), 1) // GRID_W
        for j in range(4):
            sl = slice(128 * j, 128 * j + 128)
            qs = _stack_heads(q_ref[0, u * tq:(u + 1) * tq, sl], lo, hi)
            s_sc[u, j, :, 0:kw] = _dot_nt(qs, k_ref[0, pl.ds(kstart, kw), sl])
            s_sc[u, j, :, kw:kw + past] = _dot_nt(qs, kc_ref[0, 0, :, sl])
        for j in range(4):
            sl = slice(128 * j, 128 * j + 128)
            for b0 in range(0, 2 * tq, rb):
                h = 2 * j + b0 // tq
                ql, sub = divmod(b0 % tq, GRID_W)
                d0 = ws - r0 - ql + (NBR_ROWS - 1) + NBR_TAB_PAD
                bias = jnp.concatenate([tab_ref[h, d0 + 2 * m, sub:sub + rb, :]
                                        for m in range(NBR_WIN_ROWS // 2)], axis=-1)
                rs = jnp.clip(r0 + ql - NBR_ROWS // 2, 0, rows - NBR_ROWS)
                valid = (k_row >= rs) & (k_row < rs + NBR_ROWS)
                s_nb = jnp.where(valid, s_sc[u, j, b0:b0 + rb, 0:kw] + bias, NEG)
                p, l = _softmax_block([s_nb, s_sc[u, j, b0:b0 + rb, kw:kw + past]])
                p_sc[u, j, b0:b0 + rb, :] = p
                l_sc[u, j, b0:b0 + rb, :] = l
            v_all = jnp.concatenate([v_ref[0, pl.ds(kstart, kw), sl], vc_ref[0, 0, :, sl]], axis=0)
            o = _dot(p_sc[u, j], v_all) / l_sc[u, j]
            o_ref[0, u * tq:(u + 1) * tq, sl] = _merge_heads(o, tq).astype(BF16)


def _nbr_attn(layer, q, k, v, kc, vc, table):
    b, seq, _ = q.shape
    rows = seq // GRID_W
    tile = NBR_TILE_ROWS * GRID_W
    tq = NBR_TILES_PER_STEP * tile
    assert seq % tq == 0 and rows >= NBR_WIN_ROWS and (rows - NBR_WIN_ROWS) % 2 == 0, (seq, rows)
    assert NBR_TILE_ROWS % 2 == 0 and (NBR_ROWS // 2) % 2 == 0 and GRID_W % NBR_ROW_BLOCK == 0
    past = kc.shape[2]
    keys = NBR_WIN_ROWS * GRID_W + past
    once = pl.Buffered(1)
    return pl.pallas_call(
        functools.partial(_nbr_body, rows),
        grid=(b, seq // tq),
        in_specs=[pl.BlockSpec((1, tq, 512), lambda bi, i: (bi, i, 0)),
                  pl.BlockSpec((1, seq, 512), lambda bi, i: (bi, 0, 0), pipeline_mode=once),
                  pl.BlockSpec((1, seq, 512), lambda bi, i: (bi, 0, 0), pipeline_mode=once),
                  pl.BlockSpec((1, 1, past, 512), lambda bi, i: (bi, layer, 0, 0), pipeline_mode=once),
                  pl.BlockSpec((1, 1, past, 512), lambda bi, i: (bi, layer, 0, 0), pipeline_mode=once),
                  pl.BlockSpec(table.shape, lambda bi, i: (0, 0, 0, 0), pipeline_mode=once)],
        out_specs=pl.BlockSpec((1, tq, 512), lambda bi, i: (bi, i, 0)),
        out_shape=jax.ShapeDtypeStruct((b, seq, 512), BF16),
        scratch_shapes=[pltpu.VMEM((NBR_TILES_PER_STEP, 4, 2 * tile, keys), F32),
                        pltpu.VMEM((NBR_TILES_PER_STEP, 4, 2 * tile, keys), BF16),
                        pltpu.VMEM((NBR_TILES_PER_STEP, 4, 2 * tile, LANES), F32)],
        compiler_params=_cparams(("parallel", "arbitrary")),
        name="nbr_attn",
    )(q, k, v, kc, vc, table)


def _nbr_bias_table(rel_bias):
    col = np.arange(GRID_W)
    cs = np.clip(col - NBR_COLS // 2, 0, GRID_W - NBR_COLS)
    kc = np.arange(GRID_W)
    ok = (kc[None, :] >= cs[:, None]) & (kc[None, :] < cs[:, None] + NBR_COLS)
    dc = kc[None, :] - col[:, None] + (NBR_COLS - 1)
    pick = (dc[:, :, None] == np.arange(2 * NBR_COLS - 1)[None, None, :]) & ok[:, :, None]
    t = jnp.einsum("hdk,qck->hdqc", rel_bias.astype(F32) * LOG2E, jnp.asarray(pick, F32),
                   precision=lax.Precision.HIGHEST)
    t = jnp.where(jnp.asarray(ok)[None, None], t, NEG)
    t = jnp.pad(t, ((0, 0), (NBR_TAB_PAD, NBR_TAB_PAD), (0, 0), (0, 0)))
    return jnp.concatenate([t[:, :-1], t[:, 1:]], axis=-1)


MLA_Q_TILE = 1024
MLA_KEY_CHUNK = 512
MLA_ROW_BLOCK = 64


def _mla_body(q_ref, kl_ref, kc_ref, vl_ref, vc_ref, o_ref, qs_sc, m_sc, l_sc, acc_sc, s_sc, p_sc, a_sc):
    tq = q_ref.shape[1]
    rows = 2 * tq
    seq = kl_ref.shape[1]
    past = kc_ref.shape[1]
    tk = min(MLA_KEY_CHUNK, seq)
    rb = MLA_ROW_BLOCK
    m0, m1 = _mla_masks(pl.program_id(1) % 2)
    qs_sc[...] = _stack_heads(q_ref[0], m0, m1)
    m_sc[...] = jnp.full(m_sc.shape, NEG, F32)
    l_sc[...] = jnp.zeros(l_sc.shape, F32)
    acc_sc[...] = jnp.zeros(acc_sc.shape, F32)

    def step(c, k, v):
        slab = c % 2
        n = k.shape[0]
        s_sc[slab, :, 0:n] = _dot_nt(qs_sc[...], k)
        for r in range(0, rows, rb):
            sl = slice(r, r + rb)
            sb = s_sc[slab, sl, 0:n]
            mp = None
            for t in _lane_tiles(sb):
                mp = t if mp is None else jnp.maximum(mp, t)
            m_prev = m_sc[sl]
            m_new = jnp.maximum(m_prev, jnp.max(mp, axis=-1, keepdims=True))
            alpha = jnp.exp2(m_prev - m_new)
            p = jnp.exp2(sb - _tile_lanes(m_new, n))
            psum = None
            for t in _lane_tiles(p):
                psum = t if psum is None else psum + t
            l_sc[sl] = alpha * l_sc[sl] + psum
            m_sc[sl] = m_new
            a_sc[slab, sl, :] = alpha
            p_sc[slab, sl, 0:n] = p.astype(BF16)
        acc_sc[...] = a_sc[slab] * acc_sc[...] + _dot(p_sc[slab, :, 0:n], v)

    chunks = [(kl_ref, vl_ref, o, min(tk, seq - o)) for o in range(0, seq, tk)]
    chunks += [(kc_ref, vc_ref, o, min(tk, past - o)) for o in range(0, past, tk)]
    for c, (k_ref, v_ref, o, n) in enumerate(chunks):
        step(c, k_ref[0, o:o + n, :], v_ref[0, o:o + n, :])
    l = jnp.sum(l_sc[...], axis=-1, keepdims=True)
    o_ref[0] = _merge_heads(acc_sc[...] / l, tq).astype(BF16)


def _mla_attn(q, kl, vl, kc, vc):
    b, seq, _ = q.shape
    past = kc.shape[1]
    tq = min(MLA_Q_TILE, seq)
    tk = min(MLA_KEY_CHUNK, seq)
    assert seq % tq == 0 and seq % LANES == 0 and past % LANES == 0 and (2 * tq) % MLA_ROW_BLOCK == 0
    return pl.pallas_call(
        _mla_body,
        grid=(b, 4, seq // tq),
        in_specs=[pl.BlockSpec((1, tq, 256), lambda bi, p, qi: (bi, qi, p)),
                  pl.BlockSpec((1, seq, 256), lambda bi, p, qi: (bi, 0, p)),
                  pl.BlockSpec((1, past, 256), lambda bi, p, qi: (bi, 0, p)),
                  pl.BlockSpec((1, seq, 128), lambda bi, p, qi: (bi, 0, p)),
                  pl.BlockSpec((1, past, 128), lambda bi, p, qi: (bi, 0, p))],
        out_specs=pl.BlockSpec((1, tq, 128), lambda bi, p, qi: (bi, qi, p)),
        out_shape=jax.ShapeDtypeStruct((b, seq, 512), BF16),
        scratch_shapes=[pltpu.VMEM((2 * tq, 256), BF16), pltpu.VMEM((2 * tq, LANES), F32),
                        pltpu.VMEM((2 * tq, LANES), F32), pltpu.VMEM((2 * tq, LANES), F32),
                        pltpu.VMEM((2, 2 * tq, tk), F32), pltpu.VMEM((2, 2 * tq, tk), BF16),
                        pltpu.VMEM((2, 2 * tq, LANES), F32)],
        compiler_params=_cparams(("parallel", "parallel", "arbitrary")),
        name="mla_attn",
    )(q, kl, kc, vl, vc)


def _pack_pairs(x):
    w = x.shape[1] // 2
    hi = lax.bitcast_convert_type(x[:, :w].astype(BF16).astype(F32), jnp.int32)
    lo = lax.bitcast_convert_type(x[:, w:].astype(BF16).astype(F32), jnp.int32)
    return (hi & jnp.int32(-65536)) | lax.shift_right_logical(lo, jnp.int32(16))


def _unpack_pairs(p):
    hi = lax.bitcast_convert_type(p & jnp.int32(-65536), F32)
    lo = lax.bitcast_convert_type(lax.shift_left(p, jnp.int32(16)), F32)
    return jnp.concatenate([hi, lo], axis=-1)


def _merge_body(x_ref, oa_ref, ob_ref, oc_ref, mod_ref, g_ref, wg_ref, woa_ref, wob_ref, woc_ref,
                wout_ref, wr_ref, br_ref, tri_ref, x1_ref, route_ref, h2p_ref, count_ref, count_sc):
    x = x_ref[...]
    g = g_ref[...]
    mod = mod_ref[0]
    hb = _norm_mod(x, g[0:1], mod[1:2], mod[0:1]).astype(BF16)
    m = None
    for br, (o_ref, wo_ref) in enumerate(((oa_ref, woa_ref), (ob_ref, wob_ref), (oc_ref, woc_ref))):
        z = _dot(hb, wg_ref[:, D_MODEL * br:D_MODEL * (br + 1)])
        gate = 1.0 / (1.0 + jnp.exp(-z))
        t = gate * _dot(o_ref[...], wo_ref[...])
        m = t if m is None else m + t
    y = _dot(m.astype(BF16), wout_ref[...])
    x1 = x + mod[2:3] * y
    x1_ref[...] = x1

    h2 = _norm_mod(x1, g[6:7], mod[4:5], mod[3:4])
    h_hi, h_lo = _split(h2)
    w_hi, w_lo = _split(wr_ref[...])
    both_w = _dot_nt(jnp.concatenate([w_hi, w_lo], axis=0), h_hi)
    logits = both_w[0:N_EXPERTS] + both_w[N_EXPERTS:2 * N_EXPERTS] + _dot_nt(w_hi, h_lo)
    score = 1.0 / (1.0 + jnp.exp(-logits))
    sel = score + br_ref[...]
    sel_r = [sel[e:e + 1] for e in range(N_EXPERTS)]
    sc_r = [score[e:e + 1] for e in range(N_EXPERTS)]
    picked = []
    for e in range(N_EXPERTS):
        grp, a = divmod(e, EXPERTS_PER_GROUP)
        rank = None
        for bb in range(EXPERTS_PER_GROUP):
            if bb == a:
                continue
            o = sel_r[grp * EXPERTS_PER_GROUP + bb]
            beats = (o >= sel_r[e]) if bb < a else (o > sel_r[e])
            r = jnp.where(beats, 1.0, 0.0)
            rank = r if rank is None else rank + r
        picked.append(rank < 2.0)
    gscore = []
    for grp in range(N_GROUPS):
        tot = None
        for a in range(EXPERTS_PER_GROUP):
            e = grp * EXPERTS_PER_GROUP + a
            t = jnp.where(picked[e], sel_r[e], 0.0)
            tot = t if tot is None else tot + t
        gscore.append(tot)
    best = jnp.zeros_like(gscore[0])
    best_v = gscore[0]
    for grp in range(1, N_GROUPS):
        upd = gscore[grp] > best_v
        best = jnp.where(upd, float(grp), best)
        best_v = jnp.where(upd, gscore[grp], best_v)
    cw, pk = [], []
    for a in range(EXPERTS_PER_GROUP):
        tot = flag = None
        for grp in range(N_GROUPS):
            e = grp * EXPERTS_PER_GROUP + a
            f = (best == float(grp)) & picked[e]
            t = jnp.where(f, sc_r[e], 0.0)
            tot = t if tot is None else tot + t
            flag = f if flag is None else (flag | f)
        cw.append(tot)
        pk.append(flag)
    den = cw[0] + cw[1] + cw[2] + cw[3]
    first = jnp.where(pk[0], 0.0, jnp.where(pk[1], 1.0, jnp.where(pk[2], 2.0, 3.0)))
    second = jnp.where(pk[3], 3.0, jnp.where(pk[2], 2.0, jnp.where(pk[1], 1.0, 0.0)))
    slot_e, slot_w = [], []
    for which in (first, second):
        tot = None
        for a in range(EXPERTS_PER_GROUP):
            t = jnp.where(which == float(a), cw[a], 0.0)
            tot = t if tot is None else tot + t
        slot_w.append(tot / den)
        slot_e.append(best * float(EXPERTS_PER_GROUP) + which)

    @pl.when(pl.program_id(0) == 0)
    def _():
        count_sc[...] = jnp.zeros(count_sc.shape, F32)

    tm = x.shape[0]
    eid = lax.broadcasted_iota(jnp.int32, (N_EXPERTS, tm), 0).astype(F32)
    oh = [eid == slot_e[0], eid == slot_e[1]]
    both = jnp.where(oh[0] | oh[1], 1.0, 0.0)
    seen = count_sc[...][:, 0:1] + _dot(both.astype(BF16), tri_ref[...])
    for k in range(2):
        route_ref[k:k + 1, :] = slot_w[k]
        route_ref[2 + k:3 + k, :] = slot_e[k]
        route_ref[4 + k:5 + k, :] = jnp.sum(jnp.where(oh[k], seen, 0.0), axis=0, keepdims=True)
    route_ref[6:8, :] = jnp.zeros((2, tm), F32)
    count_sc[...] = count_sc[...] + jnp.sum(both, axis=-1, keepdims=True)
    count_ref[...] = count_sc[...]
    h2p_ref[...] = _pack_pairs(h2)


def _merge(x, oa, ob, oc, mods, rows_per_mod, gains, w_gate, wo_a, wo_b, wo_c, w_out, w_r_t, b_r):
    n = x.shape[0]
    tm = min(1024, n)
    assert n % tm == 0 and (rows_per_mod % tm == 0 or rows_per_mod == n), (n, tm, rows_per_mod)
    row = lambda w: pl.BlockSpec((tm, w), lambda i: (i, 0))
    tri = jnp.asarray(np.triu(np.ones((tm, tm), np.float32), 1), BF16)
    consts = [gains, w_gate, wo_a, wo_b, wo_c, w_out, w_r_t, b_r, tri]
    return pl.pallas_call(
        _merge_body,
        grid=(n // tm,),
        in_specs=[row(D_MODEL), row(512), row(512), row(512),
                  pl.BlockSpec((1, 8, D_MODEL), lambda i: ((i * tm) // rows_per_mod, 0, 0))]
                 + [_const_spec(c.shape) for c in consts],
        out_specs=[row(D_MODEL), pl.BlockSpec((8, tm), lambda i: (0, i)), row(512),
                   pl.BlockSpec((N_EXPERTS, LANES), lambda i: (0, 0))],
        out_shape=[jax.ShapeDtypeStruct((n, D_MODEL), F32), jax.ShapeDtypeStruct((8, n), F32),
                   jax.ShapeDtypeStruct((n, 512), jnp.int32),
                   jax.ShapeDtypeStruct((N_EXPERTS, LANES), F32)],
        scratch_shapes=[pltpu.VMEM((N_EXPERTS, LANES), F32)],
        compiler_params=_cparams(("arbitrary",)),
        name="merge",
    )(x, oa, ob, oc, mods, *consts)


EXPERT_TILE = 512
SC_CORES = 2
SC_SUBCORES = 16
SC_WORKERS = SC_CORES * SC_SUBCORES
SC_WINDOW = 128


def _sc_mesh():
    return plsc.VectorSubcoreMesh(core_axis_name="c", subcore_axis_name="s", num_cores=SC_CORES,
                                  num_subcores=SC_SUBCORES)


def _sc_window_base(steps, j):
    wid = lax.axis_index("s") * SC_CORES + lax.axis_index("c")
    return pl.multiple_of((wid * steps + j) * SC_WINDOW, SC_WINDOW)


def _sc_dispatch(rows, pos0, pos1, n_out):
    n, w = rows.shape
    assert n % (SC_WORKERS * SC_WINDOW) == 0, n
    steps = n // (SC_WORKERS * SC_WINDOW)

    @functools.partial(
        pl.kernel, out_type=jax.ShapeDtypeStruct((n_out, w), rows.dtype), mesh=_sc_mesh(),
        scratch_types=[pltpu.VMEM((SC_WINDOW,), jnp.int32), pltpu.VMEM((SC_WINDOW,), jnp.int32),
                       pltpu.VMEM((SC_WINDOW, w), rows.dtype)],
        name="moe_dispatch")
    def run(x_hbm, i0_hbm, i1_hbm, o_hbm, i0_v, i1_v, rows_v):
        @pl.loop(0, steps)
        def _(j):
            base = _sc_window_base(steps, j)
            pltpu.sync_copy(i0_hbm.at[pl.ds(base, SC_WINDOW)], i0_v)
            pltpu.sync_copy(i1_hbm.at[pl.ds(base, SC_WINDOW)], i1_v)
            pltpu.sync_copy(x_hbm.at[pl.ds(base, SC_WINDOW)], rows_v)
            pltpu.sync_copy(rows_v, o_hbm.at[i0_v])
            pltpu.sync_copy(rows_v, o_hbm.at[i1_v])

    return run(rows, pos0, pos1)


def _sc_collect(rows, pos0, pos1):
    n = pos0.shape[0]
    w = rows.shape[1]
    assert n % (SC_WORKERS * SC_WINDOW) == 0, n
    steps = n // (SC_WORKERS * SC_WINDOW)
    out = jax.ShapeDtypeStruct((n, w), rows.dtype)

    @functools.partial(
        pl.kernel, out_type=[out, out], mesh=_sc_mesh(),
        scratch_types=[pltpu.VMEM((SC_WINDOW,), jnp.int32), pltpu.VMEM((SC_WINDOW, w), rows.dtype)],
        name="moe_collect")
    def run(y_hbm, i0_hbm, i1_hbm, o0_hbm, o1_hbm, i_v, rows_v):
        @pl.loop(0, steps)
        def _(j):
            base = _sc_window_base(steps, j)
            for i_hbm, o_hbm in ((i0_hbm, o0_hbm), (i1_hbm, o1_hbm)):
                pltpu.sync_copy(i_hbm.at[pl.ds(base, SC_WINDOW)], i_v)
                pltpu.sync_copy(y_hbm.at[i_v], rows_v)
                pltpu.sync_copy(rows_v, o_hbm.at[pl.ds(base, SC_WINDOW)])

    return run(rows, pos0, pos1)


def _experts_body(layer, te_ref, nv_ref, first_ref, next_ref, slot_ref, xs_ref, wg_hbm, wu_hbm, wd_hbm,
                  ys_ref, stage_g, stage_u, stage_d, wg_sc, wu_sc, wd_sc, sem):
    j = pl.program_id(0)

    def copies(expert, slot):
        return [pltpu.make_async_copy(w.at[layer, expert], st.at[slot], sem.at[slot, i])
                for i, (w, st) in enumerate(((wg_hbm, stage_g), (wu_hbm, stage_u), (wd_hbm, stage_d)))]

    @pl.when(j == 0)
    def _():
        for c in copies(te_ref[0], slot_ref[0]):
            c.start()

    @pl.when(first_ref[j] == 1)
    def _():
        slot = slot_ref[j]
        for c in copies(te_ref[j], slot):
            c.wait()

        @pl.when(next_ref[j] >= 0)
        def _():
            for c in copies(next_ref[j], 1 - slot):
                c.start(priority=1)

        wg_sc[...] = stage_g[slot].astype(BF16)
        wu_sc[...] = stage_u[slot].astype(BF16)
        wd_sc[...] = stage_d[slot].astype(BF16)

    @pl.when(j < nv_ref[0])
    def _():
        x = _unpack_pairs(xs_ref[...]).astype(BF16)
        zg = _dot(x, wg_sc[...])
        act = zg * (1.0 / (1.0 + jnp.exp(-zg))) * _dot(x, wu_sc[...])
        ys_ref[...] = _pack_pairs(_dot(act.astype(BF16), wd_sc[...]))

    @pl.when(j >= nv_ref[0])
    def _():
        ys_ref[...] = jnp.zeros(ys_ref.shape, ys_ref.dtype)


def _experts(layer, xs, tile_expert, n_valid, wg, wu, wd):
    p = xs.shape[0]
    n_tiles = p // EXPERT_TILE
    assert p % EXPERT_TILE == 0
    tile = jnp.arange(n_tiles, dtype=jnp.int32)
    last = jnp.take(tile_expert, jnp.maximum(n_valid[0] - 1, 0))
    te = jnp.where(tile < n_valid[0], tile_expert, last)
    first = jnp.concatenate([jnp.ones((1,), jnp.int32), (te[1:] != te[:-1]).astype(jnp.int32)])
    slot = (jnp.cumsum(first) - 1) % 2
    later_first = (tile[None, :] > tile[:, None]) & (first[None, :] == 1)
    nxt_tile = jnp.min(jnp.where(later_first, tile[None, :], n_tiles), axis=1)
    nxt = jnp.where(nxt_tile < n_tiles, jnp.take(te, jnp.minimum(nxt_tile, n_tiles - 1)), -1)

    any_spec = pl.BlockSpec(memory_space=pl.ANY)
    idx = lambda j, *_: (j, 0)
    grid_spec = pltpu.PrefetchScalarGridSpec(
        num_scalar_prefetch=5,
        grid=(n_tiles,),
        in_specs=[pl.BlockSpec((EXPERT_TILE, 512), idx), any_spec, any_spec, any_spec],
        out_specs=pl.BlockSpec((EXPERT_TILE, 512), idx),
        scratch_shapes=[pltpu.VMEM((2, D_MODEL, D_FF), F32), pltpu.VMEM((2, D_MODEL, D_FF), F32),
                        pltpu.VMEM((2, D_FF, D_MODEL), F32),
                        pltpu.VMEM((D_MODEL, D_FF), BF16), pltpu.VMEM((D_MODEL, D_FF), BF16),
                        pltpu.VMEM((D_FF, D_MODEL), BF16), pltpu.SemaphoreType.DMA((2, 3))])
    return pl.pallas_call(
        functools.partial(_experts_body, layer), grid_spec=grid_spec,
        out_shape=jax.ShapeDtypeStruct((p, 512), jnp.int32),
        compiler_params=_cparams(("arbitrary",)),
        name="experts",
    )(te, n_valid, first, nxt.astype(jnp.int32), slot.astype(jnp.int32), xs, wg, wu, wd)


def _combine_body(x1_ref, y0_ref, y1_ref, route_ref, mod_ref, o_ref):
    tm = x1_ref.shape[0]
    eye = jnp.where(lax.broadcasted_iota(jnp.int32, (tm, tm), 0)
                    == lax.broadcasted_iota(jnp.int32, (tm, tm), 1), 1.0, 0.0).astype(BF16)
    r_hi, r_lo = _split(route_ref[...])
    wcol = _dot_nt(eye, r_hi) + _dot_nt(eye, r_lo)
    moe = wcol[:, 0:1] * _unpack_pairs(y0_ref[...]) + wcol[:, 1:2] * _unpack_pairs(y1_ref[...])
    o_ref[...] = x1_ref[...] + mod_ref[0][5:6] * moe


def _combine(x1, y0, y1, route, mods, rows_per_mod):
    n = x1.shape[0]
    tm = min(512, n)
    row = lambda w: pl.BlockSpec((tm, w), lambda i: (i, 0))
    return pl.pallas_call(
        _combine_body,
        grid=(n // tm,),
        in_specs=[row(D_MODEL), row(512), row(512), pl.BlockSpec((8, tm), lambda i: (0, i)),
                  pl.BlockSpec((1, 8, D_MODEL), lambda i: ((i * tm) // rows_per_mod, 0, 0))],
        out_specs=row(D_MODEL),
        out_shape=jax.ShapeDtypeStruct((n, D_MODEL), F32),
        compiler_params=_cparams(("parallel",)),
        name="combine",
    )(x1, y0, y1, route, mods)


def _moe(layer, x1, route, h2p, counts, mods, rows_per_mod, wg, wu, wd):
    n = x1.shape[0]
    p = 2 * n + N_EXPERTS * EXPERT_TILE
    cnt = counts[:, 0].astype(jnp.int32)
    padded = (cnt + EXPERT_TILE - 1) // EXPERT_TILE * EXPERT_TILE
    seg_end = jnp.cumsum(padded)
    seg_off = seg_end - padded
    experts = jnp.arange(N_EXPERTS, dtype=jnp.int32)

    def position(k):
        e = route[2 + k].astype(jnp.int32)
        off = jnp.sum(jnp.where(e[:, None] == experts[None], seg_off[None], 0), axis=1)
        return off + route[4 + k].astype(jnp.int32)

    pos0, pos1 = position(0), position(1)
    tile_start = jnp.arange(p // EXPERT_TILE, dtype=jnp.int32) * EXPERT_TILE
    tile_expert = jnp.sum(tile_start[:, None] >= seg_end[None], axis=1).astype(jnp.int32)
    tile_expert = jnp.minimum(tile_expert, N_EXPERTS - 1)
    n_valid = (seg_end[-1:] // EXPERT_TILE).astype(jnp.int32)

    xs = _sc_dispatch(h2p, pos0, pos1, p)
    ys = _experts(layer, xs, tile_expert, n_valid, wg, wu, wd)
    y0, y1 = _sc_collect(ys, pos0, pos1)
    return _combine(x1, y0, y1, route, mods, rows_per_mod)


def _block_ones(n_in, g_in, n_out, g_out, value=1.0):
    r = np.arange(n_in)[:, None] // g_in
    c = np.arange(n_out)[None, :] // g_out
    return jnp.asarray(np.where(r == c, value, 0.0), dtype=BF16)


def _rope_tables(seq, head_w):
    pos = np.arange(seq)
    rows, cols = pos // GRID_W, pos % GRID_W
    a = head_w // 2
    half = a // 2
    freqs = (ROPE_BASE ** (-np.arange(half, dtype=np.float32) / half)).astype(np.float32)
    lane = np.arange(LANES) % head_w
    within = lane % a
    first = within < half
    p = np.where((lane // a == 0)[None, :], rows[:, None], cols[:, None]).astype(np.float32)
    ang = (p * freqs[within % half][None, :]).astype(np.float32)
    cos, sin = np.cos(ang), np.sin(ang)
    return (jnp.asarray(cos, F32), jnp.asarray(np.where(first[None], -sin, 0.0), F32),
            jnp.asarray(np.where(first[None], 0.0, sin), F32))


def _tile_to(v, width):
    return jnp.tile(v, width // v.shape[0])


def _layer_params(i, p):
    w_in = p["w_in"][i]
    sp = np.cumsum((512, 128, 128, 512, 512, 512, Q_LORA, KV_LORA, ROPE_DIM))
    qa, ka, va, qb, kb, vb, cq, ckv, kr, gates = jnp.split(w_in, [int(s) for s in sp], axis=1)
    qa = qa.reshape(D_MODEL, WIN_HEADS, HEAD_DIM)[:, WIN_Q_ORDER, :].reshape(D_MODEL, 512)
    w_a = jnp.concatenate([qa, ka, va, qb, kb, vb, cq, ckv, jnp.tile(kr, (1, MLA_HEADS))],
                          axis=1).astype(BF16)
    w_uq = p["w_uq"][i].reshape(Q_LORA, MLA_HEADS, QK_DIM)
    w_uq = jnp.concatenate([w_uq[:, :, :NOPE_DIM].reshape(Q_LORA, 512),
                            w_uq[:, :, NOPE_DIM:].reshape(Q_LORA, 256)], axis=1).astype(BF16)
    w_ukv = p["w_ukv"][i].reshape(KV_LORA, MLA_HEADS, NOPE_DIM + V_DIM)
    w_ukv = jnp.concatenate([w_ukv[:, :, :NOPE_DIM].reshape(KV_LORA, 512),
                             w_ukv[:, :, NOPE_DIM:].reshape(KV_LORA, 512)], axis=1).astype(BF16)
    z = jnp.zeros((D_MODEL,), F32)
    row = lambda *parts: jnp.concatenate(list(parts) + [z])[:D_MODEL]
    q_scale = HEAD_DIM ** -0.5 * LOG2E
    c_scale = QK_DIM ** -0.5 * LOG2E
    g_mla = p["g_qk_mla"][i]
    gains = jnp.stack([
        p["g_norm_mix"][i],
        row(_tile_to(p["g_qk_win"][i, 0], 512) * q_scale, _tile_to(p["g_qk_win"][i, 1], 128)),
        row(_tile_to(p["g_qk_nbr"][i, 0], 512) * q_scale, _tile_to(p["g_qk_nbr"][i, 1], 512)),
        row(p["g_q_lora"][i], p["g_kv_lora"][i]),
        row(_tile_to(g_mla[0, :NOPE_DIM], 512) * c_scale, _tile_to(g_mla[0, NOPE_DIM:], 256) * c_scale),
        row(_tile_to(g_mla[1, :NOPE_DIM], 512), _tile_to(g_mla[1, NOPE_DIM:], 256)),
        p["g_norm_ffn"][i],
        z]).astype(F32)
    wo_a = p["w_o_win"][i].reshape(WIN_HEADS, HEAD_DIM, D_MODEL)[WIN_Q_ORDER, :, :].reshape(512, D_MODEL)
    return dict(
        w_a=w_a, w_uq=w_uq, w_ukv=w_ukv, gains=gains, w_gate=gates.astype(BF16),
        wo_a=wo_a.astype(BF16), wo_b=p["w_o_nbr"][i].astype(BF16), wo_c=p["w_o_mla"][i].astype(BF16),
        w_out=p["w_out"][i].astype(BF16),
        nbr_table=_nbr_bias_table(p["nbr_rel_bias"][i]))


def kernel(x_prompt, x_sample, cache_win_k, cache_win_v, cache_nbr_k, cache_nbr_v, cache_mla_ckv, cache_mla_krope, c, c_ctx, g_norm_mix, g_norm_ffn, w_ada, b_ada, w_in, g_qk_win, win_sink, g_qk_nbr, nbr_rel_bias, g_q_lora, g_kv_lora, w_uq, w_ukv, g_qk_mla, w_o_win, w_o_nbr, w_o_mla, w_out, w_router, b_router, w_exp_gate, w_exp_up, w_exp_down):
    p = dict(g_norm_mix=g_norm_mix, g_norm_ffn=g_norm_ffn, w_in=w_in, g_qk_win=g_qk_win,
             g_qk_nbr=g_qk_nbr, nbr_rel_bias=nbr_rel_bias, g_q_lora=g_q_lora, g_kv_lora=g_kv_lora,
             w_uq=w_uq, w_ukv=w_ukv, g_qk_mla=g_qk_mla, w_o_win=w_o_win, w_o_nbr=w_o_nbr,
             w_o_mla=w_o_mla, w_out=w_out, w_exp_gate=w_exp_gate, w_exp_up=w_exp_up,
             w_exp_down=w_exp_down)
    depth = w_in.shape[0]
    batch, seq, _ = x_prompt.shape
    dec_batch, dec_seq, _ = x_sample.shape
    past = cache_win_k.shape[2]

    n_c = 1 + dec_batch
    c_rows = -(-n_c // 8) * 8
    c_all = jnp.concatenate([c_ctx[None], c, jnp.zeros((c_rows - n_c, D_MODEL), F32)], axis=0)
    mods = _ada(c_all, w_ada, b_ada).reshape(depth, c_rows, 6, D_MODEL)
    mods = jnp.pad(mods, ((0, 0), (0, 0), (0, 2), (0, 0)))

    mats = (_block_ones(512, 64, 512, 64, 1.0 / HEAD_DIM), _block_ones(512, 64, 512, 64),
            _block_ones(256, 32, 512, 64), _block_ones(512, 64, 256, 32), _block_ones(256, 32, 256, 32))
    tabs = _rope_tables(dec_seq, 64) + _rope_tables(dec_seq, 32)
    sink = win_sink.astype(F32) * LOG2E
    w_r_t = w_router.T.astype(F32)
    b_r = b_router.astype(F32).reshape(N_EXPERTS, 1)
    layers = [_layer_params(i, p) for i in range(depth)]

    def merge(x, oa, ob, oc, mod, rows_per_mod, lp):
        return _merge(x, oa, ob, oc, mod, rows_per_mod, lp["gains"], lp["w_gate"], lp["wo_a"],
                      lp["wo_b"], lp["wo_c"], lp["w_out"], w_r_t, b_r)

    n_ctx = batch * seq
    n_lat = dec_batch * dec_seq
    x_ctx = x_prompt.reshape(n_ctx, D_MODEL)
    x_lat = x_sample.reshape(n_lat, D_MODEL)
    cwk = cache_win_k.reshape(dec_batch, depth, past, 128).astype(BF16)
    cwv = cache_win_v.reshape(dec_batch, depth, past, 128).astype(BF16)
    cnk = cache_nbr_k.reshape(dec_batch, depth, past, 512).astype(BF16)
    cnv = cache_nbr_v.reshape(dec_batch, depth, past, 512).astype(BF16)
    states = []
    for i, lp in enumerate(layers):
        mod_c = mods[i, 0:1]
        outs = _inproj(x_ctx, mod_c, n_ctx, lp["gains"], lp["w_a"], lp["w_uq"], lp["w_ukv"], mats, None,
                       seq, True)
        oa, ob, oc = _ctx_attn(i, sink, seq, *outs[:9])
        states.append(outs[9:])
        merged_c = merge(x_ctx, oa, ob, oc, mod_c, n_ctx, lp)

        mod_l = mods[i, 1:1 + dec_batch]
        qa, ka, va, qb, kb, vb, qc, kc, vc = _inproj(
            x_lat, mod_l, dec_seq, lp["gains"], lp["w_a"], lp["w_uq"], lp["w_ukv"], mats, tabs, dec_seq,
            False)
        kr_t = jnp.tile(cache_mla_krope[:, i].reshape(dec_batch * past, ROPE_DIM), (1, MLA_HEADS))
        kc_c, vc_c = _mla_cache_keys(cache_mla_ckv[:, i].reshape(dec_batch * past, KV_LORA), kr_t,
                                     lp["gains"], lp["w_ukv"], mats[1], mats[3], mats[4])
        r3 = lambda a: a.reshape(dec_batch, dec_seq, a.shape[-1])
        oa = _win_attn(i, sink, r3(qa), r3(ka), r3(va), cwk, cwv)
        ob = _nbr_attn(i, r3(qb), r3(kb), r3(vb), cnk, cnv, lp["nbr_table"])
        oc = _mla_attn(r3(qc), r3(kc), r3(vc), kc_c.reshape(dec_batch, past, 1024),
                       vc_c.reshape(dec_batch, past, 512))
        flat = lambda a: a.reshape(n_lat, 512)
        merged_l = merge(x_lat, flat(oa), flat(ob), flat(oc), mod_l, dec_seq, lp)

        x_ctx = _moe(i, *merged_c, mod_c, n_ctx, w_exp_gate, w_exp_up, w_exp_down)
        x_lat = _moe(i, *merged_l, mod_l, dec_seq, w_exp_gate, w_exp_up, w_exp_down)
    y_prompt = x_ctx.reshape(batch, seq, D_MODEL)
    y_sample = x_lat.reshape(dec_batch, dec_seq, D_MODEL)

    def stack(k, shape):
        return jnp.stack([s[k].reshape((batch, seq) + shape) for s in states], axis=1)

    return (y_prompt, y_sample,
            stack(0, (WIN_KV_HEADS, HEAD_DIM)), stack(1, (WIN_KV_HEADS, HEAD_DIM)),
            stack(2, (NBR_HEADS, HEAD_DIM)), stack(3, (NBR_HEADS, HEAD_DIM)),
            stack(4, (KV_LORA,)), stack(5, (ROPE_DIM,)))
```

```python
import functools

import numpy as np
import jax
import jax.numpy as jnp
from jax import lax
from jax.experimental import pallas as pl
from jax.experimental.pallas import tpu as pltpu
from jax.experimental.pallas import tpu_sc as plsc

D_MODEL = 1024
GRID_W = 64
HEAD_DIM = 64
WIN_HEADS = 8
WIN_KV_HEADS = 2
WINDOW = 128
NBR_HEADS = 8
NBR_ROWS = 8
NBR_COLS = 16
MLA_HEADS = 8
Q_LORA = 256
KV_LORA = 128
NOPE_DIM = 64
ROPE_DIM = 32
V_DIM = 64
QK_DIM = NOPE_DIM + ROPE_DIM
N_EXPERTS = 16
N_GROUPS = 4
EXPERTS_PER_GROUP = 4
D_FF = 512
ROPE_BASE = 10000.0
EPS = 1e-6

LANES = 128
LOG2E = 1.4426950408889634
NEG = -1e30
VMEM_LIMIT = 56 * 1024 * 1024

F32 = jnp.float32
BF16 = jnp.bfloat16

C_QA, C_KA, C_VA, C_QB, C_KB, C_VB, C_CQ, C_CKV, C_KR, C_END = (
    0, 512, 640, 768, 1280, 1792, 2304, 2560, 2688, 2944)
WIN_Q_ORDER = (0, 4, 1, 5, 2, 6, 3, 7)


def _cparams(sem):
    return pltpu.CompilerParams(dimension_semantics=sem, vmem_limit_bytes=VMEM_LIMIT)


def _dot(a, b):
    return jnp.dot(a, b, preferred_element_type=F32)


def _dot_nt(a, b):
    return lax.dot_general(a, b, (((1,), (1,)), ((), ())), preferred_element_type=F32)


def _split(x):
    hi = x.astype(BF16)
    lo = (x - hi.astype(F32)).astype(BF16)
    return hi, lo


def _gsum(x2, bmat):
    return _dot(x2.astype(BF16), bmat)


def _tile_lanes(t, width):
    reps = width // t.shape[-1]
    return t if reps == 1 else jnp.concatenate([t] * reps, axis=-1)


def _rotate(x, cos, sin_a, sin_b, half):
    w = x.shape[-1]
    up = pltpu.roll(x, w - half, 1)
    dn = pltpu.roll(x, half, 1)
    return (x * _tile_lanes(cos, w) + up * _tile_lanes(sin_a, w) + dn * _tile_lanes(sin_b, w))


def _norm_mod(x, gain, scale, shift):
    ms = jnp.mean(x * x, axis=-1, keepdims=True)
    return (x * lax.rsqrt(ms + EPS) * gain) * (1.0 + scale) + shift


def _ada_body(c_ref, w_ref, b_ref, o_ref):
    c = c_ref[...]
    a = c * (1.0 / (1.0 + jnp.exp(-c)))
    a_hi, a_lo = _split(a)
    w_hi, w_lo = _split(w_ref[0])
    o_ref[0] = _dot(a_hi, w_hi) + _dot(a_hi, w_lo) + _dot(a_lo, w_hi) + b_ref[0]


def _ada(c_all, w_ada, b_ada):
    depth = w_ada.shape[0]
    rows = c_all.shape[0]
    tn = 1536
    return pl.pallas_call(
        _ada_body,
        grid=(depth, 6 * D_MODEL // tn),
        in_specs=[pl.BlockSpec((rows, D_MODEL), lambda l, j: (0, 0)),
                  pl.BlockSpec((1, D_MODEL, tn), lambda l, j: (l, 0, j)),
                  pl.BlockSpec((1, 1, tn), lambda l, j: (l, 0, j))],
        out_specs=pl.BlockSpec((1, rows, tn), lambda l, j: (l, 0, j)),
        out_shape=jax.ShapeDtypeStruct((depth, rows, 6 * D_MODEL), F32),
        compiler_params=_cparams(("parallel", "parallel")),
        name="ada",
    )(c_all, w_ada, b_ada.reshape(depth, 1, 6 * D_MODEL))


def _mla_key_tail(ckvn_b, kr_t, g, wukv_ref, bnn, bnr, brr, rope_tabs, kc_ref, vc_ref):
    kv = _dot(ckvn_b, wukv_ref[...])
    kn = kv[:, 0:512]
    vc_ref[...] = kv[:, 512:1024].astype(BF16)
    kn2 = kn * kn
    kr2 = kr_t * kr_t
    kr_sum32 = _gsum(kr2, brr)
    ssn = (_gsum(kn2, bnn) + jnp.concatenate([kr_sum32, kr_sum32], axis=-1)) * (1.0 / QK_DIM)
    ssr = (_gsum(kn2, bnr) + kr_sum32) * (1.0 / QK_DIM)
    kn = kn * lax.rsqrt(ssn + EPS) * g[5:6, 0:512]
    kr = kr_t * lax.rsqrt(ssr + EPS) * g[5:6, 512:768]
    if rope_tabs is not None:
        kr = _rotate(kr, *rope_tabs, 8)
    for p in range(4):
        kc_ref[:, 256 * p:256 * p + 128] = kn[:, 128 * p:128 * p + 128].astype(BF16)
        q4 = 128 * (p // 2)
        kc_ref[:, 256 * p + 128:256 * p + 256] = kr[:, q4:q4 + 128].astype(BF16)


def _inproj_body(rope, states, *refs):
    (x_ref, mod_ref, g_ref, w_ref, wuq_ref, wukv_ref, b64_ref, bnn_ref, brn_ref, bnr_ref,
     brr_ref) = refs[:11]
    refs = refs[11:]
    if rope:
        tabs_w = tuple(r[...] for r in refs[0:3])
        tabs_m = tuple(r[...] for r in refs[3:6])
        refs = refs[6:]
    else:
        tabs_w = tabs_m = None
    qa_ref, ka_ref, va_ref, qb_ref, kb_ref, vb_ref, qc_ref, kc_ref, vc_ref = refs[:9]
    st = refs[9:]

    g = g_ref[...]
    mod = mod_ref[0]
    hb = _norm_mod(x_ref[...], g[0:1], mod[1:2], mod[0:1]).astype(BF16)

    def proj(a, b):
        return _dot(hb, w_ref[:, a:b])

    b64 = b64_ref[...]

    def head_norm(z, bmat, gain):
        return z * lax.rsqrt(_gsum(z * z, bmat) + EPS) * gain

    qa = head_norm(proj(C_QA, C_KA), b64, g[1:2, 0:512])
    ka = head_norm(proj(C_KA, C_VA), b64[0:128, 0:128], g[1:2, 512:640])
    va = proj(C_VA, C_QB)
    if states:
        st[0][...] = ka
        st[1][...] = va
    if rope:
        qa = _rotate(qa, *tabs_w, 16)
        ka = _rotate(ka, *tabs_w, 16)
    qa_ref[...] = qa.astype(BF16)
    ka_ref[...] = ka.astype(BF16)
    va_ref[...] = va.astype(BF16)

    qb = head_norm(proj(C_QB, C_KB), b64, g[2:3, 0:512])
    kb = head_norm(proj(C_KB, C_VB), b64, g[2:3, 512:1024])
    vb = proj(C_VB, C_CQ)
    if states:
        st[2][...] = kb
        st[3][...] = vb
    qb_ref[...] = qb.astype(BF16)
    kb_ref[...] = kb.astype(BF16)
    vb_ref[...] = vb.astype(BF16)

    cq = proj(C_CQ, C_CKV)
    cqn = cq * lax.rsqrt(jnp.mean(cq * cq, axis=-1, keepdims=True) + EPS) * g[3:4, 0:256]
    qq = _dot(cqn.astype(BF16), wuq_ref[...])
    qn, qr = qq[:, 0:512], qq[:, 512:768]
    qn2, qr2 = qn * qn, qr * qr
    bnn, brn, bnr, brr = bnn_ref[...], brn_ref[...], bnr_ref[...], brr_ref[...]
    ssn = (_gsum(qn2, bnn) + _gsum(qr2, brn)) * (1.0 / QK_DIM)
    ssr = (_gsum(qn2, bnr) + _gsum(qr2, brr)) * (1.0 / QK_DIM)
    qn = qn * lax.rsqrt(ssn + EPS) * g[4:5, 0:512]
    qr = qr * lax.rsqrt(ssr + EPS) * g[4:5, 512:768]
    if rope:
        qr = _rotate(qr, *tabs_m, 8)
    for p in range(4):
        qc_ref[:, 256 * p:256 * p + 128] = qn[:, 128 * p:128 * p + 128].astype(BF16)
        q4 = 128 * (p // 2)
        qc_ref[:, 256 * p + 128:256 * p + 256] = qr[:, q4:q4 + 128].astype(BF16)

    ckv = proj(C_CKV, C_KR)
    ckvn = ckv * lax.rsqrt(jnp.mean(ckv * ckv, axis=-1, keepdims=True) + EPS) * g[3:4, 256:384]
    kr_t = proj(C_KR, C_END)
    if states:
        st[4][...] = ckvn
        st[5][...] = kr_t[:, 0:ROPE_DIM]
    _mla_key_tail(ckvn.astype(BF16), kr_t, g, wukv_ref, bnn, bnr, brr, tabs_m, kc_ref, vc_ref)


def _const_spec(shape):
    nd = len(shape)
    return pl.BlockSpec(shape, lambda i, _nd=nd: (0,) * _nd, pipeline_mode=pl.Buffered(1))


def _inproj(x, mods, rows_per_mod, gains, w_a, w_uq, w_ukv, mats, rope_tabs, seq_len, states):
    n = x.shape[0]
    tm = min(1024, n)
    rope = rope_tabs is not None
    assert n % tm == 0 and (rows_per_mod % tm == 0 or rows_per_mod == n), (n, tm, rows_per_mod)
    assert not rope or seq_len % tm == 0, (seq_len, tm)
    row = lambda w: pl.BlockSpec((tm, w), lambda i: (i, 0))
    in_specs = [row(D_MODEL),
                pl.BlockSpec((1, 8, D_MODEL), lambda i: ((i * tm) // rows_per_mod, 0, 0)),
                _const_spec(gains.shape), _const_spec(w_a.shape), _const_spec(w_uq.shape),
                _const_spec(w_ukv.shape)] + [_const_spec(m.shape) for m in mats]
    args = [x, mods, gains, w_a, w_uq, w_ukv, *mats]
    if rope:
        tiles_per_seq = seq_len // tm
        in_specs += [pl.BlockSpec((tm, LANES), lambda i: (i % tiles_per_seq, 0))] * 6
        args += list(rope_tabs)
    widths = [512, 128, 128, 512, 512, 512, 1024, 1024, 512]
    out_shape = [jax.ShapeDtypeStruct((n, w), BF16) for w in widths]
    out_specs = [row(w) for w in widths]
    if states:
        swidths = [128, 128, 512, 512, KV_LORA, ROPE_DIM]
        out_shape += [jax.ShapeDtypeStruct((n, w), F32) for w in swidths]
        out_specs += [row(w) for w in swidths]
    return pl.pallas_call(
        functools.partial(_inproj_body, rope, states),
        grid=(n // tm,), in_specs=in_specs, out_specs=out_specs, out_shape=out_shape,
        compiler_params=_cparams(("parallel",)),
        name="inproj_lat" if rope else "inproj_ctx",
    )(*args)


def _mla_cache_body(ckv_ref, kr_ref, g_ref, wukv_ref, bnn_ref, bnr_ref, brr_ref, kc_ref, vc_ref):
    _mla_key_tail(ckv_ref[...].astype(BF16), kr_ref[...], g_ref[...], wukv_ref, bnn_ref[...],
                  bnr_ref[...], brr_ref[...], None, kc_ref, vc_ref)


def _mla_cache_keys(ckv, kr_t, gains, w_ukv, bnn, bnr, brr):
    n = ckv.shape[0]
    tm = min(512, n)
    row = lambda w: pl.BlockSpec((tm, w), lambda i: (i, 0))
    return pl.pallas_call(
        _mla_cache_body,
        grid=(n // tm,),
        in_specs=[row(KV_LORA), row(256), _const_spec(gains.shape), _const_spec(w_ukv.shape),
                  _const_spec(bnn.shape), _const_spec(bnr.shape), _const_spec(brr.shape)],
        out_specs=[row(1024), row(512)],
        out_shape=[jax.ShapeDtypeStruct((n, 1024), BF16), jax.ShapeDtypeStruct((n, 512), BF16)],
        compiler_params=_cparams(("parallel",)),
        name="mla_cache_keys",
    )(ckv, kr_t, gains, w_ukv, bnn, bnr, brr)


def _lane_mask(width, ranges):
    lane = lax.broadcasted_iota(jnp.int32, (1, width), 1)
    m = None
    for lo, hi in ranges:
        c = (lane >= lo) & (lane < hi)
        m = c if m is None else (m | c)
    return jnp.where(m, 1.0, 0.0).astype(BF16)


def _stack_heads(q, mask0, mask1):
    return jnp.concatenate([q * mask0, q * mask1], axis=0)


def _lane_tiles(x):
    return [x[:, j:j + LANES] for j in range(0, x.shape[1], LANES)]


def _softmax_block(scores, sink=None):
    rows = scores[0].shape[0]
    mp = None
    for s in scores:
        for t in _lane_tiles(s):
            mp = t if mp is None else jnp.maximum(mp, t)
    base = sink if sink is not None else jnp.full((rows, LANES), NEG, F32)
    m = jnp.maximum(base, jnp.max(mp, axis=-1, keepdims=True))
    lp = None
    ps = []
    for s in scores:
        p = jnp.exp2(s - _tile_lanes(m, s.shape[1]))
        for t in _lane_tiles(p):
            lp = t if lp is None else lp + t
        ps.append(p.astype(BF16))
    if sink is not None:
        lane = lax.broadcasted_iota(jnp.int32, (rows, LANES), 1)
        lp = lp + jnp.where(lane == 0, jnp.exp2(sink - m), 0.0)
    p_all = ps[0] if len(ps) == 1 else jnp.concatenate(ps, axis=-1)
    return p_all, jnp.broadcast_to(jnp.sum(lp, axis=-1, keepdims=True), (rows, LANES))


def _softmax_pv(scores, values, sink=None):
    p_all, l = _softmax_block(scores, sink)
    v_all = values[0] if len(values) == 1 else jnp.concatenate(values, axis=0)
    return _dot(p_all, v_all) / l


def _merge_heads(o, tq):
    lane = lax.broadcasted_iota(jnp.int32, (tq, LANES), 1)
    return jnp.where(lane < HEAD_DIM, o[0:tq], o[tq:2 * tq])


def _mla_masks(p_mod2):
    lane = lax.broadcasted_iota(jnp.int32, (1, 256), 1)
    r0 = 128 + 32 * (2 * p_mod2)
    m0 = (lane < 64) | ((lane >= r0) & (lane < r0 + 32))
    m1 = ((lane >= 64) & (lane < 128)) | ((lane >= r0 + 32) & (lane < r0 + 64))
    return (jnp.where(m0, 1.0, 0.0).astype(BF16), jnp.where(m1, 1.0, 0.0).astype(BF16))


def _sink_col(sink_ref, layer, h0, h1, tq):
    row = lax.broadcasted_iota(jnp.int32, (2 * tq, LANES), 0)
    return jnp.where(row < tq, sink_ref[layer, h0], sink_ref[layer, h1])


CTX_SEQS_PER_STEP = 4


def _ctx_attn_body(layer, tq, sink_ref, qa_ref, ka_ref, va_ref, qb_ref, kb_ref, vb_ref, qc_ref, kc_ref,
                   vc_ref, oa_ref, ob_ref, oc_ref):
    lo = _lane_mask(LANES, [(0, 64)])
    hi = _lane_mask(LANES, [(64, 128)])
    for u in range(qa_ref.shape[0] // tq):
        rows = slice(u * tq, (u + 1) * tq)
        ka, va = ka_ref[rows, :], va_ref[rows, :]
        for j in range(4):
            sl = slice(128 * j, 128 * j + 128)
            qs = _stack_heads(qa_ref[rows, sl], lo, hi)
            sink = _sink_col(sink_ref, layer, j, 4 + j, tq)
            o = _softmax_pv([_dot_nt(qs, ka)], [va], sink)
            oa_ref[rows, sl] = _merge_heads(o, tq).astype(BF16)

            qs = _stack_heads(qb_ref[rows, sl], lo, hi)
            o = _softmax_pv([_dot_nt(qs, kb_ref[rows, sl])], [vb_ref[rows, sl]])
            ob_ref[rows, sl] = _merge_heads(o, tq).astype(BF16)

            m0, m1 = _mla_masks(j % 2)
            s2 = slice(256 * j, 256 * j + 256)
            qs = _stack_heads(qc_ref[rows, s2], m0, m1)
            o = _softmax_pv([_dot_nt(qs, kc_ref[rows, s2])], [vc_ref[rows, sl]])
            oc_ref[rows, sl] = _merge_heads(o, tq).astype(BF16)


def _ctx_attn(layer, sink, seq, qa, ka, va, qb, kb, vb, qc, kc, vc):
    n = qa.shape[0]
    group = CTX_SEQS_PER_STEP if (n // seq) % CTX_SEQS_PER_STEP == 0 else 1
    assert n % (group * seq) == 0
    tile, seq = group * seq, seq
    row = lambda w: pl.BlockSpec((tile, w), lambda b: (b, 0))
    ins = [qa, ka, va, qb, kb, vb, qc, kc, vc]
    return pl.pallas_call(
        functools.partial(_ctx_attn_body, layer, seq),
        grid=(n // tile,),
        in_specs=[pl.BlockSpec(memory_space=pltpu.SMEM)] + [row(a.shape[1]) for a in ins],
        out_specs=[row(512)] * 3,
        out_shape=[jax.ShapeDtypeStruct((n, 512), BF16)] * 3,
        compiler_params=_cparams(("parallel",)),
        name="ctx_attn",
    )(sink, *ins)


WIN_ROW_BLOCK = 32


WIN_Q_BLOCK = 128
WIN_BLOCKS_PER_STEP = 4


def _win_body(layer, sink_ref, q_ref, k_ref, v_ref, kc_ref, vc_ref, o_ref, s_sc, p_sc, l_sc):
    tq = WIN_Q_BLOCK
    seq = k_ref.shape[1]
    kw = 3 * tq
    rb = WIN_ROW_BLOCK
    lo = _lane_mask(LANES, [(0, 64)])
    hi = _lane_mask(LANES, [(64, 128)])
    kc, vc = kc_ref[0, 0], vc_ref[0, 0]
    for u in range(q_ref.shape[1] // tq):
        i = pl.program_id(1) * (q_ref.shape[1] // tq) + u
        kstart = pl.multiple_of(jnp.clip((i - 1) * tq, 0, seq - kw), tq)
        k_all = jnp.concatenate([k_ref[0, pl.ds(kstart, kw), :], kc], axis=0)
        v_all = jnp.concatenate([v_ref[0, pl.ds(kstart, kw), :], vc], axis=0)
        q_pos = i * tq + lax.broadcasted_iota(jnp.int32, (tq, kw), 0)
        k_pos = kstart + lax.broadcasted_iota(jnp.int32, (tq, kw), 1)
        band = jnp.abs(q_pos - k_pos) <= WINDOW
        qs = jnp.concatenate(
            [_stack_heads(q_ref[0, u * tq:(u + 1) * tq, 128 * j:128 * j + 128], lo, hi) for j in range(4)],
            axis=0)
        s_sc[u] = _dot_nt(qs, k_all)
        for j in range(4):
            for r in range(2 * tq * j, 2 * tq * (j + 1), rb):
                head = j if r < 2 * tq * j + tq else 4 + j
                q0 = r % tq
                s_band = jnp.where(band[q0:q0 + rb], s_sc[u, r:r + rb, 0:kw], NEG)
                sink = jnp.full((rb, LANES), sink_ref[layer, head], F32)
                p, l = _softmax_block([s_band, s_sc[u, r:r + rb, kw:]], sink)
                p_sc[u, r:r + rb, :] = p
                l_sc[u, r:r + rb, :] = l
            rows = slice(2 * tq * j, 2 * tq * (j + 1))
            o = _dot(p_sc[u, rows, :], v_all) / l_sc[u, rows, :]
            o_ref[0, u * tq:(u + 1) * tq, 128 * j:128 * j + 128] = _merge_heads(o, tq).astype(BF16)


def _win_attn(layer, sink, q, k, v, kc, vc):
    b, seq, _ = q.shape
    nb = min(WIN_BLOCKS_PER_STEP, seq // WIN_Q_BLOCK)
    tq = nb * WIN_Q_BLOCK
    past = kc.shape[2]
    assert WIN_Q_BLOCK == WINDOW and seq % tq == 0 and seq >= 3 * WIN_Q_BLOCK and past % LANES == 0
    keys = 3 * WIN_Q_BLOCK + past
    return pl.pallas_call(
        functools.partial(_win_body, layer),
        grid=(b, seq // tq),
        in_specs=[pl.BlockSpec(memory_space=pltpu.SMEM),
                  pl.BlockSpec((1, tq, 512), lambda bi, i: (bi, i, 0)),
                  pl.BlockSpec((1, seq, 128), lambda bi, i: (bi, 0, 0)),
                  pl.BlockSpec((1, seq, 128), lambda bi, i: (bi, 0, 0)),
                  pl.BlockSpec((1, 1, past, 128), lambda bi, i: (bi, layer, 0, 0)),
                  pl.BlockSpec((1, 1, past, 128), lambda bi, i: (bi, layer, 0, 0))],
        out_specs=pl.BlockSpec((1, tq, 512), lambda bi, i: (bi, i, 0)),
        out_shape=jax.ShapeDtypeStruct((b, seq, 512), BF16),
        scratch_shapes=[pltpu.VMEM((nb, 8 * WIN_Q_BLOCK, keys), F32),
                        pltpu.VMEM((nb, 8 * WIN_Q_BLOCK, keys), BF16),
                        pltpu.VMEM((nb, 8 * WIN_Q_BLOCK, LANES), F32)],
        compiler_params=_cparams(("parallel", "arbitrary")),
        name="win_attn",
    )(sink, q, k, v, kc, vc)


NBR_TILE_ROWS = 4
NBR_WIN_ROWS = NBR_TILE_ROWS + NBR_ROWS
NBR_TAB_PAD = NBR_WIN_ROWS - NBR_ROWS
NBR_ROW_BLOCK = 32
NBR_TILES_PER_STEP = 2


def _nbr_body(rows, q_ref, k_ref, v_ref, kc_ref, vc_ref, tab_ref, o_ref, s_sc, p_sc, l_sc):
    tq = NBR_TILE_ROWS * GRID_W
    kw = NBR_WIN_ROWS * GRID_W
    lo = _lane_mask(LANES, [(0, 64)])
    hi = _lane_mask(LANES, [(64, 128)])
    past = kc_ref.shape[2]
    rb = NBR_ROW_BLOCK
    tiles = q_ref.shape[1] // tq
    for u in range(tiles):
        r0 = NBR_TILE_ROWS * (pl.program_id(1) * tiles + u)
        ws = jnp.clip(r0 - NBR_ROWS // 2, 0, rows - NBR_WIN_ROWS)
        kstart = pl.multiple_of(ws * GRID_W, LANES)
        k_row = ws + lax.broadcasted_iota(jnp.int32, (1, kw), 1) // GRID_W
        for j in range(4):
            sl = slice(128 * j, 128 * j + 128)
            qs = _stack_heads(q_ref[0, u * tq:(u + 1) * tq, sl], lo, hi)
            s_sc[u, j, :, 0:kw] = _dot_nt(qs, k_ref[0, pl.ds(kstart, kw), sl])
            s_sc[u, j, :, kw:kw + past] = _dot_nt(qs, kc_ref[0, 0, :, sl])
        for j in range(4):
            sl = slice(128 * j, 128 * j + 128)
            for b0 in range(0, 2 * tq, rb):
                h = 2 * j + b0 // tq
                ql, sub = divmod(b0 % tq, GRID_W)
                d0 = ws - r0 - ql + (NBR_ROWS - 1) + NBR_TAB_PAD
                bias = jnp.concatenate([tab_ref[h, d0 + 2 * m, sub:sub + rb, :]
                                        for m in range(NBR_WIN_ROWS // 2)], axis=-1)
                rs = jnp.clip(r0 + ql - NBR_ROWS // 2, 0, rows - NBR_ROWS)
                valid = (k_row >= rs) & (k_row < rs + NBR_ROWS)
                s_nb = jnp.where(valid, s_sc[u, j, b0:b0 + rb, 0:kw] + bias, NEG)
                p, l = _softmax_block([s_nb, s_sc[u, j, b0:b0 + rb, kw:kw + past]])
                p_sc[u, j, b0:b0 + rb, :] = p
                l_sc[u, j, b0:b0 + rb, :] = l
            v_all = jnp.concatenate([v_ref[0, pl.ds(kstart, kw), sl], vc_ref[0, 0, :, sl]], axis=0)
            o = _dot(p_sc[u, j], v_all) / l_sc[u, j]
            o_ref[0, u * tq:(u + 1) * tq, sl] = _merge_heads(o, tq).astype(BF16)


def _nbr_attn(layer, q, k, v, kc, vc, table):
    b, seq, _ = q.shape
    rows = seq // GRID_W
    tile = NBR_TILE_ROWS * GRID_W
    tq = NBR_TILES_PER_STEP * tile
    assert seq % tq == 0 and rows >= NBR_WIN_ROWS and (rows - NBR_WIN_ROWS) % 2 == 0, (seq, rows)
    assert NBR_TILE_ROWS % 2 == 0 and (NBR_ROWS // 2) % 2 == 0 and GRID_W % NBR_ROW_BLOCK == 0
    past = kc.shape[2]
    keys = NBR_WIN_ROWS * GRID_W + past
    once = pl.Buffered(1)
    return pl.pallas_call(
        functools.partial(_nbr_body, rows),
        grid=(b, seq // tq),
        in_specs=[pl.BlockSpec((1, tq, 512), lambda bi, i: (bi, i, 0)),
                  pl.BlockSpec((1, seq, 512), lambda bi, i: (bi, 0, 0), pipeline_mode=once),
                  pl.BlockSpec((1, seq, 512), lambda bi, i: (bi, 0, 0), pipeline_mode=once),
                  pl.BlockSpec((1, 1, past, 512), lambda bi, i: (bi, layer, 0, 0), pipeline_mode=once),
                  pl.BlockSpec((1, 1, past, 512), lambda bi, i: (bi, layer, 0, 0), pipeline_mode=once),
                  pl.BlockSpec(table.shape, lambda bi, i: (0, 0, 0, 0), pipeline_mode=once)],
        out_specs=pl.BlockSpec((1, tq, 512), lambda bi, i: (bi, i, 0)),
        out_shape=jax.ShapeDtypeStruct((b, seq, 512), BF16),
        scratch_shapes=[pltpu.VMEM((NBR_TILES_PER_STEP, 4, 2 * tile, keys), F32),
                        pltpu.VMEM((NBR_TILES_PER_STEP, 4, 2 * tile, keys), BF16),
                        pltpu.VMEM((NBR_TILES_PER_STEP, 4, 2 * tile, LANES), F32)],
        compiler_params=_cparams(("parallel", "arbitrary")),
        name="nbr_attn",
    )(q, k, v, kc, vc, table)


def _nbr_bias_table(rel_bias):
    col = np.arange(GRID_W)
    cs = np.clip(col - NBR_COLS // 2, 0, GRID_W - NBR_COLS)
    kc = np.arange(GRID_W)
    ok = (kc[None, :] >= cs[:, None]) & (kc[None, :] < cs[:, None] + NBR_COLS)
    dc = kc[None, :] - col[:, None] + (NBR_COLS - 1)
    pick = (dc[:, :, None] == np.arange(2 * NBR_COLS - 1)[None, None, :]) & ok[:, :, None]
    t = jnp.einsum("hdk,qck->hdqc", rel_bias.astype(F32) * LOG2E, jnp.asarray(pick, F32),
                   precision=lax.Precision.HIGHEST)
    t = jnp.where(jnp.asarray(ok)[None, None], t, NEG)
    t = jnp.pad(t, ((0, 0), (NBR_TAB_PAD, NBR_TAB_PAD), (0, 0), (0, 0)))
    return jnp.concatenate([t[:, :-1], t[:, 1:]], axis=-1)


MLA_Q_TILE = 1024
MLA_KEY_CHUNK = 512
MLA_ROW_BLOCK = 64


def _mla_body(q_ref, kl_ref, kc_ref, vl_ref, vc_ref, o_ref, qs_sc, m_sc, l_sc, acc_sc, s_sc, p_sc, a_sc):
    tq = q_ref.shape[1]
    rows = 2 * tq
    seq = kl_ref.shape[1]
    past = kc_ref.shape[1]
    tk = min(MLA_KEY_CHUNK, seq)
    rb = MLA_ROW_BLOCK
    m0, m1 = _mla_masks(pl.program_id(1) % 2)
    qs_sc[...] = _stack_heads(q_ref[0], m0, m1)
    m_sc[...] = jnp.full(m_sc.shape, NEG, F32)
    l_sc[...] = jnp.zeros(l_sc.shape, F32)
    acc_sc[...] = jnp.zeros(acc_sc.shape, F32)

    def step(c, k, v):
        slab = c % 2
        n = k.shape[0]
        s_sc[slab, :, 0:n] = _dot_nt(qs_sc[...], k)
        for r in range(0, rows, rb):
            sl = slice(r, r + rb)
            sb = s_sc[slab, sl, 0:n]
            mp = None
            for t in _lane_tiles(sb):
                mp = t if mp is None else jnp.maximum(mp, t)
            m_prev = m_sc[sl]
            m_new = jnp.maximum(m_prev, jnp.max(mp, axis=-1, keepdims=True))
            alpha = jnp.exp2(m_prev - m_new)
            p = jnp.exp2(sb - _tile_lanes(m_new, n))
            psum = None
            for t in _lane_tiles(p):
                psum = t if psum is None else psum + t
            l_sc[sl] = alpha * l_sc[sl] + psum
            m_sc[sl] = m_new
            a_sc[slab, sl, :] = alpha
            p_sc[slab, sl, 0:n] = p.astype(BF16)
        acc_sc[...] = a_sc[slab] * acc_sc[...] + _dot(p_sc[slab, :, 0:n], v)

    chunks = [(kl_ref, vl_ref, o, min(tk, seq - o)) for o in range(0, seq, tk)]
    chunks += [(kc_ref, vc_ref, o, min(tk, past - o)) for o in range(0, past, tk)]
    for c, (k_ref, v_ref, o, n) in enumerate(chunks):
        step(c, k_ref[0, o:o + n, :], v_ref[0, o:o + n, :])
    l = jnp.sum(l_sc[...], axis=-1, keepdims=True)
    o_ref[0] = _merge_heads(acc_sc[...] / l, tq).astype(BF16)


def _mla_attn(q, kl, vl, kc, vc):
    b, seq, _ = q.shape
    past = kc.shape[1]
    tq = min(MLA_Q_TILE, seq)
    tk = min(MLA_KEY_CHUNK, seq)
    assert seq % tq == 0 and seq % LANES == 0 and past % LANES == 0 and (2 * tq) % MLA_ROW_BLOCK == 0
    return pl.pallas_call(
        _mla_body,
        grid=(b, 4, seq // tq),
        in_specs=[pl.BlockSpec((1, tq, 256), lambda bi, p, qi: (bi, qi, p)),
                  pl.BlockSpec((1, seq, 256), lambda bi, p, qi: (bi, 0, p)),
                  pl.BlockSpec((1, past, 256), lambda bi, p, qi: (bi, 0, p)),
                  pl.BlockSpec((1, seq, 128), lambda bi, p, qi: (bi, 0, p)),
                  pl.BlockSpec((1, past, 128), lambda bi, p, qi: (bi, 0, p))],
        out_specs=pl.BlockSpec((1, tq, 128), lambda bi, p, qi: (bi, qi, p)),
        out_shape=jax.ShapeDtypeStruct((b, seq, 512), BF16),
        scratch_shapes=[pltpu.VMEM((2 * tq, 256), BF16), pltpu.VMEM((2 * tq, LANES), F32),
                        pltpu.VMEM((2 * tq, LANES), F32), pltpu.VMEM((2 * tq, LANES), F32),
                        pltpu.VMEM((2, 2 * tq, tk), F32), pltpu.VMEM((2, 2 * tq, tk), BF16),
                        pltpu.VMEM((2, 2 * tq, LANES), F32)],
        compiler_params=_cparams(("parallel", "parallel", "arbitrary")),
        name="mla_attn",
    )(q, kl, kc, vl, vc)


def _pack_pairs(x):
    w = x.shape[1] // 2
    hi = lax.bitcast_convert_type(x[:, :w].astype(BF16).astype(F32), jnp.int32)
    lo = lax.bitcast_convert_type(x[:, w:].astype(BF16).astype(F32), jnp.int32)
    return (hi & jnp.int32(-65536)) | lax.shift_right_logical(lo, jnp.int32(16))


def _unpack_pairs(p):
    hi = lax.bitcast_convert_type(p & jnp.int32(-65536), F32)
    lo = lax.bitcast_convert_type(lax.shift_left(p, jnp.int32(16)), F32)
    return jnp.concatenate([hi, lo], axis=-1)


def _merge_body(x_ref, oa_ref, ob_ref, oc_ref, mod_ref, g_ref, wg_ref, woa_ref, wob_ref, woc_ref,
                wout_ref, wr_ref, br_ref, tri_ref, x1_ref, route_ref, h2p_ref, count_ref, count_sc):
    x = x_ref[...]
    g = g_ref[...]
    mod = mod_ref[0]
    hb = _norm_mod(x, g[0:1], mod[1:2], mod[0:1]).astype(BF16)
    m = None
    for br, (o_ref, wo_ref) in enumerate(((oa_ref, woa_ref), (ob_ref, wob_ref), (oc_ref, woc_ref))):
        z = _dot(hb, wg_ref[:, D_MODEL * br:D_MODEL * (br + 1)])
        gate = 1.0 / (1.0 + jnp.exp(-z))
        t = gate * _dot(o_ref[...], wo_ref[...])
        m = t if m is None else m + t
    y = _dot(m.astype(BF16), wout_ref[...])
    x1 = x + mod[2:3] * y
    x1_ref[...] = x1

    h2 = _norm_mod(x1, g[6:7], mod[4:5], mod[3:4])
    h_hi, h_lo = _split(h2)
    w_hi, w_lo = _split(wr_ref[...])
    both_w = _dot_nt(jnp.concatenate([w_hi, w_lo], axis=0), h_hi)
    logits = both_w[0:N_EXPERTS] + both_w[N_EXPERTS:2 * N_EXPERTS] + _dot_nt(w_hi, h_lo)
    score = 1.0 / (1.0 + jnp.exp(-logits))
    sel = score + br_ref[...]
    sel_r = [sel[e:e + 1] for e in range(N_EXPERTS)]
    sc_r = [score[e:e + 1] for e in range(N_EXPERTS)]
    picked = []
    for e in range(N_EXPERTS):
        grp, a = divmod(e, EXPERTS_PER_GROUP)
        rank = None
        for bb in range(EXPERTS_PER_GROUP):
            if bb == a:
                continue
            o = sel_r[grp * EXPERTS_PER_GROUP + bb]
            beats = (o >= sel_r[e]) if bb < a else (o > sel_r[e])
            r = jnp.where(beats, 1.0, 0.0)
            rank = r if rank is None else rank + r
        picked.append(rank < 2.0)
    gscore = []
    for grp in range(N_GROUPS):
        tot = None
        for a in range(EXPERTS_PER_GROUP):
            e = grp * EXPERTS_PER_GROUP + a
            t = jnp.where(picked[e], sel_r[e], 0.0)
            tot = t if tot is None else tot + t
        gscore.append(tot)
    best = jnp.zeros_like(gscore[0])
    best_v = gscore[0]
    for grp in range(1, N_GROUPS):
        upd = gscore[grp] > best_v
        best = jnp.where(upd, float(grp), best)
        best_v = jnp.where(upd, gscore[grp], best_v)
    cw, pk = [], []
    for a in range(EXPERTS_PER_GROUP):
        tot = flag = None
        for grp in range(N_GROUPS):
            e = grp * EXPERTS_PER_GROUP + a
            f = (best == float(grp)) & picked[e]
            t = jnp.where(f, sc_r[e], 0.0)
            tot = t if tot is None else tot + t
            flag = f if flag is None else (flag | f)
        cw.append(tot)
        pk.append(flag)
    den = cw[0] + cw[1] + cw[2] + cw[3]
    first = jnp.where(pk[0], 0.0, jnp.where(pk[1], 1.0, jnp.where(pk[2], 2.0, 3.0)))
    second = jnp.where(pk[3], 3.0, jnp.where(pk[2], 2.0, jnp.where(pk[1], 1.0, 0.0)))
    slot_e, slot_w = [], []
    for which in (first, second):
        tot = None
        for a in range(EXPERTS_PER_GROUP):
            t = jnp.where(which == float(a), cw[a], 0.0)
            tot = t if tot is None else tot + t
        slot_w.append(tot / den)
        slot_e.append(best * float(EXPERTS_PER_GROUP) + which)

    @pl.when(pl.program_id(0) == 0)
    def _():
        count_sc[...] = jnp.zeros(count_sc.shape, F32)

    tm = x.shape[0]
    eid = lax.broadcasted_iota(jnp.int32, (N_EXPERTS, tm), 0).astype(F32)
    oh = [eid == slot_e[0], eid == slot_e[1]]
    both = jnp.where(oh[0] | oh[1], 1.0, 0.0)
    seen = count_sc[...][:, 0:1] + _dot(both.astype(BF16), tri_ref[...])
    for k in range(2):
        route_ref[k:k + 1, :] = slot_w[k]
        route_ref[2 + k:3 + k, :] = slot_e[k]
        route_ref[4 + k:5 + k, :] = jnp.sum(jnp.where(oh[k], seen, 0.0), axis=0, keepdims=True)
    route_ref[6:8, :] = jnp.zeros((2, tm), F32)
    count_sc[...] = count_sc[...] + jnp.sum(both, axis=-1, keepdims=True)
    count_ref[...] = count_sc[...]
    h2p_ref[...] = _pack_pairs(h2)


def _merge(x, oa, ob, oc, mods, rows_per_mod, gains, w_gate, wo_a, wo_b, wo_c, w_out, w_r_t, b_r):
    n = x.shape[0]
    tm = min(1024, n)
    assert n % tm == 0 and (rows_per_mod % tm == 0 or rows_per_mod == n), (n, tm, rows_per_mod)
    row = lambda w: pl.BlockSpec((tm, w), lambda i: (i, 0))
    tri = jnp.asarray(np.triu(np.ones((tm, tm), np.float32), 1), BF16)
    consts = [gains, w_gate, wo_a, wo_b, wo_c, w_out, w_r_t, b_r, tri]
    return pl.pallas_call(
        _merge_body,
        grid=(n // tm,),
        in_specs=[row(D_MODEL), row(512), row(512), row(512),
                  pl.BlockSpec((1, 8, D_MODEL), lambda i: ((i * tm) // rows_per_mod, 0, 0))]
                 + [_const_spec(c.shape) for c in consts],
        out_specs=[row(D_MODEL), pl.BlockSpec((8, tm), lambda i: (0, i)), row(512),
                   pl.BlockSpec((N_EXPERTS, LANES), lambda i: (0, 0))],
        out_shape=[jax.ShapeDtypeStruct((n, D_MODEL), F32), jax.ShapeDtypeStruct((8, n), F32),
                   jax.ShapeDtypeStruct((n, 512), jnp.int32),
                   jax.ShapeDtypeStruct((N_EXPERTS, LANES), F32)],
        scratch_shapes=[pltpu.VMEM((N_EXPERTS, LANES), F32)],
        compiler_params=_cparams(("arbitrary",)),
        name="merge",
    )(x, oa, ob, oc, mods, *consts)


EXPERT_TILE = 512
SC_CORES = 2
SC_SUBCORES = 16
SC_WORKERS = SC_CORES * SC_SUBCORES
SC_WINDOW = 128


def _sc_mesh():
    return plsc.VectorSubcoreMesh(core_axis_name="c", subcore_axis_name="s", num_cores=SC_CORES,
                                  num_subcores=SC_SUBCORES)


def _sc_window_base(steps, j):
    wid = lax.axis_index("s") * SC_CORES + lax.axis_index("c")
    return pl.multiple_of((wid * steps + j) * SC_WINDOW, SC_WINDOW)


def _sc_dispatch(rows, pos0, pos1, n_out):
    n, w = rows.shape
    assert n % (SC_WORKERS * SC_WINDOW) == 0, n
    steps = n // (SC_WORKERS * SC_WINDOW)

    @functools.partial(
        pl.kernel, out_type=jax.ShapeDtypeStruct((n_out, w), rows.dtype), mesh=_sc_mesh(),
        scratch_types=[pltpu.VMEM((SC_WINDOW,), jnp.int32), pltpu.VMEM((SC_WINDOW,), jnp.int32),
                       pltpu.VMEM((SC_WINDOW, w), rows.dtype)],
        name="moe_dispatch")
    def run(x_hbm, i0_hbm, i1_hbm, o_hbm, i0_v, i1_v, rows_v):
        @pl.loop(0, steps)
        def _(j):
            base = _sc_window_base(steps, j)
            pltpu.sync_copy(i0_hbm.at[pl.ds(base, SC_WINDOW)], i0_v)
            pltpu.sync_copy(i1_hbm.at[pl.ds(base, SC_WINDOW)], i1_v)
            pltpu.sync_copy(x_hbm.at[pl.ds(base, SC_WINDOW)], rows_v)
            pltpu.sync_copy(rows_v, o_hbm.at[i0_v])
            pltpu.sync_copy(rows_v, o_hbm.at[i1_v])

    return run(rows, pos0, pos1)


def _sc_collect(rows, pos0, pos1):
    n = pos0.shape[0]
    w = rows.shape[1]
    assert n % (SC_WORKERS * SC_WINDOW) == 0, n
    steps = n // (SC_WORKERS * SC_WINDOW)
    out = jax.ShapeDtypeStruct((n, w), rows.dtype)

    @functools.partial(
        pl.kernel, out_type=[out, out], mesh=_sc_mesh(),
        scratch_types=[pltpu.VMEM((SC_WINDOW,), jnp.int32), pltpu.VMEM((SC_WINDOW, w), rows.dtype)],
        name="moe_collect")
    def run(y_hbm, i0_hbm, i1_hbm, o0_hbm, o1_hbm, i_v, rows_v):
        @pl.loop(0, steps)
        def _(j):
            base = _sc_window_base(steps, j)
            for i_hbm, o_hbm in ((i0_hbm, o0_hbm), (i1_hbm, o1_hbm)):
                pltpu.sync_copy(i_hbm.at[pl.ds(base, SC_WINDOW)], i_v)
                pltpu.sync_copy(y_hbm.at[i_v], rows_v)
                pltpu.sync_copy(rows_v, o_hbm.at[pl.ds(base, SC_WINDOW)])

    return run(rows, pos0, pos1)


def _experts_body(layer, te_ref, nv_ref, first_ref, next_ref, slot_ref, xs_ref, wg_hbm, wu_hbm, wd_hbm,
                  ys_ref, stage_g, stage_u, stage_d, wg_sc, wu_sc, wd_sc, sem):
    j = pl.program_id(0)

    def copies(expert, slot):
        return [pltpu.make_async_copy(w.at[layer, expert], st.at[slot], sem.at[slot, i])
                for i, (w, st) in enumerate(((wg_hbm, stage_g), (wu_hbm, stage_u), (wd_hbm, stage_d)))]

    @pl.when(j == 0)
    def _():
        for c in copies(te_ref[0], slot_ref[0]):
            c.start()

    @pl.when(first_ref[j] == 1)
    def _():
        slot = slot_ref[j]
        for c in copies(te_ref[j], slot):
            c.wait()

        @pl.when(next_ref[j] >= 0)
        def _():
            for c in copies(next_ref[j], 1 - slot):
                c.start()

        wg_sc[...] = stage_g[slot].astype(BF16)
        wu_sc[...] = stage_u[slot].astype(BF16)
        wd_sc[...] = stage_d[slot].astype(BF16)

    @pl.when(j < nv_ref[0])
    def _():
        x = _unpack_pairs(xs_ref[...]).astype(BF16)
        zg = _dot(x, wg_sc[...])
        act = zg * (1.0 / (1.0 + jnp.exp(-zg))) * _dot(x, wu_sc[...])
        ys_ref[...] = _pack_pairs(_dot(act.astype(BF16), wd_sc[...]))

    @pl.when(j >= nv_ref[0])
    def _():
        ys_ref[...] = jnp.zeros(ys_ref.shape, ys_ref.dtype)


def _experts(layer, xs, tile_expert, n_valid, wg, wu, wd):
    p = xs.shape[0]
    n_tiles = p // EXPERT_TILE
    assert p % EXPERT_TILE == 0
    tile = jnp.arange(n_tiles, dtype=jnp.int32)
    last = jnp.take(tile_expert, jnp.maximum(n_valid[0] - 1, 0))
    te = jnp.where(tile < n_valid[0], tile_expert, last)
    first = jnp.concatenate([jnp.ones((1,), jnp.int32), (te[1:] != te[:-1]).astype(jnp.int32)])
    slot = (jnp.cumsum(first) - 1) % 2
    later_first = (tile[None, :] > tile[:, None]) & (first[None, :] == 1)
    nxt_tile = jnp.min(jnp.where(later_first, tile[None, :], n_tiles), axis=1)
    nxt = jnp.where(nxt_tile < n_tiles, jnp.take(te, jnp.minimum(nxt_tile, n_tiles - 1)), -1)

    any_spec = pl.BlockSpec(memory_space=pl.ANY)
    idx = lambda j, *_: (j, 0)
    grid_spec = pltpu.PrefetchScalarGridSpec(
        num_scalar_prefetch=5,
        grid=(n_tiles,),
        in_specs=[pl.BlockSpec((EXPERT_TILE, 512), idx), any_spec, any_spec, any_spec],
        out_specs=pl.BlockSpec((EXPERT_TILE, 512), idx),
        scratch_shapes=[pltpu.VMEM((2, D_MODEL, D_FF), F32), pltpu.VMEM((2, D_MODEL, D_FF), F32),
                        pltpu.VMEM((2, D_FF, D_MODEL), F32),
                        pltpu.VMEM((D_MODEL, D_FF), BF16), pltpu.VMEM((D_MODEL, D_FF), BF16),
                        pltpu.VMEM((D_FF, D_MODEL), BF16), pltpu.SemaphoreType.DMA((2, 3))])
    return pl.pallas_call(
        functools.partial(_experts_body, layer), grid_spec=grid_spec,
        out_shape=jax.ShapeDtypeStruct((p, 512), jnp.int32),
        compiler_params=_cparams(("arbitrary",)),
        name="experts",
    )(te, n_valid, first, nxt.astype(jnp.int32), slot.astype(jnp.int32), xs, wg, wu, wd)


def _combine_body(x1_ref, y0_ref, y1_ref, route_ref, mod_ref, o_ref):
    tm = x1_ref.shape[0]
    eye = jnp.where(lax.broadcasted_iota(jnp.int32, (tm, tm), 0)
                    == lax.broadcasted_iota(jnp.int32, (tm, tm), 1), 1.0, 0.0).astype(BF16)
    r_hi, r_lo = _split(route_ref[...])
    wcol = _dot_nt(eye, r_hi) + _dot_nt(eye, r_lo)
    moe = wcol[:, 0:1] * _unpack_pairs(y0_ref[...]) + wcol[:, 1:2] * _unpack_pairs(y1_ref[...])
    o_ref[...] = x1_ref[...] + mod_ref[0][5:6] * moe


def _combine(x1, y0, y1, route, mods, rows_per_mod):
    n = x1.shape[0]
    tm = min(512, n)
    row = lambda w: pl.BlockSpec((tm, w), lambda i: (i, 0))
    return pl.pallas_call(
        _combine_body,
        grid=(n // tm,),
        in_specs=[row(D_MODEL), row(512), row(512), pl.BlockSpec((8, tm), lambda i: (0, i)),
                  pl.BlockSpec((1, 8, D_MODEL), lambda i: ((i * tm) // rows_per_mod, 0, 0))],
        out_specs=row(D_MODEL),
        out_shape=jax.ShapeDtypeStruct((n, D_MODEL), F32),
        compiler_params=_cparams(("parallel",)),
        name="combine",
    )(x1, y0, y1, route, mods)


def _moe(layer, x1, route, h2p, counts, mods, rows_per_mod, wg, wu, wd):
    n = x1.shape[0]
    p = 2 * n + N_EXPERTS * EXPERT_TILE
    cnt = counts[:, 0].astype(jnp.int32)
    padded = (cnt + EXPERT_TILE - 1) // EXPERT_TILE * EXPERT_TILE
    seg_end = jnp.cumsum(padded)
    seg_off = seg_end - padded
    experts = jnp.arange(N_EXPERTS, dtype=jnp.int32)

    def position(k):
        e = route[2 + k].astype(jnp.int32)
        off = jnp.sum(jnp.where(e[:, None] == experts[None], seg_off[None], 0), axis=1)
        return off + route[4 + k].astype(jnp.int32)

    pos0, pos1 = position(0), position(1)
    tile_start = jnp.arange(p // EXPERT_TILE, dtype=jnp.int32) * EXPERT_TILE
    tile_expert = jnp.sum(tile_start[:, None] >= seg_end[None], axis=1).astype(jnp.int32)
    tile_expert = jnp.minimum(tile_expert, N_EXPERTS - 1)
    n_valid = (seg_end[-1:] // EXPERT_TILE).astype(jnp.int32)

    xs = _sc_dispatch(h2p, pos0, pos1, p)
    ys = _experts(layer, xs, tile_expert, n_valid, wg, wu, wd)
    y0, y1 = _sc_collect(ys, pos0, pos1)
    return _combine(x1, y0, y1, route, mods, rows_per_mod)


def _block_ones(n_in, g_in, n_out, g_out, value=1.0):
    r = np.arange(n_in)[:, None] // g_in
    c = np.arange(n_out)[None, :] // g_out
    return jnp.asarray(np.where(r == c, value, 0.0), dtype=BF16)


def _rope_tables(seq, head_w):
    pos = np.arange(seq)
    rows, cols = pos // GRID_W, pos % GRID_W
    a = head_w // 2
    half = a // 2
    freqs = (ROPE_BASE ** (-np.arange(half, dtype=np.float32) / half)).astype(np.float32)
    lane = np.arange(LANES) % head_w
    within = lane % a
    first = within < half
    p = np.where((lane // a == 0)[None, :], rows[:, None], cols[:, None]).astype(np.float32)
    ang = (p * freqs[within % half][None, :]).astype(np.float32)
    cos, sin = np.cos(ang), np.sin(ang)
    return (jnp.asarray(cos, F32), jnp.asarray(np.where(first[None], -sin, 0.0), F32),
            jnp.asarray(np.where(first[None], 0.0, sin), F32))


def _tile_to(v, width):
    return jnp.tile(v, width // v.shape[0])


def _layer_params(i, p):
    w_in = p["w_in"][i]
    sp = np.cumsum((512, 128, 128, 512, 512, 512, Q_LORA, KV_LORA, ROPE_DIM))
    qa, ka, va, qb, kb, vb, cq, ckv, kr, gates = jnp.split(w_in, [int(s) for s in sp], axis=1)
    qa = qa.reshape(D_MODEL, WIN_HEADS, HEAD_DIM)[:, WIN_Q_ORDER, :].reshape(D_MODEL, 512)
    w_a = jnp.concatenate([qa, ka, va, qb, kb, vb, cq, ckv, jnp.tile(kr, (1, MLA_HEADS))],
                          axis=1).astype(BF16)
    w_uq = p["w_uq"][i].reshape(Q_LORA, MLA_HEADS, QK_DIM)
    w_uq = jnp.concatenate([w_uq[:, :, :NOPE_DIM].reshape(Q_LORA, 512),
                            w_uq[:, :, NOPE_DIM:].reshape(Q_LORA, 256)], axis=1).astype(BF16)
    w_ukv = p["w_ukv"][i].reshape(KV_LORA, MLA_HEADS, NOPE_DIM + V_DIM)
    w_ukv = jnp.concatenate([w_ukv[:, :, :NOPE_DIM].reshape(KV_LORA, 512),
                             w_ukv[:, :, NOPE_DIM:].reshape(KV_LORA, 512)], axis=1).astype(BF16)
    z = jnp.zeros((D_MODEL,), F32)
    row = lambda *parts: jnp.concatenate(list(parts) + [z])[:D_MODEL]
    q_scale = HEAD_DIM ** -0.5 * LOG2E
    c_scale = QK_DIM ** -0.5 * LOG2E
    g_mla = p["g_qk_mla"][i]
    gains = jnp.stack([
        p["g_norm_mix"][i],
        row(_tile_to(p["g_qk_win"][i, 0], 512) * q_scale, _tile_to(p["g_qk_win"][i, 1], 128)),
        row(_tile_to(p["g_qk_nbr"][i, 0], 512) * q_scale, _tile_to(p["g_qk_nbr"][i, 1], 512)),
        row(p["g_q_lora"][i], p["g_kv_lora"][i]),
        row(_tile_to(g_mla[0, :NOPE_DIM], 512) * c_scale, _tile_to(g_mla[0, NOPE_DIM:], 256) * c_scale),
        row(_tile_to(g_mla[1, :NOPE_DIM], 512), _tile_to(g_mla[1, NOPE_DIM:], 256)),
        p["g_norm_ffn"][i],
        z]).astype(F32)
    wo_a = p["w_o_win"][i].reshape(WIN_HEADS, HEAD_DIM, D_MODEL)[WIN_Q_ORDER, :, :].reshape(512, D_MODEL)
    return dict(
        w_a=w_a, w_uq=w_uq, w_ukv=w_ukv, gains=gains, w_gate=gates.astype(BF16),
        wo_a=wo_a.astype(BF16), wo_b=p["w_o_nbr"][i].astype(BF16), wo_c=p["w_o_mla"][i].astype(BF16),
        w_out=p["w_out"][i].astype(BF16),
        nbr_table=_nbr_bias_table(p["nbr_rel_bias"][i]))


def kernel(x_prompt, x_sample, cache_win_k, cache_win_v, cache_nbr_k, cache_nbr_v, cache_mla_ckv, cache_mla_krope, c, c_ctx, g_norm_mix, g_norm_ffn, w_ada, b_ada, w_in, g_qk_win, win_sink, g_qk_nbr, nbr_rel_bias, g_q_lora, g_kv_lora, w_uq, w_ukv, g_qk_mla, w_o_win, w_o_nbr, w_o_mla, w_out, w_router, b_router, w_exp_gate, w_exp_up, w_exp_down):
    p = dict(g_norm_mix=g_norm_mix, g_norm_ffn=g_norm_ffn, w_in=w_in, g_qk_win=g_qk_win,
             g_qk_nbr=g_qk_nbr, nbr_rel_bias=nbr_rel_bias, g_q_lora=g_q_lora, g_kv_lora=g_kv_lora,
             w_uq=w_uq, w_ukv=w_ukv, g_qk_mla=g_qk_mla, w_o_win=w_o_win, w_o_nbr=w_o_nbr,
             w_o_mla=w_o_mla, w_out=w_out, w_exp_gate=w_exp_gate, w_exp_up=w_exp_up,
             w_exp_down=w_exp_down)
    depth = w_in.shape[0]
    batch, seq, _ = x_prompt.shape
    dec_batch, dec_seq, _ = x_sample.shape
    past = cache_win_k.shape[2]

    n_c = 1 + dec_batch
    c_rows = -(-n_c // 8) * 8
    c_all = jnp.concatenate([c_ctx[None], c, jnp.zeros((c_rows - n_c, D_MODEL), F32)], axis=0)
    mods = _ada(c_all, w_ada, b_ada).reshape(depth, c_rows, 6, D_MODEL)
    mods = jnp.pad(mods, ((0, 0), (0, 0), (0, 2), (0, 0)))

    mats = (_block_ones(512, 64, 512, 64, 1.0 / HEAD_DIM), _block_ones(512, 64, 512, 64),
            _block_ones(256, 32, 512, 64), _block_ones(512, 64, 256, 32), _block_ones(256, 32, 256, 32))
    tabs = _rope_tables(dec_seq, 64) + _rope_tables(dec_seq, 32)
    sink = win_sink.astype(F32) * LOG2E
    w_r_t = w_router.T.astype(F32)
    b_r = b_router.astype(F32).reshape(N_EXPERTS, 1)
    layers = [_layer_params(i, p) for i in range(depth)]

    def merge(x, oa, ob, oc, mod, rows_per_mod, lp):
        return _merge(x, oa, ob, oc, mod, rows_per_mod, lp["gains"], lp["w_gate"], lp["wo_a"],
                      lp["wo_b"], lp["wo_c"], lp["w_out"], w_r_t, b_r)

    n_ctx = batch * seq
    n_lat = dec_batch * dec_seq
    x_ctx = x_prompt.reshape(n_ctx, D_MODEL)
    x_lat = x_sample.reshape(n_lat, D_MODEL)
    cwk = cache_win_k.reshape(dec_batch, depth, past, 128).astype(BF16)
    cwv = cache_win_v.reshape(dec_batch, depth, past, 128).astype(BF16)
    cnk = cache_nbr_k.reshape(dec_batch, depth, past, 512).astype(BF16)
    cnv = cache_nbr_v.reshape(dec_batch, depth, past, 512).astype(BF16)
    states = []
    for i, lp in enumerate(layers):
        mod_c = mods[i, 0:1]
        outs = _inproj(x_ctx, mod_c, n_ctx, lp["gains"], lp["w_a"], lp["w_uq"], lp["w_ukv"], mats, None,
                       seq, True)
        oa, ob, oc = _ctx_attn(i, sink, seq, *outs[:9])
        states.append(outs[9:])
        merged_c = merge(x_ctx, oa, ob, oc, mod_c, n_ctx, lp)

        mod_l = mods[i, 1:1 + dec_batch]
        qa, ka, va, qb, kb, vb, qc, kc, vc = _inproj(
            x_lat, mod_l, dec_seq, lp["gains"], lp["w_a"], lp["w_uq"], lp["w_ukv"], mats, tabs, dec_seq,
            False)
        kr_t = jnp.tile(cache_mla_krope[:, i].reshape(dec_batch * past, ROPE_DIM), (1, MLA_HEADS))
        kc_c, vc_c = _mla_cache_keys(cache_mla_ckv[:, i].reshape(dec_batch * past, KV_LORA), kr_t,
                                     lp["gains"], lp["w_ukv"], mats[1], mats[3], mats[4])
        r3 = lambda a: a.reshape(dec_batch, dec_seq, a.shape[-1])
        oa = _win_attn(i, sink, r3(qa), r3(ka), r3(va), cwk, cwv)
        ob = _nbr_attn(i, r3(qb), r3(kb), r3(vb), cnk, cnv, lp["nbr_table"])
        oc = _mla_attn(r3(qc), r3(kc), r3(vc), kc_c.reshape(dec_batch, past, 1024),
                       vc_c.reshape(dec_batch, past, 512))
        flat = lambda a: a.reshape(n_lat, 512)
        merged_l = merge(x_lat, flat(oa), flat(ob), flat(oc), mod_l, dec_seq, lp)

        x_ctx = _moe(i, *merged_c, mod_c, n_ctx, w_exp_gate, w_exp_up, w_exp_down)
        x_lat = _moe(i, *merged_l, mod_l, dec_seq, w_exp_gate, w_exp_up, w_exp_down)
    y_prompt = x_ctx.reshape(batch, seq, D_MODEL)
    y_sample = x_lat.reshape(dec_batch, dec_seq, D_MODEL)

    def stack(k, shape):
        return jnp.stack([s[k].reshape((batch, seq) + shape) for s in states], axis=1)

    return (y_prompt, y_sample,
            stack(0, (WIN_KV_HEADS, HEAD_DIM)), stack(1, (WIN_KV_HEADS, HEAD_DIM)),
            stack(2, (NBR_HEADS, HEAD_DIM)), stack(3, (NBR_HEADS, HEAD_DIM)),
            stack(4, (KV_LORA,)), stack(5, (ROPE_DIM,)))
```
